```python
import math
import jax, jax.numpy as jnp
from jax import lax
import numpy as np

D_MODEL = 1024
BATCH = 16
SEQ = 2048
DEPTH = 2

HEAD_DIM = 64
N_HEADS_FOX = 8
N_HEADS_SB = 8
DIL_PATTERNS = ((128, 1), (512, 4), (2048, 16))
N_DIL_GROUPS = len(DIL_PATTERNS)
N_HEADS_DIL = 4
D_FF = 4 * D_MODEL
ROPE_THETA = 10000.0
Q_BLOCK = 128
EPS = 1e-6
N_BRANCHES = 3

W_FOX = N_HEADS_FOX * HEAD_DIM
W_SB = N_HEADS_SB * HEAD_DIM
W_DIL = N_HEADS_DIL * HEAD_DIM
SZ_FOX_QKV = 3 * W_FOX
SZ_FORGET = N_HEADS_FOX
SZ_SB_QKV = 3 * W_SB
SZ_DIL_QKV = 3 * N_DIL_GROUPS * W_DIL
SZ_GATES = N_BRANCHES * D_MODEL
D_IN = SZ_FOX_QKV + SZ_FORGET + SZ_SB_QKV + SZ_DIL_QKV + SZ_GATES

kernel_name = "gated_parallel_fox_stickbreak_dilated"


def rms_norm(x, g):
    xf = x.astype(jnp.float32)
    y = xf * lax.rsqrt(jnp.mean(xf * xf, axis=-1, keepdims=True) + EPS)
    return (y * g.astype(jnp.float32)).astype(x.dtype)


def rope(x, positions):
    half = x.shape[-1] // 2
    inv = 1.0 / (ROPE_THETA ** (jnp.arange(half, dtype=jnp.float32) / half))
    ang = positions.astype(jnp.float32)[..., None] * inv
    cos = jnp.cos(ang)[:, :, None, :]
    sin = jnp.sin(ang)[:, :, None, :]
    xf = x.astype(jnp.float32)
    x1, x2 = xf[..., :half], xf[..., half:]
    return jnp.concatenate([x1 * cos - x2 * sin, x2 * cos + x1 * sin], axis=-1).astype(x.dtype)


def forgetting_attention(q, k, v, f_cum):
    B, T, H, D = q.shape
    scale = 1.0 / math.sqrt(D)
    f_t = jnp.transpose(f_cum, (0, 2, 1))
    outs = []
    for i in range(T // Q_BLOCK):
        lo, hi = i * Q_BLOCK, (i + 1) * Q_BLOCK
        s = jnp.einsum('bqhd,bkhd->bhqk', q[:, lo:hi], k[:, :hi]).astype(jnp.float32) * scale
        s = s + f_t[:, :, lo:hi, None] - f_t[:, :, None, :hi]
        mask = (lo + np.arange(Q_BLOCK))[:, None] >= np.arange(hi)[None, :]
        s = jnp.where(mask, s, -jnp.inf)
        p = jax.nn.softmax(s, axis=-1).astype(v.dtype)
        outs.append(jnp.einsum('bhqk,bkhd->bqhd', p, v[:, :hi]))
    return jnp.concatenate(outs, axis=1)


def stick_breaking_attention(q, k, v):
    B, T, H, D = q.shape
    scale = 1.0 / math.sqrt(D)
    outs = []
    for i in range(T // Q_BLOCK):
        lo, hi = i * Q_BLOCK, (i + 1) * Q_BLOCK
        z = jnp.einsum('bqhd,bkhd->bhqk', q[:, lo:hi], k[:, :hi]).astype(jnp.float32) * scale
        mask = (lo + np.arange(Q_BLOCK))[:, None] > np.arange(hi)[None, :]
        log_not = jnp.where(mask, jax.nn.log_sigmoid(-z), 0.0)
        later = lax.cumsum(log_not, axis=3, reverse=True) - log_not
        a = jnp.where(mask, jnp.exp(jax.nn.log_sigmoid(z) + later), 0.0)
        outs.append(jnp.einsum('bhqk,bkhd->bqhd', a.astype(v.dtype), v[:, :hi]))
    return jnp.concatenate(outs, axis=1)


def dilated_window_attention(q, k, v, window, dilation):
    B, T, H, D = q.shape
    n = T // dilation
    W = window // dilation
    Z = B * dilation
    scale = 1.0 / math.sqrt(D)

    def to_streams(a):
        return a.reshape(B, n, dilation, H, D).transpose(0, 2, 1, 3, 4).reshape(Z, n, H, D)

    qs, ks, vs = to_streams(q), to_streams(k), to_streams(v)
    qb = math.gcd(n, Q_BLOCK)
    nb = n // qb
    pad = ((0, 0), (W, 0), (0, 0), (0, 0))
    kp, vp = jnp.pad(ks, pad), jnp.pad(vs, pad)
    idx = np.arange(nb)[:, None] * qb + np.arange(qb + W)[None, :]
    kblk, vblk = kp[:, idx], vp[:, idx]
    qblk = qs.reshape(Z, nb, qb, H, D)
    s = jnp.einsum('znqhd,znkhd->znhqk', qblk, kblk).astype(jnp.float32) * scale
    dist = np.arange(qb)[:, None] + W - np.arange(qb + W)[None, :]
    band = (dist >= 0) & (dist <= W)
    mask = band[None] & (idx - W >= 0)[:, None, :]
    s = jnp.where(mask[None, :, None], s, -jnp.inf)
    m = jnp.max(s, axis=-1, keepdims=True)
    p = jnp.exp(s - m)
    den = jnp.sum(p, axis=-1, keepdims=True)
    o = jnp.einsum('znhqk,znkhd->znqhd', (p / den).astype(v.dtype), vblk)
    lse = jnp.transpose((m + jnp.log(den))[..., 0], (0, 1, 3, 2))
    o = o.reshape(B, dilation, n, H, D).transpose(0, 2, 1, 3, 4).reshape(B, T, H, D)
    lse = lse.reshape(B, dilation, n, H).transpose(0, 2, 1, 3).reshape(B, T, H)
    return o, lse


def _fwd_setup_inputs(seed: int = 0) -> dict:
    key = jax.random.key(seed)
    ks = jax.random.split(key, 17)
    f32 = jnp.float32

    def nrm(k, shape, fan_in, mult=1.0):
        return jax.random.normal(k, shape, f32) * (mult * fan_in ** -0.5)

    def gain(k, shape):
        return 1.0 + 0.05 * jax.random.normal(k, shape, f32)

    x = jax.random.normal(ks[0], (BATCH, SEQ, D_MODEL), f32)
    positions = jnp.broadcast_to(jnp.arange(SEQ, dtype=jnp.int32), (BATCH, SEQ))
    return {
        "x": x,
        "positions": positions,
        "attn_norm": gain(ks[1], (DEPTH, D_MODEL)),
        "w_in": nrm(ks[2], (DEPTH, D_MODEL, D_IN), D_MODEL),
        "b_forget": 2.0 + 0.1 * jax.random.normal(ks[3], (DEPTH, N_HEADS_FOX), f32),
        "q_norm_fox": gain(ks[4], (DEPTH, HEAD_DIM)),
        "k_norm_fox": gain(ks[5], (DEPTH, HEAD_DIM)),
        "q_norm_dil": gain(ks[6], (DEPTH, HEAD_DIM)),
        "k_norm_dil": gain(ks[7], (DEPTH, HEAD_DIM)),
        "w_up_fox": nrm(ks[8], (DEPTH, W_FOX, D_MODEL), W_FOX),
        "w_up_sb": nrm(ks[9], (DEPTH, W_SB, D_MODEL), W_SB),
        "w_up_dil": nrm(ks[10], (DEPTH, W_DIL, D_MODEL), W_DIL),
        "w_out": nrm(ks[11], (DEPTH, D_MODEL, D_MODEL), D_MODEL),
        "mlp_norm": gain(ks[12], (DEPTH, D_MODEL)),
        "w_mlp_in": nrm(ks[13], (DEPTH, D_MODEL, D_FF), D_MODEL),
        "w_mlp_out": nrm(ks[14], (DEPTH, D_FF, D_MODEL), D_FF, 0.5),
    }


def _fwd_reference(x, positions, attn_norm, w_in, b_forget, q_norm_fox, k_norm_fox, q_norm_dil, k_norm_dil,
              w_up_fox, w_up_sb, w_up_dil, w_out, mlp_norm, w_mlp_in, w_mlp_out):
    B, T, _ = x.shape
    o1 = SZ_FOX_QKV
    o2 = o1 + SZ_FORGET
    o3 = o2 + SZ_SB_QKV
    o4 = o3 + SZ_DIL_QKV
    for l in range(DEPTH):
        h = rms_norm(x, attn_norm[l])
        proj = h @ w_in[l]

        fox = proj[..., :o1].reshape(B, T, 3, N_HEADS_FOX, HEAD_DIM)
        qa = rms_norm(fox[:, :, 0], q_norm_fox[l])
        ka = rms_norm(fox[:, :, 1], k_norm_fox[l])
        va = fox[:, :, 2]
        log_f = jax.nn.log_sigmoid(proj[..., o1:o2].astype(jnp.float32) + b_forget[l].astype(jnp.float32))
        f_cum = jnp.cumsum(log_f, axis=1)
        out_a = forgetting_attention(qa, ka, va, f_cum)

        sb = proj[..., o2:o3].reshape(B, T, 3, N_HEADS_SB, HEAD_DIM)
        out_b = stick_breaking_attention(sb[:, :, 0], sb[:, :, 1], sb[:, :, 2])

        dil = proj[..., o3:o4].reshape(B, T, 3, N_DIL_GROUPS * N_HEADS_DIL, HEAD_DIM)
        qc = rope(rms_norm(dil[:, :, 0], q_norm_dil[l]), positions)
        kc = rope(rms_norm(dil[:, :, 1], k_norm_dil[l]), positions)
        vc = dil[:, :, 2]
        group_o, group_lse = [], []
        for g, (window, dilation) in enumerate(DIL_PATTERNS):
            sl = slice(g * N_HEADS_DIL, (g + 1) * N_HEADS_DIL)
            o_g, lse_g = dilated_window_attention(qc[:, :, sl], kc[:, :, sl], vc[:, :, sl], window, dilation)
            group_o.append(o_g)
            group_lse.append(lse_g)
        wts = jax.nn.softmax(jnp.stack(group_lse, axis=0), axis=0)
        out_c = jnp.sum(wts[..., None].astype(vc.dtype) * jnp.stack(group_o, axis=0), axis=0)

        gates = jax.nn.sigmoid(proj[..., o4:].astype(jnp.float32)).astype(x.dtype).reshape(B, T, N_BRANCHES, D_MODEL)
        y_a = out_a.reshape(B, T, W_FOX) @ w_up_fox[l]
        y_b = out_b.reshape(B, T, W_SB) @ w_up_sb[l]
        y_c = out_c.reshape(B, T, W_DIL) @ w_up_dil[l]
        merged = gates[:, :, 0] * y_a + gates[:, :, 1] * y_b + gates[:, :, 2] * y_c
        x = x + merged @ w_out[l]

        h2 = rms_norm(x, mlp_norm[l])
        x = x + jnp.square(jax.nn.relu(h2 @ w_mlp_in[l])) @ w_mlp_out[l]
    return x


import jax as _jax
import jax.numpy as _jnp

TWIN_FORMAT = 'train_step'
FWD_PARAMS = ['x', 'positions', 'attn_norm', 'w_in', 'b_forget', 'q_norm_fox', 'k_norm_fox', 'q_norm_dil', 'k_norm_dil', 'w_up_fox', 'w_up_sb', 'w_up_dil', 'w_out', 'mlp_norm', 'w_mlp_in', 'w_mlp_out']
TWIN_WEIGHTS = ['attn_norm', 'w_in', 'b_forget', 'q_norm_fox', 'k_norm_fox', 'q_norm_dil', 'k_norm_dil', 'w_up_fox', 'w_up_sb', 'w_up_dil', 'w_out', 'mlp_norm', 'w_mlp_in', 'w_mlp_out']
TWIN_DIFF_INPUT = 'x'
TWIN_INPUTS = ['x', 'positions', 'attn_norm', 'w_in', 'b_forget', 'q_norm_fox', 'k_norm_fox', 'q_norm_dil', 'k_norm_dil', 'w_up_fox', 'w_up_sb', 'w_up_dil', 'w_out', 'mlp_norm', 'w_mlp_in', 'w_mlp_out', 'loss_target', 'm_attn_norm', 'm_w_in', 'm_b_forget', 'm_q_norm_fox', 'm_k_norm_fox', 'm_q_norm_dil', 'm_k_norm_dil', 'm_w_up_fox', 'm_w_up_sb', 'm_w_up_dil', 'm_w_out', 'm_mlp_norm', 'm_w_mlp_in', 'm_w_mlp_out', 'v_attn_norm', 'v_w_in', 'v_b_forget', 'v_q_norm_fox', 'v_k_norm_fox', 'v_q_norm_dil', 'v_k_norm_dil', 'v_w_up_fox', 'v_w_up_sb', 'v_w_up_dil', 'v_w_out', 'v_mlp_norm', 'v_w_mlp_in', 'v_w_mlp_out']
TWIN_OUTPUTS = ['loss', 'grad_x', 'grad_attn_norm', 'grad_w_in', 'grad_b_forget', 'grad_q_norm_fox', 'grad_k_norm_fox', 'grad_q_norm_dil', 'grad_k_norm_dil', 'grad_w_up_fox', 'grad_w_up_sb', 'grad_w_up_dil', 'grad_w_out', 'grad_mlp_norm', 'grad_w_mlp_in', 'grad_w_mlp_out', 'delta_attn_norm', 'delta_w_in', 'delta_b_forget', 'delta_q_norm_fox', 'delta_k_norm_fox', 'delta_q_norm_dil', 'delta_k_norm_dil', 'delta_w_up_fox', 'delta_w_up_sb', 'delta_w_up_dil', 'delta_w_out', 'delta_mlp_norm', 'delta_w_mlp_in', 'delta_w_mlp_out', 'new_m_attn_norm', 'new_m_w_in', 'new_m_b_forget', 'new_m_q_norm_fox', 'new_m_k_norm_fox', 'new_m_q_norm_dil', 'new_m_k_norm_dil', 'new_m_w_up_fox', 'new_m_w_up_sb', 'new_m_w_up_dil', 'new_m_w_out', 'new_m_mlp_norm', 'new_m_w_mlp_in', 'new_m_w_mlp_out', 'new_v_attn_norm', 'new_v_w_in', 'new_v_b_forget', 'new_v_q_norm_fox', 'new_v_k_norm_fox', 'new_v_q_norm_dil', 'new_v_k_norm_dil', 'new_v_w_up_fox', 'new_v_w_up_sb', 'new_v_w_up_dil', 'new_v_w_out', 'new_v_mlp_norm', 'new_v_w_mlp_in', 'new_v_w_mlp_out']
TWIN_LEAF_KINDS = {'loss': 'loss', 'grad_x': 'grad_x', 'grad_attn_norm': 'grad_w', 'grad_w_in': 'grad_w', 'grad_b_forget': 'grad_w', 'grad_q_norm_fox': 'grad_w', 'grad_k_norm_fox': 'grad_w', 'grad_q_norm_dil': 'grad_w', 'grad_k_norm_dil': 'grad_w', 'grad_w_up_fox': 'grad_w', 'grad_w_up_sb': 'grad_w', 'grad_w_up_dil': 'grad_w', 'grad_w_out': 'grad_w', 'grad_mlp_norm': 'grad_w', 'grad_w_mlp_in': 'grad_w', 'grad_w_mlp_out': 'grad_w', 'delta_attn_norm': 'delta_w', 'delta_w_in': 'delta_w', 'delta_b_forget': 'delta_w', 'delta_q_norm_fox': 'delta_w', 'delta_k_norm_fox': 'delta_w', 'delta_q_norm_dil': 'delta_w', 'delta_k_norm_dil': 'delta_w', 'delta_w_up_fox': 'delta_w', 'delta_w_up_sb': 'delta_w', 'delta_w_up_dil': 'delta_w', 'delta_w_out': 'delta_w', 'delta_mlp_norm': 'delta_w', 'delta_w_mlp_in': 'delta_w', 'delta_w_mlp_out': 'delta_w', 'new_m_attn_norm': 'new_m', 'new_m_w_in': 'new_m', 'new_m_b_forget': 'new_m', 'new_m_q_norm_fox': 'new_m', 'new_m_k_norm_fox': 'new_m', 'new_m_q_norm_dil': 'new_m', 'new_m_k_norm_dil': 'new_m', 'new_m_w_up_fox': 'new_m', 'new_m_w_up_sb': 'new_m', 'new_m_w_up_dil': 'new_m', 'new_m_w_out': 'new_m', 'new_m_mlp_norm': 'new_m', 'new_m_w_mlp_in': 'new_m', 'new_m_w_mlp_out': 'new_m', 'new_v_attn_norm': 'new_v', 'new_v_w_in': 'new_v', 'new_v_b_forget': 'new_v', 'new_v_q_norm_fox': 'new_v', 'new_v_k_norm_fox': 'new_v', 'new_v_q_norm_dil': 'new_v', 'new_v_k_norm_dil': 'new_v', 'new_v_w_up_fox': 'new_v', 'new_v_w_up_sb': 'new_v', 'new_v_w_up_dil': 'new_v', 'new_v_w_out': 'new_v', 'new_v_mlp_norm': 'new_v', 'new_v_w_mlp_in': 'new_v', 'new_v_w_mlp_out': 'new_v'}


def _forward(args):
    return _fwd_reference(*[args[k] for k in FWD_PARAMS])


def _output_shape():
    out = _jax.eval_shape(lambda: _forward(_fwd_setup_inputs(0)))
    return out.shape, out.dtype

N_MICROBATCH = 1
ADAM_LR = 0.001
ADAM_B1 = 0.9
ADAM_B2 = 0.999
ADAM_EPS = 1e-08
ADAM_WD = 0.01
ADAM_STEP = 10
PER_EXAMPLE_BATCH_AXIS = {'x': 0, 'positions': 0, 'loss_target': 0}
SHARED_INPUTS = []
_WEIGHT_DTYPES = {'attn_norm': _jnp.float32, 'w_in': _jnp.float32, 'b_forget': _jnp.float32, 'q_norm_fox': _jnp.float32, 'k_norm_fox': _jnp.float32, 'q_norm_dil': _jnp.float32, 'k_norm_dil': _jnp.float32, 'w_up_fox': _jnp.float32, 'w_up_sb': _jnp.float32, 'w_up_dil': _jnp.float32, 'w_out': _jnp.float32, 'mlp_norm': _jnp.float32, 'w_mlp_in': _jnp.float32, 'w_mlp_out': _jnp.float32}
MOMENT_SCALE = {'attn_norm': 7.984534e+00, 'w_in': 5.418891e-01, 'b_forget': 6.036571e+01, 'q_norm_fox': 9.179531e+00, 'k_norm_fox': 9.071600e+00, 'q_norm_dil': 6.205394e-01, 'k_norm_dil': 6.302705e-01, 'w_up_fox': 9.835814e-01, 'w_up_sb': 9.711183e-01, 'w_up_dil': 4.535215e-01, 'w_out': 1.313457e+00, 'mlp_norm': 2.433798e+01, 'w_mlp_in': 5.975114e-01, 'w_mlp_out': 8.119639e+00}


def _to_microbatches(a, axis):
    t = _jnp.moveaxis(a, axis, 0)
    t = t.reshape((N_MICROBATCH, t.shape[0] // N_MICROBATCH) + t.shape[1:])
    return _jnp.moveaxis(t, 1, axis + 1)


def setup_inputs(seed: int = 0) -> dict:
    inp = _fwd_setup_inputs(seed)
    key = _jax.random.fold_in(_jax.random.key(seed), 7919)
    shape, _ = _output_shape()
    out = dict(inp)
    out["loss_target"] = _jax.random.normal(_jax.random.fold_in(key, 0), shape, _jnp.float32)
    for i, name in enumerate(TWIN_WEIGHTS):
        w = inp[name].astype(_jnp.float32)
        if MOMENT_SCALE is None:
            s = _jnp.sqrt(_jnp.mean(_jnp.square(w)) + 1e-30)
        else:
            s = MOMENT_SCALE[name]
        km, kv = _jax.random.split(_jax.random.fold_in(key, i + 1))
        out[name] = w
        out["m_" + name] = s * _jax.random.normal(km, w.shape, _jnp.float32)
        out["v_" + name] = (s * s) * _jax.random.uniform(kv, w.shape, _jnp.float32, 0.5, 1.5)
    if N_MICROBATCH > 1:
        for name, axis in PER_EXAMPLE_BATCH_AXIS.items():
            out[name] = _to_microbatches(out[name], axis)
    return {'x': out['x'], 'positions': out['positions'], 'attn_norm': out['attn_norm'], 'w_in': out['w_in'], 'b_forget': out['b_forget'], 'q_norm_fox': out['q_norm_fox'], 'k_norm_fox': out['k_norm_fox'], 'q_norm_dil': out['q_norm_dil'], 'k_norm_dil': out['k_norm_dil'], 'w_up_fox': out['w_up_fox'], 'w_up_sb': out['w_up_sb'], 'w_up_dil': out['w_up_dil'], 'w_out': out['w_out'], 'mlp_norm': out['mlp_norm'], 'w_mlp_in': out['w_mlp_in'], 'w_mlp_out': out['w_mlp_out'], 'loss_target': out['loss_target'], 'm_attn_norm': out['m_attn_norm'], 'm_w_in': out['m_w_in'], 'm_b_forget': out['m_b_forget'], 'm_q_norm_fox': out['m_q_norm_fox'], 'm_k_norm_fox': out['m_k_norm_fox'], 'm_q_norm_dil': out['m_q_norm_dil'], 'm_k_norm_dil': out['m_k_norm_dil'], 'm_w_up_fox': out['m_w_up_fox'], 'm_w_up_sb': out['m_w_up_sb'], 'm_w_up_dil': out['m_w_up_dil'], 'm_w_out': out['m_w_out'], 'm_mlp_norm': out['m_mlp_norm'], 'm_w_mlp_in': out['m_w_mlp_in'], 'm_w_mlp_out': out['m_w_mlp_out'], 'v_attn_norm': out['v_attn_norm'], 'v_w_in': out['v_w_in'], 'v_b_forget': out['v_b_forget'], 'v_q_norm_fox': out['v_q_norm_fox'], 'v_k_norm_fox': out['v_k_norm_fox'], 'v_q_norm_dil': out['v_q_norm_dil'], 'v_k_norm_dil': out['v_k_norm_dil'], 'v_w_up_fox': out['v_w_up_fox'], 'v_w_up_sb': out['v_w_up_sb'], 'v_w_up_dil': out['v_w_up_dil'], 'v_w_out': out['v_w_out'], 'v_mlp_norm': out['v_mlp_norm'], 'v_w_mlp_in': out['v_w_mlp_in'], 'v_w_mlp_out': out['v_w_mlp_out']}


def _loss(weights, diff, rest, loss_target):
    with _jax.named_scope("forward"):
        args = {**rest, TWIN_DIFF_INPUT: diff, **{k: w.astype(_WEIGHT_DTYPES[k]) for k, w in weights.items()}}
        y = _forward(args)
    with _jax.named_scope("loss_head"):
        err = _jnp.square(y.astype(_jnp.float32) - loss_target)
        return 0.5 * _jnp.sum(_jnp.mean(err, axis=-1)) if err.ndim else 0.5 * err


def _adamw(w, g, m, v):
    m = ADAM_B1 * m + (1.0 - ADAM_B1) * g
    v = ADAM_B2 * v + (1.0 - ADAM_B2) * _jnp.square(g)
    m_hat = m / (1.0 - ADAM_B1 ** ADAM_STEP)
    v_hat = v / (1.0 - ADAM_B2 ** ADAM_STEP)
    delta = -ADAM_LR * (m_hat / (_jnp.sqrt(v_hat) + ADAM_EPS) + ADAM_WD * w)
    return delta, m, v


def reference(x, positions, attn_norm, w_in, b_forget, q_norm_fox, k_norm_fox, q_norm_dil, k_norm_dil, w_up_fox, w_up_sb, w_up_dil, w_out, mlp_norm, w_mlp_in, w_mlp_out, loss_target, m_attn_norm, m_w_in, m_b_forget, m_q_norm_fox, m_k_norm_fox, m_q_norm_dil, m_k_norm_dil, m_w_up_fox, m_w_up_sb, m_w_up_dil, m_w_out, m_mlp_norm, m_w_mlp_in, m_w_mlp_out, v_attn_norm, v_w_in, v_b_forget, v_q_norm_fox, v_k_norm_fox, v_q_norm_dil, v_k_norm_dil, v_w_up_fox, v_w_up_sb, v_w_up_dil, v_w_out, v_mlp_norm, v_w_mlp_in, v_w_mlp_out):
    given = dict(x=x, positions=positions, attn_norm=attn_norm, w_in=w_in, b_forget=b_forget, q_norm_fox=q_norm_fox, k_norm_fox=k_norm_fox, q_norm_dil=q_norm_dil, k_norm_dil=k_norm_dil, w_up_fox=w_up_fox, w_up_sb=w_up_sb, w_up_dil=w_up_dil, w_out=w_out, mlp_norm=mlp_norm, w_mlp_in=w_mlp_in, w_mlp_out=w_mlp_out, loss_target=loss_target, m_attn_norm=m_attn_norm, m_w_in=m_w_in, m_b_forget=m_b_forget, m_q_norm_fox=m_q_norm_fox, m_k_norm_fox=m_k_norm_fox, m_q_norm_dil=m_q_norm_dil, m_k_norm_dil=m_k_norm_dil, m_w_up_fox=m_w_up_fox, m_w_up_sb=m_w_up_sb, m_w_up_dil=m_w_up_dil, m_w_out=m_w_out, m_mlp_norm=m_mlp_norm, m_w_mlp_in=m_w_mlp_in, m_w_mlp_out=m_w_mlp_out, v_attn_norm=v_attn_norm, v_w_in=v_w_in, v_b_forget=v_b_forget, v_q_norm_fox=v_q_norm_fox, v_k_norm_fox=v_k_norm_fox, v_q_norm_dil=v_q_norm_dil, v_k_norm_dil=v_k_norm_dil, v_w_up_fox=v_w_up_fox, v_w_up_sb=v_w_up_sb, v_w_up_dil=v_w_up_dil, v_w_out=v_w_out, v_mlp_norm=v_mlp_norm, v_w_mlp_in=v_w_mlp_in, v_w_mlp_out=v_w_mlp_out)
    weights = {n: given[n] for n in TWIN_WEIGHTS}
    shared = {n: given[n] for n in SHARED_INPUTS}
    per_example = {n: given[n] for n in ['x', 'positions']}
    grad_fn = _jax.value_and_grad(_loss, argnums=(0, 1))

    def one_microbatch(ex, loss_target):
        ex = dict(ex)
        diff = ex.pop(TWIN_DIFF_INPUT)
        return grad_fn(weights, diff, {**shared, **ex}, loss_target)

    if N_MICROBATCH == 1:
        loss, (grad_w, grad_x) = one_microbatch(per_example, given["loss_target"])
    else:
        def body(carry, xs):
            loss_sum, grad_sum = carry
            l_k, (gw_k, gx_k) = one_microbatch(xs[0], xs[1])
            with _jax.named_scope("update"):
                return (loss_sum + l_k, _jax.tree.map(_jnp.add, grad_sum, gw_k)), gx_k

        init = (_jnp.zeros((), _jnp.float32), _jax.tree.map(_jnp.zeros_like, weights))
        (loss, grad_w), grad_x = _jax.lax.scan(body, init, (per_example, given["loss_target"]))
    with _jax.named_scope("update"):
        delta_w, new_m, new_v = {}, {}, {}
        for n in TWIN_WEIGHTS:
            delta_w[n], new_m[n], new_v[n] = _adamw(weights[n], grad_w[n], given["m_" + n], given["v_" + n])
    return (loss, grad_x, *[grad_w[n] for n in TWIN_WEIGHTS], *[delta_w[n] for n in TWIN_WEIGHTS],
            *[new_m[n] for n in TWIN_WEIGHTS], *[new_v[n] for n in TWIN_WEIGHTS])
```

```python
import functools

import jax
import jax.numpy as jnp
from jax import lax
from jax.experimental import pallas as pl
from jax.experimental.pallas import tpu as pltpu

F32 = jnp.float32
BF16 = jnp.bfloat16
MM = jnp.bfloat16

HEAD = 64
LANES = 128
EPS = 1e-6
SCALE = 0.125
ROPE_THETA = 10000.0
DIL_PATTERNS = ((128, 1), (512, 4), (2048, 16))
ADAM_LR, ADAM_B1, ADAM_B2, ADAM_EPS, ADAM_WD, ADAM_STEP = 0.001, 0.9, 0.999, 1e-08, 0.01, 10

FOXQ, FOXK, FOXV = 0, 4, 8
SBQ, SBK, SBV = 12, 16, 20
DILQ, DILK, DILV = 24, 30, 36
GATE, FORGET, NBLK = 42, 66, 68
DPROJ = NBLK * LANES
O1, O2, O3, O4, DIN = 1536, 1544, 3080, 5384, 8456

VMEM_LIMIT = 56 * 1024 * 1024
MESH_ID = pl.DeviceIdType.MESH
ANY = pl.BlockSpec(memory_space=pl.ANY)


def _params(sem=None):
    return pltpu.CompilerParams(dimension_semantics=sem, vmem_limit_bytes=VMEM_LIMIT)


def _iota(shape, dim):
    return lax.broadcasted_iota(jnp.int32, shape, dim)


def _split2(x):
    hi = x.astype(BF16)
    lo = (x - hi.astype(F32)).astype(BF16)
    return hi, lo


def _split3(x):
    hi = x.astype(BF16)
    r = x - hi.astype(F32)
    mid = r.astype(BF16)
    lo = (r - mid.astype(F32)).astype(BF16)
    return hi, mid, lo


def _dot(a, b):
    return jnp.dot(a, b, preferred_element_type=F32)


def _dot_nt(a, b):
    return lax.dot_general(a, b, (((1,), (1,)), ((), ())), preferred_element_type=F32)


def _dot_tn(a, b):
    return lax.dot_general(a, b, (((0,), (0,)), ((), ())), preferred_element_type=F32)


def _xdot2(x, m):
    hi, lo = _split2(x)
    return _dot(hi, m) + _dot(lo, m)


def _xdot3(x, m):
    hi, mid, lo = _split3(x)
    return _dot(hi, m) + _dot(mid, m) + _dot(lo, m)


def _xdot3_left(m, x):
    hi, mid, lo = _split3(x)
    return _dot(m, hi) + _dot(m, mid) + _dot(m, lo)


def _head_mat(w):
    return ((_iota((w, w), 0) >> 6) == (_iota((w, w), 1) >> 6)).astype(BF16)


def _softplus_parts(z):
    e = jnp.exp(-jnp.abs(z))
    return e, jnp.maximum(z, 0.0) + jnp.log(1.0 + e)


def _fit(dim, want):
    t = min(want, dim)
    while dim % t:
        t -= LANES
        assert t > 0, (dim, want)
    return t


def matmul(a, b, *, ta=False, tb=False, out_dtype=F32, add=None, tm=1024, tn=512, tk=1024, name):
    K, M = a.shape if ta else a.shape[::-1]
    K2, N = b.shape[::-1] if tb else b.shape
    assert K == K2, (a.shape, b.shape, ta, tb)
    tm, tn, tk = _fit(M, tm), _fit(N, tn), _fit(K, tk)
    nk = K // tk
    dn = (((0 if ta else 1,), (1 if tb else 0,)), ((), ()))

    def body(*refs):
        if add is None:
            a_ref, b_ref, o_ref, acc_ref = refs
        else:
            a_ref, b_ref, add_ref, o_ref, acc_ref = refs
        k = pl.program_id(2)
        part = lax.dot_general(a_ref[...].astype(MM), b_ref[...].astype(MM), dn, preferred_element_type=F32)

        @pl.when(k == 0)
        def _():
            acc_ref[...] = part

        @pl.when(k > 0)
        def _():
            acc_ref[...] += part

        @pl.when(k == nk - 1)
        def _():
            r = acc_ref[...]
            if add is not None:
                r = r + add_ref[...]
            o_ref[...] = r.astype(out_dtype)

    a_spec = pl.BlockSpec((tk, tm), lambda i, j, k: (k, i)) if ta else pl.BlockSpec((tm, tk), lambda i, j, k: (i, k))
    b_spec = pl.BlockSpec((tn, tk), lambda i, j, k: (j, k)) if tb else pl.BlockSpec((tk, tn), lambda i, j, k: (k, j))
    o_spec = pl.BlockSpec((tm, tn), lambda i, j, k: (i, j))
    ins, specs = [a, b], [a_spec, b_spec]
    if add is not None:
        ins.append(add)
        specs.append(o_spec)
    return pl.pallas_call(
        body, name=name, grid=(M // tm, N // tn, nk), in_specs=specs, out_specs=o_spec,
        out_shape=jax.ShapeDtypeStruct((M, N), out_dtype), scratch_shapes=[pltpu.VMEM((tm, tn), F32)],
        compiler_params=_params(("parallel", "parallel", "arbitrary")),
    )(*ins)


def _rows(n, want=512):
    t = min(want, n)
    assert n % t == 0, (n, t)
    return t


def rmsnorm_fwd(x, g, *, name):
    n, d = x.shape
    tr = _rows(n)

    def body(x_ref, g_ref, o_ref):
        xv = x_ref[...]
        r = lax.rsqrt(jnp.mean(xv * xv, axis=1, keepdims=True) + EPS)
        o_ref[...] = (xv * r * g_ref[...]).astype(o_ref.dtype)

    row = pl.BlockSpec((tr, d), lambda i: (i, 0))
    vec = pl.BlockSpec((1, d), lambda i: (0, 0))
    return pl.pallas_call(body, name=name, grid=(n // tr,), in_specs=[row, vec], out_specs=row,
                          out_shape=jax.ShapeDtypeStruct((n, d), MM), compiler_params=_params(("parallel",)))(x, g)


def rmsnorm_bwd(x, g, dh, dres, *, name):
    n, d = x.shape
    tr = _rows(n)

    def body(x_ref, g_ref, dh_ref, dr_ref, dx_ref, dg_ref):
        @pl.when(pl.program_id(0) == 0)
        def _():
            dg_ref[...] = jnp.zeros_like(dg_ref)

        xv = x_ref[...]
        r = lax.rsqrt(jnp.mean(xv * xv, axis=1, keepdims=True) + EPS)
        y = xv * r
        dhv = dh_ref[...]
        dy = dhv * g_ref[...]
        dx_ref[...] = dr_ref[...] + r * (dy - y * jnp.mean(dy * y, axis=1, keepdims=True))
        dg_ref[...] += jnp.sum(dhv * y, axis=0, keepdims=True)

    row = pl.BlockSpec((tr, d), lambda i: (i, 0))
    vec = pl.BlockSpec((1, d), lambda i: (0, 0))
    return pl.pallas_call(
        body, name=name, grid=(n // tr,), in_specs=[row, vec, row, row], out_specs=[row, vec],
        out_shape=[jax.ShapeDtypeStruct((n, d), F32), jax.ShapeDtypeStruct((1, d), F32)],
        compiler_params=_params(("arbitrary",)))(x, g, dh, dres)


def loss_grad(y, tgt, *, name):
    n, d = y.shape
    tr = _rows(n)

    def body(y_ref, t_ref, dy_ref, acc_ref):
        @pl.when(pl.program_id(0) == 0)
        def _():
            acc_ref[...] = jnp.zeros_like(acc_ref)

        e = y_ref[...] - t_ref[...]
        dy_ref[...] = e * (1.0 / d)
        acc_ref[...] += jnp.sum(e * e, axis=0, keepdims=True)

    row = pl.BlockSpec((tr, d), lambda i: (i, 0))
    vec = pl.BlockSpec((1, d), lambda i: (0, 0))
    return pl.pallas_call(
        body, name=name, grid=(n // tr,), in_specs=[row, row], out_specs=[row, vec],
        out_shape=[jax.ShapeDtypeStruct((n, d), F32), jax.ShapeDtypeStruct((1, d), F32)],
        compiler_params=_params(("arbitrary",)))(y, tgt)


def relu2_fwd(u, *, name):
    n, d = u.shape
    tr = _rows(n)

    def body(u_ref, o_ref):
        r = jnp.maximum(u_ref[...], 0.0)
        o_ref[...] = (r * r).astype(o_ref.dtype)

    row = pl.BlockSpec((tr, d), lambda i: (i, 0))
    return pl.pallas_call(body, name=name, grid=(n // tr,), in_specs=[row], out_specs=row,
                          out_shape=jax.ShapeDtypeStruct((n, d), MM), compiler_params=_params(("parallel",)))(u)


def relu2_bwd(u, da, *, name):
    n, d = u.shape
    tr = _rows(n)

    def body(u_ref, da_ref, o_ref):
        o_ref[...] = (da_ref[...] * (2.0 * jnp.maximum(u_ref[...], 0.0))).astype(o_ref.dtype)

    row = pl.BlockSpec((tr, d), lambda i: (i, 0))
    return pl.pallas_call(body, name=name, grid=(n // tr,), in_specs=[row, row], out_specs=row,
                          out_shape=jax.ShapeDtypeStruct((n, d), MM), compiler_params=_params(("parallel",)))(u, da)


MERGE_W = 256


def _gate_specs(tr, d):
    per = d // MERGE_W
    base = GATE * LANES // MERGE_W
    return [pl.BlockSpec((tr, MERGE_W), functools.partial(lambda i, j, b: (i, base + per * b + j), b=b)) for b in range(3)]


def merge_fwd(proj, ys, *, name):
    n, d = ys[0].shape
    tr = _rows(n)

    def body(g0, g1, g2, y0, y1, y2, o_ref):
        acc = jax.nn.sigmoid(g0[...]) * y0[...]
        acc += jax.nn.sigmoid(g1[...]) * y1[...]
        acc += jax.nn.sigmoid(g2[...]) * y2[...]
        o_ref[...] = acc.astype(o_ref.dtype)

    blk = pl.BlockSpec((tr, MERGE_W), lambda i, j: (i, j))
    return pl.pallas_call(
        body, name=name, grid=(n // tr, d // MERGE_W), in_specs=_gate_specs(tr, d) + [blk] * 3, out_specs=blk,
        out_shape=jax.ShapeDtypeStruct((n, d), MM), compiler_params=_params(("parallel", "parallel")))(proj, proj, proj, *ys)


def merge_bwd(proj, ys, dm, *, name):
    n, d = dm.shape
    tr = _rows(n)

    def body(g0, g1, g2, y0, y1, y2, dm_ref, dy0, dy1, dy2, dg0, dg1, dg2):
        dmv = dm_ref[...]
        for g, y, dy, dg in ((g0, y0, dy0, dg0), (g1, y1, dy1, dg1), (g2, y2, dy2, dg2)):
            s = jax.nn.sigmoid(g[...])
            dy[...] = (dmv * s).astype(dy.dtype)
            dg[...] = (dmv * y[...] * s * (1.0 - s)).astype(dg.dtype)

    blk = pl.BlockSpec((tr, MERGE_W), lambda i, j: (i, j))
    out = jax.ShapeDtypeStruct((n, d), MM)
    return pl.pallas_call(
        body, name=name, grid=(n // tr, d // MERGE_W), in_specs=_gate_specs(tr, d) + [blk] * 4, out_specs=[blk] * 6,
        out_shape=[out] * 6, compiler_params=_params(("parallel", "parallel")))(proj, proj, proj, *ys, dm)


def adamw(w, g, m, v, *, name):
    r, c = w.shape
    tr = r
    while tr * c * 4 > (1 << 21) and tr % 16 == 0:
        tr //= 2
    c1 = 1.0 / (1.0 - ADAM_B1 ** ADAM_STEP)
    c2 = 1.0 / (1.0 - ADAM_B2 ** ADAM_STEP)

    def body(w_ref, g_ref, m_ref, v_ref, d_ref, mo_ref, vo_ref):
        gv = g_ref[...]
        m2 = ADAM_B1 * m_ref[...] + (1.0 - ADAM_B1) * gv
        v2 = ADAM_B2 * v_ref[...] + (1.0 - ADAM_B2) * (gv * gv)
        d_ref[...] = -ADAM_LR * ((m2 * c1) / (jnp.sqrt(v2 * c2) + ADAM_EPS) + ADAM_WD * w_ref[...])
        mo_ref[...] = m2
        vo_ref[...] = v2

    blk = pl.BlockSpec((tr, c), lambda i: (i, 0))
    out = jax.ShapeDtypeStruct((r, c), F32)
    return pl.pallas_call(body, name=name, grid=(r // tr,), in_specs=[blk] * 4, out_specs=[blk] * 3, out_shape=[out] * 3,
                          compiler_params=_params(("parallel",)))(w, g, m, v)


def rope_table(pos, inv, *, name):
    n = pos.shape[0]
    tr = _rows(n)

    def body(p_ref, i_ref, c_ref, s_ref):
        ang = p_ref[...].astype(F32) * i_ref[...]
        c_ref[...] = jnp.cos(ang)
        s_ref[...] = jnp.sin(ang)

    out = jax.ShapeDtypeStruct((n, LANES), F32)
    blk = pl.BlockSpec((tr, LANES), lambda i: (i, 0))
    return pl.pallas_call(
        body, name=name, grid=(n // tr,), in_specs=[pl.BlockSpec((tr, 1), lambda i: (i, 0)), pl.BlockSpec((1, LANES), lambda i: (0, 0))],
        out_specs=[blk, blk], out_shape=[out, out], compiler_params=_params(("parallel",)))(pos, inv)


def _rot_half(x):
    first = (_iota((1, LANES), 1) & 63) < 32
    return jnp.where(first, -pltpu.roll(x, LANES - 32, axis=1), pltpu.roll(x, 32, axis=1))


def _head_norm(xv, gm):
    r = lax.rsqrt(_xdot2(xv * xv, gm) * (1.0 / HEAD) + EPS)
    return r, xv * r


def _head_norm_bwd(xh, r, dxh, gm):
    return r * (dxh - xh * (_xdot2(dxh * xh, gm) * (1.0 / HEAD)))


def fox_prep_fwd(proj, qg, kg, bf, *, bsz, seq, name):
    n = bsz * seq
    tr = min(256, seq)
    nt = seq // tr
    w = 4 * LANES

    def body(q_ref, k_ref, f_ref, qg_ref, kg_ref, b_ref, qn_ref, kn_ref, fb_ref, f8_ref, carry):
        @pl.when(pl.program_id(1) == 0)
        def _():
            carry[...] = jnp.zeros_like(carry)

        gm = _head_mat(LANES)
        for src, gain, dst in ((q_ref, qg_ref, qn_ref), (k_ref, kg_ref, kn_ref)):
            for c in range(4):
                sl = slice(c * LANES, (c + 1) * LANES)
                _, xh = _head_norm(src[:, sl], gm)
                dst[:, sl] = (xh * gain[:, sl]).astype(dst.dtype)
        logf = jax.nn.log_sigmoid(f_ref[...] + b_ref[...])
        lower = (_iota((tr, tr), 1) <= _iota((tr, tr), 0)).astype(BF16)
        fcum = _xdot3_left(lower, logf) + carry[...]
        carry[...] = fcum[tr - 1:tr, :]
        f8_ref[...] = fcum
        spread = (_iota((LANES, w), 0) == (_iota((LANES, w), 1) >> 6)).astype(BF16)
        fb_ref[...] = _xdot3(fcum, spread)

    row = lambda width, blk: pl.BlockSpec((tr, width), lambda b, t: (b * nt + t, blk))
    vec = lambda width: pl.BlockSpec((1, width), lambda b, t: (0, 0))
    return pl.pallas_call(
        body, name=name, grid=(bsz, nt),
        in_specs=[row(w, FOXQ // 4), row(w, FOXK // 4), row(LANES, FORGET), vec(w), vec(w), vec(LANES)],
        out_specs=[row(w, 0), row(w, 0), row(w, 0), row(LANES, 0)],
        out_shape=[jax.ShapeDtypeStruct((n, w), MM), jax.ShapeDtypeStruct((n, w), MM),
                   jax.ShapeDtypeStruct((n, w), F32), jax.ShapeDtypeStruct((n, LANES), F32)],
        scratch_shapes=[pltpu.VMEM((1, LANES), F32)],
        compiler_params=_params(("parallel", "arbitrary")))(proj, proj, proj, qg, kg, bf)


def fox_prep_bwd(proj, qg, kg, bf, dqn, dkn, df, *, bsz, seq, name):
    n = bsz * seq
    tr = min(256, seq)
    nt = seq // tr
    w = 4 * LANES

    def body(q_ref, k_ref, f_ref, qg_ref, kg_ref, b_ref, dqn_ref, dkn_ref, df_ref,
             dq_ref, dk_ref, dl_ref, dqg_ref, dkg_ref, db_ref, carry):
        first = (pl.program_id(0) == 0) & (pl.program_id(1) == 0)

        @pl.when(first)
        def _():
            dqg_ref[...] = jnp.zeros_like(dqg_ref)
            dkg_ref[...] = jnp.zeros_like(dkg_ref)
            db_ref[...] = jnp.zeros_like(db_ref)

        @pl.when(pl.program_id(1) == 0)
        def _():
            carry[...] = jnp.zeros_like(carry)

        gm = _head_mat(LANES)
        for src, gain, dy_ref, dx_ref, dg_ref in ((q_ref, qg_ref, dqn_ref, dq_ref, dqg_ref), (k_ref, kg_ref, dkn_ref, dk_ref, dkg_ref)):
            for c in range(4):
                sl = slice(c * LANES, (c + 1) * LANES)
                r, xh = _head_norm(src[:, sl], gm)
                dy = dy_ref[:, sl]
                dg_ref[:, sl] += jnp.sum(dy * xh, axis=0, keepdims=True)
                dx_ref[:, sl] = _head_norm_bwd(xh, r, dy * gain[:, sl], gm).astype(dx_ref.dtype)
        upper = (_iota((tr, tr), 1) >= _iota((tr, tr), 0)).astype(BF16)
        dlogf = _xdot3_left(upper, df_ref[...]) + carry[...]
        carry[...] = dlogf[0:1, :]
        dlogit = dlogf * jax.nn.sigmoid(-(f_ref[...] + b_ref[...]))
        dl_ref[:, 0:LANES] = dlogit.astype(dl_ref.dtype)
        dl_ref[:, LANES:2 * LANES] = jnp.zeros((tr, LANES), dl_ref.dtype)
        db_ref[...] += jnp.sum(dlogit, axis=0, keepdims=True)

    row = lambda width, blk: pl.BlockSpec((tr, width), lambda b, t: (b * nt + nt - 1 - t, blk))
    vec = lambda width: pl.BlockSpec((1, width), lambda b, t: (0, 0))
    return pl.pallas_call(
        body, name=name, grid=(bsz, nt),
        in_specs=[row(w, FOXQ // 4), row(w, FOXK // 4), row(LANES, FORGET), vec(w), vec(w), vec(LANES),
                  row(w, 0), row(w, 0), row(LANES, 0)],
        out_specs=[row(w, 0), row(w, 0), row(2 * LANES, 0), vec(w), vec(w), vec(LANES)],
        out_shape=[jax.ShapeDtypeStruct((n, w), MM), jax.ShapeDtypeStruct((n, w), MM), jax.ShapeDtypeStruct((n, 2 * LANES), MM),
                   jax.ShapeDtypeStruct((1, w), F32), jax.ShapeDtypeStruct((1, w), F32), jax.ShapeDtypeStruct((1, LANES), F32)],
        scratch_shapes=[pltpu.VMEM((1, LANES), F32)],
        compiler_params=_params(("arbitrary", "arbitrary")))(proj, proj, proj, qg, kg, bf, dqn, dkn, df)


DIL_W = 6 * LANES


def dil_prep_fwd(proj, qg, kg, cos, sin, *, name):
    n = proj.shape[0]
    tr = _rows(n, 256)

    def body(q_ref, k_ref, qg_ref, kg_ref, c_ref, s_ref, qo_ref, ko_ref):
        gm = _head_mat(LANES)
        cv, sv = c_ref[...], s_ref[...]
        for src, gain, dst in ((q_ref, qg_ref, qo_ref), (k_ref, kg_ref, ko_ref)):
            for c in range(6):
                sl = slice(c * LANES, (c + 1) * LANES)
                _, xh = _head_norm(src[:, sl], gm)
                xn = xh * gain[:, sl]
                dst[:, sl] = (xn * cv + _rot_half(xn) * sv).astype(dst.dtype)

    row = lambda width, blk: pl.BlockSpec((tr, width), lambda i: (i, blk))
    vec = pl.BlockSpec((1, DIL_W), lambda i: (0, 0))
    out = jax.ShapeDtypeStruct((n, DIL_W), MM)
    return pl.pallas_call(
        body, name=name, grid=(n // tr,),
        in_specs=[row(DIL_W, DILQ // 6), row(DIL_W, DILK // 6), vec, vec, row(LANES, 0), row(LANES, 0)],
        out_specs=[row(DIL_W, 0), row(DIL_W, 0)], out_shape=[out, out],
        compiler_params=_params(("parallel",)))(proj, proj, qg, kg, cos, sin)


def dil_prep_bwd(proj, qg, kg, cos, sin, dqr, dkr, *, name):
    n = proj.shape[0]
    tr = _rows(n, 256)

    def body(q_ref, k_ref, qg_ref, kg_ref, c_ref, s_ref, dqr_ref, dkr_ref, dq_ref, dk_ref, dqg_ref, dkg_ref):
        @pl.when(pl.program_id(0) == 0)
        def _():
            dqg_ref[...] = jnp.zeros_like(dqg_ref)
            dkg_ref[...] = jnp.zeros_like(dkg_ref)

        gm = _head_mat(LANES)
        cv, sv = c_ref[...], s_ref[...]
        for src, gain, dy_ref, dx_ref, dg_ref in ((q_ref, qg_ref, dqr_ref, dq_ref, dqg_ref), (k_ref, kg_ref, dkr_ref, dk_ref, dkg_ref)):
            for c in range(6):
                sl = slice(c * LANES, (c + 1) * LANES)
                r, xh = _head_norm(src[:, sl], gm)
                dy = dy_ref[:, sl]
                dxn = dy * cv - _rot_half(dy * sv)
                dg_ref[:, sl] += jnp.sum(dxn * xh, axis=0, keepdims=True)
                dx_ref[:, sl] = _head_norm_bwd(xh, r, dxn * gain[:, sl], gm).astype(dx_ref.dtype)

    row = lambda width, blk: pl.BlockSpec((tr, width), lambda i: (i, blk))
    vec = pl.BlockSpec((1, DIL_W), lambda i: (0, 0))
    out = jax.ShapeDtypeStruct((n, DIL_W), MM)
    gout = jax.ShapeDtypeStruct((1, DIL_W), F32)
    return pl.pallas_call(
        body, name=name, grid=(n // tr,),
        in_specs=[row(DIL_W, DILQ // 6), row(DIL_W, DILK // 6), vec, vec, row(LANES, 0), row(LANES, 0), row(DIL_W, 0), row(DIL_W, 0)],
        out_specs=[row(DIL_W, 0), row(DIL_W, 0), vec, vec], out_shape=[out, out, gout, gout],
        compiler_params=_params(("arbitrary",)))(proj, proj, qg, kg, cos, sin, dqr, dkr)


def dil_combine_fwd(os_, lses, *, name):
    n, w = os_[0].shape
    tr = _rows(n)

    def body(o0, o1, o2, l0, l1, l2, out_ref):
        a, b, c = l0[...], l1[...], l2[...]
        m = jnp.maximum(jnp.maximum(a, b), c)
        ea, eb, ec = jnp.exp(a - m), jnp.exp(b - m), jnp.exp(c - m)
        out_ref[...] = ((ea * o0[...] + eb * o1[...] + ec * o2[...]) / (ea + eb + ec)).astype(out_ref.dtype)

    blk = pl.BlockSpec((tr, w), lambda i: (i, 0))
    return pl.pallas_call(body, name=name, grid=(n // tr,), in_specs=[blk] * 6, out_specs=blk,
                          out_shape=jax.ShapeDtypeStruct((n, w), MM), compiler_params=_params(("parallel",)))(*os_, *lses)


def dil_combine_bwd(os_, lses, dout, *, name):
    n, w = dout.shape
    tr = _rows(n)

    def body(o0, o1, o2, l0, l1, l2, d_ref, do0, do1, do2, dl0, dl1, dl2):
        a, b, c = l0[...], l1[...], l2[...]
        m = jnp.maximum(jnp.maximum(a, b), c)
        es = [jnp.exp(a - m), jnp.exp(b - m), jnp.exp(c - m)]
        inv = 1.0 / (es[0] + es[1] + es[2])
        ws = [e * inv for e in es]
        dv = d_ref[...]
        gm = _head_mat(w)
        dws = [_xdot2(dv * o[...], gm) for o in (o0, o1, o2)]
        mean = ws[0] * dws[0] + ws[1] * dws[1] + ws[2] * dws[2]
        for wg, dw, do, dl in zip(ws, dws, (do0, do1, do2), (dl0, dl1, dl2)):
            do[...] = wg * dv
            dl[...] = wg * (dw - mean)

    blk = pl.BlockSpec((tr, w), lambda i: (i, 0))
    out = jax.ShapeDtypeStruct((n, w), F32)
    return pl.pallas_call(body, name=name, grid=(n // tr,), in_specs=[blk] * 7, out_specs=[blk] * 6, out_shape=[out] * 6,
                          compiler_params=_params(("parallel",)))(*os_, *lses, dout)


BUCKET = 512


def _extents(seq, window, tq):
    need = window + tq
    if need >= seq:
        step = max(BUCKET, tq)
        if seq % step:
            return [seq], None
        return list(range(step, seq + 1, step)), step
    return [need], None


def _for_extent(qi, tq, seq, window, run):
    exts, step = _extents(seq, window, tq)
    hi = (qi + 1) * tq
    if step is None:
        ext = exts[0]
        if ext >= seq:
            run(seq, 0)
        else:
            run(ext, pl.multiple_of(jnp.maximum(hi - ext, 0), LANES))
        return
    bucket = (hi - 1) // step
    for bi, ext in enumerate(exts):
        pl.when(bucket == bi)(functools.partial(run, ext, 0))


def _lane_first():
    return _iota((1, LANES), 1) < HEAD


def _band(qi, tq, ext, start, window, dilation, strict=False):
    d = (qi * tq + _iota((tq, ext), 0)) - (start + _iota((tq, ext), 1))
    ok = (d > 0) if strict else (d >= 0)
    if window < 1 << 30:
        ok &= d <= window
    if dilation > 1:
        ok &= (d & (dilation - 1)) == 0
    return ok


def _attn_specs(bsz, seq, tq, qo, ko, vo):
    nq = seq // tq
    qspec = lambda off: pl.BlockSpec((tq, LANES), lambda b, j, i: (b * nq + i, off + j))
    kspec = lambda off: pl.BlockSpec((seq, LANES), lambda b, j, i: (b, off + j))
    return nq, qspec, kspec


def softmax_attn_fwd(q, k, v, bias, *, qo, ko, vo, pairs, bsz, seq, window, dilation, tq, name):
    n = bsz * seq
    nq, qspec, kspec = _attn_specs(bsz, seq, tq, qo, ko, vo)

    def body(*refs):
        if bias is None:
            q_ref, k_ref, v_ref, o_ref, l_ref = refs
        else:
            q_ref, k_ref, v_ref, fq_ref, fk_ref, o_ref, l_ref = refs
        qi = pl.program_id(2)

        def run(ext, start):
            qv = q_ref[...].astype(MM)
            kv = k_ref[pl.ds(start, ext), :].astype(MM)
            vv = v_ref[pl.ds(start, ext), :].astype(MM)
            ok = _band(qi, tq, ext, start, window, dilation)
            first = _lane_first()
            outs, lses = [], []
            for a in range(2):
                s = _dot_nt(jnp.where(first if a == 0 else ~first, qv, jnp.zeros_like(qv)), kv) * SCALE
                if bias is not None:
                    s = s + fq_ref[:, a * HEAD:a * HEAD + 1] - fk_ref[a:a + 1, 0:ext]
                s = jnp.where(ok, s, -jnp.inf)
                m = jnp.max(s, axis=1, keepdims=True)
                p = jnp.exp(s - m)
                den = jnp.sum(p, axis=1, keepdims=True)
                outs.append(_dot(p.astype(MM), vv) / den)
                lses.append(m + jnp.log(den))
            o_ref[...] = jnp.where(first, outs[0], outs[1]).astype(o_ref.dtype)
            l_ref[...] = jnp.where(first, lses[0], lses[1])

        _for_extent(qi, tq, seq, window, run)

    ins, specs = [q, k, v], [qspec(qo), kspec(ko), kspec(vo)]
    if bias is not None:
        ins += list(bias)
        specs += [qspec(0), pl.BlockSpec((8, seq), lambda b, j, i: (b * pairs + j, 0))]
    out = jax.ShapeDtypeStruct((n, LANES * pairs), F32)
    return pl.pallas_call(
        body, name=name, grid=(bsz, pairs, nq), in_specs=specs, out_specs=[qspec(0), qspec(0)], out_shape=[out, out],
        compiler_params=_params(("parallel", "parallel", "arbitrary")))(*ins)


def softmax_attn_bwd(q, k, v, o, do, lse, dlse, bias, *, qo, ko, vo, pairs, bsz, seq, window, dilation, tq, dq_dtype, dk_dtype, name):
    n = bsz * seq
    nq, qspec, kspec = _attn_specs(bsz, seq, tq, qo, ko, vo)
    has_bias, has_dlse = bias is not None, dlse is not None

    def body(*refs):
        refs = list(refs)
        q_ref, k_ref, v_ref, o_ref, do_ref, l_ref = refs[:6]
        del refs[:6]
        dl_ref = refs.pop(0) if has_dlse else None
        fq_ref, fk_ref = (refs.pop(0), refs.pop(0)) if has_bias else (None, None)
        dq_ref, dk_ref, dv_ref = refs[:3]
        del refs[:3]
        dfq_ref, dfk_ref = (refs.pop(0), refs.pop(0)) if has_bias else (None, None)
        dk_acc, dv_acc = refs
        qi = pl.program_id(2)

        @pl.when(qi == 0)
        def _():
            dk_acc[...] = jnp.zeros_like(dk_acc)
            dv_acc[...] = jnp.zeros_like(dv_acc)
            if has_bias:
                dfk_ref[...] = jnp.zeros_like(dfk_ref)

        def run(ext, start):
            qv = q_ref[...].astype(MM)
            kv = k_ref[pl.ds(start, ext), :].astype(MM)
            vv = v_ref[pl.ds(start, ext), :].astype(MM)
            dov = do_ref[...]
            dob = dov.astype(MM)
            prod = dov * o_ref[...]
            ok = _band(qi, tq, ext, start, window, dilation)
            first = _lane_first()
            dqs, dks, dvs, dfqs = [], [], [], []
            for a in range(2):
                mine = first if a == 0 else ~first
                col = slice(a * HEAD, a * HEAD + 1)
                delta = jnp.sum(jnp.where(mine, prod, 0.0), axis=1, keepdims=True)
                if has_dlse:
                    delta = delta - dl_ref[:, col]
                s = _dot_nt(jnp.where(mine, qv, jnp.zeros_like(qv)), kv) * SCALE
                if has_bias:
                    s = s + fq_ref[:, col] - fk_ref[a:a + 1, 0:ext]
                p = jnp.exp(jnp.where(ok, s, -jnp.inf) - l_ref[:, col])
                dp = _dot_nt(jnp.where(mine, dob, jnp.zeros_like(dob)), vv)
                ds = p * (dp - delta)
                dsb = ds.astype(MM)
                dvs.append(_dot_tn(p.astype(MM), dob))
                dks.append(_dot_tn(dsb, qv) * SCALE)
                dqs.append(_dot(dsb, kv) * SCALE)
                if has_bias:
                    dfqs.append(jnp.sum(ds, axis=1, keepdims=True))
                    dfk_ref[a:a + 1, 0:ext] += jnp.sum(ds, axis=0, keepdims=True)
            dq_ref[...] = jnp.where(first, dqs[0], dqs[1]).astype(dq_ref.dtype)
            dk_acc[pl.ds(start, ext), :] += jnp.where(first, dks[0], dks[1])
            dv_acc[pl.ds(start, ext), :] += jnp.where(first, dvs[0], dvs[1])
            if has_bias:
                dfq_ref[...] = jnp.where(first, dfqs[0], dfqs[1])

        _for_extent(qi, tq, seq, window, run)

        @pl.when(qi == nq - 1)
        def _():
            dk_ref[...] = dk_acc[...].astype(dk_ref.dtype)
            dv_ref[...] = dv_acc[...].astype(dv_ref.dtype)

    wide = LANES * pairs
    ins = [q, k, v, o, do, lse]
    specs = [qspec(qo), kspec(ko), kspec(vo), qspec(0), qspec(0), qspec(0)]
    outs = [jax.ShapeDtypeStruct((n, wide), dq_dtype), jax.ShapeDtypeStruct((n, wide), dk_dtype), jax.ShapeDtypeStruct((n, wide), MM)]
    out_specs = [qspec(0), kspec(0), kspec(0)]
    if has_dlse:
        ins.append(dlse)
        specs.append(qspec(0))
    if has_bias:
        rows = pl.BlockSpec((8, seq), lambda b, j, i: (b * pairs + j, 0))
        ins += list(bias)
        specs += [qspec(0), rows]
        outs += [jax.ShapeDtypeStruct((n, wide), F32), jax.ShapeDtypeStruct((bsz * pairs * 8, seq), F32)]
        out_specs += [qspec(0), rows]
    return pl.pallas_call(
        body, name=name, grid=(bsz, pairs, nq), in_specs=specs, out_specs=out_specs, out_shape=outs,
        scratch_shapes=[pltpu.VMEM((seq, LANES), F32), pltpu.VMEM((seq, LANES), F32)],
        compiler_params=_params(("parallel", "parallel", "arbitrary")))(*ins)


def _sb_weights(qa, kv, mask, ext, tq, lat_ref):
    z = _dot_nt(qa, kv) * SCALE
    e, sp = _softplus_parts(z)
    log_not = jnp.where(mask, -sp, 0.0)
    after = (_iota((LANES, LANES), 0) > _iota((LANES, LANES), 1)).astype(BF16)
    carry = jnp.zeros((tq, 1), F32)
    for cb in reversed(range(ext // LANES)):
        sl = slice(cb * LANES, (cb + 1) * LANES)
        blk = log_not[:, sl]
        lat_ref[:, sl] = _xdot2(blk, after) + carry
        carry = carry + jnp.sum(blk, axis=1, keepdims=True)
    att = jnp.where(mask, jnp.exp(z - sp + lat_ref[:, 0:ext]), 0.0)
    sig = jnp.where(z >= 0, 1.0, e) / (1.0 + e)
    return att, sig


def sb_attn_fwd(proj, *, bsz, seq, tq, name):
    n = bsz * seq
    pairs = 4
    nq, qspec, kspec = _attn_specs(bsz, seq, tq, SBQ, SBK, SBV)

    def body(q_ref, k_ref, v_ref, o_ref, lat_ref):
        qi = pl.program_id(2)

        def run(ext, start):
            qv = q_ref[...].astype(MM)
            kv = k_ref[0:ext, :].astype(MM)
            vv = v_ref[0:ext, :].astype(MM)
            mask = _band(qi, tq, ext, 0, 1 << 30, 1, strict=True)
            first = _lane_first()
            outs = []
            for a in range(2):
                att, _ = _sb_weights(jnp.where(first if a == 0 else ~first, qv, jnp.zeros_like(qv)), kv, mask, ext, tq, lat_ref)
                outs.append(_dot(att.astype(MM), vv))
            o_ref[...] = jnp.where(first, outs[0], outs[1]).astype(o_ref.dtype)

        _for_extent(qi, tq, seq, seq, run)

    return pl.pallas_call(
        body, name=name, grid=(bsz, pairs, nq), in_specs=[qspec(SBQ), kspec(SBK), kspec(SBV)], out_specs=qspec(0),
        out_shape=jax.ShapeDtypeStruct((n, LANES * pairs), MM), scratch_shapes=[pltpu.VMEM((tq, seq), F32)],
        compiler_params=_params(("parallel", "parallel", "arbitrary")))(proj, proj, proj)


def sb_attn_bwd(proj, do, *, bsz, seq, tq, name):
    n = bsz * seq
    pairs = 4
    nq, qspec, kspec = _attn_specs(bsz, seq, tq, SBQ, SBK, SBV)

    def body(q_ref, k_ref, v_ref, do_ref, dq_ref, dk_ref, dv_ref, lat_ref, dk_acc, dv_acc):
        qi = pl.program_id(2)

        @pl.when(qi == 0)
        def _():
            dk_acc[...] = jnp.zeros_like(dk_acc)
            dv_acc[...] = jnp.zeros_like(dv_acc)

        def run(ext, start):
            qv = q_ref[...].astype(MM)
            kv = k_ref[0:ext, :].astype(MM)
            vv = v_ref[0:ext, :].astype(MM)
            dob = do_ref[...].astype(MM)
            mask = _band(qi, tq, ext, 0, 1 << 30, 1, strict=True)
            first = _lane_first()
            before = (_iota((LANES, LANES), 0) < _iota((LANES, LANES), 1)).astype(BF16)
            dqs, dks, dvs = [], [], []
            for a in range(2):
                mine = first if a == 0 else ~first
                att, sig = _sb_weights(jnp.where(mine, qv, jnp.zeros_like(qv)), kv, mask, ext, tq, lat_ref)
                g = _dot_nt(jnp.where(mine, dob, jnp.zeros_like(dob)), vv) * att
                carry = jnp.zeros((tq, 1), F32)
                for cb in range(ext // LANES):
                    sl = slice(cb * LANES, (cb + 1) * LANES)
                    blk = g[:, sl]
                    lat_ref[:, sl] = _xdot2(blk, before) + carry
                    carry = carry + jnp.sum(blk, axis=1, keepdims=True)
                dz = jnp.where(mask, g * (1.0 - sig) - sig * lat_ref[:, 0:ext], 0.0).astype(MM)
                dvs.append(_dot_tn(att.astype(MM), dob))
                dks.append(_dot_tn(dz, qv) * SCALE)
                dqs.append(_dot(dz, kv) * SCALE)
            dq_ref[...] = jnp.where(first, dqs[0], dqs[1]).astype(dq_ref.dtype)
            dk_acc[0:ext, :] += jnp.where(first, dks[0], dks[1])
            dv_acc[0:ext, :] += jnp.where(first, dvs[0], dvs[1])

        _for_extent(qi, tq, seq, seq, run)

        @pl.when(qi == nq - 1)
        def _():
            dk_ref[...] = dk_acc[...].astype(dk_ref.dtype)
            dv_ref[...] = dv_acc[...].astype(dv_ref.dtype)

    out = jax.ShapeDtypeStruct((n, LANES * pairs), MM)
    return pl.pallas_call(
        body, name=name, grid=(bsz, pairs, nq), in_specs=[qspec(SBQ), kspec(SBK), kspec(SBV), qspec(0)],
        out_specs=[qspec(0), kspec(0), kspec(0)], out_shape=[out, out, out],
        scratch_shapes=[pltpu.VMEM((tq, seq), F32), pltpu.VMEM((seq, LANES), F32), pltpu.VMEM((seq, LANES), F32)],
        compiler_params=_params(("parallel", "parallel", "arbitrary")))(proj, proj, proj, do)


def _place():
    return lax.axis_index("x"), lax.axis_index("y"), lax.axis_index("c")


def _other_chips(x, y):
    return [(1 - x, y), (x, 1 - y), (1 - x, 1 - y)]


def _remote(src, dst, send_sems, recv_sems, k, to):
    return pltpu.make_async_remote_copy(src_ref=src, dst_ref=dst, send_sem=send_sems.at[k], recv_sem=recv_sems.at[k],
                                        device_id=to, device_id_type=MESH_ID)


def gather_chips(arrs, *, name):
    na = len(arrs)

    def body(*refs):
        ins, outs = refs[:na], refs[na:2 * na]
        send_sems, recv_sems, local_sems = refs[2 * na:]
        x, y, c = _place()
        me, sibling = 2 * x + y, (x, y, 1 - c)
        chips = _other_chips(x, y)
        owns, sends = [], []
        for t in range(na):
            rh = ins[t].shape[0] // 2
            half = lambda chip, h, t=t, rh=rh: outs[t].at[chip, pl.ds(h * rh, rh), :]
            own = pltpu.make_async_copy(ins[t], outs[t].at[me], local_sems.at[t])
            own.start()
            owns.append(own)
            for j, (px, py) in enumerate(chips):
                cp = _remote(ins[t].at[pl.ds(c * rh, rh), :], half(me, c), send_sems, recv_sems, 6 * t + j, (px, py, c))
                cp.start()
                sends.append(cp)
        for t in range(na):
            rh = ins[t].shape[0] // 2
            half = lambda chip, h, t=t, rh=rh: outs[t].at[chip, pl.ds(h * rh, rh), :]
            for j, (px, py) in enumerate(chips):
                landed = half(2 * px + py, c)
                _remote(landed, landed, send_sems, recv_sems, 6 * t + j, (px, py, c)).wait_recv()
                fw = _remote(landed, landed, send_sems, recv_sems, 6 * t + 3 + j, sibling)
                fw.start()
                sends.append(fw)
        for t in range(na):
            rh = ins[t].shape[0] // 2
            half = lambda chip, h, t=t, rh=rh: outs[t].at[chip, pl.ds(h * rh, rh), :]
            for j, (px, py) in enumerate(chips):
                passed = half(2 * px + py, 1 - c)
                _remote(passed, passed, send_sems, recv_sems, 6 * t + 3 + j, sibling).wait_recv()
        for cp in sends:
            cp.wait_send()
        for cp in owns:
            cp.wait()

    for a in arrs:
        assert a.ndim == 2 and a.shape[0] % 32 == 0, a.shape
    return pl.pallas_call(
        body, name=name, in_specs=[ANY] * na, out_specs=[ANY] * na,
        out_shape=[jax.ShapeDtypeStruct((4,) + a.shape, a.dtype) for a in arrs],
        scratch_shapes=[pltpu.SemaphoreType.DMA((6 * na,)), pltpu.SemaphoreType.DMA((6 * na,)), pltpu.SemaphoreType.DMA((na,))],
    )(*arrs)


def swap_halves(arrs, *, name):
    na = len(arrs)

    def body(*refs):
        ins, outs = refs[:na], refs[na:3 * na]
        send_sems, recv_sems, local_sems = refs[3 * na:]
        x, y, c = _place()
        sibling = (x, y, 1 - c)
        started = []
        for t in range(na):
            rh = ins[t].shape[1] // 2
            mine, theirs = outs[2 * t], outs[2 * t + 1]
            for k in range(4):
                own = pltpu.make_async_copy(ins[t].at[k, pl.ds(c * rh, rh), :], mine.at[k], local_sems.at[4 * t + k])
                own.start()
                cp = _remote(ins[t].at[k, pl.ds((1 - c) * rh, rh), :], theirs.at[k], send_sems, recv_sems, 4 * t + k, sibling)
                cp.start()
                started.append((own, cp))
        for t in range(na):
            theirs = outs[2 * t + 1]
            for k in range(4):
                _remote(theirs.at[k], theirs.at[k], send_sems, recv_sems, 4 * t + k, sibling).wait_recv()
        for own, cp in started:
            cp.wait_send()
            own.wait()

    shapes = []
    for a in arrs:
        assert a.shape[1] % 16 == 0, a.shape
        s = jax.ShapeDtypeStruct((4, a.shape[1] // 2, a.shape[2]), a.dtype)
        shapes += [s, s]
    return pl.pallas_call(
        body, name=name, in_specs=[ANY] * na, out_specs=[ANY] * (2 * na), out_shape=shapes,
        scratch_shapes=[pltpu.SemaphoreType.DMA((4 * na,)), pltpu.SemaphoreType.DMA((4 * na,)), pltpu.SemaphoreType.DMA((4 * na,))],
    )(*arrs)


def scatter_chips(arrs, *, name):
    na = len(arrs)

    def body(*refs):
        ins, outs = refs[:na], refs[na:2 * na]
        send_sems, recv_sems, local_sems = refs[2 * na:]
        x, y, c = _place()
        me = 2 * x + y
        chips = _other_chips(x, y)
        owns, sends = [], []
        for t in range(na):
            own = pltpu.make_async_copy(ins[t].at[me], outs[t].at[me], local_sems.at[t])
            own.start()
            owns.append(own)
            for j, (px, py) in enumerate(chips):
                cp = _remote(ins[t].at[2 * px + py], outs[t].at[me], send_sems, recv_sems, 3 * t + j, (px, py, c))
                cp.start()
                sends.append(cp)
        for t in range(na):
            for j, (px, py) in enumerate(chips):
                slab = outs[t].at[2 * px + py]
                _remote(slab, slab, send_sems, recv_sems, 3 * t + j, (px, py, c)).wait_recv()
        for cp in sends:
            cp.wait_send()
        for cp in owns:
            cp.wait()

    return pl.pallas_call(
        body, name=name, in_specs=[ANY] * na, out_specs=[ANY] * na,
        out_shape=[jax.ShapeDtypeStruct(a.shape, a.dtype) for a in arrs],
        scratch_shapes=[pltpu.SemaphoreType.DMA((3 * na,)), pltpu.SemaphoreType.DMA((3 * na,)), pltpu.SemaphoreType.DMA((na,))],
    )(*arrs)


def join_halves(arrs, *, name):
    na = len(arrs)

    def body(*refs):
        ins, outs = refs[:na], refs[na:2 * na]
        send_sems, recv_sems, local_sems = refs[2 * na:]
        x, y, c = _place()
        sibling = (x, y, 1 - c)
        started = []
        for t in range(na):
            own = pltpu.make_async_copy(ins[t], outs[t].at[c], local_sems.at[t])
            own.start()
            cp = _remote(ins[t], outs[t].at[c], send_sems, recv_sems, t, sibling)
            cp.start()
            started.append((own, cp))
        for t in range(na):
            theirs = outs[t].at[1 - c]
            _remote(theirs, theirs, send_sems, recv_sems, t, sibling).wait_recv()
        for own, cp in started:
            cp.wait_send()
            own.wait()

    return pl.pallas_call(
        body, name=name, in_specs=[ANY] * na, out_specs=[ANY] * na,
        out_shape=[jax.ShapeDtypeStruct((2,) + a.shape, a.dtype) for a in arrs],
        scratch_shapes=[pltpu.SemaphoreType.DMA((na,)), pltpu.SemaphoreType.DMA((na,)), pltpu.SemaphoreType.DMA((na,))],
    )(*arrs)


def add_slabs(a, b=None, *, name):
    _, r, c = a.shape
    tr = r
    while tr * c * 4 > (1 << 20) and tr % 16 == 0:
        tr //= 2

    if b is not None:
        def body(a_ref, b_ref, o_ref):
            o_ref[...] = a_ref[...] + b_ref[...]

        blk = pl.BlockSpec((1, tr, c), lambda k, i: (k, i, 0))
        return pl.pallas_call(body, name=name, grid=(4, r // tr), in_specs=[blk, blk], out_specs=blk,
                              out_shape=jax.ShapeDtypeStruct(a.shape, a.dtype), compiler_params=_params(("parallel", "parallel")))(a, b)

    def body(a_ref, o_ref):
        o_ref[...] = ((a_ref[0] + a_ref[1]) + a_ref[2]) + a_ref[3]

    return pl.pallas_call(body, name=name, grid=(r // tr,), in_specs=[pl.BlockSpec((4, tr, c), lambda i: (0, i, 0))],
                          out_specs=pl.BlockSpec((tr, c), lambda i: (i, 0)), out_shape=jax.ShapeDtypeStruct((r, c), a.dtype),
                          compiler_params=_params(("parallel",)))(a)


def all_reduce_small(a, *, name):
    def body(a_ref, o_ref, buf, send_sems, recv_sems):
        x, y, c = _place()
        me = 4 * x + 2 * y + c
        buf[me] = a_ref[...]
        sent = []
        for p in range(1, 8):
            px, py, pc = (p >> 2) & 1, (p >> 1) & 1, p & 1
            cp = _remote(a_ref, buf.at[me], send_sems, recv_sems, p - 1, (x ^ px, y ^ py, c ^ pc))
            cp.start()
            sent.append(cp)
        for p in range(1, 8):
            px, py, pc = (p >> 2) & 1, (p >> 1) & 1, p & 1
            src = 4 * (x ^ px) + 2 * (y ^ py) + (c ^ pc)
            _remote(a_ref, buf.at[src], send_sems, recv_sems, p - 1, (x ^ px, y ^ py, c ^ pc)).wait_recv()
        for cp in sent:
            cp.wait_send()
        acc = buf[0]
        for d in range(1, 8):
            acc = acc + buf[d]
        o_ref[...] = acc

    vm = pl.BlockSpec(memory_space=pltpu.VMEM)
    return pl.pallas_call(
        body, name=name, in_specs=[vm], out_specs=vm, out_shape=jax.ShapeDtypeStruct(a.shape, a.dtype),
        scratch_shapes=[pltpu.VMEM((8,) + a.shape, a.dtype), pltpu.SemaphoreType.DMA((7,)), pltpu.SemaphoreType.DMA((7,))],
    )(a)


TQ = 256


def _pad_w_in(w):
    pad = jnp.zeros((w.shape[0], DPROJ - DIN), w.dtype)
    return jnp.concatenate([w[:, :O1], w[:, O2:], w[:, O1:O2], pad], axis=1)


def _unpad_w_in(g):
    nf = O2 - O1
    return jnp.concatenate([g[:, :O1], g[:, DIN - nf:DIN], g[:, O1:DIN - nf]], axis=1)


def _layer_small(sm, l):
    row = lambda v: v.reshape(1, -1)
    return dict(
        attn_norm=row(sm["attn_norm"][l]), mlp_norm=row(sm["mlp_norm"][l]),
        qgf=row(jnp.tile(sm["q_norm_fox"][l], 8)), kgf=row(jnp.tile(sm["k_norm_fox"][l], 8)),
        qgd=row(jnp.tile(sm["q_norm_dil"][l], 12)), kgd=row(jnp.tile(sm["k_norm_dil"][l], 12)),
        bfor=row(jnp.pad(sm["b_forget"][l], (0, LANES - 8))))


def _key_rows(f8, bsz, seq):
    f = f8.reshape(bsz, seq, LANES)[:, :, :8].transpose(0, 2, 1).reshape(bsz, 4, 2, seq)
    return jnp.pad(f, ((0, 0), (0, 0), (0, 6), (0, 0))).reshape(bsz * 32, seq)


def _layer_fwd(x, w, s, cos, sin, bsz, seq, l):
    nm = lambda t: f"l{l}_{t}"
    h = rmsnorm_fwd(x, s["attn_norm"], name=nm("attn_norm"))
    proj = matmul(h, w["win"], name=nm("proj"))
    qn, kn, fb, f8 = fox_prep_fwd(proj, s["qgf"], s["kgf"], s["bfor"], bsz=bsz, seq=seq, name=nm("fox_prep"))
    fk = _key_rows(f8, bsz, seq)
    oa, la = softmax_attn_fwd(qn, kn, proj, (fb, fk), qo=0, ko=0, vo=FOXV, pairs=4, bsz=bsz, seq=seq, window=seq, dilation=1,
                              tq=TQ, name=nm("fox_attn"))
    ob = sb_attn_fwd(proj, bsz=bsz, seq=seq, tq=TQ, name=nm("sb_attn"))
    qr, kr = dil_prep_fwd(proj, s["qgd"], s["kgd"], cos, sin, name=nm("dil_prep"))
    ogs, lgs = [], []
    for g, (window, dilation) in enumerate(DIL_PATTERNS):
        og, lg = softmax_attn_fwd(qr, kr, proj, None, qo=2 * g, ko=2 * g, vo=DILV + 2 * g, pairs=2, bsz=bsz, seq=seq,
                                  window=window, dilation=dilation, tq=TQ, name=nm(f"dil_attn{g}"))
        ogs.append(og)
        lgs.append(lg)
    oc = dil_combine_fwd(ogs, lgs, name=nm("dil_combine"))
    ys = [matmul(oa, w["wuf"], name=nm("up_fox")), matmul(ob, w["wus"], name=nm("up_sb")), matmul(oc, w["wud"], name=nm("up_dil"))]
    merged = merge_fwd(proj, ys, name=nm("merge"))
    x1 = matmul(merged, w["wo"], add=x, name=nm("out_proj"))
    h2 = rmsnorm_fwd(x1, s["mlp_norm"], name=nm("mlp_norm"))
    u = matmul(h2, w["wmi"], name=nm("mlp_in"))
    act = relu2_fwd(u, name=nm("relu2"))
    x2 = matmul(act, w["wmo"], add=x1, name=nm("mlp_out"))
    saved = dict(x=x, h=h, proj=proj, qn=qn, kn=kn, fb=fb, fk=fk, oa=oa, la=la, ob=ob, qr=qr, kr=kr, ogs=ogs, lgs=lgs, oc=oc,
                 ys=ys, merged=merged, x1=x1, h2=h2, u=u, act=act)
    return x2, saved


def _layer_bwd(dx2, w, s, sv, cos, sin, bsz, seq, l):
    nm = lambda t: f"l{l}_{t}_bwd"
    n = bsz * seq
    proj = sv["proj"]
    gw = {}
    da = matmul(dx2, w["wmo"], tb=True, name=nm("mlp_out_dx"))
    gw["wmo"] = matmul(sv["act"], dx2, ta=True, name=nm("mlp_out_dw"))
    du = relu2_bwd(sv["u"], da, name=nm("relu2"))
    dh2 = matmul(du, w["wmi"], tb=True, name=nm("mlp_in_dx"))
    gw["wmi"] = matmul(sv["h2"], du, ta=True, name=nm("mlp_in_dw"))
    dx1, g_mlp_norm = rmsnorm_bwd(sv["x1"], s["mlp_norm"], dh2, dx2, name=nm("mlp_norm"))

    dmerged = matmul(dx1, w["wo"], tb=True, name=nm("out_proj_dx"))
    gw["wo"] = matmul(sv["merged"], dx1, ta=True, name=nm("out_proj_dw"))
    dya, dyb, dyc, dga, dgb, dgc = merge_bwd(proj, sv["ys"], dmerged, name=nm("merge"))
    doa = matmul(dya, w["wuf"], tb=True, name=nm("up_fox_dx"))
    gw["wuf"] = matmul(sv["oa"], dya, ta=True, name=nm("up_fox_dw"))
    dob = matmul(dyb, w["wus"], tb=True, name=nm("up_sb_dx"))
    gw["wus"] = matmul(sv["ob"], dyb, ta=True, name=nm("up_sb_dw"))
    doc = matmul(dyc, w["wud"], tb=True, name=nm("up_dil_dx"))
    gw["wud"] = matmul(sv["oc"], dyc, ta=True, name=nm("up_dil_dw"))

    outs = dil_combine_bwd(sv["ogs"], sv["lgs"], doc, name=nm("dil_combine"))
    dqs, dks, dvs = [], [], []
    for g, (window, dilation) in enumerate(DIL_PATTERNS):
        dq, dk, dv = softmax_attn_bwd(sv["qr"], sv["kr"], proj, sv["ogs"][g], outs[g], sv["lgs"][g], outs[3 + g], None,
                                      qo=2 * g, ko=2 * g, vo=DILV + 2 * g, pairs=2, bsz=bsz, seq=seq, window=window,
                                      dilation=dilation, tq=TQ, dq_dtype=F32, dk_dtype=F32, name=nm(f"dil_attn{g}"))
        dqs.append(dq)
        dks.append(dk)
        dvs.append(dv)
    d_dq, d_dk, g_qgd, g_kgd = dil_prep_bwd(proj, s["qgd"], s["kgd"], cos, sin, jnp.concatenate(dqs, axis=1),
                                            jnp.concatenate(dks, axis=1), name=nm("dil_prep"))

    s_dq, s_dk, s_dv = sb_attn_bwd(proj, dob, bsz=bsz, seq=seq, tq=TQ, name=nm("sb_attn"))

    dqn, dkn, f_dv, dfq, dfk = softmax_attn_bwd(sv["qn"], sv["kn"], proj, sv["oa"], doa, sv["la"], None, (sv["fb"], sv["fk"]),
                                                qo=0, ko=0, vo=FOXV, pairs=4, bsz=bsz, seq=seq, window=seq, dilation=1, tq=TQ,
                                                dq_dtype=F32, dk_dtype=F32, name=nm("fox_attn"))
    dfk8 = dfk.reshape(bsz, 4, 8, seq)[:, :, :2].reshape(bsz, 8, seq).transpose(0, 2, 1).reshape(n, 8)
    df = jnp.pad(dfq[:, ::HEAD] - dfk8, ((0, 0), (0, LANES - 8)))
    f_dq, f_dk, d_forget, g_qgf, g_kgf, g_bfor = fox_prep_bwd(proj, s["qgf"], s["kgf"], s["bfor"], dqn, dkn, df, bsz=bsz, seq=seq,
                                                              name=nm("fox_prep"))

    dproj = jnp.concatenate([f_dq, f_dk, f_dv, s_dq, s_dk, s_dv, d_dq, d_dk] + dvs + [dga, dgb, dgc, d_forget], axis=1)
    dh = matmul(dproj, w["win"], tb=True, name=nm("proj_dx"))
    gw["win"] = matmul(sv["h"], dproj, ta=True, name=nm("proj_dw"))
    dx, g_attn_norm = rmsnorm_bwd(sv["x"], s["attn_norm"], dh, dx1, name=nm("attn_norm"))
    gs = dict(attn_norm=g_attn_norm[0], mlp_norm=g_mlp_norm[0], b_forget=g_bfor[0, :8],
              q_norm_fox=g_qgf.reshape(8, HEAD).sum(0), k_norm_fox=g_kgf.reshape(8, HEAD).sum(0),
              q_norm_dil=g_qgd.reshape(12, HEAD).sum(0), k_norm_dil=g_kgd.reshape(12, HEAD).sum(0))
    return dx, gw, gs


def local_step(x, positions, target, weights, small):
    bsz, seq, d = x.shape
    n = bsz * seq
    depth = len(weights)
    inv = 1.0 / (ROPE_THETA ** (jnp.arange(HEAD // 2, dtype=F32) / (HEAD // 2)))
    cos, sin = rope_table(positions.reshape(n, 1), jnp.tile(inv, 4).reshape(1, LANES), name="rope_table")
    xs = x.reshape(n, d)
    saved = []
    for l in range(depth):
        xs, sv = _layer_fwd(xs, weights[l], _layer_small(small, l), cos, sin, bsz, seq, l)
        saved.append(sv)
    dy, sq = loss_grad(xs, target.reshape(n, d), name="loss")
    loss = (0.5 / d) * jnp.sum(sq)
    gws, gss = [None] * depth, [None] * depth
    for l in reversed(range(depth)):
        dy, gws[l], gss[l] = _layer_bwd(dy, weights[l], _layer_small(small, l), saved[l], cos, sin, bsz, seq, l)
    return loss, dy.reshape(bsz, seq, d), gws, gss


SMALL = ("attn_norm", "mlp_norm", "b_forget", "q_norm_fox", "k_norm_fox", "q_norm_dil", "k_norm_dil")
SMALL_ROWS = 8


def _pack_small(vals):
    flat = jnp.concatenate([vals[k].reshape(-1) for k in SMALL])
    return jnp.pad(flat, (0, SMALL_ROWS * 1024 - flat.shape[0])).reshape(SMALL_ROWS, 1024)


def _unpack_small(packed, like):
    flat, out, at = packed.reshape(-1), {}, 0
    for k in SMALL:
        size = like[k].size
        out[k] = flat[at:at + size].reshape(like[k].shape)
        at += size
    return out


def _cols_to_chips(g):
    depth, rows, cols = g.shape
    return g.reshape(depth, rows, 4, cols // 4).transpose(2, 0, 1, 3).reshape(4, depth * rows, cols // 4)


def _rows_to_chips(g):
    depth, rows, cols = g.shape
    return g.reshape(depth, 4, rows // 4, cols).transpose(1, 0, 2, 3).reshape(4, depth * rows // 4, cols)


def _chips_to_cols(a, depth):
    _, rows, c = a.shape
    return a.reshape(4, depth, rows // depth, c).transpose(1, 2, 0, 3).reshape(depth, rows // depth, 4 * c)


def _chips_to_rows(a, depth):
    _, rows, c = a.shape
    return a.reshape(4, depth, rows // depth, c).transpose(1, 0, 2, 3).reshape(depth, 4 * rows // depth, c)


def kernel(x, positions, attn_norm, w_in, b_forget, q_norm_fox, k_norm_fox, q_norm_dil, k_norm_dil, w_up_fox, w_up_sb, w_up_dil, w_out, mlp_norm, w_mlp_in, w_mlp_out, loss_target, m_attn_norm, m_w_in, m_b_forget, m_q_norm_fox, m_k_norm_fox, m_q_norm_dil, m_k_norm_dil, m_w_up_fox, m_w_up_sb, m_w_up_dil, m_w_out, m_mlp_norm, m_w_mlp_in, m_w_mlp_out, v_attn_norm, v_w_in, v_b_forget, v_q_norm_fox, v_k_norm_fox, v_q_norm_dil, v_k_norm_dil, v_w_up_fox, v_w_up_sb, v_w_up_dil, v_w_out, v_mlp_norm, v_w_mlp_in, v_w_mlp_out):
    names = ("attn_norm", "w_in", "b_forget", "q_norm_fox", "k_norm_fox", "q_norm_dil", "k_norm_dil", "w_up_fox", "w_up_sb",
             "w_up_dil", "w_out", "mlp_norm", "w_mlp_in", "w_mlp_out")
    wv = dict(zip(names, (attn_norm, w_in, b_forget, q_norm_fox, k_norm_fox, q_norm_dil, k_norm_dil, w_up_fox, w_up_sb, w_up_dil,
                          w_out, mlp_norm, w_mlp_in, w_mlp_out)))
    mv = dict(zip(names, (m_attn_norm, m_w_in, m_b_forget, m_q_norm_fox, m_k_norm_fox, m_q_norm_dil, m_k_norm_dil, m_w_up_fox,
                          m_w_up_sb, m_w_up_dil, m_w_out, m_mlp_norm, m_w_mlp_in, m_w_mlp_out)))
    vv = dict(zip(names, (v_attn_norm, v_w_in, v_b_forget, v_q_norm_fox, v_k_norm_fox, v_q_norm_dil, v_k_norm_dil, v_w_up_fox,
                          v_w_up_sb, v_w_up_dil, v_w_out, v_mlp_norm, v_w_mlp_in, v_w_mlp_out)))
    depth = w_in.shape[0]
    flat2 = lambda a: a.reshape(-1, a.shape[-1])

    ups = ("w_up_fox", "w_up_sb", "w_up_dil")
    wide = ("w_out", "w_mlp_in", "w_mlp_out")
    send = [flat2(w_in).astype(MM), jnp.concatenate([flat2(wv[k]) for k in ups]).astype(MM),
            jnp.concatenate([flat2(wv[k]) for k in wide]).astype(MM)]
    got_in, got_up, got_wide = gather_chips(send, name="gather_weights")

    def split(a, keys):
        out, at = {}, 0
        for k in keys:
            rows = wv[k].shape[0] * wv[k].shape[1]
            out[k] = a[:, at:at + rows]
            at += rows
        return out

    full = {"w_in": _chips_to_cols(got_in, depth)}
    full.update({k: _chips_to_cols(a, depth) for k, a in split(got_up, ups).items()})
    parts = split(got_wide, wide)
    full["w_out"] = _chips_to_rows(parts["w_out"], depth)
    full["w_mlp_in"] = _chips_to_cols(parts["w_mlp_in"], depth)
    full["w_mlp_out"] = _chips_to_rows(parts["w_mlp_out"], depth)
    weights = [dict(win=_pad_w_in(full["w_in"][l]), wuf=full["w_up_fox"][l], wus=full["w_up_sb"][l], wud=full["w_up_dil"][l],
                    wo=full["w_out"][l], wmi=full["w_mlp_in"][l], wmo=full["w_mlp_out"][l]) for l in range(depth)]
    small = {k: wv[k] for k in SMALL}

    loss, grad_x, gws, gss = local_step(x, positions, loss_target, weights, small)
    loss = lax.psum(loss, ("x", "y", "c"))

    g_small = {k: jnp.stack([gss[l][k] for l in range(depth)]) for k in SMALL}
    g_small = _unpack_small(all_reduce_small(_pack_small(g_small), name="reduce_small"), small)

    stack = lambda key: jnp.stack([gws[l][key] for l in range(depth)])
    g_full = {"w_in": jnp.stack([_unpad_w_in(gws[l]["win"]) for l in range(depth)]), "w_up_fox": stack("wuf"), "w_up_sb": stack("wus"),
              "w_up_dil": stack("wud"), "w_out": stack("wo"), "w_mlp_in": stack("wmi"), "w_mlp_out": stack("wmo")}
    parts = [_cols_to_chips(g_full["w_in"]), jnp.concatenate([_cols_to_chips(g_full[k]) for k in ups], axis=1),
             jnp.concatenate([_rows_to_chips(g_full["w_out"]), _cols_to_chips(g_full["w_mlp_in"]), _rows_to_chips(g_full["w_mlp_out"])], axis=1)]
    swapped = swap_halves(parts, name="reduce_swap_halves")
    pair_sums = [add_slabs(swapped[2 * t], swapped[2 * t + 1], name=f"reduce_pair_sum{t}") for t in range(3)]
    landed = scatter_chips(pair_sums, name="reduce_scatter_chips")
    reduced = [add_slabs(a, name=f"reduce_chip_sum{t}") for t, a in enumerate(landed)]
    joined = [a.reshape(-1, a.shape[-1]) for a in join_halves(reduced, name="reduce_join_halves")]
    g_big = {"w_in": joined[0].reshape(w_in.shape)}
    for a, keys in ((joined[1], ups), (joined[2], wide)):
        at = 0
        for k in keys:
            rows = wv[k].shape[0] * wv[k].shape[1]
            g_big[k] = a[at:at + rows].reshape(wv[k].shape)
            at += rows

    grads = {**g_small, **g_big}
    delta, new_m, new_v = {}, {}, {}
    d_s, m_s, v_s = adamw(_pack_small(small), _pack_small(g_small), _pack_small({k: mv[k] for k in SMALL}),
                          _pack_small({k: vv[k] for k in SMALL}), name="adamw_small")
    delta.update(_unpack_small(d_s, small))
    new_m.update(_unpack_small(m_s, small))
    new_v.update(_unpack_small(v_s, small))
    for k in ("w_in",) + ups + wide:
        d_k, m_k, v_k = adamw(flat2(wv[k]), flat2(g_big[k]), flat2(mv[k]), flat2(vv[k]), name=f"adamw_{k}")
        delta[k], new_m[k], new_v[k] = d_k.reshape(wv[k].shape), m_k.reshape(wv[k].shape), v_k.reshape(wv[k].shape)

    return (loss, grad_x, *[grads[k] for k in names], *[delta[k] for k in names], *[new_m[k] for k in names], *[new_v[k] for k in names])
```

```python
import functools

import jax
import jax.numpy as jnp
from jax import lax
from jax.experimental import pallas as pl
from jax.experimental.pallas import tpu as pltpu

F32 = jnp.float32
BF16 = jnp.bfloat16
MM = jnp.bfloat16

HEAD = 64
LANES = 128
EPS = 1e-6
SCALE = 0.125
ROPE_THETA = 10000.0
DIL_PATTERNS = ((128, 1), (512, 4), (2048, 16))
ADAM_LR, ADAM_B1, ADAM_B2, ADAM_EPS, ADAM_WD, ADAM_STEP = 0.001, 0.9, 0.999, 1e-08, 0.01, 10

FOXQ, FOXK, FOXV = 0, 4, 8
SBQ, SBK, SBV = 12, 16, 20
DILQ, DILK, DILV = 24, 30, 36
GATE, FORGET, NBLK = 42, 66, 68
DPROJ = NBLK * LANES
O1, O2, O3, O4, DIN = 1536, 1544, 3080, 5384, 8456

VMEM_LIMIT = 56 * 1024 * 1024
MESH_ID = pl.DeviceIdType.MESH
ANY = pl.BlockSpec(memory_space=pl.ANY)


def _params(sem=None):
    return pltpu.CompilerParams(dimension_semantics=sem, vmem_limit_bytes=VMEM_LIMIT)


def _iota(shape, dim):
    return lax.broadcasted_iota(jnp.int32, shape, dim)


def _split2(x):
    hi = x.astype(BF16)
    lo = (x - hi.astype(F32)).astype(BF16)
    return hi, lo


def _split3(x):
    hi = x.astype(BF16)
    r = x - hi.astype(F32)
    mid = r.astype(BF16)
    lo = (r - mid.astype(F32)).astype(BF16)
    return hi, mid, lo


def _dot(a, b):
    return jnp.dot(a, b, preferred_element_type=F32)


def _dot_nt(a, b):
    return lax.dot_general(a, b, (((1,), (1,)), ((), ())), preferred_element_type=F32)


def _dot_tn(a, b):
    return lax.dot_general(a, b, (((0,), (0,)), ((), ())), preferred_element_type=F32)


def _xdot2(x, m):
    hi, lo = _split2(x)
    return _dot(hi, m) + _dot(lo, m)


def _xdot3(x, m):
    hi, mid, lo = _split3(x)
    return _dot(hi, m) + _dot(mid, m) + _dot(lo, m)


def _xdot3_left(m, x):
    hi, mid, lo = _split3(x)
    return _dot(m, hi) + _dot(m, mid) + _dot(m, lo)


def _head_mat(w):
    return ((_iota((w, w), 0) >> 6) == (_iota((w, w), 1) >> 6)).astype(BF16)


def _softplus_parts(z):
    e = jnp.exp(-jnp.abs(z))
    return e, jnp.maximum(z, 0.0) + jnp.log(1.0 + e)


def _fit(dim, want):
    t = min(want, dim)
    while dim % t:
        t -= LANES
        assert t > 0, (dim, want)
    return t


def matmul(a, b, *, ta=False, tb=False, out_dtype=F32, add=None, tm=2048, tn=512, tk=1024, name):
    K, M = a.shape if ta else a.shape[::-1]
    K2, N = b.shape[::-1] if tb else b.shape
    assert K == K2, (a.shape, b.shape, ta, tb)
    tm, tn, tk = _fit(M, tm), _fit(N, tn), _fit(K, tk)
    nk = K // tk
    dn = (((0 if ta else 1,), (1 if tb else 0,)), ((), ()))

    def body(*refs):
        if add is None:
            a_ref, b_ref, o_ref, acc_ref = refs
        else:
            a_ref, b_ref, add_ref, o_ref, acc_ref = refs
        k = pl.program_id(2)
        part = lax.dot_general(a_ref[...].astype(MM), b_ref[...].astype(MM), dn, preferred_element_type=F32)

        @pl.when(k == 0)
        def _():
            acc_ref[...] = part

        @pl.when(k > 0)
        def _():
            acc_ref[...] += part

        @pl.when(k == nk - 1)
        def _():
            r = acc_ref[...]
            if add is not None:
                r = r + add_ref[...]
            o_ref[...] = r.astype(out_dtype)

    a_spec = pl.BlockSpec((tk, tm), lambda i, j, k: (k, i)) if ta else pl.BlockSpec((tm, tk), lambda i, j, k: (i, k))
    b_spec = pl.BlockSpec((tn, tk), lambda i, j, k: (j, k)) if tb else pl.BlockSpec((tk, tn), lambda i, j, k: (k, j))
    o_spec = pl.BlockSpec((tm, tn), lambda i, j, k: (i, j))
    ins, specs = [a, b], [a_spec, b_spec]
    if add is not None:
        ins.append(add)
        specs.append(o_spec)
    return pl.pallas_call(
        body, name=name, grid=(M // tm, N // tn, nk), in_specs=specs, out_specs=o_spec,
        out_shape=jax.ShapeDtypeStruct((M, N), out_dtype), scratch_shapes=[pltpu.VMEM((tm, tn), F32)],
        compiler_params=_params(("parallel", "parallel", "arbitrary")),
    )(*ins)


def _rows(n, want=512):
    t = min(want, n)
    assert n % t == 0, (n, t)
    return t


def rmsnorm_fwd(x, g, *, name):
    n, d = x.shape
    tr = _rows(n)

    def body(x_ref, g_ref, o_ref):
        xv = x_ref[...]
        r = lax.rsqrt(jnp.mean(xv * xv, axis=1, keepdims=True) + EPS)
        o_ref[...] = (xv * r * g_ref[...]).astype(o_ref.dtype)

    row = pl.BlockSpec((tr, d), lambda i: (i, 0))
    vec = pl.BlockSpec((1, d), lambda i: (0, 0))
    return pl.pallas_call(body, name=name, grid=(n // tr,), in_specs=[row, vec], out_specs=row,
                          out_shape=jax.ShapeDtypeStruct((n, d), MM), compiler_params=_params(("parallel",)))(x, g)


def rmsnorm_bwd(x, g, dh, dres, *, name):
    n, d = x.shape
    tr = _rows(n)

    def body(x_ref, g_ref, dh_ref, dr_ref, dx_ref, dg_ref):
        @pl.when(pl.program_id(0) == 0)
        def _():
            dg_ref[...] = jnp.zeros_like(dg_ref)

        xv = x_ref[...]
        r = lax.rsqrt(jnp.mean(xv * xv, axis=1, keepdims=True) + EPS)
        y = xv * r
        dhv = dh_ref[...]
        dy = dhv * g_ref[...]
        dx_ref[...] = dr_ref[...] + r * (dy - y * jnp.mean(dy * y, axis=1, keepdims=True))
        dg_ref[...] += jnp.sum(dhv * y, axis=0, keepdims=True)

    row = pl.BlockSpec((tr, d), lambda i: (i, 0))
    vec = pl.BlockSpec((1, d), lambda i: (0, 0))
    return pl.pallas_call(
        body, name=name, grid=(n // tr,), in_specs=[row, vec, row, row], out_specs=[row, vec],
        out_shape=[jax.ShapeDtypeStruct((n, d), F32), jax.ShapeDtypeStruct((1, d), F32)],
        compiler_params=_params(("arbitrary",)))(x, g, dh, dres)


def loss_grad(y, tgt, *, name):
    n, d = y.shape
    tr = _rows(n)

    def body(y_ref, t_ref, dy_ref, acc_ref):
        @pl.when(pl.program_id(0) == 0)
        def _():
            acc_ref[...] = jnp.zeros_like(acc_ref)

        e = y_ref[...] - t_ref[...]
        dy_ref[...] = e * (1.0 / d)
        acc_ref[...] += jnp.sum(e * e, axis=0, keepdims=True)

    row = pl.BlockSpec((tr, d), lambda i: (i, 0))
    vec = pl.BlockSpec((1, d), lambda i: (0, 0))
    return pl.pallas_call(
        body, name=name, grid=(n // tr,), in_specs=[row, row], out_specs=[row, vec],
        out_shape=[jax.ShapeDtypeStruct((n, d), F32), jax.ShapeDtypeStruct((1, d), F32)],
        compiler_params=_params(("arbitrary",)))(y, tgt)


def relu2_fwd(u, *, name):
    n, d = u.shape
    tr = _rows(n)

    def body(u_ref, o_ref):
        r = jnp.maximum(u_ref[...], 0.0)
        o_ref[...] = (r * r).astype(o_ref.dtype)

    row = pl.BlockSpec((tr, d), lambda i: (i, 0))
    return pl.pallas_call(body, name=name, grid=(n // tr,), in_specs=[row], out_specs=row,
                          out_shape=jax.ShapeDtypeStruct((n, d), MM), compiler_params=_params(("parallel",)))(u)


def relu2_bwd(u, da, *, name):
    n, d = u.shape
    tr = _rows(n)

    def body(u_ref, da_ref, o_ref):
        o_ref[...] = (da_ref[...] * (2.0 * jnp.maximum(u_ref[...], 0.0))).astype(o_ref.dtype)

    row = pl.BlockSpec((tr, d), lambda i: (i, 0))
    return pl.pallas_call(body, name=name, grid=(n // tr,), in_specs=[row, row], out_specs=row,
                          out_shape=jax.ShapeDtypeStruct((n, d), MM), compiler_params=_params(("parallel",)))(u, da)


MERGE_W = 256


def _gate_specs(tr, d):
    per = d // MERGE_W
    base = GATE * LANES // MERGE_W
    return [pl.BlockSpec((tr, MERGE_W), functools.partial(lambda i, j, b: (i, base + per * b + j), b=b)) for b in range(3)]


def merge_fwd(proj, ys, *, name):
    n, d = ys[0].shape
    tr = _rows(n)

    def body(g0, g1, g2, y0, y1, y2, o_ref):
        acc = jax.nn.sigmoid(g0[...]) * y0[...]
        acc += jax.nn.sigmoid(g1[...]) * y1[...]
        acc += jax.nn.sigmoid(g2[...]) * y2[...]
        o_ref[...] = acc.astype(o_ref.dtype)

    blk = pl.BlockSpec((tr, MERGE_W), lambda i, j: (i, j))
    return pl.pallas_call(
        body, name=name, grid=(n // tr, d // MERGE_W), in_specs=_gate_specs(tr, d) + [blk] * 3, out_specs=blk,
        out_shape=jax.ShapeDtypeStruct((n, d), MM), compiler_params=_params(("parallel", "parallel")))(proj, proj, proj, *ys)


def merge_bwd(proj, ys, dm, *, name):
    n, d = dm.shape
    tr = _rows(n)

    def body(g0, g1, g2, y0, y1, y2, dm_ref, dy0, dy1, dy2, dg0, dg1, dg2):
        dmv = dm_ref[...]
        for g, y, dy, dg in ((g0, y0, dy0, dg0), (g1, y1, dy1, dg1), (g2, y2, dy2, dg2)):
            s = jax.nn.sigmoid(g[...])
            dy[...] = (dmv * s).astype(dy.dtype)
            dg[...] = (dmv * y[...] * s * (1.0 - s)).astype(dg.dtype)

    blk = pl.BlockSpec((tr, MERGE_W), lambda i, j: (i, j))
    out = jax.ShapeDtypeStruct((n, d), MM)
    return pl.pallas_call(
        body, name=name, grid=(n // tr, d // MERGE_W), in_specs=_gate_specs(tr, d) + [blk] * 4, out_specs=[blk] * 6,
        out_shape=[out] * 6, compiler_params=_params(("parallel", "parallel")))(proj, proj, proj, *ys, dm)


def adamw(w, g, m, v, *, name):
    r, c = w.shape
    tr = r
    while tr * c * 4 > (1 << 21) and tr % 16 == 0:
        tr //= 2
    c1 = 1.0 / (1.0 - ADAM_B1 ** ADAM_STEP)
    c2 = 1.0 / (1.0 - ADAM_B2 ** ADAM_STEP)

    def body(w_ref, g_ref, m_ref, v_ref, d_ref, mo_ref, vo_ref):
        gv = g_ref[...]
        m2 = ADAM_B1 * m_ref[...] + (1.0 - ADAM_B1) * gv
        v2 = ADAM_B2 * v_ref[...] + (1.0 - ADAM_B2) * (gv * gv)
        d_ref[...] = -ADAM_LR * ((m2 * c1) / (jnp.sqrt(v2 * c2) + ADAM_EPS) + ADAM_WD * w_ref[...])
        mo_ref[...] = m2
        vo_ref[...] = v2

    blk = pl.BlockSpec((tr, c), lambda i: (i, 0))
    out = jax.ShapeDtypeStruct((r, c), F32)
    return pl.pallas_call(body, name=name, grid=(r // tr,), in_specs=[blk] * 4, out_specs=[blk] * 3, out_shape=[out] * 3,
                          compiler_params=_params(("parallel",)))(w, g, m, v)


def rope_table(pos, inv, *, name):
    n = pos.shape[0]
    tr = _rows(n)

    def body(p_ref, i_ref, c_ref, s_ref):
        ang = p_ref[...].astype(F32) * i_ref[...]
        c_ref[...] = jnp.cos(ang)
        s_ref[...] = jnp.sin(ang)

    out = jax.ShapeDtypeStruct((n, LANES), F32)
    blk = pl.BlockSpec((tr, LANES), lambda i: (i, 0))
    return pl.pallas_call(
        body, name=name, grid=(n // tr,), in_specs=[pl.BlockSpec((tr, 1), lambda i: (i, 0)), pl.BlockSpec((1, LANES), lambda i: (0, 0))],
        out_specs=[blk, blk], out_shape=[out, out], compiler_params=_params(("parallel",)))(pos, inv)


def _rot_half(x):
    first = (_iota((1, LANES), 1) & 63) < 32
    return jnp.where(first, -pltpu.roll(x, LANES - 32, axis=1), pltpu.roll(x, 32, axis=1))


def _head_norm(xv, gm):
    r = lax.rsqrt(_xdot2(xv * xv, gm) * (1.0 / HEAD) + EPS)
    return r, xv * r


def _head_norm_bwd(xh, r, dxh, gm):
    return r * (dxh - xh * (_xdot2(dxh * xh, gm) * (1.0 / HEAD)))


def fox_prep_fwd(proj, qg, kg, bf, *, bsz, seq, name):
    n = bsz * seq
    tr = min(256, seq)
    nt = seq // tr
    w = 4 * LANES

    def body(q_ref, k_ref, f_ref, qg_ref, kg_ref, b_ref, qn_ref, kn_ref, fb_ref, f8_ref, carry):
        @pl.when(pl.program_id(1) == 0)
        def _():
            carry[...] = jnp.zeros_like(carry)

        gm = _head_mat(LANES)
        for src, gain, dst in ((q_ref, qg_ref, qn_ref), (k_ref, kg_ref, kn_ref)):
            for c in range(4):
                sl = slice(c * LANES, (c + 1) * LANES)
                _, xh = _head_norm(src[:, sl], gm)
                dst[:, sl] = (xh * gain[:, sl]).astype(dst.dtype)
        logf = jax.nn.log_sigmoid(f_ref[...] + b_ref[...])
        lower = (_iota((tr, tr), 1) <= _iota((tr, tr), 0)).astype(BF16)
        fcum = _xdot3_left(lower, logf) + carry[...]
        carry[...] = fcum[tr - 1:tr, :]
        f8_ref[...] = fcum
        spread = (_iota((LANES, w), 0) == (_iota((LANES, w), 1) >> 6)).astype(BF16)
        fb_ref[...] = _xdot3(fcum, spread)

    row = lambda width, blk: pl.BlockSpec((tr, width), lambda b, t: (b * nt + t, blk))
    vec = lambda width: pl.BlockSpec((1, width), lambda b, t: (0, 0))
    return pl.pallas_call(
        body, name=name, grid=(bsz, nt),
        in_specs=[row(w, FOXQ // 4), row(w, FOXK // 4), row(LANES, FORGET), vec(w), vec(w), vec(LANES)],
        out_specs=[row(w, 0), row(w, 0), row(w, 0), row(LANES, 0)],
        out_shape=[jax.ShapeDtypeStruct((n, w), MM), jax.ShapeDtypeStruct((n, w), MM),
                   jax.ShapeDtypeStruct((n, w), F32), jax.ShapeDtypeStruct((n, LANES), F32)],
        scratch_shapes=[pltpu.VMEM((1, LANES), F32)],
        compiler_params=_params(("parallel", "arbitrary")))(proj, proj, proj, qg, kg, bf)


def fox_prep_bwd(proj, qg, kg, bf, dqn, dkn, df, *, bsz, seq, name):
    n = bsz * seq
    tr = min(256, seq)
    nt = seq // tr
    w = 4 * LANES

    def body(q_ref, k_ref, f_ref, qg_ref, kg_ref, b_ref, dqn_ref, dkn_ref, df_ref,
             dq_ref, dk_ref, dl_ref, dqg_ref, dkg_ref, db_ref, carry):
        first = (pl.program_id(0) == 0) & (pl.program_id(1) == 0)

        @pl.when(first)
        def _():
            dqg_ref[...] = jnp.zeros_like(dqg_ref)
            dkg_ref[...] = jnp.zeros_like(dkg_ref)
            db_ref[...] = jnp.zeros_like(db_ref)

        @pl.when(pl.program_id(1) == 0)
        def _():
            carry[...] = jnp.zeros_like(carry)

        gm = _head_mat(LANES)
        for src, gain, dy_ref, dx_ref, dg_ref in ((q_ref, qg_ref, dqn_ref, dq_ref, dqg_ref), (k_ref, kg_ref, dkn_ref, dk_ref, dkg_ref)):
            for c in range(4):
                sl = slice(c * LANES, (c + 1) * LANES)
                r, xh = _head_norm(src[:, sl], gm)
                dy = dy_ref[:, sl]
                dg_ref[:, sl] += jnp.sum(dy * xh, axis=0, keepdims=True)
                dx_ref[:, sl] = _head_norm_bwd(xh, r, dy * gain[:, sl], gm).astype(dx_ref.dtype)
        upper = (_iota((tr, tr), 1) >= _iota((tr, tr), 0)).astype(BF16)
        dlogf = _xdot3_left(upper, df_ref[...]) + carry[...]
        carry[...] = dlogf[0:1, :]
        dlogit = dlogf * jax.nn.sigmoid(-(f_ref[...] + b_ref[...]))
        dl_ref[:, 0:LANES] = dlogit.astype(dl_ref.dtype)
        dl_ref[:, LANES:2 * LANES] = jnp.zeros((tr, LANES), dl_ref.dtype)
        db_ref[...] += jnp.sum(dlogit, axis=0, keepdims=True)

    row = lambda width, blk: pl.BlockSpec((tr, width), lambda b, t: (b * nt + nt - 1 - t, blk))
    vec = lambda width: pl.BlockSpec((1, width), lambda b, t: (0, 0))
    return pl.pallas_call(
        body, name=name, grid=(bsz, nt),
        in_specs=[row(w, FOXQ // 4), row(w, FOXK // 4), row(LANES, FORGET), vec(w), vec(w), vec(LANES),
                  row(w, 0), row(w, 0), row(LANES, 0)],
        out_specs=[row(w, 0), row(w, 0), row(2 * LANES, 0), vec(w), vec(w), vec(LANES)],
        out_shape=[jax.ShapeDtypeStruct((n, w), MM), jax.ShapeDtypeStruct((n, w), MM), jax.ShapeDtypeStruct((n, 2 * LANES), MM),
                   jax.ShapeDtypeStruct((1, w), F32), jax.ShapeDtypeStruct((1, w), F32), jax.ShapeDtypeStruct((1, LANES), F32)],
        scratch_shapes=[pltpu.VMEM((1, LANES), F32)],
        compiler_params=_params(("arbitrary", "arbitrary")))(proj, proj, proj, qg, kg, bf, dqn, dkn, df)


DIL_W = 6 * LANES


def dil_prep_fwd(proj, qg, kg, cos, sin, *, name):
    n = proj.shape[0]
    tr = _rows(n, 256)

    def body(q_ref, k_ref, qg_ref, kg_ref, c_ref, s_ref, qo_ref, ko_ref):
        gm = _head_mat(LANES)
        cv, sv = c_ref[...], s_ref[...]
        for src, gain, dst in ((q_ref, qg_ref, qo_ref), (k_ref, kg_ref, ko_ref)):
            for c in range(6):
                sl = slice(c * LANES, (c + 1) * LANES)
                _, xh = _head_norm(src[:, sl], gm)
                xn = xh * gain[:, sl]
                dst[:, sl] = (xn * cv + _rot_half(xn) * sv).astype(dst.dtype)

    row = lambda width, blk: pl.BlockSpec((tr, width), lambda i: (i, blk))
    vec = pl.BlockSpec((1, DIL_W), lambda i: (0, 0))
    out = jax.ShapeDtypeStruct((n, DIL_W), MM)
    return pl.pallas_call(
        body, name=name, grid=(n // tr,),
        in_specs=[row(DIL_W, DILQ // 6), row(DIL_W, DILK // 6), vec, vec, row(LANES, 0), row(LANES, 0)],
        out_specs=[row(DIL_W, 0), row(DIL_W, 0)], out_shape=[out, out],
        compiler_params=_params(("parallel",)))(proj, proj, qg, kg, cos, sin)


def dil_prep_bwd(proj, qg, kg, cos, sin, dqr, dkr, *, name):
    n = proj.shape[0]
    tr = _rows(n, 256)

    def body(q_ref, k_ref, qg_ref, kg_ref, c_ref, s_ref, dqr_ref, dkr_ref, dq_ref, dk_ref, dqg_ref, dkg_ref):
        @pl.when(pl.program_id(0) == 0)
        def _():
            dqg_ref[...] = jnp.zeros_like(dqg_ref)
            dkg_ref[...] = jnp.zeros_like(dkg_ref)

        gm = _head_mat(LANES)
        cv, sv = c_ref[...], s_ref[...]
        for src, gain, dy_ref, dx_ref, dg_ref in ((q_ref, qg_ref, dqr_ref, dq_ref, dqg_ref), (k_ref, kg_ref, dkr_ref, dk_ref, dkg_ref)):
            for c in range(6):
                sl = slice(c * LANES, (c + 1) * LANES)
                r, xh = _head_norm(src[:, sl], gm)
                dy = dy_ref[:, sl]
                dxn = dy * cv - _rot_half(dy * sv)
                dg_ref[:, sl] += jnp.sum(dxn * xh, axis=0, keepdims=True)
                dx_ref[:, sl] = _head_norm_bwd(xh, r, dxn * gain[:, sl], gm).astype(dx_ref.dtype)

    row = lambda width, blk: pl.BlockSpec((tr, width), lambda i: (i, blk))
    vec = pl.BlockSpec((1, DIL_W), lambda i: (0, 0))
    out = jax.ShapeDtypeStruct((n, DIL_W), MM)
    gout = jax.ShapeDtypeStruct((1, DIL_W), F32)
    return pl.pallas_call(
        body, name=name, grid=(n // tr,),
        in_specs=[row(DIL_W, DILQ // 6), row(DIL_W, DILK // 6), vec, vec, row(LANES, 0), row(LANES, 0), row(DIL_W, 0), row(DIL_W, 0)],
        out_specs=[row(DIL_W, 0), row(DIL_W, 0), vec, vec], out_shape=[out, out, gout, gout],
        compiler_params=_params(("arbitrary",)))(proj, proj, qg, kg, cos, sin, dqr, dkr)


def dil_combine_fwd(os_, lses, *, name):
    n, w = os_[0].shape
    tr = _rows(n)

    def body(o0, o1, o2, l0, l1, l2, out_ref):
        a, b, c = l0[...], l1[...], l2[...]
        m = jnp.maximum(jnp.maximum(a, b), c)
        ea, eb, ec = jnp.exp(a - m), jnp.exp(b - m), jnp.exp(c - m)
        out_ref[...] = ((ea * o0[...] + eb * o1[...] + ec * o2[...]) / (ea + eb + ec)).astype(out_ref.dtype)

    blk = pl.BlockSpec((tr, w), lambda i: (i, 0))
    return pl.pallas_call(body, name=name, grid=(n // tr,), in_specs=[blk] * 6, out_specs=blk,
                          out_shape=jax.ShapeDtypeStruct((n, w), MM), compiler_params=_params(("parallel",)))(*os_, *lses)


def dil_combine_bwd(os_, lses, dout, *, name):
    n, w = dout.shape
    tr = _rows(n)

    def body(o0, o1, o2, l0, l1, l2, d_ref, do0, do1, do2, dl0, dl1, dl2):
        a, b, c = l0[...], l1[...], l2[...]
        m = jnp.maximum(jnp.maximum(a, b), c)
        es = [jnp.exp(a - m), jnp.exp(b - m), jnp.exp(c - m)]
        inv = 1.0 / (es[0] + es[1] + es[2])
        ws = [e * inv for e in es]
        dv = d_ref[...]
        gm = _head_mat(w)
        dws = [_xdot2(dv * o[...], gm) for o in (o0, o1, o2)]
        mean = ws[0] * dws[0] + ws[1] * dws[1] + ws[2] * dws[2]
        for wg, dw, do, dl in zip(ws, dws, (do0, do1, do2), (dl0, dl1, dl2)):
            do[...] = wg * dv
            dl[...] = wg * (dw - mean)

    blk = pl.BlockSpec((tr, w), lambda i: (i, 0))
    out = jax.ShapeDtypeStruct((n, w), F32)
    return pl.pallas_call(body, name=name, grid=(n // tr,), in_specs=[blk] * 7, out_specs=[blk] * 6, out_shape=[out] * 6,
                          compiler_params=_params(("parallel",)))(*os_, *lses, dout)


BUCKET = 512


def _extents(seq, window, tq):
    need = window + tq
    if need >= seq:
        step = max(BUCKET, tq)
        if seq % step:
            return [seq], None
        return list(range(step, seq + 1, step)), step
    return [need], None


def _for_extent(qi, tq, seq, window, run):
    exts, step = _extents(seq, window, tq)
    hi = (qi + 1) * tq
    if step is None:
        ext = exts[0]
        if ext >= seq:
            run(seq, 0)
        else:
            run(ext, pl.multiple_of(jnp.maximum(hi - ext, 0), LANES))
        return
    bucket = (hi - 1) // step
    for bi, ext in enumerate(exts):
        pl.when(bucket == bi)(functools.partial(run, ext, 0))


def _lane_first():
    return _iota((1, LANES), 1) < HEAD


def _band(qi, tq, ext, start, window, dilation, strict=False):
    d = (qi * tq + _iota((tq, ext), 0)) - (start + _iota((tq, ext), 1))
    ok = (d > 0) if strict else (d >= 0)
    if window < 1 << 30:
        ok &= d <= window
    if dilation > 1:
        ok &= (d & (dilation - 1)) == 0
    return ok


def _attn_specs(bsz, seq, tq, qo, ko, vo):
    nq = seq // tq
    qspec = lambda off: pl.BlockSpec((tq, LANES), lambda b, j, i: (b * nq + i, off + j))
    kspec = lambda off: pl.BlockSpec((seq, LANES), lambda b, j, i: (b, off + j))
    return nq, qspec, kspec


def softmax_attn_fwd(q, k, v, bias, *, qo, ko, vo, pairs, bsz, seq, window, dilation, tq, name):
    n = bsz * seq
    nq, qspec, kspec = _attn_specs(bsz, seq, tq, qo, ko, vo)

    def body(*refs):
        if bias is None:
            q_ref, k_ref, v_ref, o_ref, l_ref = refs
        else:
            q_ref, k_ref, v_ref, fq_ref, fk_ref, o_ref, l_ref = refs
        qi = pl.program_id(2)

        def run(ext, start):
            qv = q_ref[...].astype(MM)
            kv = k_ref[pl.ds(start, ext), :].astype(MM)
            vv = v_ref[pl.ds(start, ext), :].astype(MM)
            ok = _band(qi, tq, ext, start, window, dilation)
            first = _lane_first()
            outs, lses = [], []
            for a in range(2):
                s = _dot_nt(jnp.where(first if a == 0 else ~first, qv, jnp.zeros_like(qv)), kv) * SCALE
                if bias is not None:
                    s = s + fq_ref[:, a * HEAD:a * HEAD + 1] - fk_ref[a:a + 1, 0:ext]
                s = jnp.where(ok, s, -jnp.inf)
                m = jnp.max(s, axis=1, keepdims=True)
                p = jnp.exp(s - m)
                den = jnp.sum(p, axis=1, keepdims=True)
                outs.append(_dot(p.astype(MM), vv) / den)
                lses.append(m + jnp.log(den))
            o_ref[...] = jnp.where(first, outs[0], outs[1]).astype(o_ref.dtype)
            l_ref[...] = jnp.where(first, lses[0], lses[1])

        _for_extent(qi, tq, seq, window, run)

    ins, specs = [q, k, v], [qspec(qo), kspec(ko), kspec(vo)]
    if bias is not None:
        ins += list(bias)
        specs += [qspec(0), pl.BlockSpec((8, seq), lambda b, j, i: (b * pairs + j, 0))]
    out = jax.ShapeDtypeStruct((n, LANES * pairs), F32)
    return pl.pallas_call(
        body, name=name, grid=(bsz, pairs, nq), in_specs=specs, out_specs=[qspec(0), qspec(0)], out_shape=[out, out],
        compiler_params=_params(("parallel", "parallel", "arbitrary")))(*ins)


def softmax_attn_bwd(q, k, v, o, do, lse, dlse, bias, *, qo, ko, vo, pairs, bsz, seq, window, dilation, tq, dq_dtype, dk_dtype, name):
    n = bsz * seq
    nq, qspec, kspec = _attn_specs(bsz, seq, tq, qo, ko, vo)
    has_bias, has_dlse = bias is not None, dlse is not None

    def body(*refs):
        refs = list(refs)
        q_ref, k_ref, v_ref, o_ref, do_ref, l_ref = refs[:6]
        del refs[:6]
        dl_ref = refs.pop(0) if has_dlse else None
        fq_ref, fk_ref = (refs.pop(0), refs.pop(0)) if has_bias else (None, None)
        dq_ref, dk_ref, dv_ref = refs[:3]
        del refs[:3]
        dfq_ref, dfk_ref = (refs.pop(0), refs.pop(0)) if has_bias else (None, None)
        dk_acc, dv_acc = refs
        qi = pl.program_id(2)

        @pl.when(qi == 0)
        def _():
            dk_acc[...] = jnp.zeros_like(dk_acc)
            dv_acc[...] = jnp.zeros_like(dv_acc)
            if has_bias:
                dfk_ref[...] = jnp.zeros_like(dfk_ref)

        def run(ext, start):
            qv = q_ref[...].astype(MM)
            kv = k_ref[pl.ds(start, ext), :].astype(MM)
            vv = v_ref[pl.ds(start, ext), :].astype(MM)
            dov = do_ref[...]
            dob = dov.astype(MM)
            prod = dov * o_ref[...]
            ok = _band(qi, tq, ext, start, window, dilation)
            first = _lane_first()
            dqs, dks, dvs, dfqs = [], [], [], []
            for a in range(2):
                mine = first if a == 0 else ~first
                col = slice(a * HEAD, a * HEAD + 1)
                delta = jnp.sum(jnp.where(mine, prod, 0.0), axis=1, keepdims=True)
                if has_dlse:
                    delta = delta - dl_ref[:, col]
                s = _dot_nt(jnp.where(mine, qv, jnp.zeros_like(qv)), kv) * SCALE
                if has_bias:
                    s = s + fq_ref[:, col] - fk_ref[a:a + 1, 0:ext]
                p = jnp.exp(jnp.where(ok, s, -jnp.inf) - l_ref[:, col])
                dp = _dot_nt(jnp.where(mine, dob, jnp.zeros_like(dob)), vv)
                ds = p * (dp - delta)
                dsb = ds.astype(MM)
                dvs.append(_dot_tn(p.astype(MM), dob))
                dks.append(_dot_tn(dsb, qv) * SCALE)
                dqs.append(_dot(dsb, kv) * SCALE)
                if has_bias:
                    dfqs.append(jnp.sum(ds, axis=1, keepdims=True))
                    dfk_ref[a:a + 1, 0:ext] += jnp.sum(ds, axis=0, keepdims=True)
            dq_ref[...] = jnp.where(first, dqs[0], dqs[1]).astype(dq_ref.dtype)
            dk_acc[pl.ds(start, ext), :] += jnp.where(first, dks[0], dks[1])
            dv_acc[pl.ds(start, ext), :] += jnp.where(first, dvs[0], dvs[1])
            if has_bias:
                dfq_ref[...] = jnp.where(first, dfqs[0], dfqs[1])

        _for_extent(qi, tq, seq, window, run)

        @pl.when(qi == nq - 1)
        def _():
            dk_ref[...] = dk_acc[...].astype(dk_ref.dtype)
            dv_ref[...] = dv_acc[...].astype(dv_ref.dtype)

    wide = LANES * pairs
    ins = [q, k, v, o, do, lse]
    specs = [qspec(qo), kspec(ko), kspec(vo), qspec(0), qspec(0), qspec(0)]
    outs = [jax.ShapeDtypeStruct((n, wide), dq_dtype), jax.ShapeDtypeStruct((n, wide), dk_dtype), jax.ShapeDtypeStruct((n, wide), MM)]
    out_specs = [qspec(0), kspec(0), kspec(0)]
    if has_dlse:
        ins.append(dlse)
        specs.append(qspec(0))
    if has_bias:
        rows = pl.BlockSpec((8, seq), lambda b, j, i: (b * pairs + j, 0))
        ins += list(bias)
        specs += [qspec(0), rows]
        outs += [jax.ShapeDtypeStruct((n, wide), F32), jax.ShapeDtypeStruct((bsz * pairs * 8, seq), F32)]
        out_specs += [qspec(0), rows]
    return pl.pallas_call(
        body, name=name, grid=(bsz, pairs, nq), in_specs=specs, out_specs=out_specs, out_shape=outs,
        scratch_shapes=[pltpu.VMEM((seq, LANES), F32), pltpu.VMEM((seq, LANES), F32)],
        compiler_params=_params(("parallel", "parallel", "arbitrary")))(*ins)


def _sb_weights(qa, kv, mask, ext, tq, lat_ref):
    z = _dot_nt(qa, kv) * SCALE
    e, sp = _softplus_parts(z)
    log_not = jnp.where(mask, -sp, 0.0)
    after = (_iota((LANES, LANES), 0) > _iota((LANES, LANES), 1)).astype(BF16)
    carry = jnp.zeros((tq, 1), F32)
    for cb in reversed(range(ext // LANES)):
        sl = slice(cb * LANES, (cb + 1) * LANES)
        blk = log_not[:, sl]
        lat_ref[:, sl] = _xdot2(blk, after) + carry
        carry = carry + jnp.sum(blk, axis=1, keepdims=True)
    att = jnp.where(mask, jnp.exp(z - sp + lat_ref[:, 0:ext]), 0.0)
    sig = jnp.where(z >= 0, 1.0, e) / (1.0 + e)
    return att, sig


def sb_attn_fwd(proj, *, bsz, seq, tq, name):
    n = bsz * seq
    pairs = 4
    nq, qspec, kspec = _attn_specs(bsz, seq, tq, SBQ, SBK, SBV)

    def body(q_ref, k_ref, v_ref, o_ref, lat_ref):
        qi = pl.program_id(2)

        def run(ext, start):
            qv = q_ref[...].astype(MM)
            kv = k_ref[0:ext, :].astype(MM)
            vv = v_ref[0:ext, :].astype(MM)
            mask = _band(qi, tq, ext, 0, 1 << 30, 1, strict=True)
            first = _lane_first()
            outs = []
            for a in range(2):
                att, _ = _sb_weights(jnp.where(first if a == 0 else ~first, qv, jnp.zeros_like(qv)), kv, mask, ext, tq, lat_ref)
                outs.append(_dot(att.astype(MM), vv))
            o_ref[...] = jnp.where(first, outs[0], outs[1]).astype(o_ref.dtype)

        _for_extent(qi, tq, seq, seq, run)

    return pl.pallas_call(
        body, name=name, grid=(bsz, pairs, nq), in_specs=[qspec(SBQ), kspec(SBK), kspec(SBV)], out_specs=qspec(0),
        out_shape=jax.ShapeDtypeStruct((n, LANES * pairs), MM), scratch_shapes=[pltpu.VMEM((tq, seq), F32)],
        compiler_params=_params(("parallel", "parallel", "arbitrary")))(proj, proj, proj)


def sb_attn_bwd(proj, do, *, bsz, seq, tq, name):
    n = bsz * seq
    pairs = 4
    nq, qspec, kspec = _attn_specs(bsz, seq, tq, SBQ, SBK, SBV)

    def body(q_ref, k_ref, v_ref, do_ref, dq_ref, dk_ref, dv_ref, lat_ref, dk_acc, dv_acc):
        qi = pl.program_id(2)

        @pl.when(qi == 0)
        def _():
            dk_acc[...] = jnp.zeros_like(dk_acc)
            dv_acc[...] = jnp.zeros_like(dv_acc)

        def run(ext, start):
            qv = q_ref[...].astype(MM)
            kv = k_ref[0:ext, :].astype(MM)
            vv = v_ref[0:ext, :].astype(MM)
            dob = do_ref[...].astype(MM)
            mask = _band(qi, tq, ext, 0, 1 << 30, 1, strict=True)
            first = _lane_first()
            before = (_iota((LANES, LANES), 0) < _iota((LANES, LANES), 1)).astype(BF16)
            dqs, dks, dvs = [], [], []
            for a in range(2):
                mine = first if a == 0 else ~first
                att, sig = _sb_weights(jnp.where(mine, qv, jnp.zeros_like(qv)), kv, mask, ext, tq, lat_ref)
                g = _dot_nt(jnp.where(mine, dob, jnp.zeros_like(dob)), vv) * att
                carry = jnp.zeros((tq, 1), F32)
                for cb in range(ext // LANES):
                    sl = slice(cb * LANES, (cb + 1) * LANES)
                    blk = g[:, sl]
                    lat_ref[:, sl] = _xdot2(blk, before) + carry
                    carry = carry + jnp.sum(blk, axis=1, keepdims=True)
                dz = jnp.where(mask, g * (1.0 - sig) - sig * lat_ref[:, 0:ext], 0.0).astype(MM)
                dvs.append(_dot_tn(att.astype(MM), dob))
                dks.append(_dot_tn(dz, qv) * SCALE)
                dqs.append(_dot(dz, kv) * SCALE)
            dq_ref[...] = jnp.where(first, dqs[0], dqs[1]).astype(dq_ref.dtype)
            dk_acc[0:ext, :] += jnp.where(first, dks[0], dks[1])
            dv_acc[0:ext, :] += jnp.where(first, dvs[0], dvs[1])

        _for_extent(qi, tq, seq, seq, run)

        @pl.when(qi == nq - 1)
        def _():
            dk_ref[...] = dk_acc[...].astype(dk_ref.dtype)
            dv_ref[...] = dv_acc[...].astype(dv_ref.dtype)

    out = jax.ShapeDtypeStruct((n, LANES * pairs), MM)
    return pl.pallas_call(
        body, name=name, grid=(bsz, pairs, nq), in_specs=[qspec(SBQ), kspec(SBK), kspec(SBV), qspec(0)],
        out_specs=[qspec(0), kspec(0), kspec(0)], out_shape=[out, out, out],
        scratch_shapes=[pltpu.VMEM((tq, seq), F32), pltpu.VMEM((seq, LANES), F32), pltpu.VMEM((seq, LANES), F32)],
        compiler_params=_params(("parallel", "parallel", "arbitrary")))(proj, proj, proj, do)


def _place():
    return lax.axis_index("x"), lax.axis_index("y"), lax.axis_index("c")


def _other_chips(x, y):
    return [(1 - x, y), (x, 1 - y), (1 - x, 1 - y)]


def _remote(src, dst, send_sems, recv_sems, k, to):
    return pltpu.make_async_remote_copy(src_ref=src, dst_ref=dst, send_sem=send_sems.at[k], recv_sem=recv_sems.at[k],
                                        device_id=to, device_id_type=MESH_ID)


def gather_chips(arrs, *, name):
    na = len(arrs)

    def body(*refs):
        ins, outs = refs[:na], refs[na:2 * na]
        send_sems, recv_sems = refs[2 * na:]
        x, y, c = _place()
        me, sibling = 2 * x + y, (x, y, 1 - c)
        chips = _other_chips(x, y)
        sends = []
        for t in range(na):
            rh = ins[t].shape[0] // 2
            half = lambda chip, h, t=t, rh=rh: outs[t].at[chip, pl.ds(h * rh, rh), :]
            for j, (px, py) in enumerate(chips):
                cp = _remote(ins[t].at[pl.ds(c * rh, rh), :], half(me, c), send_sems, recv_sems, 6 * t + j, (px, py, c))
                cp.start()
                sends.append(cp)
        for t in range(na):
            rh = ins[t].shape[0] // 2
            half = lambda chip, h, t=t, rh=rh: outs[t].at[chip, pl.ds(h * rh, rh), :]
            for j, (px, py) in enumerate(chips):
                landed = half(2 * px + py, c)
                _remote(landed, landed, send_sems, recv_sems, 6 * t + j, (px, py, c)).wait_recv()
                fw = _remote(landed, landed, send_sems, recv_sems, 6 * t + 3 + j, sibling)
                fw.start()
                sends.append(fw)
        for t in range(na):
            rh = ins[t].shape[0] // 2
            half = lambda chip, h, t=t, rh=rh: outs[t].at[chip, pl.ds(h * rh, rh), :]
            for j, (px, py) in enumerate(chips):
                passed = half(2 * px + py, 1 - c)
                _remote(passed, passed, send_sems, recv_sems, 6 * t + 3 + j, sibling).wait_recv()
        for cp in sends:
            cp.wait_send()

    for a in arrs:
        assert a.ndim == 2 and a.shape[0] % 32 == 0, a.shape
    return pl.pallas_call(
        body, name=name, in_specs=[ANY] * na, out_specs=[ANY] * na,
        out_shape=[jax.ShapeDtypeStruct((4,) + a.shape, a.dtype) for a in arrs],
        scratch_shapes=[pltpu.SemaphoreType.DMA((6 * na,)), pltpu.SemaphoreType.DMA((6 * na,))],
    )(*arrs)


def _chunk_rows(rows, cols, limit):
    best = 16
    for t in range(16, rows + 1, 16):
        if rows % t == 0 and t * cols * 4 <= limit:
            best = t
    assert rows % best == 0, (rows, cols)
    return best


def pair_sum(a, core, *, name):
    _, rows, cols = a.shape
    rh = rows // 2
    tr = _chunk_rows(rh, cols, 2 << 20)
    nch = rh // tr
    steps = 4 * nch

    def body(core_ref, keep_ref, send_ref, o32_ref, o16_ref, landing, send_sems, recv_sems, credit):
        step = pl.program_id(0) * nch + pl.program_id(1)
        slot = lax.rem(step, 2)
        x, y, c = _place()
        sibling = (x, y, 1 - c)

        @pl.when(step >= 2)
        def _():
            pl.semaphore_wait(credit, 1)

        cp = _remote(send_ref.at[0], landing.at[slot], send_sems, recv_sems, slot, sibling)
        cp.start()
        cp.wait_recv()
        total = keep_ref[0] + landing[slot]
        o32_ref[0] = total
        o16_ref[0] = total.astype(BF16)
        cp.wait_send()

        @pl.when(step + 2 < steps)
        def _():
            pl.semaphore_signal(credit, 1, device_id=sibling, device_id_type=MESH_ID)

    blk = (1, tr, cols)
    grid_spec = pltpu.PrefetchScalarGridSpec(
        num_scalar_prefetch=1, grid=(4, nch),
        in_specs=[pl.BlockSpec(blk, lambda k, i, core: (k, core[0] * nch + i, 0)),
                  pl.BlockSpec(blk, lambda k, i, core: (k, (1 - core[0]) * nch + i, 0))],
        out_specs=[pl.BlockSpec(blk, lambda k, i, core: (k, i, 0))] * 2,
        scratch_shapes=[pltpu.VMEM((2, tr, cols), F32), pltpu.SemaphoreType.DMA((2,)), pltpu.SemaphoreType.DMA((2,)),
                        pltpu.SemaphoreType.REGULAR])
    return pl.pallas_call(
        body, name=name, grid_spec=grid_spec,
        out_shape=[jax.ShapeDtypeStruct((4, rh, cols), F32), jax.ShapeDtypeStruct((4, rh, cols), BF16)],
        compiler_params=_params(("arbitrary", "arbitrary")))(core, a, a)


def scatter_chips(arrs, *, name):
    na = len(arrs)

    def body(*refs):
        ins, outs = refs[:na], refs[na:2 * na]
        send_sems, recv_sems = refs[2 * na:]
        x, y, c = _place()
        me = 2 * x + y
        chips = _other_chips(x, y)
        sends = []
        for t in range(na):
            for j, (px, py) in enumerate(chips):
                cp = _remote(ins[t].at[2 * px + py], outs[t].at[me], send_sems, recv_sems, 3 * t + j, (px, py, c))
                cp.start()
                sends.append(cp)
        for t in range(na):
            for j, (px, py) in enumerate(chips):
                slab = outs[t].at[2 * px + py]
                _remote(slab, slab, send_sems, recv_sems, 3 * t + j, (px, py, c)).wait_recv()
        for cp in sends:
            cp.wait_send()

    return pl.pallas_call(
        body, name=name, in_specs=[ANY] * na, out_specs=[ANY] * na,
        out_shape=[jax.ShapeDtypeStruct(a.shape, a.dtype) for a in arrs],
        scratch_shapes=[pltpu.SemaphoreType.DMA((3 * na,)), pltpu.SemaphoreType.DMA((3 * na,))],
    )(*arrs)


def chip_sum_join(own, landed, chip, *, name):
    _, rh, cols = own.shape
    tr = _chunk_rows(rh, cols, 2 << 20)
    nch = rh // tr

    def body(chip_ref, own_ref, l1_ref, l2_ref, l3_ref, out_ref, res, local_sem, send_sem, recv_sem):
        i = pl.program_id(0)
        x, y, c = _place()
        sibling = (x, y, 1 - c)
        res[...] = ((own_ref[0] + l1_ref[0].astype(F32)) + l2_ref[0].astype(F32)) + l3_ref[0].astype(F32)
        rows = pl.ds(pl.multiple_of(i * tr, tr), tr)
        here = pltpu.make_async_copy(res, out_ref.at[c, rows, :], local_sem)
        here.start()
        there = pltpu.make_async_remote_copy(src_ref=res, dst_ref=out_ref.at[c, rows, :], send_sem=send_sem, recv_sem=recv_sem,
                                             device_id=sibling, device_id_type=MESH_ID)
        there.start()
        here.wait()
        there.wait_send()

        @pl.when(i == nch - 1)
        def _():
            half = out_ref.at[1 - c]
            pltpu.make_async_remote_copy(src_ref=half, dst_ref=half, send_sem=send_sem, recv_sem=recv_sem,
                                         device_id=sibling, device_id_type=MESH_ID).wait_recv()

    blk = (1, tr, cols)
    slab = lambda p: pl.BlockSpec(blk, lambda i, chip: (chip[0] ^ p, i, 0))
    grid_spec = pltpu.PrefetchScalarGridSpec(
        num_scalar_prefetch=1, grid=(nch,), in_specs=[slab(0), slab(1), slab(2), slab(3)], out_specs=ANY,
        scratch_shapes=[pltpu.VMEM((tr, cols), F32), pltpu.SemaphoreType.DMA, pltpu.SemaphoreType.DMA, pltpu.SemaphoreType.DMA])
    return pl.pallas_call(
        body, name=name, grid_spec=grid_spec, out_shape=jax.ShapeDtypeStruct((2, rh, cols), F32),
        compiler_params=_params(("arbitrary",)))(chip, own, landed, landed, landed)


def all_reduce_small(a, *, name):
    def body(a_ref, o_ref, buf, send_sems, recv_sems):
        x, y, c = _place()
        me = 4 * x + 2 * y + c
        buf[me] = a_ref[...]
        sent = []
        for p in range(1, 8):
            px, py, pc = (p >> 2) & 1, (p >> 1) & 1, p & 1
            cp = _remote(a_ref, buf.at[me], send_sems, recv_sems, p - 1, (x ^ px, y ^ py, c ^ pc))
            cp.start()
            sent.append(cp)
        for p in range(1, 8):
            px, py, pc = (p >> 2) & 1, (p >> 1) & 1, p & 1
            src = 4 * (x ^ px) + 2 * (y ^ py) + (c ^ pc)
            _remote(a_ref, buf.at[src], send_sems, recv_sems, p - 1, (x ^ px, y ^ py, c ^ pc)).wait_recv()
        for cp in sent:
            cp.wait_send()
        acc = buf[0]
        for d in range(1, 8):
            acc = acc + buf[d]
        o_ref[...] = acc

    vm = pl.BlockSpec(memory_space=pltpu.VMEM)
    return pl.pallas_call(
        body, name=name, in_specs=[vm], out_specs=vm, out_shape=jax.ShapeDtypeStruct(a.shape, a.dtype),
        scratch_shapes=[pltpu.VMEM((8,) + a.shape, a.dtype), pltpu.SemaphoreType.DMA((7,)), pltpu.SemaphoreType.DMA((7,))],
    )(a)


TQ = 256


def _pad_w_in(w):
    pad = jnp.zeros((w.shape[0], DPROJ - DIN), w.dtype)
    return jnp.concatenate([w[:, :O1], w[:, O2:], w[:, O1:O2], pad], axis=1)


def _unpad_w_in(g):
    nf = O2 - O1
    return jnp.concatenate([g[:, :O1], g[:, DIN - nf:DIN], g[:, O1:DIN - nf]], axis=1)


def _layer_small(sm, l):
    row = lambda v: v.reshape(1, -1)
    return dict(
        attn_norm=row(sm["attn_norm"][l]), mlp_norm=row(sm["mlp_norm"][l]),
        qgf=row(jnp.tile(sm["q_norm_fox"][l], 8)), kgf=row(jnp.tile(sm["k_norm_fox"][l], 8)),
        qgd=row(jnp.tile(sm["q_norm_dil"][l], 12)), kgd=row(jnp.tile(sm["k_norm_dil"][l], 12)),
        bfor=row(jnp.pad(sm["b_forget"][l], (0, LANES - 8))))


def _key_rows(f8, bsz, seq):
    f = f8.reshape(bsz, seq, LANES)[:, :, :8].transpose(0, 2, 1).reshape(bsz, 4, 2, seq)
    return jnp.pad(f, ((0, 0), (0, 0), (0, 6), (0, 0))).reshape(bsz * 32, seq)


def _layer_fwd(x, w, s, cos, sin, bsz, seq, l):
    nm = lambda t: f"l{l}_{t}"
    h = rmsnorm_fwd(x, s["attn_norm"], name=nm("attn_norm"))
    proj = matmul(h, w["win"], name=nm("proj"))
    qn, kn, fb, f8 = fox_prep_fwd(proj, s["qgf"], s["kgf"], s["bfor"], bsz=bsz, seq=seq, name=nm("fox_prep"))
    fk = _key_rows(f8, bsz, seq)
    oa, la = softmax_attn_fwd(qn, kn, proj, (fb, fk), qo=0, ko=0, vo=FOXV, pairs=4, bsz=bsz, seq=seq, window=seq, dilation=1,
                              tq=TQ, name=nm("fox_attn"))
    ob = sb_attn_fwd(proj, bsz=bsz, seq=seq, tq=TQ, name=nm("sb_attn"))
    qr, kr = dil_prep_fwd(proj, s["qgd"], s["kgd"], cos, sin, name=nm("dil_prep"))
    ogs, lgs = [], []
    for g, (window, dilation) in enumerate(DIL_PATTERNS):
        og, lg = softmax_attn_fwd(qr, kr, proj, None, qo=2 * g, ko=2 * g, vo=DILV + 2 * g, pairs=2, bsz=bsz, seq=seq,
                                  window=window, dilation=dilation, tq=TQ, name=nm(f"dil_attn{g}"))
        ogs.append(og)
        lgs.append(lg)
    oc = dil_combine_fwd(ogs, lgs, name=nm("dil_combine"))
    ys = [matmul(oa, w["wuf"], name=nm("up_fox")), matmul(ob, w["wus"], name=nm("up_sb")), matmul(oc, w["wud"], name=nm("up_dil"))]
    merged = merge_fwd(proj, ys, name=nm("merge"))
    x1 = matmul(merged, w["wo"], add=x, name=nm("out_proj"))
    h2 = rmsnorm_fwd(x1, s["mlp_norm"], name=nm("mlp_norm"))
    u = matmul(h2, w["wmi"], name=nm("mlp_in"))
    act = relu2_fwd(u, name=nm("relu2"))
    x2 = matmul(act, w["wmo"], add=x1, name=nm("mlp_out"))
    saved = dict(x=x, h=h, proj=proj, qn=qn, kn=kn, fb=fb, fk=fk, oa=oa, la=la, ob=ob, qr=qr, kr=kr, ogs=ogs, lgs=lgs, oc=oc,
                 ys=ys, merged=merged, x1=x1, h2=h2, u=u, act=act)
    return x2, saved


def _layer_bwd(dx2, w, s, sv, cos, sin, bsz, seq, l):
    nm = lambda t: f"l{l}_{t}_bwd"
    n = bsz * seq
    proj = sv["proj"]
    gw = {}
    da = matmul(dx2, w["wmo"], tb=True, name=nm("mlp_out_dx"))
    gw["wmo"] = matmul(sv["act"], dx2, ta=True, name=nm("mlp_out_dw"))
    du = relu2_bwd(sv["u"], da, name=nm("relu2"))
    dh2 = matmul(du, w["wmi"], tb=True, name=nm("mlp_in_dx"))
    gw["wmi"] = matmul(sv["h2"], du, ta=True, name=nm("mlp_in_dw"))
    dx1, g_mlp_norm = rmsnorm_bwd(sv["x1"], s["mlp_norm"], dh2, dx2, name=nm("mlp_norm"))

    dmerged = matmul(dx1, w["wo"], tb=True, name=nm("out_proj_dx"))
    gw["wo"] = matmul(sv["merged"], dx1, ta=True, name=nm("out_proj_dw"))
    dya, dyb, dyc, dga, dgb, dgc = merge_bwd(proj, sv["ys"], dmerged, name=nm("merge"))
    doa = matmul(dya, w["wuf"], tb=True, name=nm("up_fox_dx"))
    gw["wuf"] = matmul(sv["oa"], dya, ta=True, name=nm("up_fox_dw"))
    dob = matmul(dyb, w["wus"], tb=True, name=nm("up_sb_dx"))
    gw["wus"] = matmul(sv["ob"], dyb, ta=True, name=nm("up_sb_dw"))
    doc = matmul(dyc, w["wud"], tb=True, name=nm("up_dil_dx"))
    gw["wud"] = matmul(sv["oc"], dyc, ta=True, name=nm("up_dil_dw"))

    outs = dil_combine_bwd(sv["ogs"], sv["lgs"], doc, name=nm("dil_combine"))
    dqs, dks, dvs = [], [], []
    for g, (window, dilation) in enumerate(DIL_PATTERNS):
        dq, dk, dv = softmax_attn_bwd(sv["qr"], sv["kr"], proj, sv["ogs"][g], outs[g], sv["lgs"][g], outs[3 + g], None,
                                      qo=2 * g, ko=2 * g, vo=DILV + 2 * g, pairs=2, bsz=bsz, seq=seq, window=window,
                                      dilation=dilation, tq=TQ, dq_dtype=F32, dk_dtype=F32, name=nm(f"dil_attn{g}"))
        dqs.append(dq)
        dks.append(dk)
        dvs.append(dv)
    d_dq, d_dk, g_qgd, g_kgd = dil_prep_bwd(proj, s["qgd"], s["kgd"], cos, sin, jnp.concatenate(dqs, axis=1),
                                            jnp.concatenate(dks, axis=1), name=nm("dil_prep"))

    s_dq, s_dk, s_dv = sb_attn_bwd(proj, dob, bsz=bsz, seq=seq, tq=TQ, name=nm("sb_attn"))

    dqn, dkn, f_dv, dfq, dfk = softmax_attn_bwd(sv["qn"], sv["kn"], proj, sv["oa"], doa, sv["la"], None, (sv["fb"], sv["fk"]),
                                                qo=0, ko=0, vo=FOXV, pairs=4, bsz=bsz, seq=seq, window=seq, dilation=1, tq=TQ,
                                                dq_dtype=F32, dk_dtype=F32, name=nm("fox_attn"))
    dfk8 = dfk.reshape(bsz, 4, 8, seq)[:, :, :2].reshape(bsz, 8, seq).transpose(0, 2, 1).reshape(n, 8)
    df = jnp.pad(dfq[:, ::HEAD] - dfk8, ((0, 0), (0, LANES - 8)))
    f_dq, f_dk, d_forget, g_qgf, g_kgf, g_bfor = fox_prep_bwd(proj, s["qgf"], s["kgf"], s["bfor"], dqn, dkn, df, bsz=bsz, seq=seq,
                                                              name=nm("fox_prep"))

    dproj = jnp.concatenate([f_dq, f_dk, f_dv, s_dq, s_dk, s_dv, d_dq, d_dk] + dvs + [dga, dgb, dgc, d_forget], axis=1)
    dh = matmul(dproj, w["win"], tb=True, tn=1024, tk=512, name=nm("proj_dx"))
    gw["win"] = matmul(sv["h"], dproj, ta=True, tk=2048, name=nm("proj_dw"))
    dx, g_attn_norm = rmsnorm_bwd(sv["x"], s["attn_norm"], dh, dx1, name=nm("attn_norm"))
    gs = dict(attn_norm=g_attn_norm[0], mlp_norm=g_mlp_norm[0], b_forget=g_bfor[0, :8],
              q_norm_fox=g_qgf.reshape(8, HEAD).sum(0), k_norm_fox=g_kgf.reshape(8, HEAD).sum(0),
              q_norm_dil=g_qgd.reshape(12, HEAD).sum(0), k_norm_dil=g_kgd.reshape(12, HEAD).sum(0))
    return dx, gw, gs


def local_step(x, positions, target, weights, small):
    bsz, seq, d = x.shape
    n = bsz * seq
    depth = len(weights)
    inv = 1.0 / (ROPE_THETA ** (jnp.arange(HEAD // 2, dtype=F32) / (HEAD // 2)))
    cos, sin = rope_table(positions.reshape(n, 1), jnp.tile(inv, 4).reshape(1, LANES), name="rope_table")
    xs = x.reshape(n, d)
    saved = []
    for l in range(depth):
        xs, sv = _layer_fwd(xs, weights[l], _layer_small(small, l), cos, sin, bsz, seq, l)
        saved.append(sv)
    dy, sq = loss_grad(xs, target.reshape(n, d), name="loss")
    loss = (0.5 / d) * jnp.sum(sq)
    gws, gss = [None] * depth, [None] * depth
    for l in reversed(range(depth)):
        dy, gws[l], gss[l] = _layer_bwd(dy, weights[l], _layer_small(small, l), saved[l], cos, sin, bsz, seq, l)
    return loss, dy.reshape(bsz, seq, d), gws, gss


SMALL = ("attn_norm", "mlp_norm", "b_forget", "q_norm_fox", "k_norm_fox", "q_norm_dil", "k_norm_dil")
SMALL_ROWS = 8


def _pack_small(vals):
    flat = jnp.concatenate([vals[k].reshape(-1) for k in SMALL])
    return jnp.pad(flat, (0, SMALL_ROWS * 1024 - flat.shape[0])).reshape(SMALL_ROWS, 1024)


def _unpack_small(packed, like):
    flat, out, at = packed.reshape(-1), {}, 0
    for k in SMALL:
        size = like[k].size
        out[k] = flat[at:at + size].reshape(like[k].shape)
        at += size
    return out


def _cols_to_chips(g):
    depth, rows, cols = g.shape
    return g.reshape(depth, rows, 4, cols // 4).transpose(2, 0, 1, 3).reshape(4, depth * rows, cols // 4)


def _rows_to_chips(g):
    depth, rows, cols = g.shape
    return g.reshape(depth, 4, rows // 4, cols).transpose(1, 0, 2, 3).reshape(4, depth * rows // 4, cols)


def _chips_to_cols(a, depth):
    _, rows, c = a.shape
    return a.reshape(4, depth, rows // depth, c).transpose(1, 2, 0, 3).reshape(depth, rows // depth, 4 * c)


def _chips_to_rows(a, depth):
    _, rows, c = a.shape
    return a.reshape(4, depth, rows // depth, c).transpose(1, 0, 2, 3).reshape(depth, 4 * rows // depth, c)


def kernel(x, positions, attn_norm, w_in, b_forget, q_norm_fox, k_norm_fox, q_norm_dil, k_norm_dil, w_up_fox, w_up_sb, w_up_dil, w_out, mlp_norm, w_mlp_in, w_mlp_out, loss_target, m_attn_norm, m_w_in, m_b_forget, m_q_norm_fox, m_k_norm_fox, m_q_norm_dil, m_k_norm_dil, m_w_up_fox, m_w_up_sb, m_w_up_dil, m_w_out, m_mlp_norm, m_w_mlp_in, m_w_mlp_out, v_attn_norm, v_w_in, v_b_forget, v_q_norm_fox, v_k_norm_fox, v_q_norm_dil, v_k_norm_dil, v_w_up_fox, v_w_up_sb, v_w_up_dil, v_w_out, v_mlp_norm, v_w_mlp_in, v_w_mlp_out):
    names = ("attn_norm", "w_in", "b_forget", "q_norm_fox", "k_norm_fox", "q_norm_dil", "k_norm_dil", "w_up_fox", "w_up_sb",
             "w_up_dil", "w_out", "mlp_norm", "w_mlp_in", "w_mlp_out")
    wv = dict(zip(names, (attn_norm, w_in, b_forget, q_norm_fox, k_norm_fox, q_norm_dil, k_norm_dil, w_up_fox, w_up_sb, w_up_dil,
                          w_out, mlp_norm, w_mlp_in, w_mlp_out)))
    mv = dict(zip(names, (m_attn_norm, m_w_in, m_b_forget, m_q_norm_fox, m_k_norm_fox, m_q_norm_dil, m_k_norm_dil, m_w_up_fox,
                          m_w_up_sb, m_w_up_dil, m_w_out, m_mlp_norm, m_w_mlp_in, m_w_mlp_out)))
    vv = dict(zip(names, (v_attn_norm, v_w_in, v_b_forget, v_q_norm_fox, v_k_norm_fox, v_q_norm_dil, v_k_norm_dil, v_w_up_fox,
                          v_w_up_sb, v_w_up_dil, v_w_out, v_mlp_norm, v_w_mlp_in, v_w_mlp_out)))
    depth = w_in.shape[0]
    flat2 = lambda a: a.reshape(-1, a.shape[-1])

    ups = ("w_up_fox", "w_up_sb", "w_up_dil")
    wide = ("w_out", "w_mlp_in", "w_mlp_out")
    send = [flat2(w_in).astype(MM), jnp.concatenate([flat2(wv[k]) for k in ups]).astype(MM),
            jnp.concatenate([flat2(wv[k]) for k in wide]).astype(MM)]
    core = lax.axis_index("c").astype(jnp.int32).reshape(1)
    chip = (2 * lax.axis_index("x") + lax.axis_index("y")).astype(jnp.int32).reshape(1)
    got_in, got_up, got_wide = [lax.dynamic_update_index_in_dim(g, s, chip[0], 0)
                                for g, s in zip(gather_chips(send, name="gather_weights"), send)]

    def split(a, keys):
        out, at = {}, 0
        for k in keys:
            rows = wv[k].shape[0] * wv[k].shape[1]
            out[k] = a[:, at:at + rows]
            at += rows
        return out

    full = {"w_in": _chips_to_cols(got_in, depth)}
    full.update({k: _chips_to_cols(a, depth) for k, a in split(got_up, ups).items()})
    parts = split(got_wide, wide)
    full["w_out"] = _chips_to_rows(parts["w_out"], depth)
    full["w_mlp_in"] = _chips_to_cols(parts["w_mlp_in"], depth)
    full["w_mlp_out"] = _chips_to_rows(parts["w_mlp_out"], depth)
    weights = [dict(win=_pad_w_in(full["w_in"][l]), wuf=full["w_up_fox"][l], wus=full["w_up_sb"][l], wud=full["w_up_dil"][l],
                    wo=full["w_out"][l], wmi=full["w_mlp_in"][l], wmo=full["w_mlp_out"][l]) for l in range(depth)]
    small = {k: wv[k] for k in SMALL}

    loss, grad_x, gws, gss = local_step(x, positions, loss_target, weights, small)
    loss = lax.psum(loss, ("x", "y", "c"))

    g_small = {k: jnp.stack([gss[l][k] for l in range(depth)]) for k in SMALL}
    g_small = _unpack_small(all_reduce_small(_pack_small(g_small), name="reduce_small"), small)

    stack = lambda key: jnp.stack([gws[l][key] for l in range(depth)])
    g_full = {"w_in": jnp.stack([_unpad_w_in(gws[l]["win"]) for l in range(depth)]), "w_up_fox": stack("wuf"), "w_up_sb": stack("wus"),
              "w_up_dil": stack("wud"), "w_out": stack("wo"), "w_mlp_in": stack("wmi"), "w_mlp_out": stack("wmo")}
    parts = [_cols_to_chips(g_full["w_in"]), jnp.concatenate([_cols_to_chips(g_full[k]) for k in ups], axis=1),
             jnp.concatenate([_rows_to_chips(g_full["w_out"]), _cols_to_chips(g_full["w_mlp_in"]), _rows_to_chips(g_full["w_mlp_out"])], axis=1)]
    sums = [pair_sum(p, core, name=f"reduce_pair_sum{t}") for t, p in enumerate(parts)]
    landed = scatter_chips([s16 for _, s16 in sums], name="reduce_scatter_chips")
    joined = [chip_sum_join(sums[t][0], landed[t], chip, name=f"reduce_chip_sum{t}").reshape(-1, parts[t].shape[-1])
              for t in range(3)]
    g_big = {"w_in": joined[0].reshape(w_in.shape)}
    for a, keys in ((joined[1], ups), (joined[2], wide)):
        at = 0
        for k in keys:
            rows = wv[k].shape[0] * wv[k].shape[1]
            g_big[k] = a[at:at + rows].reshape(wv[k].shape)
            at += rows

    grads = {**g_small, **g_big}
    delta, new_m, new_v = {}, {}, {}
    d_s, m_s, v_s = adamw(_pack_small(small), _pack_small(g_small), _pack_small({k: mv[k] for k in SMALL}),
                          _pack_small({k: vv[k] for k in SMALL}), name="adamw_small")
    delta.update(_unpack_small(d_s, small))
    new_m.update(_unpack_small(m_s, small))
    new_v.update(_unpack_small(v_s, small))
    for k in ("w_in",) + ups + wide:
        d_k, m_k, v_k = adamw(flat2(wv[k]), flat2(g_big[k]), flat2(mv[k]), flat2(vv[k]), name=f"adamw_{k}")
        delta[k], new_m[k], new_v[k] = d_k.reshape(wv[k].shape), m_k.reshape(wv[k].shape), v_k.reshape(wv[k].shape)

    return (loss, grad_x, *[grads[k] for k in names], *[delta[k] for k in names], *[new_m[k] for k in names], *[new_v[k] for k in names])
```

```python
import functools

import jax
import jax.numpy as jnp
from jax import lax
from jax.experimental import pallas as pl
from jax.experimental.pallas import tpu as pltpu

F32 = jnp.float32
BF16 = jnp.bfloat16
MM = jnp.bfloat16

HEAD = 64
LANES = 128
EPS = 1e-6
SCALE = 0.125
ROPE_THETA = 10000.0
DIL_PATTERNS = ((128, 1), (512, 4), (2048, 16))
ADAM_LR, ADAM_B1, ADAM_B2, ADAM_EPS, ADAM_WD, ADAM_STEP = 0.001, 0.9, 0.999, 1e-08, 0.01, 10

FOXQ, FOXK, FOXV = 0, 4, 8
SBQ, SBK, SBV = 12, 16, 20
DILQ, DILK, DILV = 24, 30, 36
GATE, FORGET, NBLK = 42, 66, 68
DPROJ = NBLK * LANES
O1, O2, O3, O4, DIN = 1536, 1544, 3080, 5384, 8456

VMEM_LIMIT = 56 * 1024 * 1024
MESH_ID = pl.DeviceIdType.MESH
ANY = pl.BlockSpec(memory_space=pl.ANY)


def _params(sem=None):
    return pltpu.CompilerParams(dimension_semantics=sem, vmem_limit_bytes=VMEM_LIMIT)


def _iota(shape, dim):
    return lax.broadcasted_iota(jnp.int32, shape, dim)


def _split2(x):
    hi = x.astype(BF16)
    lo = (x - hi.astype(F32)).astype(BF16)
    return hi, lo


def _split3(x):
    hi = x.astype(BF16)
    r = x - hi.astype(F32)
    mid = r.astype(BF16)
    lo = (r - mid.astype(F32)).astype(BF16)
    return hi, mid, lo


def _dot(a, b):
    return jnp.dot(a, b, preferred_element_type=F32)


def _dot_nt(a, b):
    return lax.dot_general(a, b, (((1,), (1,)), ((), ())), preferred_element_type=F32)


def _dot_tn(a, b):
    return lax.dot_general(a, b, (((0,), (0,)), ((), ())), preferred_element_type=F32)


def _xdot2(x, m):
    hi, lo = _split2(x)
    return _dot(hi, m) + _dot(lo, m)


def _xdot3(x, m):
    hi, mid, lo = _split3(x)
    return _dot(hi, m) + _dot(mid, m) + _dot(lo, m)


def _xdot3_left(m, x):
    hi, mid, lo = _split3(x)
    return _dot(m, hi) + _dot(m, mid) + _dot(m, lo)


def _head_mat(w):
    return ((_iota((w, w), 0) >> 6) == (_iota((w, w), 1) >> 6)).astype(BF16)


def _softplus_parts(z):
    e = jnp.exp(-jnp.abs(z))
    return e, jnp.maximum(z, 0.0) + jnp.log(1.0 + e)


def _fit(dim, want):
    t = min(want, dim)
    while dim % t:
        t -= LANES
        assert t > 0, (dim, want)
    return t


def matmul(a, b, *, ta=False, tb=False, out_dtype=F32, add=None, tm=2048, tn=512, tk=1024, dest=None, name):
    K, M = a.shape if ta else a.shape[::-1]
    K2, N = b.shape[::-1] if tb else b.shape
    assert K == K2, (a.shape, b.shape, ta, tb)
    tm, tn, tk = _fit(M, tm), _fit(N, tn), _fit(K, tk)
    nk = K // tk
    dn = (((0 if ta else 1,), (1 if tb else 0,)), ((), ()))
    if dest is None:
        tiles, source = N // tn, lambda j: j
    else:
        assert add is None and not tb
        buffer, tiles, source, place = dest

    def body(*refs):
        if dest is not None:
            a_ref, b_ref, _, o_ref, acc_ref = refs
        elif add is None:
            a_ref, b_ref, o_ref, acc_ref = refs
        else:
            a_ref, b_ref, add_ref, o_ref, acc_ref = refs
        k = pl.program_id(2)
        part = lax.dot_general(a_ref[...].astype(MM), b_ref[...].astype(MM), dn, preferred_element_type=F32)

        @pl.when(k == 0)
        def _():
            acc_ref[...] = part

        @pl.when(k > 0)
        def _():
            acc_ref[...] += part

        @pl.when(k == nk - 1)
        def _():
            r = acc_ref[...]
            if add is not None:
                r = r + add_ref[...]
            o_ref[...] = r.astype(o_ref.dtype).reshape(o_ref.shape)

    a_spec = pl.BlockSpec((tk, tm), lambda i, j, k: (k, i)) if ta else pl.BlockSpec((tm, tk), lambda i, j, k: (i, k))
    b_spec = pl.BlockSpec((tn, tk), lambda i, j, k: (j, k)) if tb else pl.BlockSpec((tk, tn), lambda i, j, k: (k, source(j)))
    o_spec = pl.BlockSpec((tm, tn), lambda i, j, k: (i, j))
    ins, specs, aliases = [a, b], [a_spec, b_spec], {}
    out_shape = jax.ShapeDtypeStruct((M, N), out_dtype)
    if add is not None:
        ins.append(add)
        specs.append(o_spec)
    if dest is not None:
        ins.append(buffer)
        specs.append(ANY)
        aliases = {2: 0}
        o_spec = pl.BlockSpec((1, tm, tn), lambda i, j, k: place(i, j))
        out_shape = jax.ShapeDtypeStruct(buffer.shape, buffer.dtype)
    return pl.pallas_call(
        body, name=name, grid=(M // tm, tiles, nk), in_specs=specs, out_specs=o_spec, out_shape=out_shape,
        scratch_shapes=[pltpu.VMEM((tm, tn), F32)], input_output_aliases=aliases,
        compiler_params=_params(("parallel", "parallel", "arbitrary")),
    )(*ins)


def _rows(n, want=512):
    t = min(want, n)
    assert n % t == 0, (n, t)
    return t


def rmsnorm_fwd(x, g, *, name):
    n, d = x.shape
    tr = _rows(n)

    def body(x_ref, g_ref, o_ref):
        xv = x_ref[...]
        r = lax.rsqrt(jnp.mean(xv * xv, axis=1, keepdims=True) + EPS)
        o_ref[...] = (xv * r * g_ref[...]).astype(o_ref.dtype)

    row = pl.BlockSpec((tr, d), lambda i: (i, 0))
    vec = pl.BlockSpec((1, d), lambda i: (0, 0))
    return pl.pallas_call(body, name=name, grid=(n // tr,), in_specs=[row, vec], out_specs=row,
                          out_shape=jax.ShapeDtypeStruct((n, d), MM), compiler_params=_params(("parallel",)))(x, g)


def rmsnorm_bwd(x, g, dh, dres, *, name):
    n, d = x.shape
    tr = _rows(n)

    def body(x_ref, g_ref, dh_ref, dr_ref, dx_ref, dg_ref):
        @pl.when(pl.program_id(0) == 0)
        def _():
            dg_ref[...] = jnp.zeros_like(dg_ref)

        xv = x_ref[...]
        r = lax.rsqrt(jnp.mean(xv * xv, axis=1, keepdims=True) + EPS)
        y = xv * r
        dhv = dh_ref[...]
        dy = dhv * g_ref[...]
        dx_ref[...] = dr_ref[...] + r * (dy - y * jnp.mean(dy * y, axis=1, keepdims=True))
        dg_ref[...] += jnp.sum(dhv * y, axis=0, keepdims=True)

    row = pl.BlockSpec((tr, d), lambda i: (i, 0))
    vec = pl.BlockSpec((1, d), lambda i: (0, 0))
    return pl.pallas_call(
        body, name=name, grid=(n // tr,), in_specs=[row, vec, row, row], out_specs=[row, vec],
        out_shape=[jax.ShapeDtypeStruct((n, d), F32), jax.ShapeDtypeStruct((1, d), F32)],
        compiler_params=_params(("arbitrary",)))(x, g, dh, dres)


def loss_grad(y, tgt, *, name):
    n, d = y.shape
    tr = _rows(n)

    def body(y_ref, t_ref, dy_ref, acc_ref):
        @pl.when(pl.program_id(0) == 0)
        def _():
            acc_ref[...] = jnp.zeros_like(acc_ref)

        e = y_ref[...] - t_ref[...]
        dy_ref[...] = e * (1.0 / d)
        acc_ref[...] += jnp.sum(e * e, axis=0, keepdims=True)

    row = pl.BlockSpec((tr, d), lambda i: (i, 0))
    vec = pl.BlockSpec((1, d), lambda i: (0, 0))
    return pl.pallas_call(
        body, name=name, grid=(n // tr,), in_specs=[row, row], out_specs=[row, vec],
        out_shape=[jax.ShapeDtypeStruct((n, d), F32), jax.ShapeDtypeStruct((1, d), F32)],
        compiler_params=_params(("arbitrary",)))(y, tgt)


def relu2_fwd(u, *, name):
    n, d = u.shape
    tr = _rows(n)

    def body(u_ref, o_ref):
        r = jnp.maximum(u_ref[...], 0.0)
        o_ref[...] = (r * r).astype(o_ref.dtype)

    row = pl.BlockSpec((tr, d), lambda i: (i, 0))
    return pl.pallas_call(body, name=name, grid=(n // tr,), in_specs=[row], out_specs=row,
                          out_shape=jax.ShapeDtypeStruct((n, d), MM), compiler_params=_params(("parallel",)))(u)


def relu2_bwd(u, da, *, name):
    n, d = u.shape
    tr = _rows(n)

    def body(u_ref, da_ref, o_ref):
        o_ref[...] = (da_ref[...] * (2.0 * jnp.maximum(u_ref[...], 0.0))).astype(o_ref.dtype)

    row = pl.BlockSpec((tr, d), lambda i: (i, 0))
    return pl.pallas_call(body, name=name, grid=(n // tr,), in_specs=[row, row], out_specs=row,
                          out_shape=jax.ShapeDtypeStruct((n, d), MM), compiler_params=_params(("parallel",)))(u, da)


MERGE_W = 256


def _gate_specs(tr, d):
    per = d // MERGE_W
    base = GATE * LANES // MERGE_W
    return [pl.BlockSpec((tr, MERGE_W), functools.partial(lambda i, j, b: (i, base + per * b + j), b=b)) for b in range(3)]


def merge_fwd(proj, ys, *, name):
    n, d = ys[0].shape
    tr = _rows(n)

    def body(g0, g1, g2, y0, y1, y2, o_ref):
        acc = jax.nn.sigmoid(g0[...]) * y0[...]
        acc += jax.nn.sigmoid(g1[...]) * y1[...]
        acc += jax.nn.sigmoid(g2[...]) * y2[...]
        o_ref[...] = acc.astype(o_ref.dtype)

    blk = pl.BlockSpec((tr, MERGE_W), lambda i, j: (i, j))
    return pl.pallas_call(
        body, name=name, grid=(n // tr, d // MERGE_W), in_specs=_gate_specs(tr, d) + [blk] * 3, out_specs=blk,
        out_shape=jax.ShapeDtypeStruct((n, d), MM), compiler_params=_params(("parallel", "parallel")))(proj, proj, proj, *ys)


def merge_bwd(proj, ys, dm, *, name):
    n, d = dm.shape
    tr = _rows(n)

    def body(g0, g1, g2, y0, y1, y2, dm_ref, dy0, dy1, dy2, dg0, dg1, dg2):
        dmv = dm_ref[...]
        for g, y, dy, dg in ((g0, y0, dy0, dg0), (g1, y1, dy1, dg1), (g2, y2, dy2, dg2)):
            s = jax.nn.sigmoid(g[...])
            dy[...] = (dmv * s).astype(dy.dtype)
            dg[...] = (dmv * y[...] * s * (1.0 - s)).astype(dg.dtype)

    blk = pl.BlockSpec((tr, MERGE_W), lambda i, j: (i, j))
    out = jax.ShapeDtypeStruct((n, d), MM)
    return pl.pallas_call(
        body, name=name, grid=(n // tr, d // MERGE_W), in_specs=_gate_specs(tr, d) + [blk] * 4, out_specs=[blk] * 6,
        out_shape=[out] * 6, compiler_params=_params(("parallel", "parallel")))(proj, proj, proj, *ys, dm)


def adamw(w, g, m, v, *, name):
    r, c = w.shape
    tr = r
    while tr * c * 4 > (1 << 21) and tr % 16 == 0:
        tr //= 2
    c1 = 1.0 / (1.0 - ADAM_B1 ** ADAM_STEP)
    c2 = 1.0 / (1.0 - ADAM_B2 ** ADAM_STEP)

    def body(w_ref, g_ref, m_ref, v_ref, d_ref, mo_ref, vo_ref):
        gv = g_ref[...]
        m2 = ADAM_B1 * m_ref[...] + (1.0 - ADAM_B1) * gv
        v2 = ADAM_B2 * v_ref[...] + (1.0 - ADAM_B2) * (gv * gv)
        d_ref[...] = -ADAM_LR * ((m2 * c1) / (jnp.sqrt(v2 * c2) + ADAM_EPS) + ADAM_WD * w_ref[...])
        mo_ref[...] = m2
        vo_ref[...] = v2

    blk = pl.BlockSpec((tr, c), lambda i: (i, 0))
    out = jax.ShapeDtypeStruct((r, c), F32)
    return pl.pallas_call(body, name=name, grid=(r // tr,), in_specs=[blk] * 4, out_specs=[blk] * 3, out_shape=[out] * 3,
                          compiler_params=_params(("parallel",)))(w, g, m, v)


def rope_table(pos, inv, *, name):
    n = pos.shape[0]
    tr = _rows(n)

    def body(p_ref, i_ref, c_ref, s_ref):
        ang = p_ref[...].astype(F32) * i_ref[...]
        c_ref[...] = jnp.cos(ang)
        s_ref[...] = jnp.sin(ang)

    out = jax.ShapeDtypeStruct((n, LANES), F32)
    blk = pl.BlockSpec((tr, LANES), lambda i: (i, 0))
    return pl.pallas_call(
        body, name=name, grid=(n // tr,), in_specs=[pl.BlockSpec((tr, 1), lambda i: (i, 0)), pl.BlockSpec((1, LANES), lambda i: (0, 0))],
        out_specs=[blk, blk], out_shape=[out, out], compiler_params=_params(("parallel",)))(pos, inv)


def _rot_half(x):
    first = (_iota((1, LANES), 1) & 63) < 32
    return jnp.where(first, -pltpu.roll(x, LANES - 32, axis=1), pltpu.roll(x, 32, axis=1))


def _head_norm(xv, gm):
    r = lax.rsqrt(_xdot2(xv * xv, gm) * (1.0 / HEAD) + EPS)
    return r, xv * r


def _head_norm_bwd(xh, r, dxh, gm):
    return r * (dxh - xh * (_xdot2(dxh * xh, gm) * (1.0 / HEAD)))


def fox_prep_fwd(proj, qg, kg, bf, *, bsz, seq, name):
    n = bsz * seq
    tr = min(256, seq)
    nt = seq // tr
    w = 4 * LANES

    def body(q_ref, k_ref, f_ref, qg_ref, kg_ref, b_ref, qn_ref, kn_ref, fb_ref, f8_ref, carry):
        @pl.when(pl.program_id(1) == 0)
        def _():
            carry[...] = jnp.zeros_like(carry)

        gm = _head_mat(LANES)
        for src, gain, dst in ((q_ref, qg_ref, qn_ref), (k_ref, kg_ref, kn_ref)):
            for c in range(4):
                sl = slice(c * LANES, (c + 1) * LANES)
                _, xh = _head_norm(src[:, sl], gm)
                dst[:, sl] = (xh * gain[:, sl]).astype(dst.dtype)
        logf = jax.nn.log_sigmoid(f_ref[...] + b_ref[...])
        lower = (_iota((tr, tr), 1) <= _iota((tr, tr), 0)).astype(BF16)
        fcum = _xdot3_left(lower, logf) + carry[...]
        carry[...] = fcum[tr - 1:tr, :]
        f8_ref[...] = fcum
        spread = (_iota((LANES, w), 0) == (_iota((LANES, w), 1) >> 6)).astype(BF16)
        fb_ref[...] = _xdot3(fcum, spread)

    row = lambda width, blk: pl.BlockSpec((tr, width), lambda b, t: (b * nt + t, blk))
    vec = lambda width: pl.BlockSpec((1, width), lambda b, t: (0, 0))
    return pl.pallas_call(
        body, name=name, grid=(bsz, nt),
        in_specs=[row(w, FOXQ // 4), row(w, FOXK // 4), row(LANES, FORGET), vec(w), vec(w), vec(LANES)],
        out_specs=[row(w, 0), row(w, 0), row(w, 0), row(LANES, 0)],
        out_shape=[jax.ShapeDtypeStruct((n, w), MM), jax.ShapeDtypeStruct((n, w), MM),
                   jax.ShapeDtypeStruct((n, w), F32), jax.ShapeDtypeStruct((n, LANES), F32)],
        scratch_shapes=[pltpu.VMEM((1, LANES), F32)],
        compiler_params=_params(("parallel", "arbitrary")))(proj, proj, proj, qg, kg, bf)


def fox_prep_bwd(proj, qg, kg, bf, dqn, dkn, df, *, bsz, seq, name):
    n = bsz * seq
    tr = min(256, seq)
    nt = seq // tr
    w = 4 * LANES

    def body(q_ref, k_ref, f_ref, qg_ref, kg_ref, b_ref, dqn_ref, dkn_ref, df_ref,
             dq_ref, dk_ref, dl_ref, dqg_ref, dkg_ref, db_ref, carry):
        first = (pl.program_id(0) == 0) & (pl.program_id(1) == 0)

        @pl.when(first)
        def _():
            dqg_ref[...] = jnp.zeros_like(dqg_ref)
            dkg_ref[...] = jnp.zeros_like(dkg_ref)
            db_ref[...] = jnp.zeros_like(db_ref)

        @pl.when(pl.program_id(1) == 0)
        def _():
            carry[...] = jnp.zeros_like(carry)

        gm = _head_mat(LANES)
        for src, gain, dy_ref, dx_ref, dg_ref in ((q_ref, qg_ref, dqn_ref, dq_ref, dqg_ref), (k_ref, kg_ref, dkn_ref, dk_ref, dkg_ref)):
            for c in range(4):
                sl = slice(c * LANES, (c + 1) * LANES)
                r, xh = _head_norm(src[:, sl], gm)
                dy = dy_ref[:, sl]
                dg_ref[:, sl] += jnp.sum(dy * xh, axis=0, keepdims=True)
                dx_ref[:, sl] = _head_norm_bwd(xh, r, dy * gain[:, sl], gm).astype(dx_ref.dtype)
        upper = (_iota((tr, tr), 1) >= _iota((tr, tr), 0)).astype(BF16)
        dlogf = _xdot3_left(upper, df_ref[...]) + carry[...]
        carry[...] = dlogf[0:1, :]
        dlogit = dlogf * jax.nn.sigmoid(-(f_ref[...] + b_ref[...]))
        dl_ref[:, 0:LANES] = dlogit.astype(dl_ref.dtype)
        dl_ref[:, LANES:2 * LANES] = jnp.zeros((tr, LANES), dl_ref.dtype)
        db_ref[...] += jnp.sum(dlogit, axis=0, keepdims=True)

    row = lambda width, blk: pl.BlockSpec((tr, width), lambda b, t: (b * nt + nt - 1 - t, blk))
    vec = lambda width: pl.BlockSpec((1, width), lambda b, t: (0, 0))
    return pl.pallas_call(
        body, name=name, grid=(bsz, nt),
        in_specs=[row(w, FOXQ // 4), row(w, FOXK // 4), row(LANES, FORGET), vec(w), vec(w), vec(LANES),
                  row(w, 0), row(w, 0), row(LANES, 0)],
        out_specs=[row(w, 0), row(w, 0), row(2 * LANES, 0), vec(w), vec(w), vec(LANES)],
        out_shape=[jax.ShapeDtypeStruct((n, w), MM), jax.ShapeDtypeStruct((n, w), MM), jax.ShapeDtypeStruct((n, 2 * LANES), MM),
                   jax.ShapeDtypeStruct((1, w), F32), jax.ShapeDtypeStruct((1, w), F32), jax.ShapeDtypeStruct((1, LANES), F32)],
        scratch_shapes=[pltpu.VMEM((1, LANES), F32)],
        compiler_params=_params(("arbitrary", "arbitrary")))(proj, proj, proj, qg, kg, bf, dqn, dkn, df)


DIL_W = 6 * LANES


def dil_prep_fwd(proj, qg, kg, cos, sin, *, name):
    n = proj.shape[0]
    tr = _rows(n, 256)

    def body(q_ref, k_ref, qg_ref, kg_ref, c_ref, s_ref, qo_ref, ko_ref):
        gm = _head_mat(LANES)
        cv, sv = c_ref[...], s_ref[...]
        for src, gain, dst in ((q_ref, qg_ref, qo_ref), (k_ref, kg_ref, ko_ref)):
            for c in range(6):
                sl = slice(c * LANES, (c + 1) * LANES)
                _, xh = _head_norm(src[:, sl], gm)
                xn = xh * gain[:, sl]
                dst[:, sl] = (xn * cv + _rot_half(xn) * sv).astype(dst.dtype)

    row = lambda width, blk: pl.BlockSpec((tr, width), lambda i: (i, blk))
    vec = pl.BlockSpec((1, DIL_W), lambda i: (0, 0))
    out = jax.ShapeDtypeStruct((n, DIL_W), MM)
    return pl.pallas_call(
        body, name=name, grid=(n // tr,),
        in_specs=[row(DIL_W, DILQ // 6), row(DIL_W, DILK // 6), vec, vec, row(LANES, 0), row(LANES, 0)],
        out_specs=[row(DIL_W, 0), row(DIL_W, 0)], out_shape=[out, out],
        compiler_params=_params(("parallel",)))(proj, proj, qg, kg, cos, sin)


def dil_prep_bwd(proj, qg, kg, cos, sin, dqr, dkr, *, name):
    n = proj.shape[0]
    tr = _rows(n, 256)

    def body(q_ref, k_ref, qg_ref, kg_ref, c_ref, s_ref, dqr_ref, dkr_ref, dq_ref, dk_ref, dqg_ref, dkg_ref):
        @pl.when(pl.program_id(0) == 0)
        def _():
            dqg_ref[...] = jnp.zeros_like(dqg_ref)
            dkg_ref[...] = jnp.zeros_like(dkg_ref)

        gm = _head_mat(LANES)
        cv, sv = c_ref[...], s_ref[...]
        for src, gain, dy_ref, dx_ref, dg_ref in ((q_ref, qg_ref, dqr_ref, dq_ref, dqg_ref), (k_ref, kg_ref, dkr_ref, dk_ref, dkg_ref)):
            for c in range(6):
                sl = slice(c * LANES, (c + 1) * LANES)
                r, xh = _head_norm(src[:, sl], gm)
                dy = dy_ref[:, sl]
                dxn = dy * cv - _rot_half(dy * sv)
                dg_ref[:, sl] += jnp.sum(dxn * xh, axis=0, keepdims=True)
                dx_ref[:, sl] = _head_norm_bwd(xh, r, dxn * gain[:, sl], gm).astype(dx_ref.dtype)

    row = lambda width, blk: pl.BlockSpec((tr, width), lambda i: (i, blk))
    vec = pl.BlockSpec((1, DIL_W), lambda i: (0, 0))
    out = jax.ShapeDtypeStruct((n, DIL_W), MM)
    gout = jax.ShapeDtypeStruct((1, DIL_W), F32)
    return pl.pallas_call(
        body, name=name, grid=(n // tr,),
        in_specs=[row(DIL_W, DILQ // 6), row(DIL_W, DILK // 6), vec, vec, row(LANES, 0), row(LANES, 0), row(DIL_W, 0), row(DIL_W, 0)],
        out_specs=[row(DIL_W, 0), row(DIL_W, 0), vec, vec], out_shape=[out, out, gout, gout],
        compiler_params=_params(("arbitrary",)))(proj, proj, qg, kg, cos, sin, dqr, dkr)


def dil_combine_fwd(os_, lses, *, name):
    n, w = os_[0].shape
    tr = _rows(n)

    def body(o0, o1, o2, l0, l1, l2, out_ref):
        a, b, c = l0[...], l1[...], l2[...]
        m = jnp.maximum(jnp.maximum(a, b), c)
        ea, eb, ec = jnp.exp(a - m), jnp.exp(b - m), jnp.exp(c - m)
        out_ref[...] = ((ea * o0[...] + eb * o1[...] + ec * o2[...]) / (ea + eb + ec)).astype(out_ref.dtype)

    blk = pl.BlockSpec((tr, w), lambda i: (i, 0))
    return pl.pallas_call(body, name=name, grid=(n // tr,), in_specs=[blk] * 6, out_specs=blk,
                          out_shape=jax.ShapeDtypeStruct((n, w), MM), compiler_params=_params(("parallel",)))(*os_, *lses)


def dil_combine_bwd(os_, lses, dout, *, name):
    n, w = dout.shape
    tr = _rows(n)

    def body(o0, o1, o2, l0, l1, l2, d_ref, do0, do1, do2, dl0, dl1, dl2):
        a, b, c = l0[...], l1[...], l2[...]
        m = jnp.maximum(jnp.maximum(a, b), c)
        es = [jnp.exp(a - m), jnp.exp(b - m), jnp.exp(c - m)]
        inv = 1.0 / (es[0] + es[1] + es[2])
        ws = [e * inv for e in es]
        dv = d_ref[...]
        gm = _head_mat(w)
        dws = [_xdot2(dv * o[...], gm) for o in (o0, o1, o2)]
        mean = ws[0] * dws[0] + ws[1] * dws[1] + ws[2] * dws[2]
        for wg, dw, do, dl in zip(ws, dws, (do0, do1, do2), (dl0, dl1, dl2)):
            do[...] = wg * dv
            dl[...] = wg * (dw - mean)

    blk = pl.BlockSpec((tr, w), lambda i: (i, 0))
    out = jax.ShapeDtypeStruct((n, w), F32)
    return pl.pallas_call(body, name=name, grid=(n // tr,), in_specs=[blk] * 7, out_specs=[blk] * 6, out_shape=[out] * 6,
                          compiler_params=_params(("parallel",)))(*os_, *lses, dout)


def _key_plan(qi, tq, seq, window, run):
    if window + tq >= seq:
        for bi in range(seq // tq):
            lo = bi * tq
            segs = ([(0, lo, "bulk")] if lo else []) + [(lo, tq, "diag")]
            pl.when(qi == bi)(functools.partial(run, segs))
    else:
        ext = window + tq
        run([(pl.multiple_of(jnp.maximum((qi + 1) * tq - ext, 0), LANES), ext, "band")])


def _seg_mask(seg, qi, tq, window, dilation, strict=False):
    start, width, kind = seg
    d = _iota((tq, width), 0) - _iota((tq, width), 1)
    if kind == "bulk":
        d = d + width
    elif kind == "band":
        d = d + (qi * tq - start)
    ok = None
    if kind != "bulk":
        ok = (d > 0) if strict else (d >= 0)
    if window is not None:
        ok = (d <= window) if ok is None else ok & (d <= window)
    if dilation > 1:
        on_grid = (d & (dilation - 1)) == 0
        ok = on_grid if ok is None else ok & on_grid
    return ok


def _lane_first():
    return _iota((1, LANES), 1) < HEAD


def _attn_specs(bsz, seq, tq, qo, ko, vo):
    nq = seq // tq
    qspec = lambda off: pl.BlockSpec((tq, LANES), lambda b, j, i: (b * nq + i, off + j))
    kspec = lambda off: pl.BlockSpec((seq, LANES), lambda b, j, i: (b, off + j))
    return nq, qspec, kspec


def softmax_attn_fwd(q, k, v, bias, *, qo, ko, vo, pairs, bsz, seq, window, dilation, tq, name):
    n = bsz * seq
    nq, qspec, kspec = _attn_specs(bsz, seq, tq, qo, ko, vo)

    def body(*refs):
        if bias is None:
            q_ref, k_ref, v_ref, o_ref, l_ref = refs
        else:
            q_ref, k_ref, v_ref, fq_ref, fk_ref, o_ref, l_ref = refs
        qi = pl.program_id(2)

        def run(segs):
            qv = (q_ref[...] * SCALE).astype(MM)
            first = _lane_first()
            keys = [(k_ref[pl.ds(st, w), :].astype(MM), v_ref[pl.ds(st, w), :].astype(MM),
                     _seg_mask((st, w, kind), qi, tq, None if window >= seq else window, dilation), st, w)
                    for st, w, kind in segs]
            outs, lses = [], []
            for a in range(2):
                qa = jnp.where(first if a == 0 else ~first, qv, jnp.zeros_like(qv))
                scores = []
                for kv, _, ok, st, w in keys:
                    s = _dot_nt(qa, kv)
                    if bias is not None:
                        s = s + fq_ref[:, a * HEAD:a * HEAD + 1] - fk_ref[a:a + 1, pl.ds(st, w)]
                    scores.append(s if ok is None else jnp.where(ok, s, -jnp.inf))
                m = functools.reduce(jnp.maximum, [jnp.max(s, axis=1, keepdims=True) for s in scores])
                ps = [jnp.exp(s - m) for s in scores]
                den = sum(jnp.sum(p, axis=1, keepdims=True) for p in ps)
                acc = sum(_dot(p.astype(MM), vv) for p, (_, vv, _, _, _) in zip(ps, keys))
                outs.append(acc / den)
                lses.append(m + jnp.log(den))
            o_ref[...] = jnp.where(first, outs[0], outs[1]).astype(o_ref.dtype)
            l_ref[...] = jnp.where(first, lses[0], lses[1])

        _key_plan(qi, tq, seq, window, run)

    ins, specs = [q, k, v], [qspec(qo), kspec(ko), kspec(vo)]
    if bias is not None:
        ins += list(bias)
        specs += [qspec(0), pl.BlockSpec((8, seq), lambda b, j, i: (b * pairs + j, 0))]
    out = jax.ShapeDtypeStruct((n, LANES * pairs), F32)
    return pl.pallas_call(
        body, name=name, grid=(bsz, pairs, nq), in_specs=specs, out_specs=[qspec(0), qspec(0)], out_shape=[out, out],
        compiler_params=_params(("parallel", "parallel", "arbitrary")))(*ins)


def softmax_attn_bwd(q, k, v, o, do, lse, dlse, bias, *, qo, ko, vo, pairs, bsz, seq, window, dilation, tq, dq_dtype, dk_dtype, name):
    n = bsz * seq
    nq, qspec, kspec = _attn_specs(bsz, seq, tq, qo, ko, vo)
    has_bias, has_dlse = bias is not None, dlse is not None

    def body(*refs):
        refs = list(refs)
        q_ref, k_ref, v_ref, o_ref, do_ref, l_ref = refs[:6]
        del refs[:6]
        dl_ref = refs.pop(0) if has_dlse else None
        fq_ref, fk_ref = (refs.pop(0), refs.pop(0)) if has_bias else (None, None)
        dq_ref, dk_ref, dv_ref = refs[:3]
        del refs[:3]
        dfq_ref, dfk_ref = (refs.pop(0), refs.pop(0)) if has_bias else (None, None)
        dk_acc, dv_acc = refs
        qi = pl.program_id(2)

        @pl.when(qi == 0)
        def _():
            dk_acc[...] = jnp.zeros_like(dk_acc)
            dv_acc[...] = jnp.zeros_like(dv_acc)
            if has_bias:
                dfk_ref[...] = jnp.zeros_like(dfk_ref)

        def run(segs):
            qv = (q_ref[...] * SCALE).astype(MM)
            dov = do_ref[...]
            dob = dov.astype(MM)
            prod = dov * o_ref[...]
            first = _lane_first()
            keys = [(k_ref[pl.ds(st, w), :].astype(MM), v_ref[pl.ds(st, w), :].astype(MM),
                     _seg_mask((st, w, kind), qi, tq, None if window >= seq else window, dilation), st, w)
                    for st, w, kind in segs]
            dqs, dfqs = [], []
            dks, dvs = [[] for _ in keys], [[] for _ in keys]
            for a in range(2):
                mine = first if a == 0 else ~first
                col = slice(a * HEAD, a * HEAD + 1)
                delta = jnp.sum(jnp.where(mine, prod, 0.0), axis=1, keepdims=True)
                if has_dlse:
                    delta = delta - dl_ref[:, col]
                qa = jnp.where(mine, qv, jnp.zeros_like(qv))
                doa = jnp.where(mine, dob, jnp.zeros_like(dob))
                shift = l_ref[:, col]
                if has_bias:
                    shift = shift - fq_ref[:, col]
                dq, dfq = 0.0, 0.0
                for si, (kv, vv, ok, st, w) in enumerate(keys):
                    s = _dot_nt(qa, kv)
                    if has_bias:
                        s = s - fk_ref[a:a + 1, pl.ds(st, w)]
                    p = jnp.exp(s - shift)
                    if ok is not None:
                        p = jnp.where(ok, p, 0.0)
                    ds = p * (_dot_nt(doa, vv) - delta)
                    dsb = ds.astype(MM)
                    dvs[si].append(_dot_tn(p.astype(MM), dob))
                    dks[si].append(_dot_tn(dsb, qv))
                    dq = dq + _dot(dsb, kv)
                    if has_bias:
                        dfq = dfq + jnp.sum(ds, axis=1, keepdims=True)
                        dfk_ref[a:a + 1, pl.ds(st, w)] += jnp.sum(ds, axis=0, keepdims=True)
                dqs.append(dq * SCALE)
                dfqs.append(dfq)
            dq_ref[...] = jnp.where(first, dqs[0], dqs[1]).astype(dq_ref.dtype)
            for (_, _, _, st, w), dk, dv in zip(keys, dks, dvs):
                dk_acc[pl.ds(st, w), :] += jnp.where(first, dk[0], dk[1])
                dv_acc[pl.ds(st, w), :] += jnp.where(first, dv[0], dv[1])
            if has_bias:
                dfq_ref[...] = jnp.where(first, dfqs[0], dfqs[1])

        _key_plan(qi, tq, seq, window, run)

        @pl.when(qi == nq - 1)
        def _():
            dk_ref[...] = dk_acc[...].astype(dk_ref.dtype)
            dv_ref[...] = dv_acc[...].astype(dv_ref.dtype)

    wide = LANES * pairs
    ins = [q, k, v, o, do, lse]
    specs = [qspec(qo), kspec(ko), kspec(vo), qspec(0), qspec(0), qspec(0)]
    outs = [jax.ShapeDtypeStruct((n, wide), dq_dtype), jax.ShapeDtypeStruct((n, wide), dk_dtype), jax.ShapeDtypeStruct((n, wide), MM)]
    out_specs = [qspec(0), kspec(0), kspec(0)]
    if has_dlse:
        ins.append(dlse)
        specs.append(qspec(0))
    if has_bias:
        rows = pl.BlockSpec((8, seq), lambda b, j, i: (b * pairs + j, 0))
        ins += list(bias)
        specs += [qspec(0), rows]
        outs += [jax.ShapeDtypeStruct((n, wide), F32), jax.ShapeDtypeStruct((bsz * pairs * 8, seq), F32)]
        out_specs += [qspec(0), rows]
    return pl.pallas_call(
        body, name=name, grid=(bsz, pairs, nq), in_specs=specs, out_specs=out_specs, out_shape=outs,
        scratch_shapes=[pltpu.VMEM((seq, LANES), F32), pltpu.VMEM((seq, LANES), F32)],
        compiler_params=_params(("parallel", "parallel", "arbitrary")))(*ins)


def _running_sum(vals, mat, carry, lat_ref, start, reverse):
    nb = vals.shape[1] // LANES
    for cb in (reversed(range(nb)) if reverse else range(nb)):
        blk = vals[:, cb * LANES:(cb + 1) * LANES]
        lat_ref[:, start + cb * LANES:start + (cb + 1) * LANES] = _xdot2(blk, mat) + carry
        carry = carry + jnp.sum(blk, axis=1, keepdims=True)
    return carry


def _sb_weights(qa, keys, tq, lat_ref):
    after = (_iota((LANES, LANES), 0) > _iota((LANES, LANES), 1)).astype(BF16)
    carry = jnp.zeros((tq, 1), F32)
    logs = []
    for kv, ok, st, w in reversed(keys):
        z = _dot_nt(qa, kv)
        _, sp = _softplus_parts(z)
        visible = sp if ok is None else jnp.where(ok, sp, 0.0)
        carry = _running_sum(visible, after, carry, lat_ref, st, True)
        logs.append(z - sp)
    out = []
    for (kv, ok, st, w), log_beta in zip(keys, reversed(logs)):
        att = jnp.exp(log_beta - lat_ref[:, st:st + w])
        out.append((log_beta, att if ok is None else jnp.where(ok, att, 0.0)))
    return out


def _sb_keys(k_ref, v_ref, segs, qi, tq):
    return [(k_ref[st:st + w, :].astype(MM), v_ref[st:st + w, :].astype(MM),
             _seg_mask((st, w, kind), qi, tq, None, 1, strict=True), st, w) for st, w, kind in segs]


def sb_attn_fwd(proj, *, bsz, seq, tq, name):
    n = bsz * seq
    pairs = 4
    nq, qspec, kspec = _attn_specs(bsz, seq, tq, SBQ, SBK, SBV)

    def body(q_ref, k_ref, v_ref, o_ref, lat_ref):
        qi = pl.program_id(2)

        def run(segs):
            qv = (q_ref[...] * SCALE).astype(MM)
            keys = _sb_keys(k_ref, v_ref, segs, qi, tq)
            first = _lane_first()
            outs = []
            for a in range(2):
                qa = jnp.where(first if a == 0 else ~first, qv, jnp.zeros_like(qv))
                weights = _sb_weights(qa, [(kv, ok, st, w) for kv, _, ok, st, w in keys], tq, lat_ref)
                outs.append(sum(_dot(att.astype(MM), vv) for (_, att), (_, vv, _, _, _) in zip(weights, keys)))
            o_ref[...] = jnp.where(first, outs[0], outs[1]).astype(o_ref.dtype)

        _key_plan(qi, tq, seq, seq, run)

    return pl.pallas_call(
        body, name=name, grid=(bsz, pairs, nq), in_specs=[qspec(SBQ), kspec(SBK), kspec(SBV)], out_specs=qspec(0),
        out_shape=jax.ShapeDtypeStruct((n, LANES * pairs), MM), scratch_shapes=[pltpu.VMEM((tq, seq), F32)],
        compiler_params=_params(("parallel", "parallel", "arbitrary")))(proj, proj, proj)


def sb_attn_bwd(proj, do, *, bsz, seq, tq, name):
    n = bsz * seq
    pairs = 4
    nq, qspec, kspec = _attn_specs(bsz, seq, tq, SBQ, SBK, SBV)

    def body(q_ref, k_ref, v_ref, do_ref, dq_ref, dk_ref, dv_ref, lat_ref, dk_acc, dv_acc):
        qi = pl.program_id(2)

        @pl.when(qi == 0)
        def _():
            dk_acc[...] = jnp.zeros_like(dk_acc)
            dv_acc[...] = jnp.zeros_like(dv_acc)

        def run(segs):
            qv = (q_ref[...] * SCALE).astype(MM)
            keys = _sb_keys(k_ref, v_ref, segs, qi, tq)
            dob = do_ref[...].astype(MM)
            first = _lane_first()
            before = (_iota((LANES, LANES), 0) < _iota((LANES, LANES), 1)).astype(BF16)
            dqs = []
            dks, dvs = [[] for _ in keys], [[] for _ in keys]
            for a in range(2):
                mine = first if a == 0 else ~first
                qa = jnp.where(mine, qv, jnp.zeros_like(qv))
                doa = jnp.where(mine, dob, jnp.zeros_like(dob))
                weights = _sb_weights(qa, [(kv, ok, st, w) for kv, _, ok, st, w in keys], tq, lat_ref)
                gs = [_dot_nt(doa, vv) * att for (_, att), (_, vv, _, _, _) in zip(weights, keys)]
                carry = jnp.zeros((tq, 1), F32)
                for g, (_, _, _, st, w) in zip(gs, keys):
                    carry = _running_sum(g, before, carry, lat_ref, st, False)
                dq = 0.0
                for si, ((log_beta, att), g, (kv, _, ok, st, w)) in enumerate(zip(weights, gs, keys)):
                    dz = g - jnp.exp(log_beta) * (g + lat_ref[:, st:st + w])
                    dz = (dz if ok is None else jnp.where(ok, dz, 0.0)).astype(MM)
                    dvs[si].append(_dot_tn(att.astype(MM), dob))
                    dks[si].append(_dot_tn(dz, qv))
                    dq = dq + _dot(dz, kv)
                dqs.append(dq * SCALE)
            dq_ref[...] = jnp.where(first, dqs[0], dqs[1]).astype(dq_ref.dtype)
            for (_, _, _, st, w), dk, dv in zip(keys, dks, dvs):
                dk_acc[st:st + w, :] += jnp.where(first, dk[0], dk[1])
                dv_acc[st:st + w, :] += jnp.where(first, dv[0], dv[1])

        _key_plan(qi, tq, seq, seq, run)

        @pl.when(qi == nq - 1)
        def _():
            dk_ref[...] = dk_acc[...].astype(dk_ref.dtype)
            dv_ref[...] = dv_acc[...].astype(dv_ref.dtype)

    out = jax.ShapeDtypeStruct((n, LANES * pairs), MM)
    return pl.pallas_call(
        body, name=name, grid=(bsz, pairs, nq), in_specs=[qspec(SBQ), kspec(SBK), kspec(SBV), qspec(0)],
        out_specs=[qspec(0), kspec(0), kspec(0)], out_shape=[out, out, out],
        scratch_shapes=[pltpu.VMEM((tq, seq), F32), pltpu.VMEM((seq, LANES), F32), pltpu.VMEM((seq, LANES), F32)],
        compiler_params=_params(("parallel", "parallel", "arbitrary")))(proj, proj, proj, do)


def _place():
    return lax.axis_index("x"), lax.axis_index("y"), lax.axis_index("c")


def _other_chips(x, y):
    return [(1 - x, y), (x, 1 - y), (1 - x, 1 - y)]


def _remote(src, dst, send_sems, recv_sems, k, to):
    return pltpu.make_async_remote_copy(src_ref=src, dst_ref=dst, send_sem=send_sems.at[k], recv_sem=recv_sems.at[k],
                                        device_id=to, device_id_type=MESH_ID)


def gather_chips(arrs, *, name):
    na = len(arrs)

    def body(*refs):
        ins, outs = refs[:na], refs[na:2 * na]
        send_sems, recv_sems = refs[2 * na:]
        x, y, c = _place()
        me, sibling = 2 * x + y, (x, y, 1 - c)
        chips = _other_chips(x, y)
        sends = []
        for t in range(na):
            rh = ins[t].shape[0] // 2
            half = lambda chip, h, t=t, rh=rh: outs[t].at[chip, pl.ds(h * rh, rh), :]
            for j, (px, py) in enumerate(chips):
                cp = _remote(ins[t].at[pl.ds(c * rh, rh), :], half(me, c), send_sems, recv_sems, 6 * t + j, (px, py, c))
                cp.start()
                sends.append(cp)
        for t in range(na):
            rh = ins[t].shape[0] // 2
            half = lambda chip, h, t=t, rh=rh: outs[t].at[chip, pl.ds(h * rh, rh), :]
            for j, (px, py) in enumerate(chips):
                landed = half(2 * px + py, c)
                _remote(landed, landed, send_sems, recv_sems, 6 * t + j, (px, py, c)).wait_recv()
                fw = _remote(landed, landed, send_sems, recv_sems, 6 * t + 3 + j, sibling)
                fw.start()
                sends.append(fw)
        for t in range(na):
            rh = ins[t].shape[0] // 2
            half = lambda chip, h, t=t, rh=rh: outs[t].at[chip, pl.ds(h * rh, rh), :]
            for j, (px, py) in enumerate(chips):
                passed = half(2 * px + py, 1 - c)
                _remote(passed, passed, send_sems, recv_sems, 6 * t + 3 + j, sibling).wait_recv()
        for cp in sends:
            cp.wait_send()

    for a in arrs:
        assert a.ndim == 2 and a.shape[0] % 32 == 0, a.shape
    return pl.pallas_call(
        body, name=name, in_specs=[ANY] * na, out_specs=[ANY] * na,
        out_shape=[jax.ShapeDtypeStruct((4,) + a.shape, a.dtype) for a in arrs],
        scratch_shapes=[pltpu.SemaphoreType.DMA((6 * na,)), pltpu.SemaphoreType.DMA((6 * na,))],
    )(*arrs)


def _chunk_rows(rows, cols, limit):
    best = 16
    for t in range(16, rows + 1, 16):
        if rows % t == 0 and t * cols * 4 <= limit:
            best = t
    assert rows % best == 0, (rows, cols)
    return best


def pair_sum(a, core, *, name):
    _, rows, cols = a.shape
    rh = rows // 2
    tr = _chunk_rows(rh, cols, 2 << 20)
    nch = rh // tr
    steps = 4 * nch

    def body(core_ref, keep_ref, send_ref, o32_ref, o16_ref, landing, send_sems, recv_sems, credit):
        step = pl.program_id(0) * nch + pl.program_id(1)
        slot = lax.rem(step, 2)
        x, y, c = _place()
        sibling = (x, y, 1 - c)

        @pl.when(step >= 2)
        def _():
            pl.semaphore_wait(credit, 1)

        cp = _remote(send_ref.at[0], landing.at[slot], send_sems, recv_sems, slot, sibling)
        cp.start()
        cp.wait_recv()
        total = keep_ref[0] + landing[slot]
        o32_ref[0] = total
        o16_ref[0] = total.astype(BF16)
        cp.wait_send()

        @pl.when(step + 2 < steps)
        def _():
            pl.semaphore_signal(credit, 1, device_id=sibling, device_id_type=MESH_ID)

    blk = (1, tr, cols)
    grid_spec = pltpu.PrefetchScalarGridSpec(
        num_scalar_prefetch=1, grid=(4, nch),
        in_specs=[pl.BlockSpec(blk, lambda k, i, core: (k, core[0] * nch + i, 0)),
                  pl.BlockSpec(blk, lambda k, i, core: (k, (1 - core[0]) * nch + i, 0))],
        out_specs=[pl.BlockSpec(blk, lambda k, i, core: (k, i, 0))] * 2,
        scratch_shapes=[pltpu.VMEM((2, tr, cols), F32), pltpu.SemaphoreType.DMA((2,)), pltpu.SemaphoreType.DMA((2,)),
                        pltpu.SemaphoreType.REGULAR])
    return pl.pallas_call(
        body, name=name, grid_spec=grid_spec,
        out_shape=[jax.ShapeDtypeStruct((4, rh, cols), F32), jax.ShapeDtypeStruct((4, rh, cols), BF16)],
        compiler_params=_params(("arbitrary", "arbitrary")))(core, a, a)


def scatter_chips(arrs, *, name):
    na = len(arrs)

    def body(*refs):
        ins, outs = refs[:na], refs[na:2 * na]
        send_sems, recv_sems = refs[2 * na:]
        x, y, c = _place()
        me = 2 * x + y
        chips = _other_chips(x, y)
        sends = []
        for t in range(na):
            for j, (px, py) in enumerate(chips):
                cp = _remote(ins[t].at[2 * px + py], outs[t].at[me], send_sems, recv_sems, 3 * t + j, (px, py, c))
                cp.start()
                sends.append(cp)
        for t in range(na):
            for j, (px, py) in enumerate(chips):
                slab = outs[t].at[2 * px + py]
                _remote(slab, slab, send_sems, recv_sems, 3 * t + j, (px, py, c)).wait_recv()
        for cp in sends:
            cp.wait_send()

    return pl.pallas_call(
        body, name=name, in_specs=[ANY] * na, out_specs=[ANY] * na,
        out_shape=[jax.ShapeDtypeStruct(a.shape, a.dtype) for a in arrs],
        scratch_shapes=[pltpu.SemaphoreType.DMA((3 * na,)), pltpu.SemaphoreType.DMA((3 * na,))],
    )(*arrs)


def chip_sum_join(own, landed, chip, *, name):
    _, rh, cols = own.shape
    tr = _chunk_rows(rh, cols, 2 << 20)
    nch = rh // tr

    def body(chip_ref, own_ref, l1_ref, l2_ref, l3_ref, out_ref, res, local_sem, send_sem, recv_sem):
        i = pl.program_id(0)
        x, y, c = _place()
        sibling = (x, y, 1 - c)
        res[...] = ((own_ref[0] + l1_ref[0].astype(F32)) + l2_ref[0].astype(F32)) + l3_ref[0].astype(F32)
        rows = pl.ds(pl.multiple_of(i * tr, tr), tr)
        here = pltpu.make_async_copy(res, out_ref.at[c, rows, :], local_sem)
        here.start()
        there = pltpu.make_async_remote_copy(src_ref=res, dst_ref=out_ref.at[c, rows, :], send_sem=send_sem, recv_sem=recv_sem,
                                             device_id=sibling, device_id_type=MESH_ID)
        there.start()
        here.wait()
        there.wait_send()

        @pl.when(i == nch - 1)
        def _():
            half = out_ref.at[1 - c]
            pltpu.make_async_remote_copy(src_ref=half, dst_ref=half, send_sem=send_sem, recv_sem=recv_sem,
                                         device_id=sibling, device_id_type=MESH_ID).wait_recv()

    blk = (1, tr, cols)
    slab = lambda p: pl.BlockSpec(blk, lambda i, chip: (chip[0] ^ p, i, 0))
    grid_spec = pltpu.PrefetchScalarGridSpec(
        num_scalar_prefetch=1, grid=(nch,), in_specs=[slab(0), slab(1), slab(2), slab(3)], out_specs=ANY,
        scratch_shapes=[pltpu.VMEM((tr, cols), F32), pltpu.SemaphoreType.DMA, pltpu.SemaphoreType.DMA, pltpu.SemaphoreType.DMA])
    return pl.pallas_call(
        body, name=name, grid_spec=grid_spec, out_shape=jax.ShapeDtypeStruct((2, rh, cols), F32),
        compiler_params=_params(("arbitrary",)))(chip, own, landed, landed, landed)


def all_reduce_small(a, *, name):
    def body(a_ref, o_ref, buf, send_sems, recv_sems):
        x, y, c = _place()
        me = 4 * x + 2 * y + c
        buf[me] = a_ref[...]
        sent = []
        for p in range(1, 8):
            px, py, pc = (p >> 2) & 1, (p >> 1) & 1, p & 1
            cp = _remote(a_ref, buf.at[me], send_sems, recv_sems, p - 1, (x ^ px, y ^ py, c ^ pc))
            cp.start()
            sent.append(cp)
        for p in range(1, 8):
            px, py, pc = (p >> 2) & 1, (p >> 1) & 1, p & 1
            src = 4 * (x ^ px) + 2 * (y ^ py) + (c ^ pc)
            _remote(a_ref, buf.at[src], send_sems, recv_sems, p - 1, (x ^ px, y ^ py, c ^ pc)).wait_recv()
        for cp in sent:
            cp.wait_send()
        acc = buf[0]
        for d in range(1, 8):
            acc = acc + buf[d]
        o_ref[...] = acc

    vm = pl.BlockSpec(memory_space=pltpu.VMEM)
    return pl.pallas_call(
        body, name=name, in_specs=[vm], out_specs=vm, out_shape=jax.ShapeDtypeStruct(a.shape, a.dtype),
        scratch_shapes=[pltpu.VMEM((8,) + a.shape, a.dtype), pltpu.SemaphoreType.DMA((7,)), pltpu.SemaphoreType.DMA((7,))],
    )(a)


TQ = 256


def _pad_w_in(w):
    pad = jnp.zeros((w.shape[0], DPROJ - DIN), w.dtype)
    return jnp.concatenate([w[:, :O1], w[:, O2:], w[:, O1:O2], pad], axis=1)


def _layer_small(sm, l):
    row = lambda v: v.reshape(1, -1)
    return dict(
        attn_norm=row(sm["attn_norm"][l]), mlp_norm=row(sm["mlp_norm"][l]),
        qgf=row(jnp.tile(sm["q_norm_fox"][l], 8)), kgf=row(jnp.tile(sm["k_norm_fox"][l], 8)),
        qgd=row(jnp.tile(sm["q_norm_dil"][l], 12)), kgd=row(jnp.tile(sm["k_norm_dil"][l], 12)),
        bfor=row(jnp.pad(sm["b_forget"][l], (0, LANES - 8))))


def _key_rows(f8, bsz, seq):
    f = f8.reshape(bsz, seq, LANES)[:, :, :8].transpose(0, 2, 1).reshape(bsz, 4, 2, seq)
    return jnp.pad(f, ((0, 0), (0, 0), (0, 6), (0, 0))).reshape(bsz * 32, seq)


def _layer_fwd(x, w, s, cos, sin, bsz, seq, l):
    nm = lambda t: f"l{l}_{t}"
    h = rmsnorm_fwd(x, s["attn_norm"], name=nm("attn_norm"))
    proj = matmul(h, w["win"], name=nm("proj"))
    qn, kn, fb, f8 = fox_prep_fwd(proj, s["qgf"], s["kgf"], s["bfor"], bsz=bsz, seq=seq, name=nm("fox_prep"))
    fk = _key_rows(f8, bsz, seq)
    oa, la = softmax_attn_fwd(qn, kn, proj, (fb, fk), qo=0, ko=0, vo=FOXV, pairs=4, bsz=bsz, seq=seq, window=seq, dilation=1,
                              tq=TQ, name=nm("fox_attn"))
    ob = sb_attn_fwd(proj, bsz=bsz, seq=seq, tq=TQ, name=nm("sb_attn"))
    qr, kr = dil_prep_fwd(proj, s["qgd"], s["kgd"], cos, sin, name=nm("dil_prep"))
    ogs, lgs = [], []
    for g, (window, dilation) in enumerate(DIL_PATTERNS):
        og, lg = softmax_attn_fwd(qr, kr, proj, None, qo=2 * g, ko=2 * g, vo=DILV + 2 * g, pairs=2, bsz=bsz, seq=seq,
                                  window=window, dilation=dilation, tq=TQ, name=nm(f"dil_attn{g}"))
        ogs.append(og)
        lgs.append(lg)
    oc = dil_combine_fwd(ogs, lgs, name=nm("dil_combine"))
    ys = [matmul(oa, w["wuf"], name=nm("up_fox")), matmul(ob, w["wus"], name=nm("up_sb")), matmul(oc, w["wud"], name=nm("up_dil"))]
    merged = merge_fwd(proj, ys, name=nm("merge"))
    x1 = matmul(merged, w["wo"], add=x, name=nm("out_proj"))
    h2 = rmsnorm_fwd(x1, s["mlp_norm"], name=nm("mlp_norm"))
    u = matmul(h2, w["wmi"], name=nm("mlp_in"))
    act = relu2_fwd(u, name=nm("relu2"))
    x2 = matmul(act, w["wmo"], add=x1, name=nm("mlp_out"))
    saved = dict(x=x, h=h, proj=proj, qn=qn, kn=kn, fb=fb, fk=fk, oa=oa, la=la, ob=ob, qr=qr, kr=kr, ogs=ogs, lgs=lgs, oc=oc,
                 ys=ys, merged=merged, x1=x1, h2=h2, u=u, act=act)
    return x2, saved


WIN_TILE = 256
WIN_STRIDE, WIN_TILES = 8, 9


def grad_buffers(depth, d, dff, wf, wd):
    assert dff // 4 == d
    return dict(win=lax.empty((4, depth * d, WIN_TILES * WIN_TILE), F32), ups=lax.empty((4, depth * (2 * wf + wd), d // 4), F32),
                wide=lax.empty((4, depth * (d + dff // 4 + d // 4), d), F32))


def _layer_bwd(dx2, w, s, sv, cos, sin, bsz, seq, l, depth, bufs):
    nm = lambda t: f"l{l}_{t}_bwd"
    n = bsz * seq
    proj = sv["proj"]
    d, dff = w["wmi"].shape
    wf, wd = w["wuf"].shape[0], w["wud"].shape[0]
    bufs = dict(bufs)
    rb = 512
    per_chip = dff // 4 // rb
    da = matmul(dx2, w["wmo"], tb=True, name=nm("mlp_out_dx"))
    bufs["wide"] = matmul(sv["act"], dx2, ta=True, tm=rb, tn=d, tk=2048, name=nm("mlp_out_dw"),
                          dest=(bufs["wide"], 1, lambda j: j,
                                lambda i, j: (i // per_chip, (depth * d + l * (dff // 4)) // rb + i % per_chip, j)))
    du = relu2_bwd(sv["u"], da, name=nm("relu2"))
    dh2 = matmul(du, w["wmi"], tb=True, name=nm("mlp_in_dx"))
    bufs["wide"] = matmul(sv["h2"], du, ta=True, tm=rb, tn=dff // 4, tk=2048, name=nm("mlp_in_dw"),
                          dest=(bufs["wide"], 4, lambda j: j, lambda i, j: (j, l * d // rb + i, 0)))
    dx1, g_mlp_norm = rmsnorm_bwd(sv["x1"], s["mlp_norm"], dh2, dx2, name=nm("mlp_norm"))

    dmerged = matmul(dx1, w["wo"], tb=True, name=nm("out_proj_dx"))
    bufs["wide"] = matmul(sv["merged"], dx1, ta=True, tm=d // 4, tn=d, tk=2048, name=nm("out_proj_dw"),
                          dest=(bufs["wide"], 1, lambda j: j, lambda i, j: (i, (depth * (d + dff // 4)) // (d // 4) + l, j)))
    dya, dyb, dyc, dga, dgb, dgc = merge_bwd(proj, sv["ys"], dmerged, name=nm("merge"))
    doa = matmul(dya, w["wuf"], tb=True, name=nm("up_fox_dx"))
    bufs["ups"] = matmul(sv["oa"], dya, ta=True, tm=wf, tn=d // 4, tk=2048, name=nm("up_fox_dw"),
                         dest=(bufs["ups"], 4, lambda j: j, lambda i, j: (j, l, 0)))
    dob = matmul(dyb, w["wus"], tb=True, name=nm("up_sb_dx"))
    bufs["ups"] = matmul(sv["ob"], dyb, ta=True, tm=wf, tn=d // 4, tk=2048, name=nm("up_sb_dw"),
                         dest=(bufs["ups"], 4, lambda j: j, lambda i, j: (j, depth + l, 0)))
    doc = matmul(dyc, w["wud"], tb=True, name=nm("up_dil_dx"))
    bufs["ups"] = matmul(sv["oc"], dyc, ta=True, tm=wd, tn=d // 4, tk=2048, name=nm("up_dil_dw"),
                         dest=(bufs["ups"], 4, lambda j: j, lambda i, j: (j, 2 * depth * wf // wd + l, 0)))

    outs = dil_combine_bwd(sv["ogs"], sv["lgs"], doc, name=nm("dil_combine"))
    dqs, dks, dvs = [], [], []
    for g, (window, dilation) in enumerate(DIL_PATTERNS):
        dq, dk, dv = softmax_attn_bwd(sv["qr"], sv["kr"], proj, sv["ogs"][g], outs[g], sv["lgs"][g], outs[3 + g], None,
                                      qo=2 * g, ko=2 * g, vo=DILV + 2 * g, pairs=2, bsz=bsz, seq=seq, window=window,
                                      dilation=dilation, tq=TQ, dq_dtype=F32, dk_dtype=F32, name=nm(f"dil_attn{g}"))
        dqs.append(dq)
        dks.append(dk)
        dvs.append(dv)
    d_dq, d_dk, g_qgd, g_kgd = dil_prep_bwd(proj, s["qgd"], s["kgd"], cos, sin, jnp.concatenate(dqs, axis=1),
                                            jnp.concatenate(dks, axis=1), name=nm("dil_prep"))

    s_dq, s_dk, s_dv = sb_attn_bwd(proj, dob, bsz=bsz, seq=seq, tq=TQ, name=nm("sb_attn"))

    dqn, dkn, f_dv, dfq, dfk = softmax_attn_bwd(sv["qn"], sv["kn"], proj, sv["oa"], doa, sv["la"], None, (sv["fb"], sv["fk"]),
                                                qo=0, ko=0, vo=FOXV, pairs=4, bsz=bsz, seq=seq, window=seq, dilation=1, tq=TQ,
                                                dq_dtype=F32, dk_dtype=F32, name=nm("fox_attn"))
    dfk8 = dfk.reshape(bsz, 4, 8, seq)[:, :, :2].reshape(bsz, 8, seq).transpose(0, 2, 1).reshape(n, 8)
    df = jnp.pad(dfq[:, ::HEAD] - dfk8, ((0, 0), (0, LANES - 8)))
    f_dq, f_dk, d_forget, g_qgf, g_kgf, g_bfor = fox_prep_bwd(proj, s["qgf"], s["kgf"], s["bfor"], dqn, dkn, df, bsz=bsz, seq=seq,
                                                              name=nm("fox_prep"))

    dproj = jnp.concatenate([f_dq, f_dk, f_dv, s_dq, s_dk, s_dv, d_dq, d_dk] + dvs + [dga, dgb, dgc, d_forget], axis=1)
    dh = matmul(dproj, w["win"], tb=True, tn=1024, tk=512, name=nm("proj_dx"))
    bufs["win"] = matmul(sv["h"], dproj, ta=True, tm=d, tn=WIN_TILE, tk=2048, name=nm("proj_dw"),
                         dest=(bufs["win"], 4 * WIN_TILES, lambda j: WIN_STRIDE * (j // WIN_TILES) + j % WIN_TILES,
                               lambda i, j: (j // WIN_TILES, l, j % WIN_TILES)))
    g_forget = matmul(sv["h"], d_forget, ta=True, tk=2048, name=nm("forget_dw"))[:, :O2 - O1]
    dx, g_attn_norm = rmsnorm_bwd(sv["x"], s["attn_norm"], dh, dx1, name=nm("attn_norm"))
    gs = dict(attn_norm=g_attn_norm[0], mlp_norm=g_mlp_norm[0], b_forget=g_bfor[0, :8],
              q_norm_fox=g_qgf.reshape(8, HEAD).sum(0), k_norm_fox=g_kgf.reshape(8, HEAD).sum(0),
              q_norm_dil=g_qgd.reshape(12, HEAD).sum(0), k_norm_dil=g_kgd.reshape(12, HEAD).sum(0), w_in_forget=g_forget)
    return dx, bufs, gs


def local_step(x, positions, target, weights, small):
    bsz, seq, d = x.shape
    n = bsz * seq
    depth = len(weights)
    inv = 1.0 / (ROPE_THETA ** (jnp.arange(HEAD // 2, dtype=F32) / (HEAD // 2)))
    cos, sin = rope_table(positions.reshape(n, 1), jnp.tile(inv, 4).reshape(1, LANES), name="rope_table")
    xs = x.reshape(n, d)
    saved = []
    for l in range(depth):
        xs, sv = _layer_fwd(xs, weights[l], _layer_small(small, l), cos, sin, bsz, seq, l)
        saved.append(sv)
    dy, sq = loss_grad(xs, target.reshape(n, d), name="loss")
    loss = (0.5 / d) * jnp.sum(sq)
    w0 = weights[0]
    bufs = grad_buffers(depth, d, w0["wmi"].shape[1], w0["wuf"].shape[0], w0["wud"].shape[0])
    gss = [None] * depth
    for l in reversed(range(depth)):
        dy, bufs, gss[l] = _layer_bwd(dy, weights[l], _layer_small(small, l), saved[l], cos, sin, bsz, seq, l, depth, bufs)
    return loss, dy.reshape(bsz, seq, d), bufs, gss


SMALL = ("attn_norm", "mlp_norm", "b_forget", "q_norm_fox", "k_norm_fox", "q_norm_dil", "k_norm_dil")
SMALL_ROWS = 8


def _pack_small(vals):
    flat = jnp.concatenate([vals[k].reshape(-1) for k in SMALL])
    return jnp.pad(flat, (0, SMALL_ROWS * 1024 - flat.shape[0])).reshape(SMALL_ROWS, 1024)


def _unpack_small(packed, like):
    flat, out, at = packed.reshape(-1), {}, 0
    for k in SMALL:
        size = like[k].size
        out[k] = flat[at:at + size].reshape(like[k].shape)
        at += size
    return out


def _chips_to_cols(a, depth):
    _, rows, c = a.shape
    return a.reshape(4, depth, rows // depth, c).transpose(1, 2, 0, 3).reshape(depth, rows // depth, 4 * c)


def _chips_to_rows(a, depth):
    _, rows, c = a.shape
    return a.reshape(4, depth, rows // depth, c).transpose(1, 0, 2, 3).reshape(depth, 4 * rows // depth, c)


def kernel(x, positions, attn_norm, w_in, b_forget, q_norm_fox, k_norm_fox, q_norm_dil, k_norm_dil, w_up_fox, w_up_sb, w_up_dil, w_out, mlp_norm, w_mlp_in, w_mlp_out, loss_target, m_attn_norm, m_w_in, m_b_forget, m_q_norm_fox, m_k_norm_fox, m_q_norm_dil, m_k_norm_dil, m_w_up_fox, m_w_up_sb, m_w_up_dil, m_w_out, m_mlp_norm, m_w_mlp_in, m_w_mlp_out, v_attn_norm, v_w_in, v_b_forget, v_q_norm_fox, v_k_norm_fox, v_q_norm_dil, v_k_norm_dil, v_w_up_fox, v_w_up_sb, v_w_up_dil, v_w_out, v_mlp_norm, v_w_mlp_in, v_w_mlp_out):
    names = ("attn_norm", "w_in", "b_forget", "q_norm_fox", "k_norm_fox", "q_norm_dil", "k_norm_dil", "w_up_fox", "w_up_sb",
             "w_up_dil", "w_out", "mlp_norm", "w_mlp_in", "w_mlp_out")
    wv = dict(zip(names, (attn_norm, w_in, b_forget, q_norm_fox, k_norm_fox, q_norm_dil, k_norm_dil, w_up_fox, w_up_sb, w_up_dil,
                          w_out, mlp_norm, w_mlp_in, w_mlp_out)))
    mv = dict(zip(names, (m_attn_norm, m_w_in, m_b_forget, m_q_norm_fox, m_k_norm_fox, m_q_norm_dil, m_k_norm_dil, m_w_up_fox,
                          m_w_up_sb, m_w_up_dil, m_w_out, m_mlp_norm, m_w_mlp_in, m_w_mlp_out)))
    vv = dict(zip(names, (v_attn_norm, v_w_in, v_b_forget, v_q_norm_fox, v_k_norm_fox, v_q_norm_dil, v_k_norm_dil, v_w_up_fox,
                          v_w_up_sb, v_w_up_dil, v_w_out, v_mlp_norm, v_w_mlp_in, v_w_mlp_out)))
    depth = w_in.shape[0]
    flat2 = lambda a: a.reshape(-1, a.shape[-1])

    ups = ("w_up_fox", "w_up_sb", "w_up_dil")
    wide = ("w_mlp_in", "w_mlp_out", "w_out")
    send = [flat2(w_in).astype(MM), jnp.concatenate([flat2(wv[k]) for k in ups]).astype(MM),
            jnp.concatenate([flat2(wv[k]) for k in wide]).astype(MM)]
    core = lax.axis_index("c").astype(jnp.int32).reshape(1)
    chip = (2 * lax.axis_index("x") + lax.axis_index("y")).astype(jnp.int32).reshape(1)
    got_in, got_up, got_wide = [lax.dynamic_update_index_in_dim(g, s, chip[0], 0)
                                for g, s in zip(gather_chips(send, name="gather_weights"), send)]

    def split(a, keys):
        out, at = {}, 0
        for k in keys:
            rows = wv[k].shape[0] * wv[k].shape[1]
            out[k] = a[:, at:at + rows]
            at += rows
        return out

    full = {"w_in": _chips_to_cols(got_in, depth)}
    full.update({k: _chips_to_cols(a, depth) for k, a in split(got_up, ups).items()})
    parts = split(got_wide, wide)
    full["w_out"] = _chips_to_rows(parts["w_out"], depth)
    full["w_mlp_in"] = _chips_to_cols(parts["w_mlp_in"], depth)
    full["w_mlp_out"] = _chips_to_rows(parts["w_mlp_out"], depth)
    weights = [dict(win=_pad_w_in(full["w_in"][l]), wuf=full["w_up_fox"][l], wus=full["w_up_sb"][l], wud=full["w_up_dil"][l],
                    wo=full["w_out"][l], wmi=full["w_mlp_in"][l], wmo=full["w_mlp_out"][l]) for l in range(depth)]
    small = {k: wv[k] for k in SMALL}

    loss, grad_x, bufs, gss = local_step(x, positions, loss_target, weights, small)
    loss = lax.psum(loss, ("x", "y", "c"))

    g_small = {k: jnp.stack([gss[l][k] for l in range(depth)]) for k in SMALL}
    g_forget = jnp.stack([gss[l]["w_in_forget"] for l in range(depth)])
    summed = all_reduce_small(jnp.concatenate([_pack_small(g_small), g_forget.reshape(-1, 1024)]), name="reduce_small")
    g_small = _unpack_small(summed[:SMALL_ROWS], small)
    g_forget = summed[SMALL_ROWS:].reshape(g_forget.shape)

    parts = [bufs["win"], bufs["ups"], bufs["wide"]]
    sums = [pair_sum(p, core, name=f"reduce_pair_sum{t}") for t, p in enumerate(parts)]
    landed = scatter_chips([s16 for _, s16 in sums], name="reduce_scatter_chips")
    joined = [chip_sum_join(sums[t][0], landed[t], chip, name=f"reduce_chip_sum{t}").reshape(-1, parts[t].shape[-1])
              for t in range(3)]

    def own_w_in_columns(window):
        cols = w_in.shape[-1]
        first = jnp.concatenate([window[..., :O1], g_forget, window[..., O1:cols - (O2 - O1)]], axis=-1)
        shift = jnp.maximum((cols - WIN_STRIDE * WIN_TILE) * chip[0] - (O2 - O1), 0)
        rest = lax.dynamic_slice_in_dim(window, shift, cols, axis=2)
        return jnp.where(chip[0] == 0, first, rest)

    g_big = {"w_in": own_w_in_columns(joined[0].reshape(depth, -1, joined[0].shape[-1]))}
    for a, keys in ((joined[1], ups), (joined[2], wide)):
        at = 0
        for k in keys:
            rows = wv[k].shape[0] * wv[k].shape[1]
            g_big[k] = a[at:at + rows].reshape(wv[k].shape)
            at += rows

    grads = {**g_small, **g_big}
    delta, new_m, new_v = {}, {}, {}
    d_s, m_s, v_s = adamw(_pack_small(small), _pack_small(g_small), _pack_small({k: mv[k] for k in SMALL}),
                          _pack_small({k: vv[k] for k in SMALL}), name="adamw_small")
    delta.update(_unpack_small(d_s, small))
    new_m.update(_unpack_small(m_s, small))
    new_v.update(_unpack_small(v_s, small))
    for k in ("w_in",) + ups + wide:
        d_k, m_k, v_k = adamw(flat2(wv[k]), flat2(g_big[k]), flat2(mv[k]), flat2(vv[k]), name=f"adamw_{k}")
        delta[k], new_m[k], new_v[k] = d_k.reshape(wv[k].shape), m_k.reshape(wv[k].shape), v_k.reshape(wv[k].shape)

    return (loss, grad_x, *[grads[k] for k in names], *[delta[k] for k in names], *[new_m[k] for k in names], *[new_v[k] for k in names])
```

```python
import functools

import jax
import jax.numpy as jnp
from jax import lax
from jax.experimental import pallas as pl
from jax.experimental.pallas import tpu as pltpu

F32 = jnp.float32
BF16 = jnp.bfloat16
MM = jnp.bfloat16

HEAD = 64
LANES = 128
EPS = 1e-6
SCALE = 0.125
ROPE_THETA = 10000.0
DIL_PATTERNS = ((128, 1), (512, 4), (2048, 16))
ADAM_LR, ADAM_B1, ADAM_B2, ADAM_EPS, ADAM_WD, ADAM_STEP = 0.001, 0.9, 0.999, 1e-08, 0.01, 10

FOXQ, FOXK, FOXV = 0, 4, 8
SBQ, SBK, SBV = 12, 16, 20
DILQ, DILK, DILV = 24, 30, 36
GATE, FORGET, NBLK = 42, 66, 68
DPROJ = NBLK * LANES
O1, O2, O3, O4, DIN = 1536, 1544, 3080, 5384, 8456

VMEM_LIMIT = 56 * 1024 * 1024
MESH_ID = pl.DeviceIdType.MESH
ANY = pl.BlockSpec(memory_space=pl.ANY)


def _params(sem=None):
    return pltpu.CompilerParams(dimension_semantics=sem, vmem_limit_bytes=VMEM_LIMIT)


def _iota(shape, dim):
    return lax.broadcasted_iota(jnp.int32, shape, dim)


def _split2(x):
    hi = x.astype(BF16)
    lo = (x - hi.astype(F32)).astype(BF16)
    return hi, lo


def _split3(x):
    hi = x.astype(BF16)
    r = x - hi.astype(F32)
    mid = r.astype(BF16)
    lo = (r - mid.astype(F32)).astype(BF16)
    return hi, mid, lo


def _dot(a, b):
    return jnp.dot(a, b, preferred_element_type=F32)


def _dot_nt(a, b):
    return lax.dot_general(a, b, (((1,), (1,)), ((), ())), preferred_element_type=F32)


def _dot_tn(a, b):
    return lax.dot_general(a, b, (((0,), (0,)), ((), ())), preferred_element_type=F32)


def _xdot2(x, m):
    hi, lo = _split2(x)
    return _dot(hi, m) + _dot(lo, m)


def _xdot3(x, m):
    hi, mid, lo = _split3(x)
    return _dot(hi, m) + _dot(mid, m) + _dot(lo, m)


def _xdot3_left(m, x):
    hi, mid, lo = _split3(x)
    return _dot(m, hi) + _dot(m, mid) + _dot(m, lo)


def _head_mat(w):
    return ((_iota((w, w), 0) >> 6) == (_iota((w, w), 1) >> 6)).astype(BF16)


def _softplus_parts(z):
    e = jnp.exp(-jnp.abs(z))
    return e, jnp.maximum(z, 0.0) + jnp.log(1.0 + e)


def _fit(dim, want):
    t = min(want, dim)
    while dim % t:
        t -= LANES
        assert t > 0, (dim, want)
    return t


def matmul(a, b, *, ta=False, tb=False, out_dtype=F32, add=None, tm=2048, tn=512, tk=1024, dest=None, relu2=False,
           relu2_of=None, name):
    K, M = a.shape if ta else a.shape[::-1]
    K2, N = b.shape[::-1] if tb else b.shape
    assert K == K2, (a.shape, b.shape, ta, tb)
    tm, tn, tk = _fit(M, tm), _fit(N, tn), _fit(K, tk)
    nk = K // tk
    dn = (((0 if ta else 1,), (1 if tb else 0,)), ((), ()))
    if dest is None:
        tiles, source = N // tn, lambda j: j
    else:
        assert add is None and not tb
        buffer, tiles, source, place = dest

    extra = add if add is not None else relu2_of
    assert add is None or relu2_of is None

    def body(*refs):
        act_ref = None
        if dest is not None:
            a_ref, b_ref, _, o_ref, acc_ref = refs
        elif relu2:
            a_ref, b_ref, o_ref, act_ref, acc_ref = refs
        elif extra is None:
            a_ref, b_ref, o_ref, acc_ref = refs
        else:
            a_ref, b_ref, add_ref, o_ref, acc_ref = refs
        k = pl.program_id(2)
        part = lax.dot_general(a_ref[...].astype(MM), b_ref[...].astype(MM), dn, preferred_element_type=F32)

        @pl.when(k == 0)
        def _():
            acc_ref[...] = part

        @pl.when(k > 0)
        def _():
            acc_ref[...] += part

        @pl.when(k == nk - 1)
        def _():
            r = acc_ref[...]
            if add is not None:
                r = r + add_ref[...]
            if relu2_of is not None:
                r = r * (2.0 * jnp.maximum(add_ref[...], 0.0))
            o_ref[...] = r.astype(o_ref.dtype).reshape(o_ref.shape)
            if act_ref is not None:
                pos = jnp.maximum(r, 0.0)
                act_ref[...] = (pos * pos).astype(act_ref.dtype)

    a_spec = pl.BlockSpec((tk, tm), lambda i, j, k: (k, i)) if ta else pl.BlockSpec((tm, tk), lambda i, j, k: (i, k))
    b_spec = pl.BlockSpec((tn, tk), lambda i, j, k: (j, k)) if tb else pl.BlockSpec((tk, tn), lambda i, j, k: (k, source(j)))
    o_spec = pl.BlockSpec((tm, tn), lambda i, j, k: (i, j))
    ins, specs, aliases = [a, b], [a_spec, b_spec], {}
    out_shape = jax.ShapeDtypeStruct((M, N), out_dtype)
    if extra is not None:
        ins.append(extra)
        specs.append(o_spec)
    if relu2:
        o_spec, out_shape = [o_spec, o_spec], [out_shape, jax.ShapeDtypeStruct((M, N), MM)]
    if dest is not None:
        ins.append(buffer)
        specs.append(ANY)
        aliases = {2: 0}
        o_spec = pl.BlockSpec((1, tm, tn), lambda i, j, k: place(i, j))
        out_shape = jax.ShapeDtypeStruct(buffer.shape, buffer.dtype)
    return pl.pallas_call(
        body, name=name, grid=(M // tm, tiles, nk), in_specs=specs, out_specs=o_spec, out_shape=out_shape,
        scratch_shapes=[pltpu.VMEM((tm, tn), F32)], input_output_aliases=aliases,
        compiler_params=_params(("parallel", "parallel", "arbitrary")),
    )(*ins)


def _rows(n, want=512):
    t = min(want, n)
    assert n % t == 0, (n, t)
    return t


def rmsnorm_fwd(x, g, *, name):
    n, d = x.shape
    tr = _rows(n)

    def body(x_ref, g_ref, o_ref):
        xv = x_ref[...]
        r = lax.rsqrt(jnp.mean(xv * xv, axis=1, keepdims=True) + EPS)
        o_ref[...] = (xv * r * g_ref[...]).astype(o_ref.dtype)

    row = pl.BlockSpec((tr, d), lambda i: (i, 0))
    vec = pl.BlockSpec((1, d), lambda i: (0, 0))
    return pl.pallas_call(body, name=name, grid=(n // tr,), in_specs=[row, vec], out_specs=row,
                          out_shape=jax.ShapeDtypeStruct((n, d), MM), compiler_params=_params(("parallel",)))(x, g)


def rmsnorm_bwd(x, g, dh, dres, *, name):
    n, d = x.shape
    tr = _rows(n)

    def body(x_ref, g_ref, dh_ref, dr_ref, dx_ref, dg_ref):
        @pl.when(pl.program_id(0) == 0)
        def _():
            dg_ref[...] = jnp.zeros_like(dg_ref)

        xv = x_ref[...]
        r = lax.rsqrt(jnp.mean(xv * xv, axis=1, keepdims=True) + EPS)
        y = xv * r
        dhv = dh_ref[...]
        dy = dhv * g_ref[...]
        dx_ref[...] = dr_ref[...] + r * (dy - y * jnp.mean(dy * y, axis=1, keepdims=True))
        dg_ref[...] += jnp.sum(dhv * y, axis=0, keepdims=True)

    row = pl.BlockSpec((tr, d), lambda i: (i, 0))
    vec = pl.BlockSpec((1, d), lambda i: (0, 0))
    return pl.pallas_call(
        body, name=name, grid=(n // tr,), in_specs=[row, vec, row, row], out_specs=[row, vec],
        out_shape=[jax.ShapeDtypeStruct((n, d), F32), jax.ShapeDtypeStruct((1, d), F32)],
        compiler_params=_params(("arbitrary",)))(x, g, dh, dres)


def loss_grad(y, tgt, *, name):
    n, d = y.shape
    tr = _rows(n)

    def body(y_ref, t_ref, dy_ref, acc_ref):
        @pl.when(pl.program_id(0) == 0)
        def _():
            acc_ref[...] = jnp.zeros_like(acc_ref)

        e = y_ref[...] - t_ref[...]
        dy_ref[...] = e * (1.0 / d)
        acc_ref[...] += jnp.sum(e * e, axis=0, keepdims=True)

    row = pl.BlockSpec((tr, d), lambda i: (i, 0))
    vec = pl.BlockSpec((1, d), lambda i: (0, 0))
    return pl.pallas_call(
        body, name=name, grid=(n // tr,), in_specs=[row, row], out_specs=[row, vec],
        out_shape=[jax.ShapeDtypeStruct((n, d), F32), jax.ShapeDtypeStruct((1, d), F32)],
        compiler_params=_params(("arbitrary",)))(y, tgt)


MERGE_W = 256


def _gate_specs(tr, d):
    per = d // MERGE_W
    base = GATE * LANES // MERGE_W
    return [pl.BlockSpec((tr, MERGE_W), functools.partial(lambda i, j, b: (i, base + per * b + j), b=b)) for b in range(3)]


def merge_fwd(proj, ys, *, name):
    n, d = ys[0].shape
    tr = _rows(n)

    def body(g0, g1, g2, y0, y1, y2, o_ref):
        acc = jax.nn.sigmoid(g0[...]) * y0[...]
        acc += jax.nn.sigmoid(g1[...]) * y1[...]
        acc += jax.nn.sigmoid(g2[...]) * y2[...]
        o_ref[...] = acc.astype(o_ref.dtype)

    blk = pl.BlockSpec((tr, MERGE_W), lambda i, j: (i, j))
    return pl.pallas_call(
        body, name=name, grid=(n // tr, d // MERGE_W), in_specs=_gate_specs(tr, d) + [blk] * 3, out_specs=blk,
        out_shape=jax.ShapeDtypeStruct((n, d), MM), compiler_params=_params(("parallel", "parallel")))(proj, proj, proj, *ys)


def merge_bwd(proj, ys, dm, *, name):
    n, d = dm.shape
    tr = _rows(n)

    def body(g0, g1, g2, y0, y1, y2, dm_ref, dy0, dy1, dy2, dg0, dg1, dg2):
        dmv = dm_ref[...]
        for g, y, dy, dg in ((g0, y0, dy0, dg0), (g1, y1, dy1, dg1), (g2, y2, dy2, dg2)):
            s = jax.nn.sigmoid(g[...])
            dy[...] = (dmv * s).astype(dy.dtype)
            dg[...] = (dmv * y[...] * s * (1.0 - s)).astype(dg.dtype)

    blk = pl.BlockSpec((tr, MERGE_W), lambda i, j: (i, j))
    out = jax.ShapeDtypeStruct((n, d), MM)
    return pl.pallas_call(
        body, name=name, grid=(n // tr, d // MERGE_W), in_specs=_gate_specs(tr, d) + [blk] * 4, out_specs=[blk] * 6,
        out_shape=[out] * 6, compiler_params=_params(("parallel", "parallel")))(proj, proj, proj, *ys, dm)


def adamw(w, g, m, v, *, name):
    r, c = w.shape
    tr = r
    while tr * c * 4 > (1 << 21) and tr % 16 == 0:
        tr //= 2
    c1 = 1.0 / (1.0 - ADAM_B1 ** ADAM_STEP)
    c2 = 1.0 / (1.0 - ADAM_B2 ** ADAM_STEP)

    def body(w_ref, g_ref, m_ref, v_ref, d_ref, mo_ref, vo_ref):
        gv = g_ref[...]
        m2 = ADAM_B1 * m_ref[...] + (1.0 - ADAM_B1) * gv
        v2 = ADAM_B2 * v_ref[...] + (1.0 - ADAM_B2) * (gv * gv)
        d_ref[...] = -ADAM_LR * ((m2 * c1) / (jnp.sqrt(v2 * c2) + ADAM_EPS) + ADAM_WD * w_ref[...])
        mo_ref[...] = m2
        vo_ref[...] = v2

    blk = pl.BlockSpec((tr, c), lambda i: (i, 0))
    out = jax.ShapeDtypeStruct((r, c), F32)
    return pl.pallas_call(body, name=name, grid=(r // tr,), in_specs=[blk] * 4, out_specs=[blk] * 3, out_shape=[out] * 3,
                          compiler_params=_params(("parallel",)))(w, g, m, v)


def rope_table(pos, inv, *, name):
    n = pos.shape[0]
    tr = _rows(n)

    def body(p_ref, i_ref, c_ref, s_ref):
        ang = p_ref[...].astype(F32) * i_ref[...]
        c_ref[...] = jnp.cos(ang)
        s_ref[...] = jnp.sin(ang)

    out = jax.ShapeDtypeStruct((n, LANES), F32)
    blk = pl.BlockSpec((tr, LANES), lambda i: (i, 0))
    return pl.pallas_call(
        body, name=name, grid=(n // tr,), in_specs=[pl.BlockSpec((tr, 1), lambda i: (i, 0)), pl.BlockSpec((1, LANES), lambda i: (0, 0))],
        out_specs=[blk, blk], out_shape=[out, out], compiler_params=_params(("parallel",)))(pos, inv)


def _rot_half(x):
    first = (_iota((1, LANES), 1) & 63) < 32
    return jnp.where(first, -pltpu.roll(x, LANES - 32, axis=1), pltpu.roll(x, 32, axis=1))


def _head_norm(xv, gm):
    r = lax.rsqrt(_xdot2(xv * xv, gm) * (1.0 / HEAD) + EPS)
    return r, xv * r


def _head_norm_bwd(xh, r, dxh, gm):
    return r * (dxh - xh * (_xdot2(dxh * xh, gm) * (1.0 / HEAD)))


def fox_prep_fwd(proj, qg, kg, bf, *, bsz, seq, name):
    n = bsz * seq
    tr = min(256, seq)
    nt = seq // tr
    w = 4 * LANES

    def body(q_ref, k_ref, f_ref, qg_ref, kg_ref, b_ref, qn_ref, kn_ref, fb_ref, f8_ref, carry):
        @pl.when(pl.program_id(1) == 0)
        def _():
            carry[...] = jnp.zeros_like(carry)

        gm = _head_mat(LANES)
        for src, gain, dst in ((q_ref, qg_ref, qn_ref), (k_ref, kg_ref, kn_ref)):
            for c in range(4):
                sl = slice(c * LANES, (c + 1) * LANES)
                _, xh = _head_norm(src[:, sl], gm)
                dst[:, sl] = (xh * gain[:, sl]).astype(dst.dtype)
        logf = jax.nn.log_sigmoid(f_ref[...] + b_ref[...])
        lower = (_iota((tr, tr), 1) <= _iota((tr, tr), 0)).astype(BF16)
        fcum = _xdot3_left(lower, logf) + carry[...]
        carry[...] = fcum[tr - 1:tr, :]
        f8_ref[...] = fcum
        spread = (_iota((LANES, w), 0) == (_iota((LANES, w), 1) >> 6)).astype(BF16)
        fb_ref[...] = _xdot3(fcum, spread)

    row = lambda width, blk: pl.BlockSpec((tr, width), lambda b, t: (b * nt + t, blk))
    vec = lambda width: pl.BlockSpec((1, width), lambda b, t: (0, 0))
    return pl.pallas_call(
        body, name=name, grid=(bsz, nt),
        in_specs=[row(w, FOXQ // 4), row(w, FOXK // 4), row(LANES, FORGET), vec(w), vec(w), vec(LANES)],
        out_specs=[row(w, 0), row(w, 0), row(w, 0), row(LANES, 0)],
        out_shape=[jax.ShapeDtypeStruct((n, w), MM), jax.ShapeDtypeStruct((n, w), MM),
                   jax.ShapeDtypeStruct((n, w), F32), jax.ShapeDtypeStruct((n, LANES), F32)],
        scratch_shapes=[pltpu.VMEM((1, LANES), F32)],
        compiler_params=_params(("parallel", "arbitrary")))(proj, proj, proj, qg, kg, bf)


def fox_prep_bwd(proj, qg, kg, bf, dqn, dkn, df, *, bsz, seq, name):
    n = bsz * seq
    tr = min(256, seq)
    nt = seq // tr
    w = 4 * LANES

    def body(q_ref, k_ref, f_ref, qg_ref, kg_ref, b_ref, dqn_ref, dkn_ref, df_ref,
             dq_ref, dk_ref, dl_ref, dqg_ref, dkg_ref, db_ref, carry):
        first = (pl.program_id(0) == 0) & (pl.program_id(1) == 0)

        @pl.when(first)
        def _():
            dqg_ref[...] = jnp.zeros_like(dqg_ref)
            dkg_ref[...] = jnp.zeros_like(dkg_ref)
            db_ref[...] = jnp.zeros_like(db_ref)

        @pl.when(pl.program_id(1) == 0)
        def _():
            carry[...] = jnp.zeros_like(carry)

        gm = _head_mat(LANES)
        for src, gain, dy_ref, dx_ref, dg_ref in ((q_ref, qg_ref, dqn_ref, dq_ref, dqg_ref), (k_ref, kg_ref, dkn_ref, dk_ref, dkg_ref)):
            for c in range(4):
                sl = slice(c * LANES, (c + 1) * LANES)
                r, xh = _head_norm(src[:, sl], gm)
                dy = dy_ref[:, sl]
                dg_ref[:, sl] += jnp.sum(dy * xh, axis=0, keepdims=True)
                dx_ref[:, sl] = _head_norm_bwd(xh, r, dy * gain[:, sl], gm).astype(dx_ref.dtype)
        upper = (_iota((tr, tr), 1) >= _iota((tr, tr), 0)).astype(BF16)
        dlogf = _xdot3_left(upper, df_ref[...]) + carry[...]
        carry[...] = dlogf[0:1, :]
        dlogit = dlogf * jax.nn.sigmoid(-(f_ref[...] + b_ref[...]))
        dl_ref[:, 0:LANES] = dlogit.astype(dl_ref.dtype)
        dl_ref[:, LANES:2 * LANES] = jnp.zeros((tr, LANES), dl_ref.dtype)
        db_ref[...] += jnp.sum(dlogit, axis=0, keepdims=True)

    row = lambda width, blk: pl.BlockSpec((tr, width), lambda b, t: (b * nt + nt - 1 - t, blk))
    vec = lambda width: pl.BlockSpec((1, width), lambda b, t: (0, 0))
    return pl.pallas_call(
        body, name=name, grid=(bsz, nt),
        in_specs=[row(w, FOXQ // 4), row(w, FOXK // 4), row(LANES, FORGET), vec(w), vec(w), vec(LANES),
                  row(w, 0), row(w, 0), row(LANES, 0)],
        out_specs=[row(w, 0), row(w, 0), row(2 * LANES, 0), vec(w), vec(w), vec(LANES)],
        out_shape=[jax.ShapeDtypeStruct((n, w), MM), jax.ShapeDtypeStruct((n, w), MM), jax.ShapeDtypeStruct((n, 2 * LANES), MM),
                   jax.ShapeDtypeStruct((1, w), F32), jax.ShapeDtypeStruct((1, w), F32), jax.ShapeDtypeStruct((1, LANES), F32)],
        scratch_shapes=[pltpu.VMEM((1, LANES), F32)],
        compiler_params=_params(("arbitrary", "arbitrary")))(proj, proj, proj, qg, kg, bf, dqn, dkn, df)


DIL_W = 6 * LANES


def dil_prep_fwd(proj, qg, kg, cos, sin, *, name):
    n = proj.shape[0]
    tr = _rows(n, 256)

    def body(q_ref, k_ref, qg_ref, kg_ref, c_ref, s_ref, qo_ref, ko_ref):
        gm = _head_mat(LANES)
        cv, sv = c_ref[...], s_ref[...]
        for src, gain, dst in ((q_ref, qg_ref, qo_ref), (k_ref, kg_ref, ko_ref)):
            for c in range(6):
                sl = slice(c * LANES, (c + 1) * LANES)
                _, xh = _head_norm(src[:, sl], gm)
                xn = xh * gain[:, sl]
                dst[:, sl] = (xn * cv + _rot_half(xn) * sv).astype(dst.dtype)

    row = lambda width, blk: pl.BlockSpec((tr, width), lambda i: (i, blk))
    vec = pl.BlockSpec((1, DIL_W), lambda i: (0, 0))
    out = jax.ShapeDtypeStruct((n, DIL_W), MM)
    return pl.pallas_call(
        body, name=name, grid=(n // tr,),
        in_specs=[row(DIL_W, DILQ // 6), row(DIL_W, DILK // 6), vec, vec, row(LANES, 0), row(LANES, 0)],
        out_specs=[row(DIL_W, 0), row(DIL_W, 0)], out_shape=[out, out],
        compiler_params=_params(("parallel",)))(proj, proj, qg, kg, cos, sin)


def dil_prep_bwd(proj, qg, kg, cos, sin, dqr, dkr, *, name):
    n = proj.shape[0]
    tr = _rows(n, 256)

    def body(q_ref, k_ref, qg_ref, kg_ref, c_ref, s_ref, dqr_ref, dkr_ref, dq_ref, dk_ref, dqg_ref, dkg_ref):
        @pl.when(pl.program_id(0) == 0)
        def _():
            dqg_ref[...] = jnp.zeros_like(dqg_ref)
            dkg_ref[...] = jnp.zeros_like(dkg_ref)

        gm = _head_mat(LANES)
        cv, sv = c_ref[...], s_ref[...]
        for src, gain, dy_ref, dx_ref, dg_ref in ((q_ref, qg_ref, dqr_ref, dq_ref, dqg_ref), (k_ref, kg_ref, dkr_ref, dk_ref, dkg_ref)):
            for c in range(6):
                sl = slice(c * LANES, (c + 1) * LANES)
                r, xh = _head_norm(src[:, sl], gm)
                dy = dy_ref[:, sl]
                dxn = dy * cv - _rot_half(dy * sv)
                dg_ref[:, sl] += jnp.sum(dxn * xh, axis=0, keepdims=True)
                dx_ref[:, sl] = _head_norm_bwd(xh, r, dxn * gain[:, sl], gm).astype(dx_ref.dtype)

    row = lambda width, blk: pl.BlockSpec((tr, width), lambda i: (i, blk))
    vec = pl.BlockSpec((1, DIL_W), lambda i: (0, 0))
    out = jax.ShapeDtypeStruct((n, DIL_W), MM)
    gout = jax.ShapeDtypeStruct((1, DIL_W), F32)
    return pl.pallas_call(
        body, name=name, grid=(n // tr,),
        in_specs=[row(DIL_W, DILQ // 6), row(DIL_W, DILK // 6), vec, vec, row(LANES, 0), row(LANES, 0), row(DIL_W, 0), row(DIL_W, 0)],
        out_specs=[row(DIL_W, 0), row(DIL_W, 0), vec, vec], out_shape=[out, out, gout, gout],
        compiler_params=_params(("arbitrary",)))(proj, proj, qg, kg, cos, sin, dqr, dkr)


def dil_combine_fwd(os_, lses, *, name):
    n, w = os_[0].shape
    tr = _rows(n)

    def body(o0, o1, o2, l0, l1, l2, out_ref):
        a, b, c = l0[...], l1[...], l2[...]
        m = jnp.maximum(jnp.maximum(a, b), c)
        ea, eb, ec = jnp.exp(a - m), jnp.exp(b - m), jnp.exp(c - m)
        out_ref[...] = ((ea * o0[...] + eb * o1[...] + ec * o2[...]) / (ea + eb + ec)).astype(out_ref.dtype)

    blk = pl.BlockSpec((tr, w), lambda i: (i, 0))
    return pl.pallas_call(body, name=name, grid=(n // tr,), in_specs=[blk] * 6, out_specs=blk,
                          out_shape=jax.ShapeDtypeStruct((n, w), MM), compiler_params=_params(("parallel",)))(*os_, *lses)


def dil_combine_bwd(os_, lses, dout, *, name):
    n, w = dout.shape
    tr = _rows(n)

    def body(o0, o1, o2, l0, l1, l2, d_ref, do0, do1, do2, dl0, dl1, dl2):
        a, b, c = l0[...], l1[...], l2[...]
        m = jnp.maximum(jnp.maximum(a, b), c)
        es = [jnp.exp(a - m), jnp.exp(b - m), jnp.exp(c - m)]
        inv = 1.0 / (es[0] + es[1] + es[2])
        ws = [e * inv for e in es]
        dv = d_ref[...]
        gm = _head_mat(w)
        dws = [_xdot2(dv * o[...], gm) for o in (o0, o1, o2)]
        mean = ws[0] * dws[0] + ws[1] * dws[1] + ws[2] * dws[2]
        for wg, dw, do, dl in zip(ws, dws, (do0, do1, do2), (dl0, dl1, dl2)):
            do[...] = wg * dv
            dl[...] = wg * (dw - mean)

    blk = pl.BlockSpec((tr, w), lambda i: (i, 0))
    out = jax.ShapeDtypeStruct((n, w), F32)
    return pl.pallas_call(body, name=name, grid=(n // tr,), in_specs=[blk] * 7, out_specs=[blk] * 6, out_shape=[out] * 6,
                          compiler_params=_params(("parallel",)))(*os_, *lses, dout)


def _key_plan(qi, tq, seq, window, run):
    if window + tq >= seq:
        for bi in range(seq // tq):
            lo = bi * tq
            segs = ([(0, lo, "bulk")] if lo else []) + [(lo, tq, "diag")]
            pl.when(qi == bi)(functools.partial(run, segs))
    else:
        ext = window + tq
        run([(pl.multiple_of(jnp.maximum((qi + 1) * tq - ext, 0), LANES), ext, "band")])


def _seg_mask(seg, qi, tq, window, dilation, strict=False):
    start, width, kind = seg
    d = _iota((tq, width), 0) - _iota((tq, width), 1)
    if kind == "bulk":
        d = d + width
    elif kind == "band":
        d = d + (qi * tq - start)
    ok = None
    if kind != "bulk":
        ok = (d > 0) if strict else (d >= 0)
    if window is not None:
        ok = (d <= window) if ok is None else ok & (d <= window)
    if dilation > 1:
        on_grid = (d & (dilation - 1)) == 0
        ok = on_grid if ok is None else ok & on_grid
    return ok


def _lane_first():
    return _iota((1, LANES), 1) < HEAD


def _attn_specs(bsz, seq, tq, qo, ko, vo):
    nq = seq // tq
    qspec = lambda off: pl.BlockSpec((tq, LANES), lambda b, j, i: (b * nq + i, off + j))
    kspec = lambda off: pl.BlockSpec((seq, LANES), lambda b, j, i: (b, off + j))
    return nq, qspec, kspec


def softmax_attn_fwd(q, k, v, bias, *, qo, ko, vo, pairs, bsz, seq, window, dilation, tq, name):
    n = bsz * seq
    nq, qspec, kspec = _attn_specs(bsz, seq, tq, qo, ko, vo)

    def body(*refs):
        if bias is None:
            q_ref, k_ref, v_ref, o_ref, l_ref = refs
        else:
            q_ref, k_ref, v_ref, fq_ref, fk_ref, o_ref, l_ref = refs
        qi = pl.program_id(2)

        def run(segs):
            qv = (q_ref[...] * SCALE).astype(MM)
            first = _lane_first()
            keys = [(k_ref[pl.ds(st, w), :].astype(MM), v_ref[pl.ds(st, w), :].astype(MM),
                     _seg_mask((st, w, kind), qi, tq, None if window >= seq else window, dilation), st, w)
                    for st, w, kind in segs]
            outs, lses = [], []
            for a in range(2):
                qa = jnp.where(first if a == 0 else ~first, qv, jnp.zeros_like(qv))
                scores = []
                for kv, _, ok, st, w in keys:
                    s = _dot_nt(qa, kv)
                    if bias is not None:
                        s = s + fq_ref[:, a * HEAD:a * HEAD + 1] - fk_ref[a:a + 1, pl.ds(st, w)]
                    scores.append(s if ok is None else jnp.where(ok, s, -jnp.inf))
                m = functools.reduce(jnp.maximum, [jnp.max(s, axis=1, keepdims=True) for s in scores])
                ps = [jnp.exp(s - m) for s in scores]
                den = sum(jnp.sum(p, axis=1, keepdims=True) for p in ps)
                acc = sum(_dot(p.astype(MM), vv) for p, (_, vv, _, _, _) in zip(ps, keys))
                outs.append(acc / den)
                lses.append(m + jnp.log(den))
            o_ref[...] = jnp.where(first, outs[0], outs[1]).astype(o_ref.dtype)
            l_ref[...] = jnp.where(first, lses[0], lses[1])

        _key_plan(qi, tq, seq, window, run)

    ins, specs = [q, k, v], [qspec(qo), kspec(ko), kspec(vo)]
    if bias is not None:
        ins += list(bias)
        specs += [qspec(0), pl.BlockSpec((8, seq), lambda b, j, i: (b * pairs + j, 0))]
    out = jax.ShapeDtypeStruct((n, LANES * pairs), F32)
    return pl.pallas_call(
        body, name=name, grid=(bsz, pairs, nq), in_specs=specs, out_specs=[qspec(0), qspec(0)], out_shape=[out, out],
        compiler_params=_params(("parallel", "parallel", "arbitrary")))(*ins)


def softmax_attn_bwd(q, k, v, o, do, lse, dlse, bias, *, qo, ko, vo, pairs, bsz, seq, window, dilation, tq, dq_dtype, dk_dtype, name):
    n = bsz * seq
    nq, qspec, kspec = _attn_specs(bsz, seq, tq, qo, ko, vo)
    has_bias, has_dlse = bias is not None, dlse is not None

    def body(*refs):
        refs = list(refs)
        q_ref, k_ref, v_ref, o_ref, do_ref, l_ref = refs[:6]
        del refs[:6]
        dl_ref = refs.pop(0) if has_dlse else None
        fq_ref, fk_ref = (refs.pop(0), refs.pop(0)) if has_bias else (None, None)
        dq_ref, dk_ref, dv_ref = refs[:3]
        del refs[:3]
        dfq_ref, dfk_ref = (refs.pop(0), refs.pop(0)) if has_bias else (None, None)
        dk_acc, dv_acc = refs
        qi = pl.program_id(2)

        @pl.when(qi == 0)
        def _():
            dk_acc[...] = jnp.zeros_like(dk_acc)
            dv_acc[...] = jnp.zeros_like(dv_acc)
            if has_bias:
                dfk_ref[...] = jnp.zeros_like(dfk_ref)

        def run(segs):
            qv = (q_ref[...] * SCALE).astype(MM)
            dov = do_ref[...]
            dob = dov.astype(MM)
            prod = dov * o_ref[...]
            first = _lane_first()
            keys = [(k_ref[pl.ds(st, w), :].astype(MM), v_ref[pl.ds(st, w), :].astype(MM),
                     _seg_mask((st, w, kind), qi, tq, None if window >= seq else window, dilation), st, w)
                    for st, w, kind in segs]
            dqs, dfqs = [], []
            dks, dvs = [[] for _ in keys], [[] for _ in keys]
            for a in range(2):
                mine = first if a == 0 else ~first
                col = slice(a * HEAD, a * HEAD + 1)
                delta = jnp.sum(jnp.where(mine, prod, 0.0), axis=1, keepdims=True)
                if has_dlse:
                    delta = delta - dl_ref[:, col]
                qa = jnp.where(mine, qv, jnp.zeros_like(qv))
                doa = jnp.where(mine, dob, jnp.zeros_like(dob))
                shift = l_ref[:, col]
                if has_bias:
                    shift = shift - fq_ref[:, col]
                dq, dfq = 0.0, 0.0
                for si, (kv, vv, ok, st, w) in enumerate(keys):
                    s = _dot_nt(qa, kv)
                    if has_bias:
                        s = s - fk_ref[a:a + 1, pl.ds(st, w)]
                    p = jnp.exp(s - shift)
                    if ok is not None:
                        p = jnp.where(ok, p, 0.0)
                    ds = p * (_dot_nt(doa, vv) - delta)
                    dsb = ds.astype(MM)
                    dvs[si].append(_dot_tn(p.astype(MM), dob))
                    dks[si].append(_dot_tn(dsb, qv))
                    dq = dq + _dot(dsb, kv)
                    if has_bias:
                        dfq = dfq + jnp.sum(ds, axis=1, keepdims=True)
                        dfk_ref[a:a + 1, pl.ds(st, w)] += jnp.sum(ds, axis=0, keepdims=True)
                dqs.append(dq * SCALE)
                dfqs.append(dfq)
            dq_ref[...] = jnp.where(first, dqs[0], dqs[1]).astype(dq_ref.dtype)
            for (_, _, _, st, w), dk, dv in zip(keys, dks, dvs):
                dk_acc[pl.ds(st, w), :] += jnp.where(first, dk[0], dk[1])
                dv_acc[pl.ds(st, w), :] += jnp.where(first, dv[0], dv[1])
            if has_bias:
                dfq_ref[...] = jnp.where(first, dfqs[0], dfqs[1])

        _key_plan(qi, tq, seq, window, run)

        @pl.when(qi == nq - 1)
        def _():
            dk_ref[...] = dk_acc[...].astype(dk_ref.dtype)
            dv_ref[...] = dv_acc[...].astype(dv_ref.dtype)

    wide = LANES * pairs
    ins = [q, k, v, o, do, lse]
    specs = [qspec(qo), kspec(ko), kspec(vo), qspec(0), qspec(0), qspec(0)]
    outs = [jax.ShapeDtypeStruct((n, wide), dq_dtype), jax.ShapeDtypeStruct((n, wide), dk_dtype), jax.ShapeDtypeStruct((n, wide), MM)]
    out_specs = [qspec(0), kspec(0), kspec(0)]
    if has_dlse:
        ins.append(dlse)
        specs.append(qspec(0))
    if has_bias:
        rows = pl.BlockSpec((8, seq), lambda b, j, i: (b * pairs + j, 0))
        ins += list(bias)
        specs += [qspec(0), rows]
        outs += [jax.ShapeDtypeStruct((n, wide), F32), jax.ShapeDtypeStruct((bsz * pairs * 8, seq), F32)]
        out_specs += [qspec(0), rows]
    return pl.pallas_call(
        body, name=name, grid=(bsz, pairs, nq), in_specs=specs, out_specs=out_specs, out_shape=outs,
        scratch_shapes=[pltpu.VMEM((seq, LANES), F32), pltpu.VMEM((seq, LANES), F32)],
        compiler_params=_params(("parallel", "parallel", "arbitrary")))(*ins)


def _running_sum(vals, mat, carry, lat_ref, start, reverse):
    nb = vals.shape[1] // LANES
    for cb in (reversed(range(nb)) if reverse else range(nb)):
        blk = vals[:, cb * LANES:(cb + 1) * LANES]
        lat_ref[:, start + cb * LANES:start + (cb + 1) * LANES] = _dot(blk.astype(BF16), mat) + carry
        carry = carry + jnp.sum(blk, axis=1, keepdims=True)
    return carry


def _sb_weights(qa, keys, tq, lat_ref):
    after = (_iota((LANES, LANES), 0) > _iota((LANES, LANES), 1)).astype(BF16)
    carry = jnp.zeros((tq, 1), F32)
    logs = []
    for kv, ok, st, w in reversed(keys):
        z = _dot_nt(qa, kv)
        _, sp = _softplus_parts(z)
        visible = sp if ok is None else jnp.where(ok, sp, 0.0)
        carry = _running_sum(visible, after, carry, lat_ref, st, True)
        logs.append(z - sp)
    out = []
    for (kv, ok, st, w), log_beta in zip(keys, reversed(logs)):
        att = jnp.exp(log_beta - lat_ref[:, st:st + w])
        out.append((log_beta, att if ok is None else jnp.where(ok, att, 0.0)))
    return out


def _sb_keys(k_ref, v_ref, segs, qi, tq):
    return [(k_ref[st:st + w, :].astype(MM), v_ref[st:st + w, :].astype(MM),
             _seg_mask((st, w, kind), qi, tq, None, 1, strict=True), st, w) for st, w, kind in segs]


def sb_attn_fwd(proj, *, bsz, seq, tq, name):
    n = bsz * seq
    pairs = 4
    nq, qspec, kspec = _attn_specs(bsz, seq, tq, SBQ, SBK, SBV)

    def body(q_ref, k_ref, v_ref, o_ref, lat_ref):
        qi = pl.program_id(2)

        def run(segs):
            qv = (q_ref[...] * SCALE).astype(MM)
            keys = _sb_keys(k_ref, v_ref, segs, qi, tq)
            first = _lane_first()
            outs = []
            for a in range(2):
                qa = jnp.where(first if a == 0 else ~first, qv, jnp.zeros_like(qv))
                weights = _sb_weights(qa, [(kv, ok, st, w) for kv, _, ok, st, w in keys], tq, lat_ref)
                outs.append(sum(_dot(att.astype(MM), vv) for (_, att), (_, vv, _, _, _) in zip(weights, keys)))
            o_ref[...] = jnp.where(first, outs[0], outs[1]).astype(o_ref.dtype)

        _key_plan(qi, tq, seq, seq, run)

    return pl.pallas_call(
        body, name=name, grid=(bsz, pairs, nq), in_specs=[qspec(SBQ), kspec(SBK), kspec(SBV)], out_specs=qspec(0),
        out_shape=jax.ShapeDtypeStruct((n, LANES * pairs), MM), scratch_shapes=[pltpu.VMEM((tq, seq), F32)],
        compiler_params=_params(("parallel", "parallel", "arbitrary")))(proj, proj, proj)


def sb_attn_bwd(proj, do, *, bsz, seq, tq, name):
    n = bsz * seq
    pairs = 4
    nq, qspec, kspec = _attn_specs(bsz, seq, tq, SBQ, SBK, SBV)

    def body(q_ref, k_ref, v_ref, do_ref, dq_ref, dk_ref, dv_ref, lat_ref, dk_acc, dv_acc):
        qi = pl.program_id(2)

        @pl.when(qi == 0)
        def _():
            dk_acc[...] = jnp.zeros_like(dk_acc)
            dv_acc[...] = jnp.zeros_like(dv_acc)

        def run(segs):
            qv = (q_ref[...] * SCALE).astype(MM)
            keys = _sb_keys(k_ref, v_ref, segs, qi, tq)
            dob = do_ref[...].astype(MM)
            first = _lane_first()
            before = (_iota((LANES, LANES), 0) < _iota((LANES, LANES), 1)).astype(BF16)
            dqs = []
            dks, dvs = [[] for _ in keys], [[] for _ in keys]
            for a in range(2):
                mine = first if a == 0 else ~first
                qa = jnp.where(mine, qv, jnp.zeros_like(qv))
                doa = jnp.where(mine, dob, jnp.zeros_like(dob))
                weights = _sb_weights(qa, [(kv, ok, st, w) for kv, _, ok, st, w in keys], tq, lat_ref)
                gs = [_dot_nt(doa, vv) * att for (_, att), (_, vv, _, _, _) in zip(weights, keys)]
                carry = jnp.zeros((tq, 1), F32)
                for g, (_, _, _, st, w) in zip(gs, keys):
                    carry = _running_sum(g, before, carry, lat_ref, st, False)
                dq = 0.0
                for si, ((log_beta, att), g, (kv, _, ok, st, w)) in enumerate(zip(weights, gs, keys)):
                    dz = g - jnp.exp(log_beta) * (g + lat_ref[:, st:st + w])
                    dz = (dz if ok is None else jnp.where(ok, dz, 0.0)).astype(MM)
                    dvs[si].append(_dot_tn(att.astype(MM), dob))
                    dks[si].append(_dot_tn(dz, qv))
                    dq = dq + _dot(dz, kv)
                dqs.append(dq * SCALE)
            dq_ref[...] = jnp.where(first, dqs[0], dqs[1]).astype(dq_ref.dtype)
            for (_, _, _, st, w), dk, dv in zip(keys, dks, dvs):
                dk_acc[st:st + w, :] += jnp.where(first, dk[0], dk[1])
                dv_acc[st:st + w, :] += jnp.where(first, dv[0], dv[1])

        _key_plan(qi, tq, seq, seq, run)

        @pl.when(qi == nq - 1)
        def _():
            dk_ref[...] = dk_acc[...].astype(dk_ref.dtype)
            dv_ref[...] = dv_acc[...].astype(dv_ref.dtype)

    out = jax.ShapeDtypeStruct((n, LANES * pairs), MM)
    return pl.pallas_call(
        body, name=name, grid=(bsz, pairs, nq), in_specs=[qspec(SBQ), kspec(SBK), kspec(SBV), qspec(0)],
        out_specs=[qspec(0), kspec(0), kspec(0)], out_shape=[out, out, out],
        scratch_shapes=[pltpu.VMEM((tq, seq), F32), pltpu.VMEM((seq, LANES), F32), pltpu.VMEM((seq, LANES), F32)],
        compiler_params=_params(("parallel", "parallel", "arbitrary")))(proj, proj, proj, do)


def _place():
    return lax.axis_index("x"), lax.axis_index("y"), lax.axis_index("c")


def _other_chips(x, y):
    return [(1 - x, y), (x, 1 - y), (1 - x, 1 - y)]


def _remote(src, dst, send_sems, recv_sems, k, to):
    return pltpu.make_async_remote_copy(src_ref=src, dst_ref=dst, send_sem=send_sems.at[k], recv_sem=recv_sems.at[k],
                                        device_id=to, device_id_type=MESH_ID)


def gather_chips(arrs, *, name):
    na = len(arrs)

    def body(*refs):
        ins, outs = refs[:na], refs[na:2 * na]
        send_sems, recv_sems = refs[2 * na:]
        x, y, c = _place()
        me, sibling = 2 * x + y, (x, y, 1 - c)
        chips = _other_chips(x, y)
        sends = []
        for t in range(na):
            rh = ins[t].shape[0] // 2
            half = lambda chip, h, t=t, rh=rh: outs[t].at[chip, pl.ds(h * rh, rh), :]
            for j, (px, py) in enumerate(chips):
                cp = _remote(ins[t].at[pl.ds(c * rh, rh), :], half(me, c), send_sems, recv_sems, 6 * t + j, (px, py, c))
                cp.start()
                sends.append(cp)
        for t in range(na):
            rh = ins[t].shape[0] // 2
            half = lambda chip, h, t=t, rh=rh: outs[t].at[chip, pl.ds(h * rh, rh), :]
            for j, (px, py) in enumerate(chips):
                landed = half(2 * px + py, c)
                _remote(landed, landed, send_sems, recv_sems, 6 * t + j, (px, py, c)).wait_recv()
                fw = _remote(landed, landed, send_sems, recv_sems, 6 * t + 3 + j, sibling)
                fw.start()
                sends.append(fw)
        for t in range(na):
            rh = ins[t].shape[0] // 2
            half = lambda chip, h, t=t, rh=rh: outs[t].at[chip, pl.ds(h * rh, rh), :]
            for j, (px, py) in enumerate(chips):
                passed = half(2 * px + py, 1 - c)
                _remote(passed, passed, send_sems, recv_sems, 6 * t + 3 + j, sibling).wait_recv()
        for cp in sends:
            cp.wait_send()

    for a in arrs:
        assert a.ndim == 2 and a.shape[0] % 32 == 0, a.shape
    return pl.pallas_call(
        body, name=name, in_specs=[ANY] * na, out_specs=[ANY] * na,
        out_shape=[jax.ShapeDtypeStruct((4,) + a.shape, a.dtype) for a in arrs],
        scratch_shapes=[pltpu.SemaphoreType.DMA((6 * na,)), pltpu.SemaphoreType.DMA((6 * na,))],
    )(*arrs)


def _chunk_rows(rows, cols, limit):
    best = 16
    for t in range(16, rows + 1, 16):
        if rows % t == 0 and t * cols * 4 <= limit:
            best = t
    assert rows % best == 0, (rows, cols)
    return best


def pair_sum(a, core, *, name):
    _, rows, cols = a.shape
    rh = rows // 2
    tr = _chunk_rows(rh, cols, 2 << 20)
    nch = rh // tr
    steps = 4 * nch

    def body(core_ref, keep_ref, send_ref, o32_ref, o16_ref, landing, send_sems, recv_sems, credit):
        step = pl.program_id(0) * nch + pl.program_id(1)
        slot = lax.rem(step, 2)
        x, y, c = _place()
        sibling = (x, y, 1 - c)

        @pl.when(step >= 2)
        def _():
            pl.semaphore_wait(credit, 1)

        cp = _remote(send_ref.at[0], landing.at[slot], send_sems, recv_sems, slot, sibling)
        cp.start()
        cp.wait_recv()
        total = keep_ref[0] + landing[slot]
        o32_ref[0] = total
        o16_ref[0] = total.astype(BF16)
        cp.wait_send()

        @pl.when(step + 2 < steps)
        def _():
            pl.semaphore_signal(credit, 1, device_id=sibling, device_id_type=MESH_ID)

    blk = (1, tr, cols)
    grid_spec = pltpu.PrefetchScalarGridSpec(
        num_scalar_prefetch=1, grid=(4, nch),
        in_specs=[pl.BlockSpec(blk, lambda k, i, core: (k, core[0] * nch + i, 0)),
                  pl.BlockSpec(blk, lambda k, i, core: (k, (1 - core[0]) * nch + i, 0))],
        out_specs=[pl.BlockSpec(blk, lambda k, i, core: (k, i, 0))] * 2,
        scratch_shapes=[pltpu.VMEM((2, tr, cols), F32), pltpu.SemaphoreType.DMA((2,)), pltpu.SemaphoreType.DMA((2,)),
                        pltpu.SemaphoreType.REGULAR])
    return pl.pallas_call(
        body, name=name, grid_spec=grid_spec,
        out_shape=[jax.ShapeDtypeStruct((4, rh, cols), F32), jax.ShapeDtypeStruct((4, rh, cols), BF16)],
        compiler_params=_params(("arbitrary", "arbitrary")))(core, a, a)


def scatter_chips(arrs, *, name):
    na = len(arrs)

    def body(*refs):
        ins, outs = refs[:na], refs[na:2 * na]
        send_sems, recv_sems = refs[2 * na:]
        x, y, c = _place()
        me = 2 * x + y
        chips = _other_chips(x, y)
        sends = []
        for t in range(na):
            for j, (px, py) in enumerate(chips):
                cp = _remote(ins[t].at[2 * px + py], outs[t].at[me], send_sems, recv_sems, 3 * t + j, (px, py, c))
                cp.start()
                sends.append(cp)
        for t in range(na):
            for j, (px, py) in enumerate(chips):
                slab = outs[t].at[2 * px + py]
                _remote(slab, slab, send_sems, recv_sems, 3 * t + j, (px, py, c)).wait_recv()
        for cp in sends:
            cp.wait_send()

    return pl.pallas_call(
        body, name=name, in_specs=[ANY] * na, out_specs=[ANY] * na,
        out_shape=[jax.ShapeDtypeStruct(a.shape, a.dtype) for a in arrs],
        scratch_shapes=[pltpu.SemaphoreType.DMA((3 * na,)), pltpu.SemaphoreType.DMA((3 * na,))],
    )(*arrs)


def chip_sum_join(own, landed, chip, *, name):
    _, rh, cols = own.shape
    tr = _chunk_rows(rh, cols, 2 << 20)
    nch = rh // tr

    def body(chip_ref, own_ref, l1_ref, l2_ref, l3_ref, out_ref, res, local_sem, send_sem, recv_sem):
        i = pl.program_id(0)
        x, y, c = _place()
        sibling = (x, y, 1 - c)
        res[...] = ((own_ref[0] + l1_ref[0].astype(F32)) + l2_ref[0].astype(F32)) + l3_ref[0].astype(F32)
        rows = pl.ds(pl.multiple_of(i * tr, tr), tr)
        here = pltpu.make_async_copy(res, out_ref.at[c, rows, :], local_sem)
        here.start()
        there = pltpu.make_async_remote_copy(src_ref=res, dst_ref=out_ref.at[c, rows, :], send_sem=send_sem, recv_sem=recv_sem,
                                             device_id=sibling, device_id_type=MESH_ID)
        there.start()
        here.wait()
        there.wait_send()

        @pl.when(i == nch - 1)
        def _():
            half = out_ref.at[1 - c]
            pltpu.make_async_remote_copy(src_ref=half, dst_ref=half, send_sem=send_sem, recv_sem=recv_sem,
                                         device_id=sibling, device_id_type=MESH_ID).wait_recv()

    blk = (1, tr, cols)
    slab = lambda p: pl.BlockSpec(blk, lambda i, chip: (chip[0] ^ p, i, 0))
    grid_spec = pltpu.PrefetchScalarGridSpec(
        num_scalar_prefetch=1, grid=(nch,), in_specs=[slab(0), slab(1), slab(2), slab(3)], out_specs=ANY,
        scratch_shapes=[pltpu.VMEM((tr, cols), F32), pltpu.SemaphoreType.DMA, pltpu.SemaphoreType.DMA, pltpu.SemaphoreType.DMA])
    return pl.pallas_call(
        body, name=name, grid_spec=grid_spec, out_shape=jax.ShapeDtypeStruct((2, rh, cols), F32),
        compiler_params=_params(("arbitrary",)))(chip, own, landed, landed, landed)


def all_reduce_small(a, *, name):
    def body(a_ref, o_ref, buf, send_sems, recv_sems):
        x, y, c = _place()
        me = 4 * x + 2 * y + c
        buf[me] = a_ref[...]
        sent = []
        for p in range(1, 8):
            px, py, pc = (p >> 2) & 1, (p >> 1) & 1, p & 1
            cp = _remote(a_ref, buf.at[me], send_sems, recv_sems, p - 1, (x ^ px, y ^ py, c ^ pc))
            cp.start()
            sent.append(cp)
        for p in range(1, 8):
            px, py, pc = (p >> 2) & 1, (p >> 1) & 1, p & 1
            src = 4 * (x ^ px) + 2 * (y ^ py) + (c ^ pc)
            _remote(a_ref, buf.at[src], send_sems, recv_sems, p - 1, (x ^ px, y ^ py, c ^ pc)).wait_recv()
        for cp in sent:
            cp.wait_send()
        acc = buf[0]
        for d in range(1, 8):
            acc = acc + buf[d]
        o_ref[...] = acc

    vm = pl.BlockSpec(memory_space=pltpu.VMEM)
    return pl.pallas_call(
        body, name=name, in_specs=[vm], out_specs=vm, out_shape=jax.ShapeDtypeStruct(a.shape, a.dtype),
        scratch_shapes=[pltpu.VMEM((8,) + a.shape, a.dtype), pltpu.SemaphoreType.DMA((7,)), pltpu.SemaphoreType.DMA((7,))],
    )(a)


TQ = 256


def _pad_w_in(w):
    pad = jnp.zeros((w.shape[0], DPROJ - DIN), w.dtype)
    return jnp.concatenate([w[:, :O1], w[:, O2:], w[:, O1:O2], pad], axis=1)


def _layer_small(sm, l):
    row = lambda v: v.reshape(1, -1)
    return dict(
        attn_norm=row(sm["attn_norm"][l]), mlp_norm=row(sm["mlp_norm"][l]),
        qgf=row(jnp.tile(sm["q_norm_fox"][l], 8)), kgf=row(jnp.tile(sm["k_norm_fox"][l], 8)),
        qgd=row(jnp.tile(sm["q_norm_dil"][l], 12)), kgd=row(jnp.tile(sm["k_norm_dil"][l], 12)),
        bfor=row(jnp.pad(sm["b_forget"][l], (0, LANES - 8))))


def _key_rows(f8, bsz, seq):
    f = f8.reshape(bsz, seq, LANES)[:, :, :8].transpose(0, 2, 1).reshape(bsz, 4, 2, seq)
    return jnp.pad(f, ((0, 0), (0, 0), (0, 6), (0, 0))).reshape(bsz * 32, seq)


def _layer_fwd(x, w, s, cos, sin, bsz, seq, l):
    nm = lambda t: f"l{l}_{t}"
    h = rmsnorm_fwd(x, s["attn_norm"], name=nm("attn_norm"))
    proj = matmul(h, w["win"], name=nm("proj"))
    qn, kn, fb, f8 = fox_prep_fwd(proj, s["qgf"], s["kgf"], s["bfor"], bsz=bsz, seq=seq, name=nm("fox_prep"))
    fk = _key_rows(f8, bsz, seq)
    oa, la = softmax_attn_fwd(qn, kn, proj, (fb, fk), qo=0, ko=0, vo=FOXV, pairs=4, bsz=bsz, seq=seq, window=seq, dilation=1,
                              tq=TQ, name=nm("fox_attn"))
    ob = sb_attn_fwd(proj, bsz=bsz, seq=seq, tq=TQ, name=nm("sb_attn"))
    qr, kr = dil_prep_fwd(proj, s["qgd"], s["kgd"], cos, sin, name=nm("dil_prep"))
    ogs, lgs = [], []
    for g, (window, dilation) in enumerate(DIL_PATTERNS):
        og, lg = softmax_attn_fwd(qr, kr, proj, None, qo=2 * g, ko=2 * g, vo=DILV + 2 * g, pairs=2, bsz=bsz, seq=seq,
                                  window=window, dilation=dilation, tq=TQ, name=nm(f"dil_attn{g}"))
        ogs.append(og)
        lgs.append(lg)
    oc = dil_combine_fwd(ogs, lgs, name=nm("dil_combine"))
    ys = [matmul(oa, w["wuf"], name=nm("up_fox")), matmul(ob, w["wus"], name=nm("up_sb")), matmul(oc, w["wud"], name=nm("up_dil"))]
    merged = merge_fwd(proj, ys, name=nm("merge"))
    x1 = matmul(merged, w["wo"], add=x, name=nm("out_proj"))
    h2 = rmsnorm_fwd(x1, s["mlp_norm"], name=nm("mlp_norm"))
    u, act = matmul(h2, w["wmi"], relu2=True, name=nm("mlp_in"))
    x2 = matmul(act, w["wmo"], add=x1, name=nm("mlp_out"))
    saved = dict(x=x, h=h, proj=proj, qn=qn, kn=kn, fb=fb, fk=fk, oa=oa, la=la, ob=ob, qr=qr, kr=kr, ogs=ogs, lgs=lgs, oc=oc,
                 ys=ys, merged=merged, x1=x1, h2=h2, u=u, act=act)
    return x2, saved


WIN_TILE = 256
WIN_STRIDE, WIN_TILES = 8, 9


def grad_buffers(depth, d, dff, wf, wd):
    assert dff // 4 == d
    return dict(win=lax.empty((4, depth * d, WIN_TILES * WIN_TILE), F32), ups=lax.empty((4, depth * (2 * wf + wd), d // 4), F32),
                wide=lax.empty((4, depth * (d + dff // 4 + d // 4), d), F32))


def _layer_bwd(dx2, w, s, sv, cos, sin, bsz, seq, l, depth, bufs):
    nm = lambda t: f"l{l}_{t}_bwd"
    n = bsz * seq
    proj = sv["proj"]
    d, dff = w["wmi"].shape
    wf, wd = w["wuf"].shape[0], w["wud"].shape[0]
    bufs = dict(bufs)
    rb = 512
    per_chip = dff // 4 // rb
    du = matmul(dx2, w["wmo"], tb=True, relu2_of=sv["u"], out_dtype=MM, name=nm("mlp_out_dx"))
    bufs["wide"] = matmul(sv["act"], dx2, ta=True, tm=rb, tn=d, tk=2048, name=nm("mlp_out_dw"),
                          dest=(bufs["wide"], 1, lambda j: j,
                                lambda i, j: (i // per_chip, (depth * d + l * (dff // 4)) // rb + i % per_chip, j)))
    dh2 = matmul(du, w["wmi"], tb=True, name=nm("mlp_in_dx"))
    bufs["wide"] = matmul(sv["h2"], du, ta=True, tm=rb, tn=dff // 4, tk=2048, name=nm("mlp_in_dw"),
                          dest=(bufs["wide"], 4, lambda j: j, lambda i, j: (j, l * d // rb + i, 0)))
    dx1, g_mlp_norm = rmsnorm_bwd(sv["x1"], s["mlp_norm"], dh2, dx2, name=nm("mlp_norm"))

    dmerged = matmul(dx1, w["wo"], tb=True, name=nm("out_proj_dx"))
    bufs["wide"] = matmul(sv["merged"], dx1, ta=True, tm=d // 4, tn=d, tk=2048, name=nm("out_proj_dw"),
                          dest=(bufs["wide"], 1, lambda j: j, lambda i, j: (i, (depth * (d + dff // 4)) // (d // 4) + l, j)))
    dya, dyb, dyc, dga, dgb, dgc = merge_bwd(proj, sv["ys"], dmerged, name=nm("merge"))
    doa = matmul(dya, w["wuf"], tb=True, name=nm("up_fox_dx"))
    bufs["ups"] = matmul(sv["oa"], dya, ta=True, tm=wf, tn=d // 4, tk=2048, name=nm("up_fox_dw"),
                         dest=(bufs["ups"], 4, lambda j: j, lambda i, j: (j, l, 0)))
    dob = matmul(dyb, w["wus"], tb=True, name=nm("up_sb_dx"))
    bufs["ups"] = matmul(sv["ob"], dyb, ta=True, tm=wf, tn=d // 4, tk=2048, name=nm("up_sb_dw"),
                         dest=(bufs["ups"], 4, lambda j: j, lambda i, j: (j, depth + l, 0)))
    doc = matmul(dyc, w["wud"], tb=True, name=nm("up_dil_dx"))
    bufs["ups"] = matmul(sv["oc"], dyc, ta=True, tm=wd, tn=d // 4, tk=2048, name=nm("up_dil_dw"),
                         dest=(bufs["ups"], 4, lambda j: j, lambda i, j: (j, 2 * depth * wf // wd + l, 0)))

    outs = dil_combine_bwd(sv["ogs"], sv["lgs"], doc, name=nm("dil_combine"))
    dqs, dks, dvs = [], [], []
    for g, (window, dilation) in enumerate(DIL_PATTERNS):
        dq, dk, dv = softmax_attn_bwd(sv["qr"], sv["kr"], proj, sv["ogs"][g], outs[g], sv["lgs"][g], outs[3 + g], None,
                                      qo=2 * g, ko=2 * g, vo=DILV + 2 * g, pairs=2, bsz=bsz, seq=seq, window=window,
                                      dilation=dilation, tq=TQ, dq_dtype=F32, dk_dtype=F32, name=nm(f"dil_attn{g}"))
        dqs.append(dq)
        dks.append(dk)
        dvs.append(dv)
    d_dq, d_dk, g_qgd, g_kgd = dil_prep_bwd(proj, s["qgd"], s["kgd"], cos, sin, jnp.concatenate(dqs, axis=1),
                                            jnp.concatenate(dks, axis=1), name=nm("dil_prep"))

    s_dq, s_dk, s_dv = sb_attn_bwd(proj, dob, bsz=bsz, seq=seq, tq=TQ, name=nm("sb_attn"))

    dqn, dkn, f_dv, dfq, dfk = softmax_attn_bwd(sv["qn"], sv["kn"], proj, sv["oa"], doa, sv["la"], None, (sv["fb"], sv["fk"]),
                                                qo=0, ko=0, vo=FOXV, pairs=4, bsz=bsz, seq=seq, window=seq, dilation=1, tq=TQ,
                                                dq_dtype=F32, dk_dtype=F32, name=nm("fox_attn"))
    dfk8 = dfk.reshape(bsz, 4, 8, seq)[:, :, :2].reshape(bsz, 8, seq).transpose(0, 2, 1).reshape(n, 8)
    df = jnp.pad(dfq[:, ::HEAD] - dfk8, ((0, 0), (0, LANES - 8)))
    f_dq, f_dk, d_forget, g_qgf, g_kgf, g_bfor = fox_prep_bwd(proj, s["qgf"], s["kgf"], s["bfor"], dqn, dkn, df, bsz=bsz, seq=seq,
                                                              name=nm("fox_prep"))

    dproj = jnp.concatenate([f_dq, f_dk, f_dv, s_dq, s_dk, s_dv, d_dq, d_dk] + dvs + [dga, dgb, dgc, d_forget], axis=1)
    dh = matmul(dproj, w["win"], tb=True, tn=1024, tk=512, name=nm("proj_dx"))
    bufs["win"] = matmul(sv["h"], dproj, ta=True, tm=d, tn=WIN_TILE, tk=2048, name=nm("proj_dw"),
                         dest=(bufs["win"], 4 * WIN_TILES, lambda j: WIN_STRIDE * (j // WIN_TILES) + j % WIN_TILES,
                               lambda i, j: (j // WIN_TILES, l, j % WIN_TILES)))
    g_forget = matmul(sv["h"], d_forget, ta=True, tk=2048, name=nm("forget_dw"))[:, :O2 - O1]
    dx, g_attn_norm = rmsnorm_bwd(sv["x"], s["attn_norm"], dh, dx1, name=nm("attn_norm"))
    gs = dict(attn_norm=g_attn_norm[0], mlp_norm=g_mlp_norm[0], b_forget=g_bfor[0, :8],
              q_norm_fox=g_qgf.reshape(8, HEAD).sum(0), k_norm_fox=g_kgf.reshape(8, HEAD).sum(0),
              q_norm_dil=g_qgd.reshape(12, HEAD).sum(0), k_norm_dil=g_kgd.reshape(12, HEAD).sum(0), w_in_forget=g_forget)
    return dx, bufs, gs


def local_step(x, positions, target, weights, small):
    bsz, seq, d = x.shape
    n = bsz * seq
    depth = len(weights)
    inv = 1.0 / (ROPE_THETA ** (jnp.arange(HEAD // 2, dtype=F32) / (HEAD // 2)))
    cos, sin = rope_table(positions.reshape(n, 1), jnp.tile(inv, 4).reshape(1, LANES), name="rope_table")
    xs = x.reshape(n, d)
    saved = []
    for l in range(depth):
        xs, sv = _layer_fwd(xs, weights[l], _layer_small(small, l), cos, sin, bsz, seq, l)
        saved.append(sv)
    dy, sq = loss_grad(xs, target.reshape(n, d), name="loss")
    loss = (0.5 / d) * jnp.sum(sq)
    w0 = weights[0]
    bufs = grad_buffers(depth, d, w0["wmi"].shape[1], w0["wuf"].shape[0], w0["wud"].shape[0])
    gss = [None] * depth
    for l in reversed(range(depth)):
        dy, bufs, gss[l] = _layer_bwd(dy, weights[l], _layer_small(small, l), saved[l], cos, sin, bsz, seq, l, depth, bufs)
    return loss, dy.reshape(bsz, seq, d), bufs, gss


SMALL = ("attn_norm", "mlp_norm", "b_forget", "q_norm_fox", "k_norm_fox", "q_norm_dil", "k_norm_dil")
SMALL_ROWS = 8


def _pack_small(vals):
    flat = jnp.concatenate([vals[k].reshape(-1) for k in SMALL])
    return jnp.pad(flat, (0, SMALL_ROWS * 1024 - flat.shape[0])).reshape(SMALL_ROWS, 1024)


def _unpack_small(packed, like):
    flat, out, at = packed.reshape(-1), {}, 0
    for k in SMALL:
        size = like[k].size
        out[k] = flat[at:at + size].reshape(like[k].shape)
        at += size
    return out


def _chips_to_cols(a, depth):
    _, rows, c = a.shape
    return a.reshape(4, depth, rows // depth, c).transpose(1, 2, 0, 3).reshape(depth, rows // depth, 4 * c)


def _chips_to_rows(a, depth):
    _, rows, c = a.shape
    return a.reshape(4, depth, rows // depth, c).transpose(1, 0, 2, 3).reshape(depth, 4 * rows // depth, c)


def kernel(x, positions, attn_norm, w_in, b_forget, q_norm_fox, k_norm_fox, q_norm_dil, k_norm_dil, w_up_fox, w_up_sb, w_up_dil, w_out, mlp_norm, w_mlp_in, w_mlp_out, loss_target, m_attn_norm, m_w_in, m_b_forget, m_q_norm_fox, m_k_norm_fox, m_q_norm_dil, m_k_norm_dil, m_w_up_fox, m_w_up_sb, m_w_up_dil, m_w_out, m_mlp_norm, m_w_mlp_in, m_w_mlp_out, v_attn_norm, v_w_in, v_b_forget, v_q_norm_fox, v_k_norm_fox, v_q_norm_dil, v_k_norm_dil, v_w_up_fox, v_w_up_sb, v_w_up_dil, v_w_out, v_mlp_norm, v_w_mlp_in, v_w_mlp_out):
    names = ("attn_norm", "w_in", "b_forget", "q_norm_fox", "k_norm_fox", "q_norm_dil", "k_norm_dil", "w_up_fox", "w_up_sb",
             "w_up_dil", "w_out", "mlp_norm", "w_mlp_in", "w_mlp_out")
    wv = dict(zip(names, (attn_norm, w_in, b_forget, q_norm_fox, k_norm_fox, q_norm_dil, k_norm_dil, w_up_fox, w_up_sb, w_up_dil,
                          w_out, mlp_norm, w_mlp_in, w_mlp_out)))
    mv = dict(zip(names, (m_attn_norm, m_w_in, m_b_forget, m_q_norm_fox, m_k_norm_fox, m_q_norm_dil, m_k_norm_dil, m_w_up_fox,
                          m_w_up_sb, m_w_up_dil, m_w_out, m_mlp_norm, m_w_mlp_in, m_w_mlp_out)))
    vv = dict(zip(names, (v_attn_norm, v_w_in, v_b_forget, v_q_norm_fox, v_k_norm_fox, v_q_norm_dil, v_k_norm_dil, v_w_up_fox,
                          v_w_up_sb, v_w_up_dil, v_w_out, v_mlp_norm, v_w_mlp_in, v_w_mlp_out)))
    depth = w_in.shape[0]
    flat2 = lambda a: a.reshape(-1, a.shape[-1])

    ups = ("w_up_fox", "w_up_sb", "w_up_dil")
    wide = ("w_mlp_in", "w_mlp_out", "w_out")
    send = [flat2(w_in).astype(MM), jnp.concatenate([flat2(wv[k]) for k in ups]).astype(MM),
            jnp.concatenate([flat2(wv[k]) for k in wide]).astype(MM)]
    core = lax.axis_index("c").astype(jnp.int32).reshape(1)
    chip = (2 * lax.axis_index("x") + lax.axis_index("y")).astype(jnp.int32).reshape(1)
    got_in, got_up, got_wide = [lax.dynamic_update_index_in_dim(g, s, chip[0], 0)
                                for g, s in zip(gather_chips(send, name="gather_weights"), send)]

    def split(a, keys):
        out, at = {}, 0
        for k in keys:
            rows = wv[k].shape[0] * wv[k].shape[1]
            out[k] = a[:, at:at + rows]
            at += rows
        return out

    full = {"w_in": _chips_to_cols(got_in, depth)}
    full.update({k: _chips_to_cols(a, depth) for k, a in split(got_up, ups).items()})
    parts = split(got_wide, wide)
    full["w_out"] = _chips_to_rows(parts["w_out"], depth)
    full["w_mlp_in"] = _chips_to_cols(parts["w_mlp_in"], depth)
    full["w_mlp_out"] = _chips_to_rows(parts["w_mlp_out"], depth)
    weights = [dict(win=_pad_w_in(full["w_in"][l]), wuf=full["w_up_fox"][l], wus=full["w_up_sb"][l], wud=full["w_up_dil"][l],
                    wo=full["w_out"][l], wmi=full["w_mlp_in"][l], wmo=full["w_mlp_out"][l]) for l in range(depth)]
    small = {k: wv[k] for k in SMALL}

    loss, grad_x, bufs, gss = local_step(x, positions, loss_target, weights, small)
    loss = lax.psum(loss, ("x", "y", "c"))

    g_small = {k: jnp.stack([gss[l][k] for l in range(depth)]) for k in SMALL}
    g_forget = jnp.stack([gss[l]["w_in_forget"] for l in range(depth)])
    summed = all_reduce_small(jnp.concatenate([_pack_small(g_small), g_forget.reshape(-1, 1024)]), name="reduce_small")
    g_small = _unpack_small(summed[:SMALL_ROWS], small)
    g_forget = summed[SMALL_ROWS:].reshape(g_forget.shape)

    parts = [bufs["win"], bufs["ups"], bufs["wide"]]
    sums = [pair_sum(p, core, name=f"reduce_pair_sum{t}") for t, p in enumerate(parts)]
    landed = scatter_chips([s16 for _, s16 in sums], name="reduce_scatter_chips")
    joined = [chip_sum_join(sums[t][0], landed[t], chip, name=f"reduce_chip_sum{t}").reshape(-1, parts[t].shape[-1])
              for t in range(3)]

    def own_w_in_columns(window):
        cols = w_in.shape[-1]
        first = jnp.concatenate([window[..., :O1], g_forget, window[..., O1:cols - (O2 - O1)]], axis=-1)
        shift = jnp.maximum((cols - WIN_STRIDE * WIN_TILE) * chip[0] - (O2 - O1), 0)
        rest = lax.dynamic_slice_in_dim(window, shift, cols, axis=2)
        return jnp.where(chip[0] == 0, first, rest)

    g_big = {"w_in": own_w_in_columns(joined[0].reshape(depth, -1, joined[0].shape[-1]))}
    for a, keys in ((joined[1], ups), (joined[2], wide)):
        at = 0
        for k in keys:
            rows = wv[k].shape[0] * wv[k].shape[1]
            g_big[k] = a[at:at + rows].reshape(wv[k].shape)
            at += rows

    grads = {**g_small, **g_big}
    delta, new_m, new_v = {}, {}, {}
    d_s, m_s, v_s = adamw(_pack_small(small), _pack_small(g_small), _pack_small({k: mv[k] for k in SMALL}),
                          _pack_small({k: vv[k] for k in SMALL}), name="adamw_small")
    delta.update(_unpack_small(d_s, small))
    new_m.update(_unpack_small(m_s, small))
    new_v.update(_unpack_small(v_s, small))
    for k in ("w_in",) + ups + wide:
        d_k, m_k, v_k = adamw(flat2(wv[k]), flat2(g_big[k]), flat2(mv[k]), flat2(vv[k]), name=f"adamw_{k}")
        delta[k], new_m[k], new_v[k] = d_k.reshape(wv[k].shape), m_k.reshape(wv[k].shape), v_k.reshape(wv[k].shape)

    return (loss, grad_x, *[grads[k] for k in names], *[delta[k] for k in names], *[new_m[k] for k in names], *[new_v[k] for k in names])
```

```python
import functools

import jax
import jax.numpy as jnp
from jax import lax
from jax.experimental import pallas as pl
from jax.experimental.pallas import tpu as pltpu

F32 = jnp.float32
BF16 = jnp.bfloat16
MM = jnp.bfloat16

HEAD = 64
LANES = 128
EPS = 1e-6
SCALE = 0.125
ROPE_THETA = 10000.0
DIL_PATTERNS = ((128, 1), (512, 4), (2048, 16))
ADAM_LR, ADAM_B1, ADAM_B2, ADAM_EPS, ADAM_WD, ADAM_STEP = 0.001, 0.9, 0.999, 1e-08, 0.01, 10

FOXQ, FOXK, FOXV = 0, 4, 8
SBQ, SBK, SBV = 12, 16, 20
DILQ, DILK, DILV = 24, 30, 36
GATE, FORGET, NBLK = 42, 66, 68
DPROJ = NBLK * LANES
O1, O2, O3, O4, DIN = 1536, 1544, 3080, 5384, 8456

VMEM_LIMIT = 56 * 1024 * 1024
MESH_ID = pl.DeviceIdType.MESH
ANY = pl.BlockSpec(memory_space=pl.ANY)


def _params(sem=None):
    return pltpu.CompilerParams(dimension_semantics=sem, vmem_limit_bytes=VMEM_LIMIT)


def _iota(shape, dim):
    return lax.broadcasted_iota(jnp.int32, shape, dim)


def _split2(x):
    hi = x.astype(BF16)
    lo = (x - hi.astype(F32)).astype(BF16)
    return hi, lo


def _split3(x):
    hi = x.astype(BF16)
    r = x - hi.astype(F32)
    mid = r.astype(BF16)
    lo = (r - mid.astype(F32)).astype(BF16)
    return hi, mid, lo


def _dot(a, b):
    return jnp.dot(a, b, preferred_element_type=F32)


def _dot_nt(a, b):
    return lax.dot_general(a, b, (((1,), (1,)), ((), ())), preferred_element_type=F32)


def _dot_tn(a, b):
    return lax.dot_general(a, b, (((0,), (0,)), ((), ())), preferred_element_type=F32)


def _xdot2(x, m):
    hi, lo = _split2(x)
    return _dot(hi, m) + _dot(lo, m)


def _xdot3(x, m):
    hi, mid, lo = _split3(x)
    return _dot(hi, m) + _dot(mid, m) + _dot(lo, m)


def _xdot3_left(m, x):
    hi, mid, lo = _split3(x)
    return _dot(m, hi) + _dot(m, mid) + _dot(m, lo)


def _head_mat(w):
    return ((_iota((w, w), 0) >> 6) == (_iota((w, w), 1) >> 6)).astype(BF16)


def _softplus_parts(z):
    e = jnp.exp(-jnp.abs(z))
    return e, jnp.maximum(z, 0.0) + jnp.log(1.0 + e)


def _fit(dim, want):
    t = min(want, dim)
    while dim % t:
        t -= LANES
        assert t > 0, (dim, want)
    return t


def matmul(a, b, *, ta=False, tb=False, out_dtype=F32, add=None, tm=2048, tn=512, tk=1024, dest=None, relu2=False,
           relu2_of=None, name):
    K, M = a.shape if ta else a.shape[::-1]
    K2, N = b.shape[::-1] if tb else b.shape
    assert K == K2, (a.shape, b.shape, ta, tb)
    tm, tn, tk = _fit(M, tm), _fit(N, tn), _fit(K, tk)
    nk = K // tk
    dn = (((0 if ta else 1,), (1 if tb else 0,)), ((), ()))
    if dest is None:
        tiles, source = N // tn, lambda j: j
    else:
        assert add is None and not tb
        buffer, tiles, source, place = dest

    extra = add if add is not None else relu2_of
    assert add is None or relu2_of is None

    def body(*refs):
        act_ref = None
        if dest is not None:
            a_ref, b_ref, _, o_ref, acc_ref = refs
        elif relu2:
            a_ref, b_ref, o_ref, act_ref, acc_ref = refs
        elif extra is None:
            a_ref, b_ref, o_ref, acc_ref = refs
        else:
            a_ref, b_ref, add_ref, o_ref, acc_ref = refs
        k = pl.program_id(2)
        part = lax.dot_general(a_ref[...].astype(MM), b_ref[...].astype(MM), dn, preferred_element_type=F32)

        @pl.when(k == 0)
        def _():
            acc_ref[...] = part

        @pl.when(k > 0)
        def _():
            acc_ref[...] += part

        @pl.when(k == nk - 1)
        def _():
            r = acc_ref[...]
            if add is not None:
                r = r + add_ref[...]
            if relu2_of is not None:
                r = r * (2.0 * jnp.maximum(add_ref[...], 0.0))
            o_ref[...] = r.astype(o_ref.dtype).reshape(o_ref.shape)
            if act_ref is not None:
                pos = jnp.maximum(r, 0.0)
                act_ref[...] = (pos * pos).astype(act_ref.dtype)

    a_spec = pl.BlockSpec((tk, tm), lambda i, j, k: (k, i)) if ta else pl.BlockSpec((tm, tk), lambda i, j, k: (i, k))
    b_spec = pl.BlockSpec((tn, tk), lambda i, j, k: (j, k)) if tb else pl.BlockSpec((tk, tn), lambda i, j, k: (k, source(j)))
    o_spec = pl.BlockSpec((tm, tn), lambda i, j, k: (i, j))
    ins, specs, aliases = [a, b], [a_spec, b_spec], {}
    out_shape = jax.ShapeDtypeStruct((M, N), out_dtype)
    if extra is not None:
        ins.append(extra)
        specs.append(o_spec)
    if relu2:
        o_spec, out_shape = [o_spec, o_spec], [out_shape, jax.ShapeDtypeStruct((M, N), MM)]
    if dest is not None:
        ins.append(buffer)
        specs.append(ANY)
        aliases = {2: 0}
        o_spec = pl.BlockSpec((1, tm, tn), lambda i, j, k: place(i, j))
        out_shape = jax.ShapeDtypeStruct(buffer.shape, buffer.dtype)
    return pl.pallas_call(
        body, name=name, grid=(M // tm, tiles, nk), in_specs=specs, out_specs=o_spec, out_shape=out_shape,
        scratch_shapes=[pltpu.VMEM((tm, tn), F32)], input_output_aliases=aliases,
        compiler_params=_params(("parallel", "parallel", "arbitrary")),
    )(*ins)


def _rows(n, want=512):
    t = min(want, n)
    assert n % t == 0, (n, t)
    return t


def rmsnorm_fwd(x, g, *, name):
    n, d = x.shape
    tr = _rows(n)

    def body(x_ref, g_ref, o_ref):
        xv = x_ref[...]
        r = lax.rsqrt(jnp.mean(xv * xv, axis=1, keepdims=True) + EPS)
        o_ref[...] = (xv * r * g_ref[...]).astype(o_ref.dtype)

    row = pl.BlockSpec((tr, d), lambda i: (i, 0))
    vec = pl.BlockSpec((1, d), lambda i: (0, 0))
    return pl.pallas_call(body, name=name, grid=(n // tr,), in_specs=[row, vec], out_specs=row,
                          out_shape=jax.ShapeDtypeStruct((n, d), MM), compiler_params=_params(("parallel",)))(x, g)


def rmsnorm_bwd(x, g, dh, dres, *, name):
    n, d = x.shape
    tr = _rows(n)

    def body(x_ref, g_ref, dh_ref, dr_ref, dx_ref, dg_ref):
        @pl.when(pl.program_id(0) == 0)
        def _():
            dg_ref[...] = jnp.zeros_like(dg_ref)

        xv = x_ref[...]
        r = lax.rsqrt(jnp.mean(xv * xv, axis=1, keepdims=True) + EPS)
        y = xv * r
        dhv = dh_ref[...]
        dy = dhv * g_ref[...]
        dx_ref[...] = dr_ref[...] + r * (dy - y * jnp.mean(dy * y, axis=1, keepdims=True))
        dg_ref[...] += jnp.sum(dhv * y, axis=0, keepdims=True)

    row = pl.BlockSpec((tr, d), lambda i: (i, 0))
    vec = pl.BlockSpec((1, d), lambda i: (0, 0))
    return pl.pallas_call(
        body, name=name, grid=(n // tr,), in_specs=[row, vec, row, row], out_specs=[row, vec],
        out_shape=[jax.ShapeDtypeStruct((n, d), F32), jax.ShapeDtypeStruct((1, d), F32)],
        compiler_params=_params(("arbitrary",)))(x, g, dh, dres)


def loss_grad(y, tgt, *, name):
    n, d = y.shape
    tr = _rows(n)

    def body(y_ref, t_ref, dy_ref, acc_ref):
        @pl.when(pl.program_id(0) == 0)
        def _():
            acc_ref[...] = jnp.zeros_like(acc_ref)

        e = y_ref[...] - t_ref[...]
        dy_ref[...] = e * (1.0 / d)
        acc_ref[...] += jnp.sum(e * e, axis=0, keepdims=True)

    row = pl.BlockSpec((tr, d), lambda i: (i, 0))
    vec = pl.BlockSpec((1, d), lambda i: (0, 0))
    return pl.pallas_call(
        body, name=name, grid=(n // tr,), in_specs=[row, row], out_specs=[row, vec],
        out_shape=[jax.ShapeDtypeStruct((n, d), F32), jax.ShapeDtypeStruct((1, d), F32)],
        compiler_params=_params(("arbitrary",)))(y, tgt)


MERGE_W = 256


def _gate_specs(tr, d):
    per = d // MERGE_W
    base = GATE * LANES // MERGE_W
    return [pl.BlockSpec((tr, MERGE_W), functools.partial(lambda i, j, b: (i, base + per * b + j), b=b)) for b in range(3)]


def merge_fwd(proj, ys, *, name):
    n, d = ys[0].shape
    tr = _rows(n)

    def body(g0, g1, g2, y0, y1, y2, o_ref):
        acc = jax.nn.sigmoid(g0[...]) * y0[...]
        acc += jax.nn.sigmoid(g1[...]) * y1[...]
        acc += jax.nn.sigmoid(g2[...]) * y2[...]
        o_ref[...] = acc.astype(o_ref.dtype)

    blk = pl.BlockSpec((tr, MERGE_W), lambda i, j: (i, j))
    return pl.pallas_call(
        body, name=name, grid=(n // tr, d // MERGE_W), in_specs=_gate_specs(tr, d) + [blk] * 3, out_specs=blk,
        out_shape=jax.ShapeDtypeStruct((n, d), MM), compiler_params=_params(("parallel", "parallel")))(proj, proj, proj, *ys)


def merge_bwd(proj, ys, dm, *, name):
    n, d = dm.shape
    tr = _rows(n)

    def body(g0, g1, g2, y0, y1, y2, dm_ref, dy0, dy1, dy2, dg0, dg1, dg2):
        dmv = dm_ref[...]
        for g, y, dy, dg in ((g0, y0, dy0, dg0), (g1, y1, dy1, dg1), (g2, y2, dy2, dg2)):
            s = jax.nn.sigmoid(g[...])
            dy[...] = (dmv * s).astype(dy.dtype)
            dg[...] = (dmv * y[...] * s * (1.0 - s)).astype(dg.dtype)

    blk = pl.BlockSpec((tr, MERGE_W), lambda i, j: (i, j))
    out = jax.ShapeDtypeStruct((n, d), MM)
    return pl.pallas_call(
        body, name=name, grid=(n // tr, d // MERGE_W), in_specs=_gate_specs(tr, d) + [blk] * 4, out_specs=[blk] * 6,
        out_shape=[out] * 6, compiler_params=_params(("parallel", "parallel")))(proj, proj, proj, *ys, dm)


def adamw(w, g, m, v, *, name):
    r, c = w.shape
    tr = r
    while tr * c * 4 > (1 << 21) and tr % 16 == 0:
        tr //= 2
    c1 = 1.0 / (1.0 - ADAM_B1 ** ADAM_STEP)
    c2 = 1.0 / (1.0 - ADAM_B2 ** ADAM_STEP)

    def body(w_ref, g_ref, m_ref, v_ref, d_ref, mo_ref, vo_ref):
        gv = g_ref[...]
        m2 = ADAM_B1 * m_ref[...] + (1.0 - ADAM_B1) * gv
        v2 = ADAM_B2 * v_ref[...] + (1.0 - ADAM_B2) * (gv * gv)
        d_ref[...] = -ADAM_LR * ((m2 * c1) / (jnp.sqrt(v2 * c2) + ADAM_EPS) + ADAM_WD * w_ref[...])
        mo_ref[...] = m2
        vo_ref[...] = v2

    blk = pl.BlockSpec((tr, c), lambda i: (i, 0))
    out = jax.ShapeDtypeStruct((r, c), F32)
    return pl.pallas_call(body, name=name, grid=(r // tr,), in_specs=[blk] * 4, out_specs=[blk] * 3, out_shape=[out] * 3,
                          compiler_params=_params(("parallel",)))(w, g, m, v)


def rope_table(pos, inv, *, name):
    n = pos.shape[0]
    tr = _rows(n)

    def body(p_ref, i_ref, c_ref, s_ref):
        ang = p_ref[...].astype(F32) * i_ref[...]
        c_ref[...] = jnp.cos(ang)
        s_ref[...] = jnp.sin(ang)

    out = jax.ShapeDtypeStruct((n, LANES), F32)
    blk = pl.BlockSpec((tr, LANES), lambda i: (i, 0))
    return pl.pallas_call(
        body, name=name, grid=(n // tr,), in_specs=[pl.BlockSpec((tr, 1), lambda i: (i, 0)), pl.BlockSpec((1, LANES), lambda i: (0, 0))],
        out_specs=[blk, blk], out_shape=[out, out], compiler_params=_params(("parallel",)))(pos, inv)


def _rot_half(x):
    first = (_iota((1, LANES), 1) & 63) < 32
    return jnp.where(first, -pltpu.roll(x, LANES - 32, axis=1), pltpu.roll(x, 32, axis=1))


def _head_norm(xv, gm):
    r = lax.rsqrt(_xdot2(xv * xv, gm) * (1.0 / HEAD) + EPS)
    return r, xv * r


def _head_norm_bwd(xh, r, dxh, gm):
    return r * (dxh - xh * (_xdot2(dxh * xh, gm) * (1.0 / HEAD)))


def fox_prep_fwd(proj, qg, kg, bf, *, bsz, seq, name):
    n = bsz * seq
    tr = min(256, seq)
    nt = seq // tr
    w = 4 * LANES

    def body(q_ref, k_ref, f_ref, qg_ref, kg_ref, b_ref, qn_ref, kn_ref, fb_ref, f8_ref, carry):
        @pl.when(pl.program_id(1) == 0)
        def _():
            carry[...] = jnp.zeros_like(carry)

        gm = _head_mat(LANES)
        for src, gain, dst in ((q_ref, qg_ref, qn_ref), (k_ref, kg_ref, kn_ref)):
            for c in range(4):
                sl = slice(c * LANES, (c + 1) * LANES)
                _, xh = _head_norm(src[:, sl], gm)
                dst[:, sl] = (xh * gain[:, sl]).astype(dst.dtype)
        logf = jax.nn.log_sigmoid(f_ref[...] + b_ref[...])
        lower = (_iota((tr, tr), 1) <= _iota((tr, tr), 0)).astype(BF16)
        fcum = _xdot3_left(lower, logf) + carry[...]
        carry[...] = fcum[tr - 1:tr, :]
        f8_ref[...] = fcum
        spread = (_iota((LANES, w), 0) == (_iota((LANES, w), 1) >> 6)).astype(BF16)
        fb_ref[...] = _xdot3(fcum, spread)

    row = lambda width, blk: pl.BlockSpec((tr, width), lambda b, t: (b * nt + t, blk))
    vec = lambda width: pl.BlockSpec((1, width), lambda b, t: (0, 0))
    return pl.pallas_call(
        body, name=name, grid=(bsz, nt),
        in_specs=[row(w, FOXQ // 4), row(w, FOXK // 4), row(LANES, FORGET), vec(w), vec(w), vec(LANES)],
        out_specs=[row(w, 0), row(w, 0), row(w, 0), row(LANES, 0)],
        out_shape=[jax.ShapeDtypeStruct((n, w), MM), jax.ShapeDtypeStruct((n, w), MM),
                   jax.ShapeDtypeStruct((n, w), F32), jax.ShapeDtypeStruct((n, LANES), F32)],
        scratch_shapes=[pltpu.VMEM((1, LANES), F32)],
        compiler_params=_params(("parallel", "arbitrary")))(proj, proj, proj, qg, kg, bf)


def fox_prep_bwd(proj, qg, kg, bf, dqn, dkn, df, *, bsz, seq, name):
    n = bsz * seq
    tr = min(256, seq)
    nt = seq // tr
    w = 4 * LANES

    def body(q_ref, k_ref, f_ref, qg_ref, kg_ref, b_ref, dqn_ref, dkn_ref, df_ref,
             dq_ref, dk_ref, dl_ref, dqg_ref, dkg_ref, db_ref, carry):
        first = (pl.program_id(0) == 0) & (pl.program_id(1) == 0)

        @pl.when(first)
        def _():
            dqg_ref[...] = jnp.zeros_like(dqg_ref)
            dkg_ref[...] = jnp.zeros_like(dkg_ref)
            db_ref[...] = jnp.zeros_like(db_ref)

        @pl.when(pl.program_id(1) == 0)
        def _():
            carry[...] = jnp.zeros_like(carry)

        gm = _head_mat(LANES)
        for src, gain, dy_ref, dx_ref, dg_ref in ((q_ref, qg_ref, dqn_ref, dq_ref, dqg_ref), (k_ref, kg_ref, dkn_ref, dk_ref, dkg_ref)):
            for c in range(4):
                sl = slice(c * LANES, (c + 1) * LANES)
                r, xh = _head_norm(src[:, sl], gm)
                dy = dy_ref[:, sl]
                dg_ref[:, sl] += jnp.sum(dy * xh, axis=0, keepdims=True)
                dx_ref[:, sl] = _head_norm_bwd(xh, r, dy * gain[:, sl], gm).astype(dx_ref.dtype)
        upper = (_iota((tr, tr), 1) >= _iota((tr, tr), 0)).astype(BF16)
        dlogf = _xdot3_left(upper, df_ref[...]) + carry[...]
        carry[...] = dlogf[0:1, :]
        dlogit = dlogf * jax.nn.sigmoid(-(f_ref[...] + b_ref[...]))
        dl_ref[:, 0:LANES] = dlogit.astype(dl_ref.dtype)
        dl_ref[:, LANES:2 * LANES] = jnp.zeros((tr, LANES), dl_ref.dtype)
        db_ref[...] += jnp.sum(dlogit, axis=0, keepdims=True)

    row = lambda width, blk: pl.BlockSpec((tr, width), lambda b, t: (b * nt + nt - 1 - t, blk))
    vec = lambda width: pl.BlockSpec((1, width), lambda b, t: (0, 0))
    return pl.pallas_call(
        body, name=name, grid=(bsz, nt),
        in_specs=[row(w, FOXQ // 4), row(w, FOXK // 4), row(LANES, FORGET), vec(w), vec(w), vec(LANES),
                  row(w, 0), row(w, 0), row(LANES, 0)],
        out_specs=[row(w, 0), row(w, 0), row(2 * LANES, 0), vec(w), vec(w), vec(LANES)],
        out_shape=[jax.ShapeDtypeStruct((n, w), MM), jax.ShapeDtypeStruct((n, w), MM), jax.ShapeDtypeStruct((n, 2 * LANES), MM),
                   jax.ShapeDtypeStruct((1, w), F32), jax.ShapeDtypeStruct((1, w), F32), jax.ShapeDtypeStruct((1, LANES), F32)],
        scratch_shapes=[pltpu.VMEM((1, LANES), F32)],
        compiler_params=_params(("arbitrary", "arbitrary")))(proj, proj, proj, qg, kg, bf, dqn, dkn, df)


DIL_W = 6 * LANES


def dil_prep_fwd(proj, qg, kg, cos, sin, *, name):
    n = proj.shape[0]
    tr = _rows(n, 256)

    def body(q_ref, k_ref, qg_ref, kg_ref, c_ref, s_ref, qo_ref, ko_ref):
        gm = _head_mat(LANES)
        cv, sv = c_ref[...], s_ref[...]
        for src, gain, dst in ((q_ref, qg_ref, qo_ref), (k_ref, kg_ref, ko_ref)):
            for c in range(6):
                sl = slice(c * LANES, (c + 1) * LANES)
                _, xh = _head_norm(src[:, sl], gm)
                xn = xh * gain[:, sl]
                dst[:, sl] = (xn * cv + _rot_half(xn) * sv).astype(dst.dtype)

    row = lambda width, blk: pl.BlockSpec((tr, width), lambda i: (i, blk))
    vec = pl.BlockSpec((1, DIL_W), lambda i: (0, 0))
    out = jax.ShapeDtypeStruct((n, DIL_W), MM)
    return pl.pallas_call(
        body, name=name, grid=(n // tr,),
        in_specs=[row(DIL_W, DILQ // 6), row(DIL_W, DILK // 6), vec, vec, row(LANES, 0), row(LANES, 0)],
        out_specs=[row(DIL_W, 0), row(DIL_W, 0)], out_shape=[out, out],
        compiler_params=_params(("parallel",)))(proj, proj, qg, kg, cos, sin)


def dil_prep_bwd(proj, qg, kg, cos, sin, dqr, dkr, *, name):
    n = proj.shape[0]
    tr = _rows(n, 256)

    def body(q_ref, k_ref, qg_ref, kg_ref, c_ref, s_ref, dqr_ref, dkr_ref, dq_ref, dk_ref, dqg_ref, dkg_ref):
        @pl.when(pl.program_id(0) == 0)
        def _():
            dqg_ref[...] = jnp.zeros_like(dqg_ref)
            dkg_ref[...] = jnp.zeros_like(dkg_ref)

        gm = _head_mat(LANES)
        cv, sv = c_ref[...], s_ref[...]
        for src, gain, dy_ref, dx_ref, dg_ref in ((q_ref, qg_ref, dqr_ref, dq_ref, dqg_ref), (k_ref, kg_ref, dkr_ref, dk_ref, dkg_ref)):
            for c in range(6):
                sl = slice(c * LANES, (c + 1) * LANES)
                r, xh = _head_norm(src[:, sl], gm)
                dy = dy_ref[:, sl]
                dxn = dy * cv - _rot_half(dy * sv)
                dg_ref[:, sl] += jnp.sum(dxn * xh, axis=0, keepdims=True)
                dx_ref[:, sl] = _head_norm_bwd(xh, r, dxn * gain[:, sl], gm).astype(dx_ref.dtype)

    row = lambda width, blk: pl.BlockSpec((tr, width), lambda i: (i, blk))
    vec = pl.BlockSpec((1, DIL_W), lambda i: (0, 0))
    out = jax.ShapeDtypeStruct((n, DIL_W), MM)
    gout = jax.ShapeDtypeStruct((1, DIL_W), F32)
    return pl.pallas_call(
        body, name=name, grid=(n // tr,),
        in_specs=[row(DIL_W, DILQ // 6), row(DIL_W, DILK // 6), vec, vec, row(LANES, 0), row(LANES, 0), row(DIL_W, 0), row(DIL_W, 0)],
        out_specs=[row(DIL_W, 0), row(DIL_W, 0), vec, vec], out_shape=[out, out, gout, gout],
        compiler_params=_params(("arbitrary",)))(proj, proj, qg, kg, cos, sin, dqr, dkr)


GROUP_W = 2 * LANES
STRIDE_ROWS = 256


def restride(items, *, bsz, seq, to_streams, name):
    n = bsz * seq
    rows = min(STRIDE_ROWS, seq)
    nb = seq // rows
    ni = len(items)

    def body(*refs):
        ins, outs, tile = refs[:ni], refs[ni:2 * ni], refs[2 * ni]
        ib = pl.program_id(1)
        for (_, _, r, _), src, dst in zip(items, ins, outs):
            chunk, per = rows // r, seq // r
            for half in range(GROUP_W // LANES):
                lanes = slice(half * LANES, (half + 1) * LANES)
                if to_streams:
                    tile[half] = src[:, lanes].astype(F32)
                    for c in range(r):
                        at = pl.multiple_of(c * per + chunk * ib, chunk)
                        dst[pl.ds(at, chunk), lanes] = tile[half, pl.ds(c, chunk, stride=r), :].astype(dst.dtype)
                else:
                    for c in range(r):
                        at = pl.multiple_of(c * per + chunk * ib, chunk)
                        tile[half, pl.ds(c, chunk, stride=r), :] = src[pl.ds(at, chunk), lanes].astype(F32)
                    dst[:, lanes] = tile[half].astype(dst.dtype)

    whole = pl.BlockSpec((seq, GROUP_W), lambda b, i: (b, 0))
    block = lambda col: pl.BlockSpec((rows, GROUP_W), lambda b, i: (b * nb + i, col))
    for _, _, r, _ in items:
        assert rows % (16 * r) == 0, (rows, r)
    return pl.pallas_call(
        body, name=name, grid=(bsz, nb),
        in_specs=[block(col) if to_streams else whole for _, col, _, _ in items],
        out_specs=[whole if to_streams else block(0) for _ in items],
        out_shape=[jax.ShapeDtypeStruct((n, GROUP_W), dt) for _, _, _, dt in items],
        scratch_shapes=[pltpu.VMEM((GROUP_W // LANES, rows, LANES), F32)],
        compiler_params=_params(("arbitrary", "arbitrary")))(*[a for a, _, _, _ in items])


def dil_combine_fwd(os_, lses, *, name):
    n, w = os_[0].shape
    tr = _rows(n)

    def body(o0, o1, o2, l0, l1, l2, out_ref):
        a, b, c = l0[...], l1[...], l2[...]
        m = jnp.maximum(jnp.maximum(a, b), c)
        ea, eb, ec = jnp.exp(a - m), jnp.exp(b - m), jnp.exp(c - m)
        out_ref[...] = ((ea * o0[...] + eb * o1[...] + ec * o2[...]) / (ea + eb + ec)).astype(out_ref.dtype)

    blk = pl.BlockSpec((tr, w), lambda i: (i, 0))
    return pl.pallas_call(body, name=name, grid=(n // tr,), in_specs=[blk] * 6, out_specs=blk,
                          out_shape=jax.ShapeDtypeStruct((n, w), MM), compiler_params=_params(("parallel",)))(*os_, *lses)


def dil_combine_bwd(os_, lses, dout, *, name):
    n, w = dout.shape
    tr = _rows(n)

    def body(o0, o1, o2, l0, l1, l2, d_ref, do0, do1, do2, dl0, dl1, dl2):
        a, b, c = l0[...], l1[...], l2[...]
        m = jnp.maximum(jnp.maximum(a, b), c)
        es = [jnp.exp(a - m), jnp.exp(b - m), jnp.exp(c - m)]
        inv = 1.0 / (es[0] + es[1] + es[2])
        ws = [e * inv for e in es]
        dv = d_ref[...]
        gm = _head_mat(w)
        dws = [_xdot2(dv * o[...], gm) for o in (o0, o1, o2)]
        mean = ws[0] * dws[0] + ws[1] * dws[1] + ws[2] * dws[2]
        for wg, dw, do, dl in zip(ws, dws, (do0, do1, do2), (dl0, dl1, dl2)):
            do[...] = wg * dv
            dl[...] = wg * (dw - mean)

    blk = pl.BlockSpec((tr, w), lambda i: (i, 0))
    out = jax.ShapeDtypeStruct((n, w), F32)
    return pl.pallas_call(body, name=name, grid=(n // tr,), in_specs=[blk] * 7, out_specs=[blk] * 6, out_shape=[out] * 6,
                          compiler_params=_params(("parallel",)))(*os_, *lses, dout)


def _key_plan(qi, tq, seq, window, run):
    if window + tq >= seq:
        for bi in range(seq // tq):
            lo = bi * tq
            segs = ([(0, lo, "bulk")] if lo else []) + [(lo, tq, "diag")]
            pl.when(qi == bi)(functools.partial(run, segs))
    else:
        ext = window + tq
        run([(pl.multiple_of(jnp.maximum((qi + 1) * tq - ext, 0), LANES), ext, "band")])


def _seg_mask(seg, qi, tq, window, dilation, strict=False):
    start, width, kind = seg
    d = _iota((tq, width), 0) - _iota((tq, width), 1)
    if kind == "bulk":
        d = d + width
    elif kind == "band":
        d = d + (qi * tq - start)
    ok = None
    if kind != "bulk":
        ok = (d > 0) if strict else (d >= 0)
    if window is not None:
        ok = (d <= window) if ok is None else ok & (d <= window)
    if dilation > 1:
        on_grid = (d & (dilation - 1)) == 0
        ok = on_grid if ok is None else ok & on_grid
    return ok


def _lane_first():
    return _iota((1, LANES), 1) < HEAD


def _attn_specs(bsz, seq, tq, qo, ko, vo):
    nq = seq // tq
    qspec = lambda off: pl.BlockSpec((tq, LANES), lambda b, j, i: (b * nq + i, off + j))
    kspec = lambda off: pl.BlockSpec((seq, LANES), lambda b, j, i: (b, off + j))
    return nq, qspec, kspec


def softmax_attn_fwd(q, k, v, bias, *, qo, ko, vo, pairs, bsz, seq, window, dilation, tq, name):
    n = bsz * seq
    nq, qspec, kspec = _attn_specs(bsz, seq, tq, qo, ko, vo)

    def body(*refs):
        if bias is None:
            q_ref, k_ref, v_ref, o_ref, l_ref = refs
        else:
            q_ref, k_ref, v_ref, fq_ref, fk_ref, o_ref, l_ref = refs
        qi = pl.program_id(2)

        def run(segs):
            qv = (q_ref[...] * SCALE).astype(MM)
            first = _lane_first()
            keys = [(k_ref[pl.ds(st, w), :].astype(MM), v_ref[pl.ds(st, w), :].astype(MM),
                     _seg_mask((st, w, kind), qi, tq, None if window >= seq else window, dilation), st, w)
                    for st, w, kind in segs]
            outs, lses = [], []
            for a in range(2):
                qa = jnp.where(first if a == 0 else ~first, qv, jnp.zeros_like(qv))
                scores = []
                for kv, _, ok, st, w in keys:
                    s = _dot_nt(qa, kv)
                    if bias is not None:
                        s = s + fq_ref[:, a * HEAD:a * HEAD + 1] - fk_ref[a:a + 1, pl.ds(st, w)]
                    scores.append(s if ok is None else jnp.where(ok, s, -jnp.inf))
                m = functools.reduce(jnp.maximum, [jnp.max(s, axis=1, keepdims=True) for s in scores])
                ps = [jnp.exp(s - m) for s in scores]
                den = sum(jnp.sum(p, axis=1, keepdims=True) for p in ps)
                acc = sum(_dot(p.astype(MM), vv) for p, (_, vv, _, _, _) in zip(ps, keys))
                outs.append(acc / den)
                lses.append(m + jnp.log(den))
            o_ref[...] = jnp.where(first, outs[0], outs[1]).astype(o_ref.dtype)
            l_ref[...] = jnp.where(first, lses[0], lses[1])

        _key_plan(qi, tq, seq, window, run)

    ins, specs = [q, k, v], [qspec(qo), kspec(ko), kspec(vo)]
    if bias is not None:
        ins += list(bias)
        specs += [qspec(0), pl.BlockSpec((8, seq), lambda b, j, i: (b * pairs + j, 0))]
    out = jax.ShapeDtypeStruct((n, LANES * pairs), F32)
    return pl.pallas_call(
        body, name=name, grid=(bsz, pairs, nq), in_specs=specs, out_specs=[qspec(0), qspec(0)], out_shape=[out, out],
        compiler_params=_params(("parallel", "parallel", "arbitrary")))(*ins)


def softmax_attn_bwd(q, k, v, o, do, lse, dlse, bias, *, qo, ko, vo, pairs, bsz, seq, window, dilation, tq, dq_dtype, dk_dtype, name):
    n = bsz * seq
    nq, qspec, kspec = _attn_specs(bsz, seq, tq, qo, ko, vo)
    has_bias, has_dlse = bias is not None, dlse is not None

    def body(*refs):
        refs = list(refs)
        q_ref, k_ref, v_ref, o_ref, do_ref, l_ref = refs[:6]
        del refs[:6]
        dl_ref = refs.pop(0) if has_dlse else None
        fq_ref, fk_ref = (refs.pop(0), refs.pop(0)) if has_bias else (None, None)
        dq_ref, dk_ref, dv_ref = refs[:3]
        del refs[:3]
        dfq_ref, dfk_ref = (refs.pop(0), refs.pop(0)) if has_bias else (None, None)
        dk_acc, dv_acc = refs
        qi = pl.program_id(2)

        @pl.when(qi == 0)
        def _():
            dk_acc[...] = jnp.zeros_like(dk_acc)
            dv_acc[...] = jnp.zeros_like(dv_acc)
            if has_bias:
                dfk_ref[...] = jnp.zeros_like(dfk_ref)

        def run(segs):
            qv = (q_ref[...] * SCALE).astype(MM)
            dov = do_ref[...]
            dob = dov.astype(MM)
            prod = dov * o_ref[...]
            first = _lane_first()
            keys = [(k_ref[pl.ds(st, w), :].astype(MM), v_ref[pl.ds(st, w), :].astype(MM),
                     _seg_mask((st, w, kind), qi, tq, None if window >= seq else window, dilation), st, w)
                    for st, w, kind in segs]
            dqs, dfqs = [], []
            dks, dvs = [[] for _ in keys], [[] for _ in keys]
            for a in range(2):
                mine = first if a == 0 else ~first
                col = slice(a * HEAD, a * HEAD + 1)
                delta = jnp.sum(jnp.where(mine, prod, 0.0), axis=1, keepdims=True)
                if has_dlse:
                    delta = delta - dl_ref[:, col]
                qa = jnp.where(mine, qv, jnp.zeros_like(qv))
                doa = jnp.where(mine, dob, jnp.zeros_like(dob))
                shift = l_ref[:, col]
                if has_bias:
                    shift = shift - fq_ref[:, col]
                dq, dfq = 0.0, 0.0
                for si, (kv, vv, ok, st, w) in enumerate(keys):
                    s = _dot_nt(qa, kv)
                    if has_bias:
                        s = s - fk_ref[a:a + 1, pl.ds(st, w)]
                    p = jnp.exp(s - shift)
                    if ok is not None:
                        p = jnp.where(ok, p, 0.0)
                    ds = p * (_dot_nt(doa, vv) - delta)
                    dsb = ds.astype(MM)
                    dvs[si].append(_dot_tn(p.astype(MM), dob))
                    dks[si].append(_dot_tn(dsb, qv))
                    dq = dq + _dot(dsb, kv)
                    if has_bias:
                        dfq = dfq + jnp.sum(ds, axis=1, keepdims=True)
                        dfk_ref[a:a + 1, pl.ds(st, w)] += jnp.sum(ds, axis=0, keepdims=True)
                dqs.append(dq * SCALE)
                dfqs.append(dfq)
            dq_ref[...] = jnp.where(first, dqs[0], dqs[1]).astype(dq_ref.dtype)
            for (_, _, _, st, w), dk, dv in zip(keys, dks, dvs):
                dk_acc[pl.ds(st, w), :] += jnp.where(first, dk[0], dk[1])
                dv_acc[pl.ds(st, w), :] += jnp.where(first, dv[0], dv[1])
            if has_bias:
                dfq_ref[...] = jnp.where(first, dfqs[0], dfqs[1])

        _key_plan(qi, tq, seq, window, run)

        @pl.when(qi == nq - 1)
        def _():
            dk_ref[...] = dk_acc[...].astype(dk_ref.dtype)
            dv_ref[...] = dv_acc[...].astype(dv_ref.dtype)

    wide = LANES * pairs
    ins = [q, k, v, o, do, lse]
    specs = [qspec(qo), kspec(ko), kspec(vo), qspec(0), qspec(0), qspec(0)]
    outs = [jax.ShapeDtypeStruct((n, wide), dq_dtype), jax.ShapeDtypeStruct((n, wide), dk_dtype), jax.ShapeDtypeStruct((n, wide), MM)]
    out_specs = [qspec(0), kspec(0), kspec(0)]
    if has_dlse:
        ins.append(dlse)
        specs.append(qspec(0))
    if has_bias:
        rows = pl.BlockSpec((8, seq), lambda b, j, i: (b * pairs + j, 0))
        ins += list(bias)
        specs += [qspec(0), rows]
        outs += [jax.ShapeDtypeStruct((n, wide), F32), jax.ShapeDtypeStruct((bsz * pairs * 8, seq), F32)]
        out_specs += [qspec(0), rows]
    return pl.pallas_call(
        body, name=name, grid=(bsz, pairs, nq), in_specs=specs, out_specs=out_specs, out_shape=outs,
        scratch_shapes=[pltpu.VMEM((seq, LANES), F32), pltpu.VMEM((seq, LANES), F32)],
        compiler_params=_params(("parallel", "parallel", "arbitrary")))(*ins)


def _running_sum(vals, mat, carry, lat_ref, start, reverse):
    nb = vals.shape[1] // LANES
    for cb in (reversed(range(nb)) if reverse else range(nb)):
        blk = vals[:, cb * LANES:(cb + 1) * LANES]
        lat_ref[:, start + cb * LANES:start + (cb + 1) * LANES] = _dot(blk.astype(BF16), mat) + carry
        carry = carry + jnp.sum(blk, axis=1, keepdims=True)
    return carry


def _sb_weights(qa, keys, tq, lat_ref):
    after = (_iota((LANES, LANES), 0) > _iota((LANES, LANES), 1)).astype(BF16)
    carry = jnp.zeros((tq, 1), F32)
    logs = []
    for kv, ok, st, w in reversed(keys):
        z = _dot_nt(qa, kv)
        _, sp = _softplus_parts(z)
        visible = sp if ok is None else jnp.where(ok, sp, 0.0)
        carry = _running_sum(visible, after, carry, lat_ref, st, True)
        logs.append(z - sp)
    out = []
    for (kv, ok, st, w), log_beta in zip(keys, reversed(logs)):
        att = jnp.exp(log_beta - lat_ref[:, st:st + w])
        out.append((log_beta, att if ok is None else jnp.where(ok, att, 0.0)))
    return out


def _sb_keys(k_ref, v_ref, segs, qi, tq):
    return [(k_ref[st:st + w, :].astype(MM), v_ref[st:st + w, :].astype(MM),
             _seg_mask((st, w, kind), qi, tq, None, 1, strict=True), st, w) for st, w, kind in segs]


def sb_attn_fwd(proj, *, bsz, seq, tq, name):
    n = bsz * seq
    pairs = 4
    nq, qspec, kspec = _attn_specs(bsz, seq, tq, SBQ, SBK, SBV)

    def body(q_ref, k_ref, v_ref, o_ref, lat_ref):
        qi = pl.program_id(2)

        def run(segs):
            qv = (q_ref[...] * SCALE).astype(MM)
            keys = _sb_keys(k_ref, v_ref, segs, qi, tq)
            first = _lane_first()
            outs = []
            for a in range(2):
                qa = jnp.where(first if a == 0 else ~first, qv, jnp.zeros_like(qv))
                weights = _sb_weights(qa, [(kv, ok, st, w) for kv, _, ok, st, w in keys], tq, lat_ref)
                outs.append(sum(_dot(att.astype(MM), vv) for (_, att), (_, vv, _, _, _) in zip(weights, keys)))
            o_ref[...] = jnp.where(first, outs[0], outs[1]).astype(o_ref.dtype)

        _key_plan(qi, tq, seq, seq, run)

    return pl.pallas_call(
        body, name=name, grid=(bsz, pairs, nq), in_specs=[qspec(SBQ), kspec(SBK), kspec(SBV)], out_specs=qspec(0),
        out_shape=jax.ShapeDtypeStruct((n, LANES * pairs), MM), scratch_shapes=[pltpu.VMEM((tq, seq), F32)],
        compiler_params=_params(("parallel", "parallel", "arbitrary")))(proj, proj, proj)


def sb_attn_bwd(proj, do, *, bsz, seq, tq, name):
    n = bsz * seq
    pairs = 4
    nq, qspec, kspec = _attn_specs(bsz, seq, tq, SBQ, SBK, SBV)

    def body(q_ref, k_ref, v_ref, do_ref, dq_ref, dk_ref, dv_ref, lat_ref, dk_acc, dv_acc):
        qi = pl.program_id(2)

        @pl.when(qi == 0)
        def _():
            dk_acc[...] = jnp.zeros_like(dk_acc)
            dv_acc[...] = jnp.zeros_like(dv_acc)

        def run(segs):
            qv = (q_ref[...] * SCALE).astype(MM)
            keys = _sb_keys(k_ref, v_ref, segs, qi, tq)
            dob = do_ref[...].astype(MM)
            first = _lane_first()
            before = (_iota((LANES, LANES), 0) < _iota((LANES, LANES), 1)).astype(BF16)
            dqs = []
            dks, dvs = [[] for _ in keys], [[] for _ in keys]
            for a in range(2):
                mine = first if a == 0 else ~first
                qa = jnp.where(mine, qv, jnp.zeros_like(qv))
                doa = jnp.where(mine, dob, jnp.zeros_like(dob))
                weights = _sb_weights(qa, [(kv, ok, st, w) for kv, _, ok, st, w in keys], tq, lat_ref)
                gs = [_dot_nt(doa, vv) * att for (_, att), (_, vv, _, _, _) in zip(weights, keys)]
                carry = jnp.zeros((tq, 1), F32)
                for g, (_, _, _, st, w) in zip(gs, keys):
                    carry = _running_sum(g, before, carry, lat_ref, st, False)
                dq = 0.0
                for si, ((log_beta, att), g, (kv, _, ok, st, w)) in enumerate(zip(weights, gs, keys)):
                    dz = g - jnp.exp(log_beta) * (g + lat_ref[:, st:st + w])
                    dz = (dz if ok is None else jnp.where(ok, dz, 0.0)).astype(MM)
                    dvs[si].append(_dot_tn(att.astype(MM), dob))
                    dks[si].append(_dot_tn(dz, qv))
                    dq = dq + _dot(dz, kv)
                dqs.append(dq * SCALE)
            dq_ref[...] = jnp.where(first, dqs[0], dqs[1]).astype(dq_ref.dtype)
            for (_, _, _, st, w), dk, dv in zip(keys, dks, dvs):
                dk_acc[st:st + w, :] += jnp.where(first, dk[0], dk[1])
                dv_acc[st:st + w, :] += jnp.where(first, dv[0], dv[1])

        _key_plan(qi, tq, seq, seq, run)

        @pl.when(qi == nq - 1)
        def _():
            dk_ref[...] = dk_acc[...].astype(dk_ref.dtype)
            dv_ref[...] = dv_acc[...].astype(dv_ref.dtype)

    out = jax.ShapeDtypeStruct((n, LANES * pairs), MM)
    return pl.pallas_call(
        body, name=name, grid=(bsz, pairs, nq), in_specs=[qspec(SBQ), kspec(SBK), kspec(SBV), qspec(0)],
        out_specs=[qspec(0), kspec(0), kspec(0)], out_shape=[out, out, out],
        scratch_shapes=[pltpu.VMEM((tq, seq), F32), pltpu.VMEM((seq, LANES), F32), pltpu.VMEM((seq, LANES), F32)],
        compiler_params=_params(("parallel", "parallel", "arbitrary")))(proj, proj, proj, do)


def _place():
    return lax.axis_index("x"), lax.axis_index("y"), lax.axis_index("c")


def _other_chips(x, y):
    return [(1 - x, y), (x, 1 - y), (1 - x, 1 - y)]


def _remote(src, dst, send_sems, recv_sems, k, to):
    return pltpu.make_async_remote_copy(src_ref=src, dst_ref=dst, send_sem=send_sems.at[k], recv_sem=recv_sems.at[k],
                                        device_id=to, device_id_type=MESH_ID)


def gather_chips(arrs, *, name):
    na = len(arrs)

    def body(*refs):
        ins, outs = refs[:na], refs[na:2 * na]
        send_sems, recv_sems = refs[2 * na:]
        x, y, c = _place()
        me, sibling = 2 * x + y, (x, y, 1 - c)
        chips = _other_chips(x, y)
        sends = []
        for t in range(na):
            rh = ins[t].shape[0] // 2
            half = lambda chip, h, t=t, rh=rh: outs[t].at[chip, pl.ds(h * rh, rh), :]
            for j, (px, py) in enumerate(chips):
                cp = _remote(ins[t].at[pl.ds(c * rh, rh), :], half(me, c), send_sems, recv_sems, 6 * t + j, (px, py, c))
                cp.start()
                sends.append(cp)
        for t in range(na):
            rh = ins[t].shape[0] // 2
            half = lambda chip, h, t=t, rh=rh: outs[t].at[chip, pl.ds(h * rh, rh), :]
            for j, (px, py) in enumerate(chips):
                landed = half(2 * px + py, c)
                _remote(landed, landed, send_sems, recv_sems, 6 * t + j, (px, py, c)).wait_recv()
                fw = _remote(landed, landed, send_sems, recv_sems, 6 * t + 3 + j, sibling)
                fw.start()
                sends.append(fw)
        for t in range(na):
            rh = ins[t].shape[0] // 2
            half = lambda chip, h, t=t, rh=rh: outs[t].at[chip, pl.ds(h * rh, rh), :]
            for j, (px, py) in enumerate(chips):
                passed = half(2 * px + py, 1 - c)
                _remote(passed, passed, send_sems, recv_sems, 6 * t + 3 + j, sibling).wait_recv()
        for cp in sends:
            cp.wait_send()

    for a in arrs:
        assert a.ndim == 2 and a.shape[0] % 32 == 0, a.shape
    return pl.pallas_call(
        body, name=name, in_specs=[ANY] * na, out_specs=[ANY] * na,
        out_shape=[jax.ShapeDtypeStruct((4,) + a.shape, a.dtype) for a in arrs],
        scratch_shapes=[pltpu.SemaphoreType.DMA((6 * na,)), pltpu.SemaphoreType.DMA((6 * na,))],
    )(*arrs)


def _chunk_rows(rows, cols, limit):
    best = 16
    for t in range(16, rows + 1, 16):
        if rows % t == 0 and t * cols * 4 <= limit:
            best = t
    assert rows % best == 0, (rows, cols)
    return best


def pair_sum(a, core, *, name):
    _, rows, cols = a.shape
    rh = rows // 2
    tr = _chunk_rows(rh, cols, 2 << 20)
    nch = rh // tr
    steps = 4 * nch

    def body(core_ref, keep_ref, send_ref, o32_ref, o16_ref, landing, send_sems, recv_sems, credit):
        step = pl.program_id(0) * nch + pl.program_id(1)
        slot = lax.rem(step, 2)
        x, y, c = _place()
        sibling = (x, y, 1 - c)

        @pl.when(step >= 2)
        def _():
            pl.semaphore_wait(credit, 1)

        cp = _remote(send_ref.at[0], landing.at[slot], send_sems, recv_sems, slot, sibling)
        cp.start()
        cp.wait_recv()
        total = keep_ref[0] + landing[slot]
        o32_ref[0] = total
        o16_ref[0] = total.astype(BF16)
        cp.wait_send()

        @pl.when(step + 2 < steps)
        def _():
            pl.semaphore_signal(credit, 1, device_id=sibling, device_id_type=MESH_ID)

    blk = (1, tr, cols)
    grid_spec = pltpu.PrefetchScalarGridSpec(
        num_scalar_prefetch=1, grid=(4, nch),
        in_specs=[pl.BlockSpec(blk, lambda k, i, core: (k, core[0] * nch + i, 0)),
                  pl.BlockSpec(blk, lambda k, i, core: (k, (1 - core[0]) * nch + i, 0))],
        out_specs=[pl.BlockSpec(blk, lambda k, i, core: (k, i, 0))] * 2,
        scratch_shapes=[pltpu.VMEM((2, tr, cols), F32), pltpu.SemaphoreType.DMA((2,)), pltpu.SemaphoreType.DMA((2,)),
                        pltpu.SemaphoreType.REGULAR])
    return pl.pallas_call(
        body, name=name, grid_spec=grid_spec,
        out_shape=[jax.ShapeDtypeStruct((4, rh, cols), F32), jax.ShapeDtypeStruct((4, rh, cols), BF16)],
        compiler_params=_params(("arbitrary", "arbitrary")))(core, a, a)


def scatter_chips(arrs, *, name):
    na = len(arrs)

    def body(*refs):
        ins, outs = refs[:na], refs[na:2 * na]
        send_sems, recv_sems = refs[2 * na:]
        x, y, c = _place()
        me = 2 * x + y
        chips = _other_chips(x, y)
        sends = []
        for t in range(na):
            for j, (px, py) in enumerate(chips):
                cp = _remote(ins[t].at[2 * px + py], outs[t].at[me], send_sems, recv_sems, 3 * t + j, (px, py, c))
                cp.start()
                sends.append(cp)
        for t in range(na):
            for j, (px, py) in enumerate(chips):
                slab = outs[t].at[2 * px + py]
                _remote(slab, slab, send_sems, recv_sems, 3 * t + j, (px, py, c)).wait_recv()
        for cp in sends:
            cp.wait_send()

    return pl.pallas_call(
        body, name=name, in_specs=[ANY] * na, out_specs=[ANY] * na,
        out_shape=[jax.ShapeDtypeStruct(a.shape, a.dtype) for a in arrs],
        scratch_shapes=[pltpu.SemaphoreType.DMA((3 * na,)), pltpu.SemaphoreType.DMA((3 * na,))],
    )(*arrs)


def chip_sum_join(own, landed, chip, *, name):
    _, rh, cols = own.shape
    tr = _chunk_rows(rh, cols, 2 << 20)
    nch = rh // tr

    def body(chip_ref, own_ref, l1_ref, l2_ref, l3_ref, out_ref, res, local_sem, send_sem, recv_sem):
        i = pl.program_id(0)
        x, y, c = _place()
        sibling = (x, y, 1 - c)
        res[...] = ((own_ref[0] + l1_ref[0].astype(F32)) + l2_ref[0].astype(F32)) + l3_ref[0].astype(F32)
        rows = pl.ds(pl.multiple_of(i * tr, tr), tr)
        here = pltpu.make_async_copy(res, out_ref.at[c, rows, :], local_sem)
        here.start()
        there = pltpu.make_async_remote_copy(src_ref=res, dst_ref=out_ref.at[c, rows, :], send_sem=send_sem, recv_sem=recv_sem,
                                             device_id=sibling, device_id_type=MESH_ID)
        there.start()
        here.wait()
        there.wait_send()

        @pl.when(i == nch - 1)
        def _():
            half = out_ref.at[1 - c]
            pltpu.make_async_remote_copy(src_ref=half, dst_ref=half, send_sem=send_sem, recv_sem=recv_sem,
                                         device_id=sibling, device_id_type=MESH_ID).wait_recv()

    blk = (1, tr, cols)
    slab = lambda p: pl.BlockSpec(blk, lambda i, chip: (chip[0] ^ p, i, 0))
    grid_spec = pltpu.PrefetchScalarGridSpec(
        num_scalar_prefetch=1, grid=(nch,), in_specs=[slab(0), slab(1), slab(2), slab(3)], out_specs=ANY,
        scratch_shapes=[pltpu.VMEM((tr, cols), F32), pltpu.SemaphoreType.DMA, pltpu.SemaphoreType.DMA, pltpu.SemaphoreType.DMA])
    return pl.pallas_call(
        body, name=name, grid_spec=grid_spec, out_shape=jax.ShapeDtypeStruct((2, rh, cols), F32),
        compiler_params=_params(("arbitrary",)))(chip, own, landed, landed, landed)


def all_reduce_small(a, *, name):
    def body(a_ref, o_ref, buf, send_sems, recv_sems):
        x, y, c = _place()
        me = 4 * x + 2 * y + c
        buf[me] = a_ref[...]
        sent = []
        for p in range(1, 8):
            px, py, pc = (p >> 2) & 1, (p >> 1) & 1, p & 1
            cp = _remote(a_ref, buf.at[me], send_sems, recv_sems, p - 1, (x ^ px, y ^ py, c ^ pc))
            cp.start()
            sent.append(cp)
        for p in range(1, 8):
            px, py, pc = (p >> 2) & 1, (p >> 1) & 1, p & 1
            src = 4 * (x ^ px) + 2 * (y ^ py) + (c ^ pc)
            _remote(a_ref, buf.at[src], send_sems, recv_sems, p - 1, (x ^ px, y ^ py, c ^ pc)).wait_recv()
        for cp in sent:
            cp.wait_send()
        acc = buf[0]
        for d in range(1, 8):
            acc = acc + buf[d]
        o_ref[...] = acc

    vm = pl.BlockSpec(memory_space=pltpu.VMEM)
    return pl.pallas_call(
        body, name=name, in_specs=[vm], out_specs=vm, out_shape=jax.ShapeDtypeStruct(a.shape, a.dtype),
        scratch_shapes=[pltpu.VMEM((8,) + a.shape, a.dtype), pltpu.SemaphoreType.DMA((7,)), pltpu.SemaphoreType.DMA((7,))],
    )(a)


TQ = 256


def _pad_w_in(w):
    pad = jnp.zeros((w.shape[0], DPROJ - DIN), w.dtype)
    return jnp.concatenate([w[:, :O1], w[:, O2:], w[:, O1:O2], pad], axis=1)


def _layer_small(sm, l):
    row = lambda v: v.reshape(1, -1)
    return dict(
        attn_norm=row(sm["attn_norm"][l]), mlp_norm=row(sm["mlp_norm"][l]),
        qgf=row(jnp.tile(sm["q_norm_fox"][l], 8)), kgf=row(jnp.tile(sm["k_norm_fox"][l], 8)),
        qgd=row(jnp.tile(sm["q_norm_dil"][l], 12)), kgd=row(jnp.tile(sm["k_norm_dil"][l], 12)),
        bfor=row(jnp.pad(sm["b_forget"][l], (0, LANES - 8))))


def _key_rows(f8, bsz, seq):
    f = f8.reshape(bsz, seq, LANES)[:, :, :8].transpose(0, 2, 1).reshape(bsz, 4, 2, seq)
    return jnp.pad(f, ((0, 0), (0, 0), (0, 6), (0, 0))).reshape(bsz * 32, seq)


def _layer_fwd(x, w, s, cos, sin, bsz, seq, l):
    nm = lambda t: f"l{l}_{t}"
    h = rmsnorm_fwd(x, s["attn_norm"], name=nm("attn_norm"))
    proj = matmul(h, w["win"], name=nm("proj"))
    qn, kn, fb, f8 = fox_prep_fwd(proj, s["qgf"], s["kgf"], s["bfor"], bsz=bsz, seq=seq, name=nm("fox_prep"))
    fk = _key_rows(f8, bsz, seq)
    oa, la = softmax_attn_fwd(qn, kn, proj, (fb, fk), qo=0, ko=0, vo=FOXV, pairs=4, bsz=bsz, seq=seq, window=seq, dilation=1,
                              tq=TQ, name=nm("fox_attn"))
    ob = sb_attn_fwd(proj, bsz=bsz, seq=seq, tq=TQ, name=nm("sb_attn"))
    qr, kr = dil_prep_fwd(proj, s["qgd"], s["kgd"], cos, sin, name=nm("dil_prep"))
    strided = [(g, r) for g, (_, r) in enumerate(DIL_PATTERNS) if r > 1]
    vcol = DILV * LANES // GROUP_W
    moved = restride([(qr, g, r, MM) for g, r in strided] + [(kr, g, r, MM) for g, r in strided]
                     + [(proj, vcol + g, r, MM) for g, r in strided], bsz=bsz, seq=seq, to_streams=True, name=nm("dil_to_streams"))
    ns = len(strided)
    streams = {g: (moved[i], moved[ns + i], moved[2 * ns + i]) for i, (g, _) in enumerate(strided)}
    ogs, lgs, stream_out = [None] * len(DIL_PATTERNS), [None] * len(DIL_PATTERNS), {}
    for g, (window, dilation) in enumerate(DIL_PATTERNS):
        if dilation == 1:
            ogs[g], lgs[g] = softmax_attn_fwd(qr, kr, proj, None, qo=2 * g, ko=2 * g, vo=DILV + 2 * g, pairs=2, bsz=bsz, seq=seq,
                                              window=window, dilation=1, tq=TQ, name=nm(f"dil_attn{g}"))
        else:
            per = seq // dilation
            stream_out[g] = softmax_attn_fwd(*streams[g], None, qo=0, ko=0, vo=0, pairs=2, bsz=bsz * dilation, seq=per,
                                             window=window // dilation, dilation=1, tq=min(TQ, per), name=nm(f"dil_attn{g}"))
    back = restride([(a, 0, r, F32) for g, r in strided for a in stream_out[g]], bsz=bsz, seq=seq, to_streams=False,
                    name=nm("dil_from_streams"))
    for i, (g, _) in enumerate(strided):
        ogs[g], lgs[g] = back[2 * i], back[2 * i + 1]
    oc = dil_combine_fwd(ogs, lgs, name=nm("dil_combine"))
    ys = [matmul(oa, w["wuf"], name=nm("up_fox")), matmul(ob, w["wus"], name=nm("up_sb")), matmul(oc, w["wud"], name=nm("up_dil"))]
    merged = merge_fwd(proj, ys, name=nm("merge"))
    x1 = matmul(merged, w["wo"], add=x, name=nm("out_proj"))
    h2 = rmsnorm_fwd(x1, s["mlp_norm"], name=nm("mlp_norm"))
    u, act = matmul(h2, w["wmi"], relu2=True, name=nm("mlp_in"))
    x2 = matmul(act, w["wmo"], add=x1, name=nm("mlp_out"))
    saved = dict(x=x, h=h, proj=proj, qn=qn, kn=kn, fb=fb, fk=fk, oa=oa, la=la, ob=ob, qr=qr, kr=kr, ogs=ogs, lgs=lgs, oc=oc,
                 ys=ys, merged=merged, x1=x1, h2=h2, u=u, act=act, streams=streams, stream_out=stream_out)
    return x2, saved


WIN_TILE = 256
WIN_STRIDE, WIN_TILES = 8, 9


def grad_buffers(depth, d, dff, wf, wd):
    assert dff // 4 == d
    return dict(win=lax.empty((4, depth * d, WIN_TILES * WIN_TILE), F32), ups=lax.empty((4, depth * (2 * wf + wd), d // 4), F32),
                wide=lax.empty((4, depth * (d + dff // 4 + d // 4), d), F32))


def _layer_bwd(dx2, w, s, sv, cos, sin, bsz, seq, l, depth, bufs):
    nm = lambda t: f"l{l}_{t}_bwd"
    n = bsz * seq
    proj = sv["proj"]
    d, dff = w["wmi"].shape
    wf, wd = w["wuf"].shape[0], w["wud"].shape[0]
    bufs = dict(bufs)
    rb = 512
    per_chip = dff // 4 // rb
    du = matmul(dx2, w["wmo"], tb=True, relu2_of=sv["u"], out_dtype=MM, name=nm("mlp_out_dx"))
    bufs["wide"] = matmul(sv["act"], dx2, ta=True, tm=rb, tn=d, tk=2048, name=nm("mlp_out_dw"),
                          dest=(bufs["wide"], 1, lambda j: j,
                                lambda i, j: (i // per_chip, (depth * d + l * (dff // 4)) // rb + i % per_chip, j)))
    dh2 = matmul(du, w["wmi"], tb=True, name=nm("mlp_in_dx"))
    bufs["wide"] = matmul(sv["h2"], du, ta=True, tm=rb, tn=dff // 4, tk=2048, name=nm("mlp_in_dw"),
                          dest=(bufs["wide"], 4, lambda j: j, lambda i, j: (j, l * d // rb + i, 0)))
    dx1, g_mlp_norm = rmsnorm_bwd(sv["x1"], s["mlp_norm"], dh2, dx2, name=nm("mlp_norm"))

    dmerged = matmul(dx1, w["wo"], tb=True, name=nm("out_proj_dx"))
    bufs["wide"] = matmul(sv["merged"], dx1, ta=True, tm=d // 4, tn=d, tk=2048, name=nm("out_proj_dw"),
                          dest=(bufs["wide"], 1, lambda j: j, lambda i, j: (i, (depth * (d + dff // 4)) // (d // 4) + l, j)))
    dya, dyb, dyc, dga, dgb, dgc = merge_bwd(proj, sv["ys"], dmerged, name=nm("merge"))
    doa = matmul(dya, w["wuf"], tb=True, name=nm("up_fox_dx"))
    bufs["ups"] = matmul(sv["oa"], dya, ta=True, tm=wf, tn=d // 4, tk=2048, name=nm("up_fox_dw"),
                         dest=(bufs["ups"], 4, lambda j: j, lambda i, j: (j, l, 0)))
    dob = matmul(dyb, w["wus"], tb=True, name=nm("up_sb_dx"))
    bufs["ups"] = matmul(sv["ob"], dyb, ta=True, tm=wf, tn=d // 4, tk=2048, name=nm("up_sb_dw"),
                         dest=(bufs["ups"], 4, lambda j: j, lambda i, j: (j, depth + l, 0)))
    doc = matmul(dyc, w["wud"], tb=True, name=nm("up_dil_dx"))
    bufs["ups"] = matmul(sv["oc"], dyc, ta=True, tm=wd, tn=d // 4, tk=2048, name=nm("up_dil_dw"),
                         dest=(bufs["ups"], 4, lambda j: j, lambda i, j: (j, 2 * depth * wf // wd + l, 0)))

    outs = dil_combine_bwd(sv["ogs"], sv["lgs"], doc, name=nm("dil_combine"))
    strided = [(g, r) for g, (_, r) in enumerate(DIL_PATTERNS) if r > 1]
    moved = restride([(outs[j], 0, r, F32) for g, r in strided for j in (g, 3 + g)], bsz=bsz, seq=seq, to_streams=True,
                     name=nm("dil_to_streams"))
    ng = len(DIL_PATTERNS)
    dqs, dks, dvs, stream_grads = [None] * ng, [None] * ng, [None] * ng, []
    for g, (window, dilation) in enumerate(DIL_PATTERNS):
        if dilation == 1:
            dqs[g], dks[g], dvs[g] = softmax_attn_bwd(
                sv["qr"], sv["kr"], proj, sv["ogs"][g], outs[g], sv["lgs"][g], outs[3 + g], None, qo=2 * g, ko=2 * g,
                vo=DILV + 2 * g, pairs=2, bsz=bsz, seq=seq, window=window, dilation=1, tq=TQ, dq_dtype=F32, dk_dtype=F32,
                name=nm(f"dil_attn{g}"))
        else:
            i, per = [sg for sg, _ in strided].index(g), seq // dilation
            so, sl = sv["stream_out"][g]
            stream_grads += softmax_attn_bwd(
                *sv["streams"][g], so, moved[2 * i], sl, moved[2 * i + 1], None, qo=0, ko=0, vo=0, pairs=2, bsz=bsz * dilation,
                seq=per, window=window // dilation, dilation=1, tq=min(TQ, per), dq_dtype=F32, dk_dtype=F32,
                name=nm(f"dil_attn{g}"))
    back = restride([(a, 0, r, dt) for i, (g, r) in enumerate(strided) for a, dt in zip(stream_grads[3 * i:3 * i + 3], (F32, F32, MM))],
                    bsz=bsz, seq=seq, to_streams=False, name=nm("dil_from_streams"))
    for i, (g, _) in enumerate(strided):
        dqs[g], dks[g], dvs[g] = back[3 * i:3 * i + 3]
    d_dq, d_dk, g_qgd, g_kgd = dil_prep_bwd(proj, s["qgd"], s["kgd"], cos, sin, jnp.concatenate(dqs, axis=1),
                                            jnp.concatenate(dks, axis=1), name=nm("dil_prep"))

    s_dq, s_dk, s_dv = sb_attn_bwd(proj, dob, bsz=bsz, seq=seq, tq=TQ, name=nm("sb_attn"))

    dqn, dkn, f_dv, dfq, dfk = softmax_attn_bwd(sv["qn"], sv["kn"], proj, sv["oa"], doa, sv["la"], None, (sv["fb"], sv["fk"]),
                                                qo=0, ko=0, vo=FOXV, pairs=4, bsz=bsz, seq=seq, window=seq, dilation=1, tq=TQ,
                                                dq_dtype=F32, dk_dtype=F32, name=nm("fox_attn"))
    dfk8 = dfk.reshape(bsz, 4, 8, seq)[:, :, :2].reshape(bsz, 8, seq).transpose(0, 2, 1).reshape(n, 8)
    df = jnp.pad(dfq[:, ::HEAD] - dfk8, ((0, 0), (0, LANES - 8)))
    f_dq, f_dk, d_forget, g_qgf, g_kgf, g_bfor = fox_prep_bwd(proj, s["qgf"], s["kgf"], s["bfor"], dqn, dkn, df, bsz=bsz, seq=seq,
                                                              name=nm("fox_prep"))

    dproj = jnp.concatenate([f_dq, f_dk, f_dv, s_dq, s_dk, s_dv, d_dq, d_dk] + dvs + [dga, dgb, dgc, d_forget], axis=1)
    dh = matmul(dproj, w["win"], tb=True, tn=1024, tk=512, name=nm("proj_dx"))
    bufs["win"] = matmul(sv["h"], dproj, ta=True, tm=d, tn=WIN_TILE, tk=2048, name=nm("proj_dw"),
                         dest=(bufs["win"], 4 * WIN_TILES, lambda j: WIN_STRIDE * (j // WIN_TILES) + j % WIN_TILES,
                               lambda i, j: (j // WIN_TILES, l, j % WIN_TILES)))
    g_forget = matmul(sv["h"], d_forget, ta=True, tk=2048, name=nm("forget_dw"))[:, :O2 - O1]
    dx, g_attn_norm = rmsnorm_bwd(sv["x"], s["attn_norm"], dh, dx1, name=nm("attn_norm"))
    gs = dict(attn_norm=g_attn_norm[0], mlp_norm=g_mlp_norm[0], b_forget=g_bfor[0, :8],
              q_norm_fox=g_qgf.reshape(8, HEAD).sum(0), k_norm_fox=g_kgf.reshape(8, HEAD).sum(0),
              q_norm_dil=g_qgd.reshape(12, HEAD).sum(0), k_norm_dil=g_kgd.reshape(12, HEAD).sum(0), w_in_forget=g_forget)
    return dx, bufs, gs


def local_step(x, positions, target, weights, small):
    bsz, seq, d = x.shape
    n = bsz * seq
    depth = len(weights)
    inv = 1.0 / (ROPE_THETA ** (jnp.arange(HEAD // 2, dtype=F32) / (HEAD // 2)))
    cos, sin = rope_table(positions.reshape(n, 1), jnp.tile(inv, 4).reshape(1, LANES), name="rope_table")
    xs = x.reshape(n, d)
    saved = []
    for l in range(depth):
        xs, sv = _layer_fwd(xs, weights[l], _layer_small(small, l), cos, sin, bsz, seq, l)
        saved.append(sv)
    dy, sq = loss_grad(xs, target.reshape(n, d), name="loss")
    loss = (0.5 / d) * jnp.sum(sq)
    w0 = weights[0]
    bufs = grad_buffers(depth, d, w0["wmi"].shape[1], w0["wuf"].shape[0], w0["wud"].shape[0])
    gss = [None] * depth
    for l in reversed(range(depth)):
        dy, bufs, gss[l] = _layer_bwd(dy, weights[l], _layer_small(small, l), saved[l], cos, sin, bsz, seq, l, depth, bufs)
    return loss, dy.reshape(bsz, seq, d), bufs, gss


SMALL = ("attn_norm", "mlp_norm", "b_forget", "q_norm_fox", "k_norm_fox", "q_norm_dil", "k_norm_dil")
SMALL_ROWS = 8


def _pack_small(vals):
    flat = jnp.concatenate([vals[k].reshape(-1) for k in SMALL])
    return jnp.pad(flat, (0, SMALL_ROWS * 1024 - flat.shape[0])).reshape(SMALL_ROWS, 1024)


def _unpack_small(packed, like):
    flat, out, at = packed.reshape(-1), {}, 0
    for k in SMALL:
        size = like[k].size
        out[k] = flat[at:at + size].reshape(like[k].shape)
        at += size
    return out


def _chips_to_cols(a, depth):
    _, rows, c = a.shape
    return a.reshape(4, depth, rows // depth, c).transpose(1, 2, 0, 3).reshape(depth, rows // depth, 4 * c)


def _chips_to_rows(a, depth):
    _, rows, c = a.shape
    return a.reshape(4, depth, rows // depth, c).transpose(1, 0, 2, 3).reshape(depth, 4 * rows // depth, c)


def kernel(x, positions, attn_norm, w_in, b_forget, q_norm_fox, k_norm_fox, q_norm_dil, k_norm_dil, w_up_fox, w_up_sb, w_up_dil, w_out, mlp_norm, w_mlp_in, w_mlp_out, loss_target, m_attn_norm, m_w_in, m_b_forget, m_q_norm_fox, m_k_norm_fox, m_q_norm_dil, m_k_norm_dil, m_w_up_fox, m_w_up_sb, m_w_up_dil, m_w_out, m_mlp_norm, m_w_mlp_in, m_w_mlp_out, v_attn_norm, v_w_in, v_b_forget, v_q_norm_fox, v_k_norm_fox, v_q_norm_dil, v_k_norm_dil, v_w_up_fox, v_w_up_sb, v_w_up_dil, v_w_out, v_mlp_norm, v_w_mlp_in, v_w_mlp_out):
    names = ("attn_norm", "w_in", "b_forget", "q_norm_fox", "k_norm_fox", "q_norm_dil", "k_norm_dil", "w_up_fox", "w_up_sb",
             "w_up_dil", "w_out", "mlp_norm", "w_mlp_in", "w_mlp_out")
    wv = dict(zip(names, (attn_norm, w_in, b_forget, q_norm_fox, k_norm_fox, q_norm_dil, k_norm_dil, w_up_fox, w_up_sb, w_up_dil,
                          w_out, mlp_norm, w_mlp_in, w_mlp_out)))
    mv = dict(zip(names, (m_attn_norm, m_w_in, m_b_forget, m_q_norm_fox, m_k_norm_fox, m_q_norm_dil, m_k_norm_dil, m_w_up_fox,
                          m_w_up_sb, m_w_up_dil, m_w_out, m_mlp_norm, m_w_mlp_in, m_w_mlp_out)))
    vv = dict(zip(names, (v_attn_norm, v_w_in, v_b_forget, v_q_norm_fox, v_k_norm_fox, v_q_norm_dil, v_k_norm_dil, v_w_up_fox,
                          v_w_up_sb, v_w_up_dil, v_w_out, v_mlp_norm, v_w_mlp_in, v_w_mlp_out)))
    depth = w_in.shape[0]
    flat2 = lambda a: a.reshape(-1, a.shape[-1])

    ups = ("w_up_fox", "w_up_sb", "w_up_dil")
    wide = ("w_mlp_in", "w_mlp_out", "w_out")
    send = [flat2(w_in).astype(MM), jnp.concatenate([flat2(wv[k]) for k in ups]).astype(MM),
            jnp.concatenate([flat2(wv[k]) for k in wide]).astype(MM)]
    core = lax.axis_index("c").astype(jnp.int32).reshape(1)
    chip = (2 * lax.axis_index("x") + lax.axis_index("y")).astype(jnp.int32).reshape(1)
    got_in, got_up, got_wide = [lax.dynamic_update_index_in_dim(g, s, chip[0], 0)
                                for g, s in zip(gather_chips(send, name="gather_weights"), send)]

    def split(a, keys):
        out, at = {}, 0
        for k in keys:
            rows = wv[k].shape[0] * wv[k].shape[1]
            out[k] = a[:, at:at + rows]
            at += rows
        return out

    full = {"w_in": _chips_to_cols(got_in, depth)}
    full.update({k: _chips_to_cols(a, depth) for k, a in split(got_up, ups).items()})
    parts = split(got_wide, wide)
    full["w_out"] = _chips_to_rows(parts["w_out"], depth)
    full["w_mlp_in"] = _chips_to_cols(parts["w_mlp_in"], depth)
    full["w_mlp_out"] = _chips_to_rows(parts["w_mlp_out"], depth)
    weights = [dict(win=_pad_w_in(full["w_in"][l]), wuf=full["w_up_fox"][l], wus=full["w_up_sb"][l], wud=full["w_up_dil"][l],
                    wo=full["w_out"][l], wmi=full["w_mlp_in"][l], wmo=full["w_mlp_out"][l]) for l in range(depth)]
    small = {k: wv[k] for k in SMALL}

    loss, grad_x, bufs, gss = local_step(x, positions, loss_target, weights, small)
    loss = lax.psum(loss, ("x", "y", "c"))

    g_small = {k: jnp.stack([gss[l][k] for l in range(depth)]) for k in SMALL}
    g_forget = jnp.stack([gss[l]["w_in_forget"] for l in range(depth)])
    summed = all_reduce_small(jnp.concatenate([_pack_small(g_small), g_forget.reshape(-1, 1024)]), name="reduce_small")
    g_small = _unpack_small(summed[:SMALL_ROWS], small)
    g_forget = summed[SMALL_ROWS:].reshape(g_forget.shape)

    parts = [bufs["win"], bufs["ups"], bufs["wide"]]
    sums = [pair_sum(p, core, name=f"reduce_pair_sum{t}") for t, p in enumerate(parts)]
    landed = scatter_chips([s16 for _, s16 in sums], name="reduce_scatter_chips")
    joined = [chip_sum_join(sums[t][0], landed[t], chip, name=f"reduce_chip_sum{t}").reshape(-1, parts[t].shape[-1])
              for t in range(3)]

    def own_w_in_columns(window):
        cols = w_in.shape[-1]
        first = jnp.concatenate([window[..., :O1], g_forget, window[..., O1:cols - (O2 - O1)]], axis=-1)
        shift = jnp.maximum((cols - WIN_STRIDE * WIN_TILE) * chip[0] - (O2 - O1), 0)
        rest = lax.dynamic_slice_in_dim(window, shift, cols, axis=2)
        return jnp.where(chip[0] == 0, first, rest)

    g_big = {"w_in": own_w_in_columns(joined[0].reshape(depth, -1, joined[0].shape[-1]))}
    for a, keys in ((joined[1], ups), (joined[2], wide)):
        at = 0
        for k in keys:
            rows = wv[k].shape[0] * wv[k].shape[1]
            g_big[k] = a[at:at + rows].reshape(wv[k].shape)
            at += rows

    grads = {**g_small, **g_big}
    delta, new_m, new_v = {}, {}, {}
    d_s, m_s, v_s = adamw(_pack_small(small), _pack_small(g_small), _pack_small({k: mv[k] for k in SMALL}),
                          _pack_small({k: vv[k] for k in SMALL}), name="adamw_small")
    delta.update(_unpack_small(d_s, small))
    new_m.update(_unpack_small(m_s, small))
    new_v.update(_unpack_small(v_s, small))
    for k in ("w_in",) + ups + wide:
        d_k, m_k, v_k = adamw(flat2(wv[k]), flat2(g_big[k]), flat2(mv[k]), flat2(vv[k]), name=f"adamw_{k}")
        delta[k], new_m[k], new_v[k] = d_k.reshape(wv[k].shape), m_k.reshape(wv[k].shape), v_k.reshape(wv[k].shape)

    return (loss, grad_x, *[grads[k] for k in names], *[delta[k] for k in names], *[new_m[k] for k in names], *[new_v[k] for k in names])
```

```python
import functools

import jax
import jax.numpy as jnp
from jax import lax
from jax.experimental import pallas as pl
from jax.experimental.pallas import tpu as pltpu

F32 = jnp.float32
BF16 = jnp.bfloat16
MM = jnp.bfloat16

HEAD = 64
LANES = 128
EPS = 1e-6
SCALE = 0.125
ROPE_THETA = 10000.0
DIL_PATTERNS = ((128, 1), (512, 4), (2048, 16))
ADAM_LR, ADAM_B1, ADAM_B2, ADAM_EPS, ADAM_WD, ADAM_STEP = 0.001, 0.9, 0.999, 1e-08, 0.01, 10

FOXQ, FOXK, FOXV = 0, 4, 8
SBQ, SBK, SBV = 12, 16, 20
DILQ, DILK, DILV = 24, 30, 36
GATE, FORGET, NBLK = 42, 66, 68
DPROJ = NBLK * LANES
O1, O2, O3, O4, DIN = 1536, 1544, 3080, 5384, 8456

VMEM_LIMIT = 56 * 1024 * 1024
MESH_ID = pl.DeviceIdType.MESH
ANY = pl.BlockSpec(memory_space=pl.ANY)


def _params(sem=None):
    return pltpu.CompilerParams(dimension_semantics=sem, vmem_limit_bytes=VMEM_LIMIT)


def _iota(shape, dim):
    return lax.broadcasted_iota(jnp.int32, shape, dim)


def _split2(x):
    hi = x.astype(BF16)
    lo = (x - hi.astype(F32)).astype(BF16)
    return hi, lo


def _split3(x):
    hi = x.astype(BF16)
    r = x - hi.astype(F32)
    mid = r.astype(BF16)
    lo = (r - mid.astype(F32)).astype(BF16)
    return hi, mid, lo


def _dot(a, b):
    return jnp.dot(a, b, preferred_element_type=F32)


def _dot_nt(a, b):
    return lax.dot_general(a, b, (((1,), (1,)), ((), ())), preferred_element_type=F32)


def _dot_tn(a, b):
    return lax.dot_general(a, b, (((0,), (0,)), ((), ())), preferred_element_type=F32)


def _xdot2(x, m):
    hi, lo = _split2(x)
    return _dot(hi, m) + _dot(lo, m)


def _xdot3(x, m):
    hi, mid, lo = _split3(x)
    return _dot(hi, m) + _dot(mid, m) + _dot(lo, m)


def _xdot3_left(m, x):
    hi, mid, lo = _split3(x)
    return _dot(m, hi) + _dot(m, mid) + _dot(m, lo)


def _head_mat(w):
    return ((_iota((w, w), 0) >> 6) == (_iota((w, w), 1) >> 6)).astype(BF16)


def _softplus_parts(z):
    e = jnp.exp(-jnp.abs(z))
    return e, jnp.maximum(z, 0.0) + jnp.log(1.0 + e)


def _fit(dim, want):
    t = min(want, dim)
    while dim % t:
        t -= LANES
        assert t > 0, (dim, want)
    return t


def matmul(a, b, *, ta=False, tb=False, out_dtype=F32, add=None, tm=2048, tn=512, tk=1024, dest=None, relu2=False,
           relu2_of=None, name):
    K, M = a.shape if ta else a.shape[::-1]
    K2, N = b.shape[::-1] if tb else b.shape
    assert K == K2, (a.shape, b.shape, ta, tb)
    tm, tn, tk = _fit(M, tm), _fit(N, tn), _fit(K, tk)
    nk = K // tk
    dn = (((0 if ta else 1,), (1 if tb else 0,)), ((), ()))
    if dest is None:
        tiles, source = N // tn, lambda j: j
    else:
        assert add is None and not tb
        buffer, tiles, source, place = dest

    extra = add if add is not None else relu2_of
    assert add is None or relu2_of is None

    def body(*refs):
        act_ref = None
        if dest is not None:
            a_ref, b_ref, _, o_ref, acc_ref = refs
        elif relu2:
            a_ref, b_ref, o_ref, act_ref, acc_ref = refs
        elif extra is None:
            a_ref, b_ref, o_ref, acc_ref = refs
        else:
            a_ref, b_ref, add_ref, o_ref, acc_ref = refs
        k = pl.program_id(2)
        part = lax.dot_general(a_ref[...].astype(MM), b_ref[...].astype(MM), dn, preferred_element_type=F32)

        @pl.when(k == 0)
        def _():
            acc_ref[...] = part

        @pl.when(k > 0)
        def _():
            acc_ref[...] += part

        @pl.when(k == nk - 1)
        def _():
            r = acc_ref[...]
            if add is not None:
                r = r + add_ref[...]
            if relu2_of is not None:
                r = r * (2.0 * jnp.maximum(add_ref[...], 0.0))
            o_ref[...] = r.astype(o_ref.dtype).reshape(o_ref.shape)
            if act_ref is not None:
                pos = jnp.maximum(r, 0.0)
                act_ref[...] = (pos * pos).astype(act_ref.dtype)

    a_spec = pl.BlockSpec((tk, tm), lambda i, j, k: (k, i)) if ta else pl.BlockSpec((tm, tk), lambda i, j, k: (i, k))
    b_spec = pl.BlockSpec((tn, tk), lambda i, j, k: (j, k)) if tb else pl.BlockSpec((tk, tn), lambda i, j, k: (k, source(j)))
    o_spec = pl.BlockSpec((tm, tn), lambda i, j, k: (i, j))
    ins, specs, aliases = [a, b], [a_spec, b_spec], {}
    out_shape = jax.ShapeDtypeStruct((M, N), out_dtype)
    if extra is not None:
        ins.append(extra)
        specs.append(o_spec)
    if relu2:
        o_spec, out_shape = [o_spec, o_spec], [out_shape, jax.ShapeDtypeStruct((M, N), MM)]
    if dest is not None:
        ins.append(buffer)
        specs.append(ANY)
        aliases = {2: 0}
        o_spec = pl.BlockSpec((1, tm, tn), lambda i, j, k: place(i, j))
        out_shape = jax.ShapeDtypeStruct(buffer.shape, buffer.dtype)
    return pl.pallas_call(
        body, name=name, grid=(M // tm, tiles, nk), in_specs=specs, out_specs=o_spec, out_shape=out_shape,
        scratch_shapes=[pltpu.VMEM((tm, tn), F32)], input_output_aliases=aliases,
        compiler_params=_params(("parallel", "parallel", "arbitrary")),
    )(*ins)


def _rows(n, want=512):
    t = min(want, n)
    assert n % t == 0, (n, t)
    return t


def rmsnorm_fwd(x, g, *, name):
    n, d = x.shape
    tr = _rows(n)

    def body(x_ref, g_ref, o_ref):
        xv = x_ref[...]
        r = lax.rsqrt(jnp.mean(xv * xv, axis=1, keepdims=True) + EPS)
        o_ref[...] = (xv * r * g_ref[...]).astype(o_ref.dtype)

    row = pl.BlockSpec((tr, d), lambda i: (i, 0))
    vec = pl.BlockSpec((1, d), lambda i: (0, 0))
    return pl.pallas_call(body, name=name, grid=(n // tr,), in_specs=[row, vec], out_specs=row,
                          out_shape=jax.ShapeDtypeStruct((n, d), MM), compiler_params=_params(("parallel",)))(x, g)


def rmsnorm_bwd(x, g, dh, dres, *, name):
    n, d = x.shape
    tr = _rows(n)

    def body(x_ref, g_ref, dh_ref, dr_ref, dx_ref, dg_ref):
        @pl.when(pl.program_id(0) == 0)
        def _():
            dg_ref[...] = jnp.zeros_like(dg_ref)

        xv = x_ref[...]
        r = lax.rsqrt(jnp.mean(xv * xv, axis=1, keepdims=True) + EPS)
        y = xv * r
        dhv = dh_ref[...]
        dy = dhv * g_ref[...]
        dx_ref[...] = dr_ref[...] + r * (dy - y * jnp.mean(dy * y, axis=1, keepdims=True))
        dg_ref[...] += jnp.sum(dhv * y, axis=0, keepdims=True)

    row = pl.BlockSpec((tr, d), lambda i: (i, 0))
    vec = pl.BlockSpec((1, d), lambda i: (0, 0))
    return pl.pallas_call(
        body, name=name, grid=(n // tr,), in_specs=[row, vec, row, row], out_specs=[row, vec],
        out_shape=[jax.ShapeDtypeStruct((n, d), F32), jax.ShapeDtypeStruct((1, d), F32)],
        compiler_params=_params(("arbitrary",)))(x, g, dh, dres)


def loss_grad(y, tgt, *, name):
    n, d = y.shape
    tr = _rows(n)

    def body(y_ref, t_ref, dy_ref, acc_ref):
        @pl.when(pl.program_id(0) == 0)
        def _():
            acc_ref[...] = jnp.zeros_like(acc_ref)

        e = y_ref[...] - t_ref[...]
        dy_ref[...] = e * (1.0 / d)
        acc_ref[...] += jnp.sum(e * e, axis=0, keepdims=True)

    row = pl.BlockSpec((tr, d), lambda i: (i, 0))
    vec = pl.BlockSpec((1, d), lambda i: (0, 0))
    return pl.pallas_call(
        body, name=name, grid=(n // tr,), in_specs=[row, row], out_specs=[row, vec],
        out_shape=[jax.ShapeDtypeStruct((n, d), F32), jax.ShapeDtypeStruct((1, d), F32)],
        compiler_params=_params(("arbitrary",)))(y, tgt)


MERGE_W = 256


def _gate_specs(tr, d):
    per = d // MERGE_W
    base = GATE * LANES // MERGE_W
    return [pl.BlockSpec((tr, MERGE_W), functools.partial(lambda i, j, b: (i, base + per * b + j), b=b)) for b in range(3)]


def merge_fwd(proj, ys, *, name):
    n, d = ys[0].shape
    tr = _rows(n)

    def body(g0, g1, g2, y0, y1, y2, o_ref):
        acc = jax.nn.sigmoid(g0[...]) * y0[...]
        acc += jax.nn.sigmoid(g1[...]) * y1[...]
        acc += jax.nn.sigmoid(g2[...]) * y2[...]
        o_ref[...] = acc.astype(o_ref.dtype)

    blk = pl.BlockSpec((tr, MERGE_W), lambda i, j: (i, j))
    return pl.pallas_call(
        body, name=name, grid=(n // tr, d // MERGE_W), in_specs=_gate_specs(tr, d) + [blk] * 3, out_specs=blk,
        out_shape=jax.ShapeDtypeStruct((n, d), MM), compiler_params=_params(("parallel", "parallel")))(proj, proj, proj, *ys)


def merge_bwd(proj, ys, dm, *, name):
    n, d = dm.shape
    tr = _rows(n)

    def body(g0, g1, g2, y0, y1, y2, dm_ref, dy0, dy1, dy2, dg0, dg1, dg2):
        dmv = dm_ref[...]
        for g, y, dy, dg in ((g0, y0, dy0, dg0), (g1, y1, dy1, dg1), (g2, y2, dy2, dg2)):
            s = jax.nn.sigmoid(g[...])
            dy[...] = (dmv * s).astype(dy.dtype)
            dg[...] = (dmv * y[...] * s * (1.0 - s)).astype(dg.dtype)

    blk = pl.BlockSpec((tr, MERGE_W), lambda i, j: (i, j))
    out = jax.ShapeDtypeStruct((n, d), MM)
    return pl.pallas_call(
        body, name=name, grid=(n // tr, d // MERGE_W), in_specs=_gate_specs(tr, d) + [blk] * 4, out_specs=[blk] * 6,
        out_shape=[out] * 6, compiler_params=_params(("parallel", "parallel")))(proj, proj, proj, *ys, dm)


def adamw(w, g, m, v, *, name):
    r, c = w.shape
    tr = r
    while tr * c * 4 > (1 << 21) and tr % 16 == 0:
        tr //= 2
    c1 = 1.0 / (1.0 - ADAM_B1 ** ADAM_STEP)
    c2 = 1.0 / (1.0 - ADAM_B2 ** ADAM_STEP)

    def body(w_ref, g_ref, m_ref, v_ref, d_ref, mo_ref, vo_ref):
        gv = g_ref[...]
        m2 = ADAM_B1 * m_ref[...] + (1.0 - ADAM_B1) * gv
        v2 = ADAM_B2 * v_ref[...] + (1.0 - ADAM_B2) * (gv * gv)
        d_ref[...] = -ADAM_LR * ((m2 * c1) / (jnp.sqrt(v2 * c2) + ADAM_EPS) + ADAM_WD * w_ref[...])
        mo_ref[...] = m2
        vo_ref[...] = v2

    blk = pl.BlockSpec((tr, c), lambda i: (i, 0))
    out = jax.ShapeDtypeStruct((r, c), F32)
    return pl.pallas_call(body, name=name, grid=(r // tr,), in_specs=[blk] * 4, out_specs=[blk] * 3, out_shape=[out] * 3,
                          compiler_params=_params(("parallel",)))(w, g, m, v)


def rope_table(pos, inv, *, name):
    n = pos.shape[0]
    tr = _rows(n)

    def body(p_ref, i_ref, c_ref, s_ref):
        ang = p_ref[...].astype(F32) * i_ref[...]
        c_ref[...] = jnp.cos(ang)
        s_ref[...] = jnp.sin(ang)

    out = jax.ShapeDtypeStruct((n, LANES), F32)
    blk = pl.BlockSpec((tr, LANES), lambda i: (i, 0))
    return pl.pallas_call(
        body, name=name, grid=(n // tr,), in_specs=[pl.BlockSpec((tr, 1), lambda i: (i, 0)), pl.BlockSpec((1, LANES), lambda i: (0, 0))],
        out_specs=[blk, blk], out_shape=[out, out], compiler_params=_params(("parallel",)))(pos, inv)


def _rot_half(x):
    first = (_iota((1, LANES), 1) & 63) < 32
    return jnp.where(first, -pltpu.roll(x, LANES - 32, axis=1), pltpu.roll(x, 32, axis=1))


def _head_norm(xv, gm):
    r = lax.rsqrt(_xdot2(xv * xv, gm) * (1.0 / HEAD) + EPS)
    return r, xv * r


def _head_norm_bwd(xh, r, dxh, gm):
    return r * (dxh - xh * (_xdot2(dxh * xh, gm) * (1.0 / HEAD)))


def fox_prep_fwd(proj, qg, kg, bf, *, bsz, seq, name):
    n = bsz * seq
    tr = min(256, seq)
    nt = seq // tr
    w = 4 * LANES

    def body(q_ref, k_ref, f_ref, qg_ref, kg_ref, b_ref, qn_ref, kn_ref, fb_ref, f8_ref, carry):
        @pl.when(pl.program_id(1) == 0)
        def _():
            carry[...] = jnp.zeros_like(carry)

        gm = _head_mat(LANES)
        for src, gain, dst in ((q_ref, qg_ref, qn_ref), (k_ref, kg_ref, kn_ref)):
            for c in range(4):
                sl = slice(c * LANES, (c + 1) * LANES)
                _, xh = _head_norm(src[:, sl], gm)
                dst[:, sl] = (xh * gain[:, sl]).astype(dst.dtype)
        logf = jax.nn.log_sigmoid(f_ref[...] + b_ref[...])
        lower = (_iota((tr, tr), 1) <= _iota((tr, tr), 0)).astype(BF16)
        fcum = _xdot3_left(lower, logf) + carry[...]
        carry[...] = fcum[tr - 1:tr, :]
        f8_ref[...] = fcum
        spread = (_iota((LANES, w), 0) == (_iota((LANES, w), 1) >> 6)).astype(BF16)
        fb_ref[...] = _xdot3(fcum, spread)

    row = lambda width, blk: pl.BlockSpec((tr, width), lambda b, t: (b * nt + t, blk))
    vec = lambda width: pl.BlockSpec((1, width), lambda b, t: (0, 0))
    return pl.pallas_call(
        body, name=name, grid=(bsz, nt),
        in_specs=[row(w, FOXQ // 4), row(w, FOXK // 4), row(LANES, FORGET), vec(w), vec(w), vec(LANES)],
        out_specs=[row(w, 0), row(w, 0), row(w, 0), row(LANES, 0)],
        out_shape=[jax.ShapeDtypeStruct((n, w), MM), jax.ShapeDtypeStruct((n, w), MM),
                   jax.ShapeDtypeStruct((n, w), F32), jax.ShapeDtypeStruct((n, LANES), F32)],
        scratch_shapes=[pltpu.VMEM((1, LANES), F32)],
        compiler_params=_params(("parallel", "arbitrary")))(proj, proj, proj, qg, kg, bf)


def fox_prep_bwd(proj, qg, kg, bf, dqn, dkn, df, *, bsz, seq, name):
    n = bsz * seq
    tr = min(256, seq)
    nt = seq // tr
    w = 4 * LANES

    def body(q_ref, k_ref, f_ref, qg_ref, kg_ref, b_ref, dqn_ref, dkn_ref, df_ref,
             dq_ref, dk_ref, dl_ref, dqg_ref, dkg_ref, db_ref, carry):
        first = (pl.program_id(0) == 0) & (pl.program_id(1) == 0)

        @pl.when(first)
        def _():
            dqg_ref[...] = jnp.zeros_like(dqg_ref)
            dkg_ref[...] = jnp.zeros_like(dkg_ref)
            db_ref[...] = jnp.zeros_like(db_ref)

        @pl.when(pl.program_id(1) == 0)
        def _():
            carry[...] = jnp.zeros_like(carry)

        gm = _head_mat(LANES)
        for src, gain, dy_ref, dx_ref, dg_ref in ((q_ref, qg_ref, dqn_ref, dq_ref, dqg_ref), (k_ref, kg_ref, dkn_ref, dk_ref, dkg_ref)):
            for c in range(4):
                sl = slice(c * LANES, (c + 1) * LANES)
                r, xh = _head_norm(src[:, sl], gm)
                dy = dy_ref[:, sl]
                dg_ref[:, sl] += jnp.sum(dy * xh, axis=0, keepdims=True)
                dx_ref[:, sl] = _head_norm_bwd(xh, r, dy * gain[:, sl], gm).astype(dx_ref.dtype)
        upper = (_iota((tr, tr), 1) >= _iota((tr, tr), 0)).astype(BF16)
        dlogf = _xdot3_left(upper, df_ref[...]) + carry[...]
        carry[...] = dlogf[0:1, :]
        dlogit = dlogf * jax.nn.sigmoid(-(f_ref[...] + b_ref[...]))
        dl_ref[:, 0:LANES] = dlogit.astype(dl_ref.dtype)
        dl_ref[:, LANES:2 * LANES] = jnp.zeros((tr, LANES), dl_ref.dtype)
        db_ref[...] += jnp.sum(dlogit, axis=0, keepdims=True)

    row = lambda width, blk: pl.BlockSpec((tr, width), lambda b, t: (b * nt + nt - 1 - t, blk))
    vec = lambda width: pl.BlockSpec((1, width), lambda b, t: (0, 0))
    return pl.pallas_call(
        body, name=name, grid=(bsz, nt),
        in_specs=[row(w, FOXQ // 4), row(w, FOXK // 4), row(LANES, FORGET), vec(w), vec(w), vec(LANES),
                  row(w, 0), row(w, 0), row(LANES, 0)],
        out_specs=[row(w, 0), row(w, 0), row(2 * LANES, 0), vec(w), vec(w), vec(LANES)],
        out_shape=[jax.ShapeDtypeStruct((n, w), MM), jax.ShapeDtypeStruct((n, w), MM), jax.ShapeDtypeStruct((n, 2 * LANES), MM),
                   jax.ShapeDtypeStruct((1, w), F32), jax.ShapeDtypeStruct((1, w), F32), jax.ShapeDtypeStruct((1, LANES), F32)],
        scratch_shapes=[pltpu.VMEM((1, LANES), F32)],
        compiler_params=_params(("arbitrary", "arbitrary")))(proj, proj, proj, qg, kg, bf, dqn, dkn, df)


DIL_W = 6 * LANES


def dil_prep_fwd(proj, qg, kg, cos, sin, *, name):
    n = proj.shape[0]
    tr = _rows(n, 256)

    def body(q_ref, k_ref, qg_ref, kg_ref, c_ref, s_ref, qo_ref, ko_ref):
        gm = _head_mat(LANES)
        cv, sv = c_ref[...], s_ref[...]
        for src, gain, dst in ((q_ref, qg_ref, qo_ref), (k_ref, kg_ref, ko_ref)):
            for c in range(6):
                sl = slice(c * LANES, (c + 1) * LANES)
                _, xh = _head_norm(src[:, sl], gm)
                xn = xh * gain[:, sl]
                dst[:, sl] = (xn * cv + _rot_half(xn) * sv).astype(dst.dtype)

    row = lambda width, blk: pl.BlockSpec((tr, width), lambda i: (i, blk))
    vec = pl.BlockSpec((1, DIL_W), lambda i: (0, 0))
    out = jax.ShapeDtypeStruct((n, DIL_W), MM)
    return pl.pallas_call(
        body, name=name, grid=(n // tr,),
        in_specs=[row(DIL_W, DILQ // 6), row(DIL_W, DILK // 6), vec, vec, row(LANES, 0), row(LANES, 0)],
        out_specs=[row(DIL_W, 0), row(DIL_W, 0)], out_shape=[out, out],
        compiler_params=_params(("parallel",)))(proj, proj, qg, kg, cos, sin)


def dil_prep_bwd(proj, qg, kg, cos, sin, dqr, dkr, *, name):
    n = proj.shape[0]
    tr = _rows(n, 256)

    def body(q_ref, k_ref, qg_ref, kg_ref, c_ref, s_ref, dqr_ref, dkr_ref, dq_ref, dk_ref, dqg_ref, dkg_ref):
        @pl.when(pl.program_id(0) == 0)
        def _():
            dqg_ref[...] = jnp.zeros_like(dqg_ref)
            dkg_ref[...] = jnp.zeros_like(dkg_ref)

        gm = _head_mat(LANES)
        cv, sv = c_ref[...], s_ref[...]
        for src, gain, dy_ref, dx_ref, dg_ref in ((q_ref, qg_ref, dqr_ref, dq_ref, dqg_ref), (k_ref, kg_ref, dkr_ref, dk_ref, dkg_ref)):
            for c in range(6):
                sl = slice(c * LANES, (c + 1) * LANES)
                r, xh = _head_norm(src[:, sl], gm)
                dy = dy_ref[:, sl]
                dxn = dy * cv - _rot_half(dy * sv)
                dg_ref[:, sl] += jnp.sum(dxn * xh, axis=0, keepdims=True)
                dx_ref[:, sl] = _head_norm_bwd(xh, r, dxn * gain[:, sl], gm).astype(dx_ref.dtype)

    row = lambda width, blk: pl.BlockSpec((tr, width), lambda i: (i, blk))
    vec = pl.BlockSpec((1, DIL_W), lambda i: (0, 0))
    out = jax.ShapeDtypeStruct((n, DIL_W), MM)
    gout = jax.ShapeDtypeStruct((1, DIL_W), F32)
    return pl.pallas_call(
        body, name=name, grid=(n // tr,),
        in_specs=[row(DIL_W, DILQ // 6), row(DIL_W, DILK // 6), vec, vec, row(LANES, 0), row(LANES, 0), row(DIL_W, 0), row(DIL_W, 0)],
        out_specs=[row(DIL_W, 0), row(DIL_W, 0), vec, vec], out_shape=[out, out, gout, gout],
        compiler_params=_params(("arbitrary",)))(proj, proj, qg, kg, cos, sin, dqr, dkr)


def dil_combine_fwd(os_, lses, *, name):
    n, w = os_[0].shape
    tr = _rows(n)

    def body(o0, o1, o2, l0, l1, l2, out_ref):
        a, b, c = l0[...], l1[...], l2[...]
        m = jnp.maximum(jnp.maximum(a, b), c)
        ea, eb, ec = jnp.exp(a - m), jnp.exp(b - m), jnp.exp(c - m)
        out_ref[...] = ((ea * o0[...] + eb * o1[...] + ec * o2[...]) / (ea + eb + ec)).astype(out_ref.dtype)

    blk = pl.BlockSpec((tr, w), lambda i: (i, 0))
    return pl.pallas_call(body, name=name, grid=(n // tr,), in_specs=[blk] * 6, out_specs=blk,
                          out_shape=jax.ShapeDtypeStruct((n, w), MM), compiler_params=_params(("parallel",)))(*os_, *lses)


def dil_combine_bwd(os_, lses, dout, *, name):
    n, w = dout.shape
    tr = _rows(n)

    def body(o0, o1, o2, l0, l1, l2, d_ref, do0, do1, do2, dl0, dl1, dl2):
        a, b, c = l0[...], l1[...], l2[...]
        m = jnp.maximum(jnp.maximum(a, b), c)
        es = [jnp.exp(a - m), jnp.exp(b - m), jnp.exp(c - m)]
        inv = 1.0 / (es[0] + es[1] + es[2])
        ws = [e * inv for e in es]
        dv = d_ref[...]
        gm = _head_mat(w)
        dws = [_xdot2(dv * o[...], gm) for o in (o0, o1, o2)]
        mean = ws[0] * dws[0] + ws[1] * dws[1] + ws[2] * dws[2]
        for wg, dw, do, dl in zip(ws, dws, (do0, do1, do2), (dl0, dl1, dl2)):
            do[...] = wg * dv
            dl[...] = wg * (dw - mean)

    blk = pl.BlockSpec((tr, w), lambda i: (i, 0))
    out = jax.ShapeDtypeStruct((n, w), F32)
    return pl.pallas_call(body, name=name, grid=(n // tr,), in_specs=[blk] * 7, out_specs=[blk] * 6, out_shape=[out] * 6,
                          compiler_params=_params(("parallel",)))(*os_, *lses, dout)


def _key_plan(qi, tq, seq, window, run):
    if window + tq >= seq:
        for bi in range(seq // tq):
            lo = bi * tq
            segs = ([(0, lo, "bulk")] if lo else []) + [(lo, tq, "diag")]
            pl.when(qi == bi)(functools.partial(run, segs))
    else:
        ext = window + tq
        run([(pl.multiple_of(jnp.maximum((qi + 1) * tq - ext, 0), LANES), ext, "band")])


def _seg_mask(seg, qi, tq, window, dilation, strict=False):
    start, width, kind = seg
    d = _iota((tq, width), 0) - _iota((tq, width), 1)
    if kind == "bulk":
        d = d + width
    elif kind == "band":
        d = d + (qi * tq - start)
    ok = None
    if kind != "bulk":
        ok = (d > 0) if strict else (d >= 0)
    if window is not None:
        ok = (d <= window) if ok is None else ok & (d <= window)
    if dilation > 1:
        on_grid = (d & (dilation - 1)) == 0
        ok = on_grid if ok is None else ok & on_grid
    return ok


def _lane_first():
    return _iota((1, LANES), 1) < HEAD


def _attn_specs(bsz, seq, tq, qo, ko, vo):
    nq = seq // tq
    qspec = lambda off: pl.BlockSpec((tq, LANES), lambda b, j, i: (b * nq + i, off + j))
    kspec = lambda off: pl.BlockSpec((seq, LANES), lambda b, j, i: (b, off + j))
    return nq, qspec, kspec


def softmax_attn_fwd(q, k, v, bias, *, qo, ko, vo, pairs, bsz, seq, window, dilation, tq, name):
    n = bsz * seq
    nq, qspec, kspec = _attn_specs(bsz, seq, tq, qo, ko, vo)

    def body(*refs):
        if bias is None:
            q_ref, k_ref, v_ref, o_ref, l_ref = refs
        else:
            q_ref, k_ref, v_ref, fq_ref, fk_ref, o_ref, l_ref = refs
        qi = pl.program_id(2)

        def run(segs):
            qv = (q_ref[...] * SCALE).astype(MM)
            first = _lane_first()
            keys = [(k_ref[pl.ds(st, w), :].astype(MM), v_ref[pl.ds(st, w), :].astype(MM),
                     _seg_mask((st, w, kind), qi, tq, None if window >= seq else window, dilation), st, w)
                    for st, w, kind in segs]
            outs, lses = [], []
            for a in range(2):
                qa = jnp.where(first if a == 0 else ~first, qv, jnp.zeros_like(qv))
                scores = []
                for kv, _, ok, st, w in keys:
                    s = _dot_nt(qa, kv)
                    if bias is not None:
                        s = s + fq_ref[:, a * HEAD:a * HEAD + 1] - fk_ref[a:a + 1, pl.ds(st, w)]
                    scores.append(s if ok is None else jnp.where(ok, s, -jnp.inf))
                m = functools.reduce(jnp.maximum, [jnp.max(s, axis=1, keepdims=True) for s in scores])
                ps = [jnp.exp(s - m) for s in scores]
                den = sum(jnp.sum(p, axis=1, keepdims=True) for p in ps)
                acc = sum(_dot(p.astype(MM), vv) for p, (_, vv, _, _, _) in zip(ps, keys))
                outs.append(acc / den)
                lses.append(m + jnp.log(den))
            o_ref[...] = jnp.where(first, outs[0], outs[1]).astype(o_ref.dtype)
            l_ref[...] = jnp.where(first, lses[0], lses[1])

        _key_plan(qi, tq, seq, window, run)

    ins, specs = [q, k, v], [qspec(qo), kspec(ko), kspec(vo)]
    if bias is not None:
        ins += list(bias)
        specs += [qspec(0), pl.BlockSpec((8, seq), lambda b, j, i: (b * pairs + j, 0))]
    out = jax.ShapeDtypeStruct((n, LANES * pairs), F32)
    return pl.pallas_call(
        body, name=name, grid=(bsz, pairs, nq), in_specs=specs, out_specs=[qspec(0), qspec(0)], out_shape=[out, out],
        compiler_params=_params(("parallel", "parallel", "arbitrary")))(*ins)


def softmax_attn_bwd(q, k, v, o, do, lse, dlse, bias, *, qo, ko, vo, pairs, bsz, seq, window, dilation, tq, dq_dtype, dk_dtype, name):
    n = bsz * seq
    nq, qspec, kspec = _attn_specs(bsz, seq, tq, qo, ko, vo)
    has_bias, has_dlse = bias is not None, dlse is not None

    def body(*refs):
        refs = list(refs)
        q_ref, k_ref, v_ref, o_ref, do_ref, l_ref = refs[:6]
        del refs[:6]
        dl_ref = refs.pop(0) if has_dlse else None
        fq_ref, fk_ref = (refs.pop(0), refs.pop(0)) if has_bias else (None, None)
        dq_ref, dk_ref, dv_ref = refs[:3]
        del refs[:3]
        dfq_ref, dfk_ref = (refs.pop(0), refs.pop(0)) if has_bias else (None, None)
        dk_acc, dv_acc = refs
        qi = pl.program_id(2)

        @pl.when(qi == 0)
        def _():
            dk_acc[...] = jnp.zeros_like(dk_acc)
            dv_acc[...] = jnp.zeros_like(dv_acc)
            if has_bias:
                dfk_ref[...] = jnp.zeros_like(dfk_ref)

        def run(segs):
            qv = (q_ref[...] * SCALE).astype(MM)
            dov = do_ref[...]
            dob = dov.astype(MM)
            prod = dov * o_ref[...]
            first = _lane_first()
            keys = [(k_ref[pl.ds(st, w), :].astype(MM), v_ref[pl.ds(st, w), :].astype(MM),
                     _seg_mask((st, w, kind), qi, tq, None if window >= seq else window, dilation), st, w)
                    for st, w, kind in segs]
            dqs, dfqs = [], []
            dks, dvs = [[] for _ in keys], [[] for _ in keys]
            for a in range(2):
                mine = first if a == 0 else ~first
                col = slice(a * HEAD, a * HEAD + 1)
                delta = jnp.sum(jnp.where(mine, prod, 0.0), axis=1, keepdims=True)
                if has_dlse:
                    delta = delta - dl_ref[:, col]
                qa = jnp.where(mine, qv, jnp.zeros_like(qv))
                doa = jnp.where(mine, dob, jnp.zeros_like(dob))
                shift = l_ref[:, col]
                if has_bias:
                    shift = shift - fq_ref[:, col]
                dq, dfq = 0.0, 0.0
                for si, (kv, vv, ok, st, w) in enumerate(keys):
                    s = _dot_nt(qa, kv)
                    if has_bias:
                        s = s - fk_ref[a:a + 1, pl.ds(st, w)]
                    p = jnp.exp(s - shift)
                    if ok is not None:
                        p = jnp.where(ok, p, 0.0)
                    ds = p * (_dot_nt(doa, vv) - delta)
                    dsb = ds.astype(MM)
                    dvs[si].append(_dot_tn(p.astype(MM), dob))
                    dks[si].append(_dot_tn(dsb, qv))
                    dq = dq + _dot(dsb, kv)
                    if has_bias:
                        dfq = dfq + jnp.sum(ds, axis=1, keepdims=True)
                        dfk_ref[a:a + 1, pl.ds(st, w)] += jnp.sum(ds, axis=0, keepdims=True)
                dqs.append(dq * SCALE)
                dfqs.append(dfq)
            dq_ref[...] = jnp.where(first, dqs[0], dqs[1]).astype(dq_ref.dtype)
            for (_, _, _, st, w), dk, dv in zip(keys, dks, dvs):
                dk_acc[pl.ds(st, w), :] += jnp.where(first, dk[0], dk[1])
                dv_acc[pl.ds(st, w), :] += jnp.where(first, dv[0], dv[1])
            if has_bias:
                dfq_ref[...] = jnp.where(first, dfqs[0], dfqs[1])

        _key_plan(qi, tq, seq, window, run)

        @pl.when(qi == nq - 1)
        def _():
            dk_ref[...] = dk_acc[...].astype(dk_ref.dtype)
            dv_ref[...] = dv_acc[...].astype(dv_ref.dtype)

    wide = LANES * pairs
    ins = [q, k, v, o, do, lse]
    specs = [qspec(qo), kspec(ko), kspec(vo), qspec(0), qspec(0), qspec(0)]
    outs = [jax.ShapeDtypeStruct((n, wide), dq_dtype), jax.ShapeDtypeStruct((n, wide), dk_dtype), jax.ShapeDtypeStruct((n, wide), MM)]
    out_specs = [qspec(0), kspec(0), kspec(0)]
    if has_dlse:
        ins.append(dlse)
        specs.append(qspec(0))
    if has_bias:
        rows = pl.BlockSpec((8, seq), lambda b, j, i: (b * pairs + j, 0))
        ins += list(bias)
        specs += [qspec(0), rows]
        outs += [jax.ShapeDtypeStruct((n, wide), F32), jax.ShapeDtypeStruct((bsz * pairs * 8, seq), F32)]
        out_specs += [qspec(0), rows]
    return pl.pallas_call(
        body, name=name, grid=(bsz, pairs, nq), in_specs=specs, out_specs=out_specs, out_shape=outs,
        scratch_shapes=[pltpu.VMEM((seq, LANES), F32), pltpu.VMEM((seq, LANES), F32)],
        compiler_params=_params(("parallel", "parallel", "arbitrary")))(*ins)


def _running_sum(vals, mat, carry, lat_ref, start, reverse):
    nb = vals.shape[1] // LANES
    for cb in (reversed(range(nb)) if reverse else range(nb)):
        blk = vals[:, cb * LANES:(cb + 1) * LANES]
        lat_ref[:, start + cb * LANES:start + (cb + 1) * LANES] = _dot(blk.astype(BF16), mat) + carry
        carry = carry + jnp.sum(blk, axis=1, keepdims=True)
    return carry


def _sb_weights(qa, keys, tq, lat_ref):
    after = (_iota((LANES, LANES), 0) > _iota((LANES, LANES), 1)).astype(BF16)
    carry = jnp.zeros((tq, 1), F32)
    logs = []
    for kv, ok, st, w in reversed(keys):
        z = _dot_nt(qa, kv)
        _, sp = _softplus_parts(z)
        visible = sp if ok is None else jnp.where(ok, sp, 0.0)
        carry = _running_sum(visible, after, carry, lat_ref, st, True)
        logs.append(z - sp)
    out = []
    for (kv, ok, st, w), log_beta in zip(keys, reversed(logs)):
        att = jnp.exp(log_beta - lat_ref[:, st:st + w])
        out.append((log_beta, att if ok is None else jnp.where(ok, att, 0.0)))
    return out


def _sb_keys(k_ref, v_ref, segs, qi, tq):
    return [(k_ref[st:st + w, :].astype(MM), v_ref[st:st + w, :].astype(MM),
             _seg_mask((st, w, kind), qi, tq, None, 1, strict=True), st, w) for st, w, kind in segs]


def sb_attn_fwd(proj, *, bsz, seq, tq, name):
    n = bsz * seq
    pairs = 4
    nq, qspec, kspec = _attn_specs(bsz, seq, tq, SBQ, SBK, SBV)

    def body(q_ref, k_ref, v_ref, o_ref, lat_ref):
        qi = pl.program_id(2)

        def run(segs):
            qv = (q_ref[...] * SCALE).astype(MM)
            keys = _sb_keys(k_ref, v_ref, segs, qi, tq)
            first = _lane_first()
            outs = []
            for a in range(2):
                qa = jnp.where(first if a == 0 else ~first, qv, jnp.zeros_like(qv))
                weights = _sb_weights(qa, [(kv, ok, st, w) for kv, _, ok, st, w in keys], tq, lat_ref)
                outs.append(sum(_dot(att.astype(MM), vv) for (_, att), (_, vv, _, _, _) in zip(weights, keys)))
            o_ref[...] = jnp.where(first, outs[0], outs[1]).astype(o_ref.dtype)

        _key_plan(qi, tq, seq, seq, run)

    return pl.pallas_call(
        body, name=name, grid=(bsz, pairs, nq), in_specs=[qspec(SBQ), kspec(SBK), kspec(SBV)], out_specs=qspec(0),
        out_shape=jax.ShapeDtypeStruct((n, LANES * pairs), MM), scratch_shapes=[pltpu.VMEM((tq, seq), F32)],
        compiler_params=_params(("parallel", "parallel", "arbitrary")))(proj, proj, proj)


def sb_attn_bwd(proj, do, *, bsz, seq, tq, name):
    n = bsz * seq
    pairs = 4
    nq, qspec, kspec = _attn_specs(bsz, seq, tq, SBQ, SBK, SBV)

    def body(q_ref, k_ref, v_ref, do_ref, dq_ref, dk_ref, dv_ref, lat_ref, dk_acc, dv_acc):
        qi = pl.program_id(2)

        @pl.when(qi == 0)
        def _():
            dk_acc[...] = jnp.zeros_like(dk_acc)
            dv_acc[...] = jnp.zeros_like(dv_acc)

        def run(segs):
            qv = (q_ref[...] * SCALE).astype(MM)
            keys = _sb_keys(k_ref, v_ref, segs, qi, tq)
            dob = do_ref[...].astype(MM)
            first = _lane_first()
            before = (_iota((LANES, LANES), 0) < _iota((LANES, LANES), 1)).astype(BF16)
            dqs = []
            dks, dvs = [[] for _ in keys], [[] for _ in keys]
            for a in range(2):
                mine = first if a == 0 else ~first
                qa = jnp.where(mine, qv, jnp.zeros_like(qv))
                doa = jnp.where(mine, dob, jnp.zeros_like(dob))
                weights = _sb_weights(qa, [(kv, ok, st, w) for kv, _, ok, st, w in keys], tq, lat_ref)
                gs = [_dot_nt(doa, vv) * att for (_, att), (_, vv, _, _, _) in zip(weights, keys)]
                carry = jnp.zeros((tq, 1), F32)
                for g, (_, _, _, st, w) in zip(gs, keys):
                    carry = _running_sum(g, before, carry, lat_ref, st, False)
                dq = 0.0
                for si, ((log_beta, att), g, (kv, _, ok, st, w)) in enumerate(zip(weights, gs, keys)):
                    dz = g - jnp.exp(log_beta) * (g + lat_ref[:, st:st + w])
                    dz = (dz if ok is None else jnp.where(ok, dz, 0.0)).astype(MM)
                    dvs[si].append(_dot_tn(att.astype(MM), dob))
                    dks[si].append(_dot_tn(dz, qv))
                    dq = dq + _dot(dz, kv)
                dqs.append(dq * SCALE)
            dq_ref[...] = jnp.where(first, dqs[0], dqs[1]).astype(dq_ref.dtype)
            for (_, _, _, st, w), dk, dv in zip(keys, dks, dvs):
                dk_acc[st:st + w, :] += jnp.where(first, dk[0], dk[1])
                dv_acc[st:st + w, :] += jnp.where(first, dv[0], dv[1])

        _key_plan(qi, tq, seq, seq, run)

        @pl.when(qi == nq - 1)
        def _():
            dk_ref[...] = dk_acc[...].astype(dk_ref.dtype)
            dv_ref[...] = dv_acc[...].astype(dv_ref.dtype)

    out = jax.ShapeDtypeStruct((n, LANES * pairs), MM)
    return pl.pallas_call(
        body, name=name, grid=(bsz, pairs, nq), in_specs=[qspec(SBQ), kspec(SBK), kspec(SBV), qspec(0)],
        out_specs=[qspec(0), kspec(0), kspec(0)], out_shape=[out, out, out],
        scratch_shapes=[pltpu.VMEM((tq, seq), F32), pltpu.VMEM((seq, LANES), F32), pltpu.VMEM((seq, LANES), F32)],
        compiler_params=_params(("parallel", "parallel", "arbitrary")))(proj, proj, proj, do)


def _place():
    return lax.axis_index("x"), lax.axis_index("y"), lax.axis_index("c")


def _other_chips(x, y):
    return [(1 - x, y), (x, 1 - y), (1 - x, 1 - y)]


def _remote(src, dst, send_sems, recv_sems, k, to):
    return pltpu.make_async_remote_copy(src_ref=src, dst_ref=dst, send_sem=send_sems.at[k], recv_sem=recv_sems.at[k],
                                        device_id=to, device_id_type=MESH_ID)


def gather_chips(arrs, *, name):
    na = len(arrs)

    def body(*refs):
        ins, outs = refs[:na], refs[na:2 * na]
        send_sems, recv_sems = refs[2 * na:]
        x, y, c = _place()
        me, sibling = 2 * x + y, (x, y, 1 - c)
        chips = _other_chips(x, y)
        sends = []
        for t in range(na):
            rh = ins[t].shape[0] // 2
            half = lambda chip, h, t=t, rh=rh: outs[t].at[chip, pl.ds(h * rh, rh), :]
            for j, (px, py) in enumerate(chips):
                cp = _remote(ins[t].at[pl.ds(c * rh, rh), :], half(me, c), send_sems, recv_sems, 6 * t + j, (px, py, c))
                cp.start()
                sends.append(cp)
        for t in range(na):
            rh = ins[t].shape[0] // 2
            half = lambda chip, h, t=t, rh=rh: outs[t].at[chip, pl.ds(h * rh, rh), :]
            for j, (px, py) in enumerate(chips):
                landed = half(2 * px + py, c)
                _remote(landed, landed, send_sems, recv_sems, 6 * t + j, (px, py, c)).wait_recv()
                fw = _remote(landed, landed, send_sems, recv_sems, 6 * t + 3 + j, sibling)
                fw.start()
                sends.append(fw)
        for t in range(na):
            rh = ins[t].shape[0] // 2
            half = lambda chip, h, t=t, rh=rh: outs[t].at[chip, pl.ds(h * rh, rh), :]
            for j, (px, py) in enumerate(chips):
                passed = half(2 * px + py, 1 - c)
                _remote(passed, passed, send_sems, recv_sems, 6 * t + 3 + j, sibling).wait_recv()
        for cp in sends:
            cp.wait_send()

    for a in arrs:
        assert a.ndim == 2 and a.shape[0] % 32 == 0, a.shape
    return pl.pallas_call(
        body, name=name, in_specs=[ANY] * na, out_specs=[ANY] * na,
        out_shape=[jax.ShapeDtypeStruct((4,) + a.shape, a.dtype) for a in arrs],
        scratch_shapes=[pltpu.SemaphoreType.DMA((6 * na,)), pltpu.SemaphoreType.DMA((6 * na,))],
    )(*arrs)


CHUNK_BYTES = 2 << 20


def _chunk_rows(rows, cols, limit):
    best = 16
    for t in range(16, rows + 1, 16):
        if rows % t == 0 and t * cols * 4 <= limit:
            best = t
    assert rows % best == 0, (rows, cols)
    return best


def pair_sum_scatter(a, place, *, name):
    _, rows, cols = a.shape
    rh = rows // 2
    tr = _chunk_rows(rh, cols, CHUNK_BYTES)
    nch = rh // tr
    steps = 4 * nch

    def body(place_ref, keep_ref, send_ref, own_ref, landed_ref, landing, out16, res, pair_send, pair_recv, credit,
             chip_send, chip_recv, local_sem):
        i, j = pl.program_id(0), pl.program_id(1)
        step = i * 4 + j
        slot = lax.rem(step, 2)
        x, y, c = _place()
        sibling = (x, y, 1 - c)
        me = 2 * x + y
        rows_i = pl.ds(pl.multiple_of(i * tr, tr), tr)

        def to_chip(p, s):
            return pltpu.make_async_remote_copy(
                src_ref=out16.at[s], dst_ref=landed_ref.at[me, rows_i, :], send_sem=chip_send.at[s], recv_sem=chip_recv.at[p - 1],
                device_id=(x ^ (p >> 1), y ^ (p & 1), c), device_id_type=MESH_ID)

        @pl.when(step >= 2)
        def _():
            pl.semaphore_wait(credit, 1)

        cp = _remote(send_ref.at[0], landing.at[slot], pair_send, pair_recv, slot, sibling)
        cp.start()
        cp.wait_recv()
        total = keep_ref[0] + landing[slot]

        for p, s, before in ((1, 0, i > 0), (2, 1, i > 0), (3, 0, None)):
            @pl.when(j == p - 1)
            def _(p=p, s=s, before=before):
                if before is None:
                    to_chip(1, s).wait_send()
                else:
                    pl.when(before)(lambda: to_chip(1, s).wait_send())
                out16[s] = total.astype(BF16)
                to_chip(p, s).start()

        @pl.when(j == 3)
        def _():
            res[...] = total
            here = pltpu.make_async_copy(res, own_ref.at[rows_i, :], local_sem)
            here.start()
            here.wait()

        cp.wait_send()

        @pl.when(step + 2 < steps)
        def _():
            pl.semaphore_signal(credit, 1, device_id=sibling, device_id_type=MESH_ID)

        @pl.when(step == steps - 1)
        def _():
            to_chip(1, 1).wait_send()
            to_chip(1, 0).wait_send()
            for p in (1, 2, 3):
                slab = landed_ref.at[me ^ p]
                pltpu.make_async_remote_copy(src_ref=slab, dst_ref=slab, send_sem=chip_send.at[0], recv_sem=chip_recv.at[p - 1],
                                             device_id=(x ^ (p >> 1), y ^ (p & 1), c), device_id_type=MESH_ID).wait_recv()

    blk = (1, tr, cols)
    slab_of = lambda j, place: place[1] ^ ((j + 1) & 3)
    grid_spec = pltpu.PrefetchScalarGridSpec(
        num_scalar_prefetch=1, grid=(nch, 4),
        in_specs=[pl.BlockSpec(blk, lambda i, j, place: (slab_of(j, place), place[0] * nch + i, 0)),
                  pl.BlockSpec(blk, lambda i, j, place: (slab_of(j, place), (1 - place[0]) * nch + i, 0))],
        out_specs=[ANY, ANY],
        scratch_shapes=[pltpu.VMEM((2, tr, cols), F32), pltpu.VMEM((2, tr, cols), BF16), pltpu.VMEM((tr, cols), F32),
                        pltpu.SemaphoreType.DMA((2,)), pltpu.SemaphoreType.DMA((2,)), pltpu.SemaphoreType.REGULAR,
                        pltpu.SemaphoreType.DMA((2,)), pltpu.SemaphoreType.DMA((3,)), pltpu.SemaphoreType.DMA])
    return pl.pallas_call(
        body, name=name, grid_spec=grid_spec,
        out_shape=[jax.ShapeDtypeStruct((rh, cols), F32), jax.ShapeDtypeStruct((4, rh, cols), BF16)],
        compiler_params=_params(("arbitrary", "arbitrary")))(place, a, a)


def chip_sum_join(own, landed, chip, *, name):
    rh, cols = own.shape
    tr = _chunk_rows(rh, cols, CHUNK_BYTES)
    nch = rh // tr

    def body(chip_ref, own_ref, l1_ref, l2_ref, l3_ref, out_ref, res, local_sem, send_sem, recv_sem):
        i = pl.program_id(0)
        x, y, c = _place()
        sibling = (x, y, 1 - c)
        res[...] = ((own_ref[...] + l1_ref[0].astype(F32)) + l2_ref[0].astype(F32)) + l3_ref[0].astype(F32)
        rows = pl.ds(pl.multiple_of(i * tr, tr), tr)
        here = pltpu.make_async_copy(res, out_ref.at[c, rows, :], local_sem)
        here.start()
        there = pltpu.make_async_remote_copy(src_ref=res, dst_ref=out_ref.at[c, rows, :], send_sem=send_sem, recv_sem=recv_sem,
                                             device_id=sibling, device_id_type=MESH_ID)
        there.start()
        here.wait()
        there.wait_send()

        @pl.when(i == nch - 1)
        def _():
            half = out_ref.at[1 - c]
            pltpu.make_async_remote_copy(src_ref=half, dst_ref=half, send_sem=send_sem, recv_sem=recv_sem,
                                         device_id=sibling, device_id_type=MESH_ID).wait_recv()

    blk = (1, tr, cols)
    slab = lambda p: pl.BlockSpec(blk, lambda i, chip: (chip[0] ^ p, i, 0))
    grid_spec = pltpu.PrefetchScalarGridSpec(
        num_scalar_prefetch=1, grid=(nch,), out_specs=ANY,
        in_specs=[pl.BlockSpec((tr, cols), lambda i, chip: (i, 0)), slab(1), slab(2), slab(3)],
        scratch_shapes=[pltpu.VMEM((tr, cols), F32), pltpu.SemaphoreType.DMA, pltpu.SemaphoreType.DMA, pltpu.SemaphoreType.DMA])
    return pl.pallas_call(
        body, name=name, grid_spec=grid_spec, out_shape=jax.ShapeDtypeStruct((2, rh, cols), F32),
        compiler_params=_params(("arbitrary",)))(chip, own, landed, landed, landed)


def all_reduce_small(a, *, name):
    def body(a_ref, o_ref, buf, send_sems, recv_sems):
        x, y, c = _place()
        me = 4 * x + 2 * y + c
        buf[me] = a_ref[...]
        sent = []
        for p in range(1, 8):
            px, py, pc = (p >> 2) & 1, (p >> 1) & 1, p & 1
            cp = _remote(a_ref, buf.at[me], send_sems, recv_sems, p - 1, (x ^ px, y ^ py, c ^ pc))
            cp.start()
            sent.append(cp)
        for p in range(1, 8):
            px, py, pc = (p >> 2) & 1, (p >> 1) & 1, p & 1
            src = 4 * (x ^ px) + 2 * (y ^ py) + (c ^ pc)
            _remote(a_ref, buf.at[src], send_sems, recv_sems, p - 1, (x ^ px, y ^ py, c ^ pc)).wait_recv()
        for cp in sent:
            cp.wait_send()
        acc = buf[0]
        for d in range(1, 8):
            acc = acc + buf[d]
        o_ref[...] = acc

    vm = pl.BlockSpec(memory_space=pltpu.VMEM)
    return pl.pallas_call(
        body, name=name, in_specs=[vm], out_specs=vm, out_shape=jax.ShapeDtypeStruct(a.shape, a.dtype),
        scratch_shapes=[pltpu.VMEM((8,) + a.shape, a.dtype), pltpu.SemaphoreType.DMA((7,)), pltpu.SemaphoreType.DMA((7,))],
    )(a)


TQ = 256


def _layer_small(sm, l):
    row = lambda v: v.reshape(1, -1)
    return dict(
        attn_norm=row(sm["attn_norm"][l]), mlp_norm=row(sm["mlp_norm"][l]),
        qgf=row(jnp.tile(sm["q_norm_fox"][l], 8)), kgf=row(jnp.tile(sm["k_norm_fox"][l], 8)),
        qgd=row(jnp.tile(sm["q_norm_dil"][l], 12)), kgd=row(jnp.tile(sm["k_norm_dil"][l], 12)),
        bfor=row(jnp.pad(sm["b_forget"][l], (0, LANES - 8))))


def _key_rows(f8, bsz, seq):
    f = f8.reshape(bsz, seq, LANES)[:, :, :8].transpose(0, 2, 1).reshape(bsz, 4, 2, seq)
    return jnp.pad(f, ((0, 0), (0, 0), (0, 6), (0, 0))).reshape(bsz * 32, seq)


def _layer_fwd(x, w, s, cos, sin, bsz, seq, l):
    nm = lambda t: f"l{l}_{t}"
    h = rmsnorm_fwd(x, s["attn_norm"], name=nm("attn_norm"))
    proj = matmul(h, w["win"], name=nm("proj"))
    qn, kn, fb, f8 = fox_prep_fwd(proj, s["qgf"], s["kgf"], s["bfor"], bsz=bsz, seq=seq, name=nm("fox_prep"))
    fk = _key_rows(f8, bsz, seq)
    oa, la = softmax_attn_fwd(qn, kn, proj, (fb, fk), qo=0, ko=0, vo=FOXV, pairs=4, bsz=bsz, seq=seq, window=seq, dilation=1,
                              tq=TQ, name=nm("fox_attn"))
    ob = sb_attn_fwd(proj, bsz=bsz, seq=seq, tq=TQ, name=nm("sb_attn"))
    qr, kr = dil_prep_fwd(proj, s["qgd"], s["kgd"], cos, sin, name=nm("dil_prep"))
    ogs, lgs = [], []
    for g, (window, dilation) in enumerate(DIL_PATTERNS):
        og, lg = softmax_attn_fwd(qr, kr, proj, None, qo=2 * g, ko=2 * g, vo=DILV + 2 * g, pairs=2, bsz=bsz, seq=seq,
                                  window=window, dilation=dilation, tq=TQ, name=nm(f"dil_attn{g}"))
        ogs.append(og)
        lgs.append(lg)
    oc = dil_combine_fwd(ogs, lgs, name=nm("dil_combine"))
    ys = [matmul(oa, w["wuf"], name=nm("up_fox")), matmul(ob, w["wus"], name=nm("up_sb")), matmul(oc, w["wud"], name=nm("up_dil"))]
    merged = merge_fwd(proj, ys, name=nm("merge"))
    x1 = matmul(merged, w["wo"], add=x, name=nm("out_proj"))
    h2 = rmsnorm_fwd(x1, s["mlp_norm"], name=nm("mlp_norm"))
    u, act = matmul(h2, w["wmi"], relu2=True, name=nm("mlp_in"))
    x2 = matmul(act, w["wmo"], add=x1, name=nm("mlp_out"))
    saved = dict(x=x, h=h, proj=proj, qn=qn, kn=kn, fb=fb, fk=fk, oa=oa, la=la, ob=ob, qr=qr, kr=kr, ogs=ogs, lgs=lgs, oc=oc,
                 ys=ys, merged=merged, x1=x1, h2=h2, u=u, act=act)
    return x2, saved


WIN_TILE = 256
WIN_STRIDE, WIN_TILES = 8, 9


def grad_buffers(depth, d, dff, wf, wd):
    assert dff // 4 == d
    return dict(win=lax.empty((4, depth * d, WIN_TILES * WIN_TILE), F32), ups=lax.empty((4, depth * (2 * wf + wd), d // 4), F32),
                wide=lax.empty((4, depth * (d + dff // 4 + d // 4), d), F32))


def _layer_bwd(dx2, w, s, sv, cos, sin, bsz, seq, l, depth, bufs):
    nm = lambda t: f"l{l}_{t}_bwd"
    n = bsz * seq
    proj = sv["proj"]
    d, dff = w["wmi"].shape
    wf, wd = w["wuf"].shape[0], w["wud"].shape[0]
    bufs = dict(bufs)
    rb = 512
    per_chip = dff // 4 // rb
    du = matmul(dx2, w["wmo"], tb=True, relu2_of=sv["u"], out_dtype=MM, name=nm("mlp_out_dx"))
    bufs["wide"] = matmul(sv["act"], dx2, ta=True, tm=rb, tn=d, tk=2048, name=nm("mlp_out_dw"),
                          dest=(bufs["wide"], 1, lambda j: j,
                                lambda i, j: (i // per_chip, (depth * d + l * (dff // 4)) // rb + i % per_chip, j)))
    dh2 = matmul(du, w["wmi"], tb=True, name=nm("mlp_in_dx"))
    bufs["wide"] = matmul(sv["h2"], du, ta=True, tm=rb, tn=dff // 4, tk=2048, name=nm("mlp_in_dw"),
                          dest=(bufs["wide"], 4, lambda j: j, lambda i, j: (j, l * d // rb + i, 0)))
    dx1, g_mlp_norm = rmsnorm_bwd(sv["x1"], s["mlp_norm"], dh2, dx2, name=nm("mlp_norm"))

    dmerged = matmul(dx1, w["wo"], tb=True, name=nm("out_proj_dx"))
    bufs["wide"] = matmul(sv["merged"], dx1, ta=True, tm=d // 4, tn=d, tk=2048, name=nm("out_proj_dw"),
                          dest=(bufs["wide"], 1, lambda j: j, lambda i, j: (i, (depth * (d + dff // 4)) // (d // 4) + l, j)))
    dya, dyb, dyc, dga, dgb, dgc = merge_bwd(proj, sv["ys"], dmerged, name=nm("merge"))
    doa = matmul(dya, w["wuf"], tb=True, name=nm("up_fox_dx"))
    bufs["ups"] = matmul(sv["oa"], dya, ta=True, tm=wf, tn=d // 4, tk=2048, name=nm("up_fox_dw"),
                         dest=(bufs["ups"], 4, lambda j: j, lambda i, j: (j, l, 0)))
    dob = matmul(dyb, w["wus"], tb=True, name=nm("up_sb_dx"))
    bufs["ups"] = matmul(sv["ob"], dyb, ta=True, tm=wf, tn=d // 4, tk=2048, name=nm("up_sb_dw"),
                         dest=(bufs["ups"], 4, lambda j: j, lambda i, j: (j, depth + l, 0)))
    doc = matmul(dyc, w["wud"], tb=True, name=nm("up_dil_dx"))
    bufs["ups"] = matmul(sv["oc"], dyc, ta=True, tm=wd, tn=d // 4, tk=2048, name=nm("up_dil_dw"),
                         dest=(bufs["ups"], 4, lambda j: j, lambda i, j: (j, 2 * depth * wf // wd + l, 0)))

    outs = dil_combine_bwd(sv["ogs"], sv["lgs"], doc, name=nm("dil_combine"))
    dqs, dks, dvs = [], [], []
    for g, (window, dilation) in enumerate(DIL_PATTERNS):
        dq, dk, dv = softmax_attn_bwd(sv["qr"], sv["kr"], proj, sv["ogs"][g], outs[g], sv["lgs"][g], outs[3 + g], None,
                                      qo=2 * g, ko=2 * g, vo=DILV + 2 * g, pairs=2, bsz=bsz, seq=seq, window=window,
                                      dilation=dilation, tq=TQ, dq_dtype=F32, dk_dtype=F32, name=nm(f"dil_attn{g}"))
        dqs.append(dq)
        dks.append(dk)
        dvs.append(dv)
    d_dq, d_dk, g_qgd, g_kgd = dil_prep_bwd(proj, s["qgd"], s["kgd"], cos, sin, jnp.concatenate(dqs, axis=1),
                                            jnp.concatenate(dks, axis=1), name=nm("dil_prep"))

    s_dq, s_dk, s_dv = sb_attn_bwd(proj, dob, bsz=bsz, seq=seq, tq=TQ, name=nm("sb_attn"))

    dqn, dkn, f_dv, dfq, dfk = softmax_attn_bwd(sv["qn"], sv["kn"], proj, sv["oa"], doa, sv["la"], None, (sv["fb"], sv["fk"]),
                                                qo=0, ko=0, vo=FOXV, pairs=4, bsz=bsz, seq=seq, window=seq, dilation=1, tq=TQ,
                                                dq_dtype=F32, dk_dtype=F32, name=nm("fox_attn"))
    dfk8 = dfk.reshape(bsz, 4, 8, seq)[:, :, :2].reshape(bsz, 8, seq).transpose(0, 2, 1).reshape(n, 8)
    df = jnp.pad(dfq[:, ::HEAD] - dfk8, ((0, 0), (0, LANES - 8)))
    f_dq, f_dk, d_forget, g_qgf, g_kgf, g_bfor = fox_prep_bwd(proj, s["qgf"], s["kgf"], s["bfor"], dqn, dkn, df, bsz=bsz, seq=seq,
                                                              name=nm("fox_prep"))

    dproj = jnp.concatenate([f_dq, f_dk, f_dv, s_dq, s_dk, s_dv, d_dq, d_dk] + dvs + [dga, dgb, dgc, d_forget], axis=1)
    dh = matmul(dproj, w["win"], tb=True, tn=1024, tk=512, name=nm("proj_dx"))
    bufs["win"] = matmul(sv["h"], dproj, ta=True, tm=d, tn=WIN_TILE, tk=2048, name=nm("proj_dw"),
                         dest=(bufs["win"], 4 * WIN_TILES, lambda j: WIN_STRIDE * (j // WIN_TILES) + j % WIN_TILES,
                               lambda i, j: (j // WIN_TILES, l, j % WIN_TILES)))
    g_forget = matmul(sv["h"], d_forget, ta=True, tk=2048, name=nm("forget_dw"))[:, :O2 - O1]
    dx, g_attn_norm = rmsnorm_bwd(sv["x"], s["attn_norm"], dh, dx1, name=nm("attn_norm"))
    gs = dict(attn_norm=g_attn_norm[0], mlp_norm=g_mlp_norm[0], b_forget=g_bfor[0, :8],
              q_norm_fox=g_qgf.reshape(8, HEAD).sum(0), k_norm_fox=g_kgf.reshape(8, HEAD).sum(0),
              q_norm_dil=g_qgd.reshape(12, HEAD).sum(0), k_norm_dil=g_kgd.reshape(12, HEAD).sum(0), w_in_forget=g_forget)
    return dx, bufs, gs


def local_step(x, positions, target, weights, small):
    bsz, seq, d = x.shape
    n = bsz * seq
    depth = len(weights)
    inv = 1.0 / (ROPE_THETA ** (jnp.arange(HEAD // 2, dtype=F32) / (HEAD // 2)))
    cos, sin = rope_table(positions.reshape(n, 1), jnp.tile(inv, 4).reshape(1, LANES), name="rope_table")
    xs = x.reshape(n, d)
    saved = []
    for l in range(depth):
        xs, sv = _layer_fwd(xs, weights[l], _layer_small(small, l), cos, sin, bsz, seq, l)
        saved.append(sv)
    dy, sq = loss_grad(xs, target.reshape(n, d), name="loss")
    loss = (0.5 / d) * jnp.sum(sq)
    w0 = weights[0]
    bufs = grad_buffers(depth, d, w0["wmi"].shape[1], w0["wuf"].shape[0], w0["wud"].shape[0])
    gss = [None] * depth
    for l in reversed(range(depth)):
        dy, bufs, gss[l] = _layer_bwd(dy, weights[l], _layer_small(small, l), saved[l], cos, sin, bsz, seq, l, depth, bufs)
    return loss, dy.reshape(bsz, seq, d), bufs, gss


SMALL = ("attn_norm", "mlp_norm", "b_forget", "q_norm_fox", "k_norm_fox", "q_norm_dil", "k_norm_dil")
SMALL_ROWS = 8


def _pack_small(vals):
    flat = jnp.concatenate([vals[k].reshape(-1) for k in SMALL])
    return jnp.pad(flat, (0, SMALL_ROWS * 1024 - flat.shape[0])).reshape(SMALL_ROWS, 1024)


def _unpack_small(packed, like):
    flat, out, at = packed.reshape(-1), {}, 0
    for k in SMALL:
        size = like[k].size
        out[k] = flat[at:at + size].reshape(like[k].shape)
        at += size
    return out


def kernel(x, positions, attn_norm, w_in, b_forget, q_norm_fox, k_norm_fox, q_norm_dil, k_norm_dil, w_up_fox, w_up_sb, w_up_dil, w_out, mlp_norm, w_mlp_in, w_mlp_out, loss_target, m_attn_norm, m_w_in, m_b_forget, m_q_norm_fox, m_k_norm_fox, m_q_norm_dil, m_k_norm_dil, m_w_up_fox, m_w_up_sb, m_w_up_dil, m_w_out, m_mlp_norm, m_w_mlp_in, m_w_mlp_out, v_attn_norm, v_w_in, v_b_forget, v_q_norm_fox, v_k_norm_fox, v_q_norm_dil, v_k_norm_dil, v_w_up_fox, v_w_up_sb, v_w_up_dil, v_w_out, v_mlp_norm, v_w_mlp_in, v_w_mlp_out):
    names = ("attn_norm", "w_in", "b_forget", "q_norm_fox", "k_norm_fox", "q_norm_dil", "k_norm_dil", "w_up_fox", "w_up_sb",
             "w_up_dil", "w_out", "mlp_norm", "w_mlp_in", "w_mlp_out")
    wv = dict(zip(names, (attn_norm, w_in, b_forget, q_norm_fox, k_norm_fox, q_norm_dil, k_norm_dil, w_up_fox, w_up_sb, w_up_dil,
                          w_out, mlp_norm, w_mlp_in, w_mlp_out)))
    mv = dict(zip(names, (m_attn_norm, m_w_in, m_b_forget, m_q_norm_fox, m_k_norm_fox, m_q_norm_dil, m_k_norm_dil, m_w_up_fox,
                          m_w_up_sb, m_w_up_dil, m_w_out, m_mlp_norm, m_w_mlp_in, m_w_mlp_out)))
    vv = dict(zip(names, (v_attn_norm, v_w_in, v_b_forget, v_q_norm_fox, v_k_norm_fox, v_q_norm_dil, v_k_norm_dil, v_w_up_fox,
                          v_w_up_sb, v_w_up_dil, v_w_out, v_mlp_norm, v_w_mlp_in, v_w_mlp_out)))
    depth = w_in.shape[0]
    flat2 = lambda a: a.reshape(-1, a.shape[-1])

    ups = ("w_up_fox", "w_up_sb", "w_up_dil")
    wide = ("w_mlp_in", "w_mlp_out", "w_out")
    send = [flat2(w_in).astype(MM), jnp.concatenate([flat2(wv[k]) for k in ups]).astype(MM),
            jnp.concatenate([flat2(wv[k]) for k in wide]).astype(MM)]
    core = lax.axis_index("c").astype(jnp.int32).reshape(1)
    chip = (2 * lax.axis_index("x") + lax.axis_index("y")).astype(jnp.int32).reshape(1)
    got_in, got_up, got_wide = [lax.dynamic_update_index_in_dim(g, s, chip[0], 0)
                                for g, s in zip(gather_chips(send, name="gather_weights"), send)]

    def layer_pieces(a, keys, l):
        out, at = {}, 0
        for k in keys:
            rows = wv[k].shape[1]
            out[k] = [a[c, at + l * rows:at + (l + 1) * rows] for c in range(4)]
            at += depth * rows
        return out

    def layer_weights(l):
        p = layer_pieces(got_in, ("w_in",), l)["w_in"]
        pad = jnp.zeros((p[0].shape[0], DPROJ - DIN), p[0].dtype)
        win = jnp.concatenate([p[0][:, :O1], p[0][:, O2:], p[1], p[2], p[3], p[0][:, O1:O2], pad], axis=1)
        up = {k: jnp.concatenate(v, axis=1) for k, v in layer_pieces(got_up, ups, l).items()}
        wd = layer_pieces(got_wide, wide, l)
        return dict(win=win, wuf=up["w_up_fox"], wus=up["w_up_sb"], wud=up["w_up_dil"], wo=jnp.concatenate(wd["w_out"], axis=0),
                    wmi=jnp.concatenate(wd["w_mlp_in"], axis=1), wmo=jnp.concatenate(wd["w_mlp_out"], axis=0))

    weights = [layer_weights(l) for l in range(depth)]
    small = {k: wv[k] for k in SMALL}

    loss, grad_x, bufs, gss = local_step(x, positions, loss_target, weights, small)
    loss = lax.psum(loss, ("x", "y", "c"))

    g_small = {k: jnp.stack([gss[l][k] for l in range(depth)]) for k in SMALL}
    g_forget = jnp.stack([gss[l]["w_in_forget"] for l in range(depth)])
    summed = all_reduce_small(jnp.concatenate([_pack_small(g_small), g_forget.reshape(-1, 1024)]), name="reduce_small")
    g_small = _unpack_small(summed[:SMALL_ROWS], small)
    g_forget = summed[SMALL_ROWS:].reshape(g_forget.shape)

    parts = [bufs["win"], bufs["ups"], bufs["wide"]]
    place = jnp.concatenate([core, chip])
    sums = [pair_sum_scatter(p, place, name=f"reduce_pair_sum{t}") for t, p in enumerate(parts)]
    joined = [chip_sum_join(own, landed, chip, name=f"reduce_chip_sum{t}").reshape(-1, parts[t].shape[-1])
              for t, (own, landed) in enumerate(sums)]

    def own_w_in_columns(window):
        cols = w_in.shape[-1]
        first = jnp.concatenate([window[..., :O1], g_forget, window[..., O1:cols - (O2 - O1)]], axis=-1)
        shift = jnp.maximum((cols - WIN_STRIDE * WIN_TILE) * chip[0] - (O2 - O1), 0)
        rest = lax.dynamic_slice_in_dim(window, shift, cols, axis=2)
        return jnp.where(chip[0] == 0, first, rest)

    g_big = {"w_in": own_w_in_columns(joined[0].reshape(depth, -1, joined[0].shape[-1]))}
    for a, keys in ((joined[1], ups), (joined[2], wide)):
        at = 0
        for k in keys:
            rows = wv[k].shape[0] * wv[k].shape[1]
            g_big[k] = a[at:at + rows].reshape(wv[k].shape)
            at += rows

    grads = {**g_small, **g_big}
    delta, new_m, new_v = {}, {}, {}
    d_s, m_s, v_s = adamw(_pack_small(small), _pack_small(g_small), _pack_small({k: mv[k] for k in SMALL}),
                          _pack_small({k: vv[k] for k in SMALL}), name="adamw_small")
    delta.update(_unpack_small(d_s, small))
    new_m.update(_unpack_small(m_s, small))
    new_v.update(_unpack_small(v_s, small))
    for k in ("w_in",) + ups + wide:
        d_k, m_k, v_k = adamw(flat2(wv[k]), flat2(g_big[k]), flat2(mv[k]), flat2(vv[k]), name=f"adamw_{k}")
        delta[k], new_m[k], new_v[k] = d_k.reshape(wv[k].shape), m_k.reshape(wv[k].shape), v_k.reshape(wv[k].shape)

    return (loss, grad_x, *[grads[k] for k in names], *[delta[k] for k in names], *[new_m[k] for k in names], *[new_v[k] for k in names])
```

```python
import functools

import jax
import jax.numpy as jnp
from jax import lax
from jax.experimental import pallas as pl
from jax.experimental.pallas import tpu as pltpu

F32 = jnp.float32
BF16 = jnp.bfloat16
MM = jnp.bfloat16

HEAD = 64
LANES = 128
EPS = 1e-6
SCALE = 0.125
ROPE_THETA = 10000.0
DIL_PATTERNS = ((128, 1), (512, 4), (2048, 16))
ADAM_LR, ADAM_B1, ADAM_B2, ADAM_EPS, ADAM_WD, ADAM_STEP = 0.001, 0.9, 0.999, 1e-08, 0.01, 10

FOXQ, FOXK, FOXV = 0, 4, 8
SBQ, SBK, SBV = 12, 16, 20
DILQ, DILK, DILV = 24, 30, 36
GATE, FORGET, NBLK = 42, 66, 68
DPROJ = NBLK * LANES
O1, O2, O3, O4, DIN = 1536, 1544, 3080, 5384, 8456

VMEM_LIMIT = 56 * 1024 * 1024
MESH_ID = pl.DeviceIdType.MESH
ANY = pl.BlockSpec(memory_space=pl.ANY)


def _params(sem=None):
    return pltpu.CompilerParams(dimension_semantics=sem, vmem_limit_bytes=VMEM_LIMIT)


def _iota(shape, dim):
    return lax.broadcasted_iota(jnp.int32, shape, dim)


def _split2(x):
    hi = x.astype(BF16)
    lo = (x - hi.astype(F32)).astype(BF16)
    return hi, lo


def _split3(x):
    hi = x.astype(BF16)
    r = x - hi.astype(F32)
    mid = r.astype(BF16)
    lo = (r - mid.astype(F32)).astype(BF16)
    return hi, mid, lo


def _dot(a, b):
    return jnp.dot(a, b, preferred_element_type=F32)


def _dot_nt(a, b):
    return lax.dot_general(a, b, (((1,), (1,)), ((), ())), preferred_element_type=F32)


def _dot_tn(a, b):
    return lax.dot_general(a, b, (((0,), (0,)), ((), ())), preferred_element_type=F32)


def _xdot2(x, m):
    hi, lo = _split2(x)
    return _dot(hi, m) + _dot(lo, m)


def _xdot3(x, m):
    hi, mid, lo = _split3(x)
    return _dot(hi, m) + _dot(mid, m) + _dot(lo, m)


def _xdot3_left(m, x):
    hi, mid, lo = _split3(x)
    return _dot(m, hi) + _dot(m, mid) + _dot(m, lo)


def _head_mat(w):
    return ((_iota((w, w), 0) >> 6) == (_iota((w, w), 1) >> 6)).astype(BF16)


def _softplus_parts(z):
    e = jnp.exp(-jnp.abs(z))
    return e, jnp.maximum(z, 0.0) + jnp.log(1.0 + e)


def _fit(dim, want):
    t = min(want, dim)
    while dim % t:
        t -= LANES
        assert t > 0, (dim, want)
    return t


def matmul(a, b, *, ta=False, tb=False, out_dtype=F32, add=None, tm=2048, tn=512, tk=1024, dest=None, relu2=False,
           relu2_of=None, name):
    K, M = a.shape if ta else a.shape[::-1]
    K2, N = b.shape[::-1] if tb else b.shape
    assert K == K2, (a.shape, b.shape, ta, tb)
    tm, tn, tk = _fit(M, tm), _fit(N, tn), _fit(K, tk)
    nk = K // tk
    dn = (((0 if ta else 1,), (1 if tb else 0,)), ((), ()))
    if dest is None:
        tiles, source = N // tn, lambda j: j
    else:
        assert add is None and not tb
        buffer, tiles, source, place = dest

    extra = add if add is not None else relu2_of
    assert add is None or relu2_of is None

    def body(*refs):
        act_ref = None
        if dest is not None:
            a_ref, b_ref, _, o_ref, acc_ref = refs
        elif relu2:
            a_ref, b_ref, o_ref, act_ref, acc_ref = refs
        elif extra is None:
            a_ref, b_ref, o_ref, acc_ref = refs
        else:
            a_ref, b_ref, add_ref, o_ref, acc_ref = refs
        k = pl.program_id(2)
        part = lax.dot_general(a_ref[...].astype(MM), b_ref[...].astype(MM), dn, preferred_element_type=F32)

        @pl.when(k == 0)
        def _():
            acc_ref[...] = part

        @pl.when(k > 0)
        def _():
            acc_ref[...] += part

        @pl.when(k == nk - 1)
        def _():
            r = acc_ref[...]
            if add is not None:
                r = r + add_ref[...]
            if relu2_of is not None:
                r = r * (2.0 * jnp.maximum(add_ref[...], 0.0))
            o_ref[...] = r.astype(o_ref.dtype).reshape(o_ref.shape)
            if act_ref is not None:
                pos = jnp.maximum(r, 0.0)
                act_ref[...] = (pos * pos).astype(act_ref.dtype)

    a_spec = pl.BlockSpec((tk, tm), lambda i, j, k: (k, i)) if ta else pl.BlockSpec((tm, tk), lambda i, j, k: (i, k))
    b_spec = pl.BlockSpec((tn, tk), lambda i, j, k: (j, k)) if tb else pl.BlockSpec((tk, tn), lambda i, j, k: (k, source(j)))
    o_spec = pl.BlockSpec((tm, tn), lambda i, j, k: (i, j))
    ins, specs, aliases = [a, b], [a_spec, b_spec], {}
    out_shape = jax.ShapeDtypeStruct((M, N), out_dtype)
    if extra is not None:
        ins.append(extra)
        specs.append(o_spec)
    if relu2:
        o_spec, out_shape = [o_spec, o_spec], [out_shape, jax.ShapeDtypeStruct((M, N), MM)]
    if dest is not None:
        ins.append(buffer)
        specs.append(ANY)
        aliases = {2: 0}
        o_spec = pl.BlockSpec((1, tm, tn), lambda i, j, k: place(i, j))
        out_shape = jax.ShapeDtypeStruct(buffer.shape, buffer.dtype)
    return pl.pallas_call(
        body, name=name, grid=(M // tm, tiles, nk), in_specs=specs, out_specs=o_spec, out_shape=out_shape,
        scratch_shapes=[pltpu.VMEM((tm, tn), F32)], input_output_aliases=aliases,
        compiler_params=_params(("parallel", "parallel", "arbitrary")),
    )(*ins)


def _rows(n, want=512):
    t = min(want, n)
    assert n % t == 0, (n, t)
    return t


def rmsnorm_fwd(x, g, *, name):
    n, d = x.shape
    tr = _rows(n)

    def body(x_ref, g_ref, o_ref, t_ref):
        xv = x_ref[...]
        r = lax.rsqrt(jnp.mean(xv * xv, axis=1, keepdims=True) + EPS)
        y = xv * r * g_ref[...]
        o_ref[...] = y.astype(o_ref.dtype)
        t_ref[...] = y.T.astype(t_ref.dtype)

    row = pl.BlockSpec((tr, d), lambda i: (i, 0))
    vec = pl.BlockSpec((1, d), lambda i: (0, 0))
    return pl.pallas_call(
        body, name=name, grid=(n // tr,), in_specs=[row, vec], out_specs=[row, pl.BlockSpec((d, tr), lambda i: (0, i))],
        out_shape=[jax.ShapeDtypeStruct((n, d), MM), jax.ShapeDtypeStruct((d, n), MM)], compiler_params=_params(("parallel",)))(x, g)


def rmsnorm_bwd(x, g, dh, dres, *, name):
    n, d = x.shape
    tr = _rows(n)

    def body(x_ref, g_ref, dh_ref, dr_ref, dx_ref, dg_ref):
        @pl.when(pl.program_id(0) == 0)
        def _():
            dg_ref[...] = jnp.zeros_like(dg_ref)

        xv = x_ref[...]
        r = lax.rsqrt(jnp.mean(xv * xv, axis=1, keepdims=True) + EPS)
        y = xv * r
        dhv = dh_ref[...]
        dy = dhv * g_ref[...]
        dx_ref[...] = dr_ref[...] + r * (dy - y * jnp.mean(dy * y, axis=1, keepdims=True))
        dg_ref[...] += jnp.sum(dhv * y, axis=0, keepdims=True)

    row = pl.BlockSpec((tr, d), lambda i: (i, 0))
    vec = pl.BlockSpec((1, d), lambda i: (0, 0))
    return pl.pallas_call(
        body, name=name, grid=(n // tr,), in_specs=[row, vec, row, row], out_specs=[row, vec],
        out_shape=[jax.ShapeDtypeStruct((n, d), F32), jax.ShapeDtypeStruct((1, d), F32)],
        compiler_params=_params(("arbitrary",)))(x, g, dh, dres)


def loss_grad(y, tgt, *, name):
    n, d = y.shape
    tr = _rows(n)

    def body(y_ref, t_ref, dy_ref, acc_ref):
        @pl.when(pl.program_id(0) == 0)
        def _():
            acc_ref[...] = jnp.zeros_like(acc_ref)

        e = y_ref[...] - t_ref[...]
        dy_ref[...] = e * (1.0 / d)
        acc_ref[...] += jnp.sum(e * e, axis=0, keepdims=True)

    row = pl.BlockSpec((tr, d), lambda i: (i, 0))
    vec = pl.BlockSpec((1, d), lambda i: (0, 0))
    return pl.pallas_call(
        body, name=name, grid=(n // tr,), in_specs=[row, row], out_specs=[row, vec],
        out_shape=[jax.ShapeDtypeStruct((n, d), F32), jax.ShapeDtypeStruct((1, d), F32)],
        compiler_params=_params(("arbitrary",)))(y, tgt)


MERGE_W = 256


def _gate_specs(tr, d):
    per = d // MERGE_W
    base = GATE * LANES // MERGE_W
    return [pl.BlockSpec((tr, MERGE_W), functools.partial(lambda i, j, b: (i, base + per * b + j), b=b)) for b in range(3)]


def merge_fwd(proj, ys, *, name):
    n, d = ys[0].shape
    tr = _rows(n)

    def body(g0, g1, g2, y0, y1, y2, o_ref):
        acc = jax.nn.sigmoid(g0[...]) * y0[...]
        acc += jax.nn.sigmoid(g1[...]) * y1[...]
        acc += jax.nn.sigmoid(g2[...]) * y2[...]
        o_ref[...] = acc.astype(o_ref.dtype)

    blk = pl.BlockSpec((tr, MERGE_W), lambda i, j: (i, j))
    return pl.pallas_call(
        body, name=name, grid=(n // tr, d // MERGE_W), in_specs=_gate_specs(tr, d) + [blk] * 3, out_specs=blk,
        out_shape=jax.ShapeDtypeStruct((n, d), MM), compiler_params=_params(("parallel", "parallel")))(proj, proj, proj, *ys)


def merge_bwd(proj, ys, dm, *, name):
    n, d = dm.shape
    tr = _rows(n)

    def body(g0, g1, g2, y0, y1, y2, dm_ref, dy0, dy1, dy2, dg0, dg1, dg2):
        dmv = dm_ref[...]
        for g, y, dy, dg in ((g0, y0, dy0, dg0), (g1, y1, dy1, dg1), (g2, y2, dy2, dg2)):
            s = jax.nn.sigmoid(g[...])
            dy[...] = (dmv * s).astype(dy.dtype)
            dg[...] = (dmv * y[...] * s * (1.0 - s)).astype(dg.dtype)

    blk = pl.BlockSpec((tr, MERGE_W), lambda i, j: (i, j))
    out = jax.ShapeDtypeStruct((n, d), MM)
    return pl.pallas_call(
        body, name=name, grid=(n // tr, d // MERGE_W), in_specs=_gate_specs(tr, d) + [blk] * 4, out_specs=[blk] * 6,
        out_shape=[out] * 6, compiler_params=_params(("parallel", "parallel")))(proj, proj, proj, *ys, dm)


def adamw(w, g, m, v, *, name):
    r, c = w.shape
    tr = r
    while tr * c * 4 > (1 << 21) and tr % 16 == 0:
        tr //= 2
    c1 = 1.0 / (1.0 - ADAM_B1 ** ADAM_STEP)
    c2 = 1.0 / (1.0 - ADAM_B2 ** ADAM_STEP)

    def body(w_ref, g_ref, m_ref, v_ref, d_ref, mo_ref, vo_ref):
        gv = g_ref[...]
        m2 = ADAM_B1 * m_ref[...] + (1.0 - ADAM_B1) * gv
        v2 = ADAM_B2 * v_ref[...] + (1.0 - ADAM_B2) * (gv * gv)
        d_ref[...] = -ADAM_LR * ((m2 * c1) / (jnp.sqrt(v2 * c2) + ADAM_EPS) + ADAM_WD * w_ref[...])
        mo_ref[...] = m2
        vo_ref[...] = v2

    blk = pl.BlockSpec((tr, c), lambda i: (i, 0))
    out = jax.ShapeDtypeStruct((r, c), F32)
    return pl.pallas_call(body, name=name, grid=(r // tr,), in_specs=[blk] * 4, out_specs=[blk] * 3, out_shape=[out] * 3,
                          compiler_params=_params(("parallel",)))(w, g, m, v)


def rope_table(pos, inv, *, name):
    n = pos.shape[0]
    tr = _rows(n)

    def body(p_ref, i_ref, c_ref, s_ref):
        ang = p_ref[...].astype(F32) * i_ref[...]
        c_ref[...] = jnp.cos(ang)
        s_ref[...] = jnp.sin(ang)

    out = jax.ShapeDtypeStruct((n, LANES), F32)
    blk = pl.BlockSpec((tr, LANES), lambda i: (i, 0))
    return pl.pallas_call(
        body, name=name, grid=(n // tr,), in_specs=[pl.BlockSpec((tr, 1), lambda i: (i, 0)), pl.BlockSpec((1, LANES), lambda i: (0, 0))],
        out_specs=[blk, blk], out_shape=[out, out], compiler_params=_params(("parallel",)))(pos, inv)


def _rot_half(x):
    first = (_iota((1, LANES), 1) & 63) < 32
    return jnp.where(first, -pltpu.roll(x, LANES - 32, axis=1), pltpu.roll(x, 32, axis=1))


def _head_norm(xv, gm):
    r = lax.rsqrt(_xdot2(xv * xv, gm) * (1.0 / HEAD) + EPS)
    return r, xv * r


def _head_norm_bwd(xh, r, dxh, gm):
    return r * (dxh - xh * (_xdot2(dxh * xh, gm) * (1.0 / HEAD)))


def fox_prep_fwd(proj, qg, kg, bf, *, bsz, seq, name):
    n = bsz * seq
    tr = min(256, seq)
    nt = seq // tr
    w = 4 * LANES

    def body(q_ref, k_ref, f_ref, qg_ref, kg_ref, b_ref, qn_ref, kn_ref, fb_ref, f8_ref, carry):
        @pl.when(pl.program_id(1) == 0)
        def _():
            carry[...] = jnp.zeros_like(carry)

        gm = _head_mat(LANES)
        for src, gain, dst in ((q_ref, qg_ref, qn_ref), (k_ref, kg_ref, kn_ref)):
            for c in range(4):
                sl = slice(c * LANES, (c + 1) * LANES)
                _, xh = _head_norm(src[:, sl], gm)
                dst[:, sl] = (xh * gain[:, sl]).astype(dst.dtype)
        logf = jax.nn.log_sigmoid(f_ref[...] + b_ref[...])
        lower = (_iota((tr, tr), 1) <= _iota((tr, tr), 0)).astype(BF16)
        fcum = _xdot3_left(lower, logf) + carry[...]
        carry[...] = fcum[tr - 1:tr, :]
        f8_ref[...] = fcum
        spread = (_iota((LANES, w), 0) == (_iota((LANES, w), 1) >> 6)).astype(BF16)
        fb_ref[...] = _xdot3(fcum, spread)

    row = lambda width, blk: pl.BlockSpec((tr, width), lambda b, t: (b * nt + t, blk))
    vec = lambda width: pl.BlockSpec((1, width), lambda b, t: (0, 0))
    return pl.pallas_call(
        body, name=name, grid=(bsz, nt),
        in_specs=[row(w, FOXQ // 4), row(w, FOXK // 4), row(LANES, FORGET), vec(w), vec(w), vec(LANES)],
        out_specs=[row(w, 0), row(w, 0), row(w, 0), row(LANES, 0)],
        out_shape=[jax.ShapeDtypeStruct((n, w), MM), jax.ShapeDtypeStruct((n, w), MM),
                   jax.ShapeDtypeStruct((n, w), F32), jax.ShapeDtypeStruct((n, LANES), F32)],
        scratch_shapes=[pltpu.VMEM((1, LANES), F32)],
        compiler_params=_params(("parallel", "arbitrary")))(proj, proj, proj, qg, kg, bf)


def fox_prep_bwd(proj, qg, kg, bf, dqn, dkn, df, *, bsz, seq, name):
    n = bsz * seq
    tr = min(256, seq)
    nt = seq // tr
    w = 4 * LANES

    def body(q_ref, k_ref, f_ref, qg_ref, kg_ref, b_ref, dqn_ref, dkn_ref, df_ref,
             dq_ref, dk_ref, dl_ref, dqg_ref, dkg_ref, db_ref, carry):
        first = (pl.program_id(0) == 0) & (pl.program_id(1) == 0)

        @pl.when(first)
        def _():
            dqg_ref[...] = jnp.zeros_like(dqg_ref)
            dkg_ref[...] = jnp.zeros_like(dkg_ref)
            db_ref[...] = jnp.zeros_like(db_ref)

        @pl.when(pl.program_id(1) == 0)
        def _():
            carry[...] = jnp.zeros_like(carry)

        gm = _head_mat(LANES)
        for src, gain, dy_ref, dx_ref, dg_ref in ((q_ref, qg_ref, dqn_ref, dq_ref, dqg_ref), (k_ref, kg_ref, dkn_ref, dk_ref, dkg_ref)):
            for c in range(4):
                sl = slice(c * LANES, (c + 1) * LANES)
                r, xh = _head_norm(src[:, sl], gm)
                dy = dy_ref[:, sl]
                dg_ref[:, sl] += jnp.sum(dy * xh, axis=0, keepdims=True)
                dx_ref[:, sl] = _head_norm_bwd(xh, r, dy * gain[:, sl], gm).astype(dx_ref.dtype)
        upper = (_iota((tr, tr), 1) >= _iota((tr, tr), 0)).astype(BF16)
        dlogf = _xdot3_left(upper, df_ref[...]) + carry[...]
        carry[...] = dlogf[0:1, :]
        dlogit = dlogf * jax.nn.sigmoid(-(f_ref[...] + b_ref[...]))
        dl_ref[:, 0:LANES] = dlogit.astype(dl_ref.dtype)
        dl_ref[:, LANES:2 * LANES] = jnp.zeros((tr, LANES), dl_ref.dtype)
        db_ref[...] += jnp.sum(dlogit, axis=0, keepdims=True)

    row = lambda width, blk: pl.BlockSpec((tr, width), lambda b, t: (b * nt + nt - 1 - t, blk))
    vec = lambda width: pl.BlockSpec((1, width), lambda b, t: (0, 0))
    return pl.pallas_call(
        body, name=name, grid=(bsz, nt),
        in_specs=[row(w, FOXQ // 4), row(w, FOXK // 4), row(LANES, FORGET), vec(w), vec(w), vec(LANES),
                  row(w, 0), row(w, 0), row(LANES, 0)],
        out_specs=[row(w, 0), row(w, 0), row(2 * LANES, 0), vec(w), vec(w), vec(LANES)],
        out_shape=[jax.ShapeDtypeStruct((n, w), MM), jax.ShapeDtypeStruct((n, w), MM), jax.ShapeDtypeStruct((n, 2 * LANES), MM),
                   jax.ShapeDtypeStruct((1, w), F32), jax.ShapeDtypeStruct((1, w), F32), jax.ShapeDtypeStruct((1, LANES), F32)],
        scratch_shapes=[pltpu.VMEM((1, LANES), F32)],
        compiler_params=_params(("arbitrary", "arbitrary")))(proj, proj, proj, qg, kg, bf, dqn, dkn, df)


DIL_W = 6 * LANES


def dil_prep_fwd(proj, qg, kg, cos, sin, *, name):
    n = proj.shape[0]
    tr = _rows(n, 256)

    def body(q_ref, k_ref, qg_ref, kg_ref, c_ref, s_ref, qo_ref, ko_ref):
        gm = _head_mat(LANES)
        cv, sv = c_ref[...], s_ref[...]
        for src, gain, dst in ((q_ref, qg_ref, qo_ref), (k_ref, kg_ref, ko_ref)):
            for c in range(6):
                sl = slice(c * LANES, (c + 1) * LANES)
                _, xh = _head_norm(src[:, sl], gm)
                xn = xh * gain[:, sl]
                dst[:, sl] = (xn * cv + _rot_half(xn) * sv).astype(dst.dtype)

    row = lambda width, blk: pl.BlockSpec((tr, width), lambda i: (i, blk))
    vec = pl.BlockSpec((1, DIL_W), lambda i: (0, 0))
    out = jax.ShapeDtypeStruct((n, DIL_W), MM)
    return pl.pallas_call(
        body, name=name, grid=(n // tr,),
        in_specs=[row(DIL_W, DILQ // 6), row(DIL_W, DILK // 6), vec, vec, row(LANES, 0), row(LANES, 0)],
        out_specs=[row(DIL_W, 0), row(DIL_W, 0)], out_shape=[out, out],
        compiler_params=_params(("parallel",)))(proj, proj, qg, kg, cos, sin)


def dil_prep_bwd(proj, qg, kg, cos, sin, dqr, dkr, *, name):
    n = proj.shape[0]
    tr = _rows(n, 256)

    def body(q_ref, k_ref, qg_ref, kg_ref, c_ref, s_ref, dqr_ref, dkr_ref, dq_ref, dk_ref, dqg_ref, dkg_ref):
        @pl.when(pl.program_id(0) == 0)
        def _():
            dqg_ref[...] = jnp.zeros_like(dqg_ref)
            dkg_ref[...] = jnp.zeros_like(dkg_ref)

        gm = _head_mat(LANES)
        cv, sv = c_ref[...], s_ref[...]
        for src, gain, dy_ref, dx_ref, dg_ref in ((q_ref, qg_ref, dqr_ref, dq_ref, dqg_ref), (k_ref, kg_ref, dkr_ref, dk_ref, dkg_ref)):
            for c in range(6):
                sl = slice(c * LANES, (c + 1) * LANES)
                r, xh = _head_norm(src[:, sl], gm)
                dy = dy_ref[:, sl]
                dxn = dy * cv - _rot_half(dy * sv)
                dg_ref[:, sl] += jnp.sum(dxn * xh, axis=0, keepdims=True)
                dx_ref[:, sl] = _head_norm_bwd(xh, r, dxn * gain[:, sl], gm).astype(dx_ref.dtype)

    row = lambda width, blk: pl.BlockSpec((tr, width), lambda i: (i, blk))
    vec = pl.BlockSpec((1, DIL_W), lambda i: (0, 0))
    out = jax.ShapeDtypeStruct((n, DIL_W), MM)
    gout = jax.ShapeDtypeStruct((1, DIL_W), F32)
    return pl.pallas_call(
        body, name=name, grid=(n // tr,),
        in_specs=[row(DIL_W, DILQ // 6), row(DIL_W, DILK // 6), vec, vec, row(LANES, 0), row(LANES, 0), row(DIL_W, 0), row(DIL_W, 0)],
        out_specs=[row(DIL_W, 0), row(DIL_W, 0), vec, vec], out_shape=[out, out, gout, gout],
        compiler_params=_params(("arbitrary",)))(proj, proj, qg, kg, cos, sin, dqr, dkr)


def dil_combine_fwd(os_, lses, *, name):
    n, w = os_[0].shape
    tr = _rows(n)

    def body(o0, o1, o2, l0, l1, l2, out_ref):
        a, b, c = l0[...], l1[...], l2[...]
        m = jnp.maximum(jnp.maximum(a, b), c)
        ea, eb, ec = jnp.exp(a - m), jnp.exp(b - m), jnp.exp(c - m)
        out_ref[...] = ((ea * o0[...] + eb * o1[...] + ec * o2[...]) / (ea + eb + ec)).astype(out_ref.dtype)

    blk = pl.BlockSpec((tr, w), lambda i: (i, 0))
    return pl.pallas_call(body, name=name, grid=(n // tr,), in_specs=[blk] * 6, out_specs=blk,
                          out_shape=jax.ShapeDtypeStruct((n, w), MM), compiler_params=_params(("parallel",)))(*os_, *lses)


def dil_combine_bwd(os_, lses, dout, *, name):
    n, w = dout.shape
    tr = _rows(n)

    def body(o0, o1, o2, l0, l1, l2, d_ref, do0, do1, do2, dl0, dl1, dl2):
        a, b, c = l0[...], l1[...], l2[...]
        m = jnp.maximum(jnp.maximum(a, b), c)
        es = [jnp.exp(a - m), jnp.exp(b - m), jnp.exp(c - m)]
        inv = 1.0 / (es[0] + es[1] + es[2])
        ws = [e * inv for e in es]
        dv = d_ref[...]
        gm = _head_mat(w)
        dws = [_xdot2(dv * o[...], gm) for o in (o0, o1, o2)]
        mean = ws[0] * dws[0] + ws[1] * dws[1] + ws[2] * dws[2]
        for wg, dw, do, dl in zip(ws, dws, (do0, do1, do2), (dl0, dl1, dl2)):
            do[...] = wg * dv
            dl[...] = wg * (dw - mean)

    blk = pl.BlockSpec((tr, w), lambda i: (i, 0))
    out = jax.ShapeDtypeStruct((n, w), F32)
    return pl.pallas_call(body, name=name, grid=(n // tr,), in_specs=[blk] * 7, out_specs=[blk] * 6, out_shape=[out] * 6,
                          compiler_params=_params(("parallel",)))(*os_, *lses, dout)


def _key_plan(qi, tq, seq, window, run):
    if window + tq >= seq:
        for bi in range(seq // tq):
            lo = bi * tq
            segs = ([(0, lo, "bulk")] if lo else []) + [(lo, tq, "diag")]
            pl.when(qi == bi)(functools.partial(run, segs))
    else:
        ext = window + tq
        run([(pl.multiple_of(jnp.maximum((qi + 1) * tq - ext, 0), LANES), ext, "band")])


def _seg_mask(seg, qi, tq, window, dilation, strict=False):
    start, width, kind = seg
    d = _iota((tq, width), 0) - _iota((tq, width), 1)
    if kind == "bulk":
        d = d + width
    elif kind == "band":
        d = d + (qi * tq - start)
    ok = None
    if kind != "bulk":
        ok = (d > 0) if strict else (d >= 0)
    if window is not None:
        ok = (d <= window) if ok is None else ok & (d <= window)
    if dilation > 1:
        on_grid = (d & (dilation - 1)) == 0
        ok = on_grid if ok is None else ok & on_grid
    return ok


def _lane_first():
    return _iota((1, LANES), 1) < HEAD


def _attn_specs(bsz, seq, tq, qo, ko, vo):
    nq = seq // tq
    qspec = lambda off: pl.BlockSpec((tq, LANES), lambda b, j, i: (b * nq + i, off + j))
    kspec = lambda off: pl.BlockSpec((seq, LANES), lambda b, j, i: (b, off + j))
    return nq, qspec, kspec


def softmax_attn_fwd(q, k, v, bias, *, qo, ko, vo, pairs, bsz, seq, window, dilation, tq, name):
    n = bsz * seq
    nq, qspec, kspec = _attn_specs(bsz, seq, tq, qo, ko, vo)

    def body(*refs):
        if bias is None:
            q_ref, k_ref, v_ref, o_ref, l_ref = refs
        else:
            q_ref, k_ref, v_ref, fq_ref, fk_ref, o_ref, l_ref = refs
        qi = pl.program_id(2)

        def run(segs):
            qv = (q_ref[...] * SCALE).astype(MM)
            first = _lane_first()
            keys = [(k_ref[pl.ds(st, w), :].astype(MM), v_ref[pl.ds(st, w), :].astype(MM),
                     _seg_mask((st, w, kind), qi, tq, None if window >= seq else window, dilation), st, w)
                    for st, w, kind in segs]
            outs, lses = [], []
            for a in range(2):
                qa = jnp.where(first if a == 0 else ~first, qv, jnp.zeros_like(qv))
                scores = []
                for kv, _, ok, st, w in keys:
                    s = _dot_nt(qa, kv)
                    if bias is not None:
                        s = s + fq_ref[:, a * HEAD:a * HEAD + 1] - fk_ref[a:a + 1, pl.ds(st, w)]
                    scores.append(s if ok is None else jnp.where(ok, s, -jnp.inf))
                m = functools.reduce(jnp.maximum, [jnp.max(s, axis=1, keepdims=True) for s in scores])
                ps = [jnp.exp(s - m) for s in scores]
                den = sum(jnp.sum(p, axis=1, keepdims=True) for p in ps)
                acc = sum(_dot(p.astype(MM), vv) for p, (_, vv, _, _, _) in zip(ps, keys))
                outs.append(acc / den)
                lses.append(m + jnp.log(den))
            o_ref[...] = jnp.where(first, outs[0], outs[1]).astype(o_ref.dtype)
            l_ref[...] = jnp.where(first, lses[0], lses[1])

        _key_plan(qi, tq, seq, window, run)

    ins, specs = [q, k, v], [qspec(qo), kspec(ko), kspec(vo)]
    if bias is not None:
        ins += list(bias)
        specs += [qspec(0), pl.BlockSpec((8, seq), lambda b, j, i: (b * pairs + j, 0))]
    out = jax.ShapeDtypeStruct((n, LANES * pairs), F32)
    return pl.pallas_call(
        body, name=name, grid=(bsz, pairs, nq), in_specs=specs, out_specs=[qspec(0), qspec(0)], out_shape=[out, out],
        compiler_params=_params(("parallel", "parallel", "arbitrary")))(*ins)


def softmax_attn_bwd(q, k, v, o, do, lse, dlse, bias, *, qo, ko, vo, pairs, bsz, seq, window, dilation, tq, dq_dtype, dk_dtype, name):
    n = bsz * seq
    nq, qspec, kspec = _attn_specs(bsz, seq, tq, qo, ko, vo)
    has_bias, has_dlse = bias is not None, dlse is not None

    def body(*refs):
        refs = list(refs)
        q_ref, k_ref, v_ref, o_ref, do_ref, l_ref = refs[:6]
        del refs[:6]
        dl_ref = refs.pop(0) if has_dlse else None
        fq_ref, fk_ref = (refs.pop(0), refs.pop(0)) if has_bias else (None, None)
        dq_ref, dk_ref, dv_ref = refs[:3]
        del refs[:3]
        dfq_ref, dfk_ref = (refs.pop(0), refs.pop(0)) if has_bias else (None, None)
        dk_acc, dv_acc = refs
        qi = pl.program_id(2)

        @pl.when(qi == 0)
        def _():
            dk_acc[...] = jnp.zeros_like(dk_acc)
            dv_acc[...] = jnp.zeros_like(dv_acc)
            if has_bias:
                dfk_ref[...] = jnp.zeros_like(dfk_ref)

        def run(segs):
            qv = (q_ref[...] * SCALE).astype(MM)
            dov = do_ref[...]
            dob = dov.astype(MM)
            prod = dov * o_ref[...]
            first = _lane_first()
            keys = [(k_ref[pl.ds(st, w), :].astype(MM), v_ref[pl.ds(st, w), :].astype(MM),
                     _seg_mask((st, w, kind), qi, tq, None if window >= seq else window, dilation), st, w)
                    for st, w, kind in segs]
            dqs, dfqs = [], []
            dks, dvs = [[] for _ in keys], [[] for _ in keys]
            for a in range(2):
                mine = first if a == 0 else ~first
                col = slice(a * HEAD, a * HEAD + 1)
                delta = jnp.sum(jnp.where(mine, prod, 0.0), axis=1, keepdims=True)
                if has_dlse:
                    delta = delta - dl_ref[:, col]
                qa = jnp.where(mine, qv, jnp.zeros_like(qv))
                doa = jnp.where(mine, dob, jnp.zeros_like(dob))
                shift = l_ref[:, col]
                if has_bias:
                    shift = shift - fq_ref[:, col]
                dq, dfq = 0.0, 0.0
                for si, (kv, vv, ok, st, w) in enumerate(keys):
                    s = _dot_nt(qa, kv)
                    if has_bias:
                        s = s - fk_ref[a:a + 1, pl.ds(st, w)]
                    p = jnp.exp(s - shift)
                    if ok is not None:
                        p = jnp.where(ok, p, 0.0)
                    ds = p * (_dot_nt(doa, vv) - delta)
                    dsb = ds.astype(MM)
                    dvs[si].append(_dot_tn(p.astype(MM), dob))
                    dks[si].append(_dot_tn(dsb, qv))
                    dq = dq + _dot(dsb, kv)
                    if has_bias:
                        dfq = dfq + jnp.sum(ds, axis=1, keepdims=True)
                        dfk_ref[a:a + 1, pl.ds(st, w)] += jnp.sum(ds, axis=0, keepdims=True)
                dqs.append(dq * SCALE)
                dfqs.append(dfq)
            dq_ref[...] = jnp.where(first, dqs[0], dqs[1]).astype(dq_ref.dtype)
            for (_, _, _, st, w), dk, dv in zip(keys, dks, dvs):
                dk_acc[pl.ds(st, w), :] += jnp.where(first, dk[0], dk[1])
                dv_acc[pl.ds(st, w), :] += jnp.where(first, dv[0], dv[1])
            if has_bias:
                dfq_ref[...] = jnp.where(first, dfqs[0], dfqs[1])

        _key_plan(qi, tq, seq, window, run)

        @pl.when(qi == nq - 1)
        def _():
            dk_ref[...] = dk_acc[...].astype(dk_ref.dtype)
            dv_ref[...] = dv_acc[...].astype(dv_ref.dtype)

    wide = LANES * pairs
    ins = [q, k, v, o, do, lse]
    specs = [qspec(qo), kspec(ko), kspec(vo), qspec(0), qspec(0), qspec(0)]
    outs = [jax.ShapeDtypeStruct((n, wide), dq_dtype), jax.ShapeDtypeStruct((n, wide), dk_dtype), jax.ShapeDtypeStruct((n, wide), MM)]
    out_specs = [qspec(0), kspec(0), kspec(0)]
    if has_dlse:
        ins.append(dlse)
        specs.append(qspec(0))
    if has_bias:
        rows = pl.BlockSpec((8, seq), lambda b, j, i: (b * pairs + j, 0))
        ins += list(bias)
        specs += [qspec(0), rows]
        outs += [jax.ShapeDtypeStruct((n, wide), F32), jax.ShapeDtypeStruct((bsz * pairs * 8, seq), F32)]
        out_specs += [qspec(0), rows]
    return pl.pallas_call(
        body, name=name, grid=(bsz, pairs, nq), in_specs=specs, out_specs=out_specs, out_shape=outs,
        scratch_shapes=[pltpu.VMEM((seq, LANES), F32), pltpu.VMEM((seq, LANES), F32)],
        compiler_params=_params(("parallel", "parallel", "arbitrary")))(*ins)


def _running_sum(vals, mat, carry, lat_ref, start, reverse):
    nb = vals.shape[1] // LANES
    for cb in (reversed(range(nb)) if reverse else range(nb)):
        blk = vals[:, cb * LANES:(cb + 1) * LANES]
        lat_ref[:, start + cb * LANES:start + (cb + 1) * LANES] = _dot(blk.astype(BF16), mat) + carry
        carry = carry + jnp.sum(blk, axis=1, keepdims=True)
    return carry


def _sb_weights(qa, keys, tq, lat_ref):
    after = (_iota((LANES, LANES), 0) > _iota((LANES, LANES), 1)).astype(BF16)
    carry = jnp.zeros((tq, 1), F32)
    logs = []
    for kv, ok, st, w in reversed(keys):
        z = _dot_nt(qa, kv)
        _, sp = _softplus_parts(z)
        visible = sp if ok is None else jnp.where(ok, sp, 0.0)
        carry = _running_sum(visible, after, carry, lat_ref, st, True)
        logs.append(z - sp)
    out = []
    for (kv, ok, st, w), log_beta in zip(keys, reversed(logs)):
        att = jnp.exp(log_beta - lat_ref[:, st:st + w])
        out.append((log_beta, att if ok is None else jnp.where(ok, att, 0.0)))
    return out


def _sb_keys(k_ref, v_ref, segs, qi, tq):
    return [(k_ref[st:st + w, :].astype(MM), v_ref[st:st + w, :].astype(MM),
             _seg_mask((st, w, kind), qi, tq, None, 1, strict=True), st, w) for st, w, kind in segs]


def sb_attn_fwd(proj, *, bsz, seq, tq, name):
    n = bsz * seq
    pairs = 4
    nq, qspec, kspec = _attn_specs(bsz, seq, tq, SBQ, SBK, SBV)

    def body(q_ref, k_ref, v_ref, o_ref, lat_ref):
        qi = pl.program_id(2)

        def run(segs):
            qv = (q_ref[...] * SCALE).astype(MM)
            keys = _sb_keys(k_ref, v_ref, segs, qi, tq)
            first = _lane_first()
            outs = []
            for a in range(2):
                qa = jnp.where(first if a == 0 else ~first, qv, jnp.zeros_like(qv))
                weights = _sb_weights(qa, [(kv, ok, st, w) for kv, _, ok, st, w in keys], tq, lat_ref)
                outs.append(sum(_dot(att.astype(MM), vv) for (_, att), (_, vv, _, _, _) in zip(weights, keys)))
            o_ref[...] = jnp.where(first, outs[0], outs[1]).astype(o_ref.dtype)

        _key_plan(qi, tq, seq, seq, run)

    return pl.pallas_call(
        body, name=name, grid=(bsz, pairs, nq), in_specs=[qspec(SBQ), kspec(SBK), kspec(SBV)], out_specs=qspec(0),
        out_shape=jax.ShapeDtypeStruct((n, LANES * pairs), MM), scratch_shapes=[pltpu.VMEM((tq, seq), F32)],
        compiler_params=_params(("parallel", "parallel", "arbitrary")))(proj, proj, proj)


def sb_attn_bwd(proj, do, *, bsz, seq, tq, name):
    n = bsz * seq
    pairs = 4
    nq, qspec, kspec = _attn_specs(bsz, seq, tq, SBQ, SBK, SBV)

    def body(q_ref, k_ref, v_ref, do_ref, dq_ref, dk_ref, dv_ref, lat_ref, dk_acc, dv_acc):
        qi = pl.program_id(2)

        @pl.when(qi == 0)
        def _():
            dk_acc[...] = jnp.zeros_like(dk_acc)
            dv_acc[...] = jnp.zeros_like(dv_acc)

        def run(segs):
            qv = (q_ref[...] * SCALE).astype(MM)
            keys = _sb_keys(k_ref, v_ref, segs, qi, tq)
            dob = do_ref[...].astype(MM)
            first = _lane_first()
            before = (_iota((LANES, LANES), 0) < _iota((LANES, LANES), 1)).astype(BF16)
            dqs = []
            dks, dvs = [[] for _ in keys], [[] for _ in keys]
            for a in range(2):
                mine = first if a == 0 else ~first
                qa = jnp.where(mine, qv, jnp.zeros_like(qv))
                doa = jnp.where(mine, dob, jnp.zeros_like(dob))
                weights = _sb_weights(qa, [(kv, ok, st, w) for kv, _, ok, st, w in keys], tq, lat_ref)
                gs = [_dot_nt(doa, vv) * att for (_, att), (_, vv, _, _, _) in zip(weights, keys)]
                carry = jnp.zeros((tq, 1), F32)
                for g, (_, _, _, st, w) in zip(gs, keys):
                    carry = _running_sum(g, before, carry, lat_ref, st, False)
                dq = 0.0
                for si, ((log_beta, att), g, (kv, _, ok, st, w)) in enumerate(zip(weights, gs, keys)):
                    dz = g - jnp.exp(log_beta) * (g + lat_ref[:, st:st + w])
                    dz = (dz if ok is None else jnp.where(ok, dz, 0.0)).astype(MM)
                    dvs[si].append(_dot_tn(att.astype(MM), dob))
                    dks[si].append(_dot_tn(dz, qv))
                    dq = dq + _dot(dz, kv)
                dqs.append(dq * SCALE)
            dq_ref[...] = jnp.where(first, dqs[0], dqs[1]).astype(dq_ref.dtype)
            for (_, _, _, st, w), dk, dv in zip(keys, dks, dvs):
                dk_acc[st:st + w, :] += jnp.where(first, dk[0], dk[1])
                dv_acc[st:st + w, :] += jnp.where(first, dv[0], dv[1])

        _key_plan(qi, tq, seq, seq, run)

        @pl.when(qi == nq - 1)
        def _():
            dk_ref[...] = dk_acc[...].astype(dk_ref.dtype)
            dv_ref[...] = dv_acc[...].astype(dv_ref.dtype)

    out = jax.ShapeDtypeStruct((n, LANES * pairs), MM)
    return pl.pallas_call(
        body, name=name, grid=(bsz, pairs, nq), in_specs=[qspec(SBQ), kspec(SBK), kspec(SBV), qspec(0)],
        out_specs=[qspec(0), kspec(0), kspec(0)], out_shape=[out, out, out],
        scratch_shapes=[pltpu.VMEM((tq, seq), F32), pltpu.VMEM((seq, LANES), F32), pltpu.VMEM((seq, LANES), F32)],
        compiler_params=_params(("parallel", "parallel", "arbitrary")))(proj, proj, proj, do)


def _place():
    return lax.axis_index("x"), lax.axis_index("y"), lax.axis_index("c")


def _other_chips(x, y):
    return [(1 - x, y), (x, 1 - y), (1 - x, 1 - y)]


def _remote(src, dst, send_sems, recv_sems, k, to):
    return pltpu.make_async_remote_copy(src_ref=src, dst_ref=dst, send_sem=send_sems.at[k], recv_sem=recv_sems.at[k],
                                        device_id=to, device_id_type=MESH_ID)


def gather_chips(arrs, *, name):
    na = len(arrs)

    def body(*refs):
        ins, outs = refs[:na], refs[na:2 * na]
        send_sems, recv_sems = refs[2 * na:]
        x, y, c = _place()
        me, sibling = 2 * x + y, (x, y, 1 - c)
        chips = _other_chips(x, y)
        sends = []
        for t in range(na):
            rh = ins[t].shape[0] // 2
            half = lambda chip, h, t=t, rh=rh: outs[t].at[chip, pl.ds(h * rh, rh), :]
            for j, (px, py) in enumerate(chips):
                cp = _remote(ins[t].at[pl.ds(c * rh, rh), :], half(me, c), send_sems, recv_sems, 6 * t + j, (px, py, c))
                cp.start()
                sends.append(cp)
        for t in range(na):
            rh = ins[t].shape[0] // 2
            half = lambda chip, h, t=t, rh=rh: outs[t].at[chip, pl.ds(h * rh, rh), :]
            for j, (px, py) in enumerate(chips):
                landed = half(2 * px + py, c)
                _remote(landed, landed, send_sems, recv_sems, 6 * t + j, (px, py, c)).wait_recv()
                fw = _remote(landed, landed, send_sems, recv_sems, 6 * t + 3 + j, sibling)
                fw.start()
                sends.append(fw)
        for t in range(na):
            rh = ins[t].shape[0] // 2
            half = lambda chip, h, t=t, rh=rh: outs[t].at[chip, pl.ds(h * rh, rh), :]
            for j, (px, py) in enumerate(chips):
                passed = half(2 * px + py, 1 - c)
                _remote(passed, passed, send_sems, recv_sems, 6 * t + 3 + j, sibling).wait_recv()
        for cp in sends:
            cp.wait_send()

    for a in arrs:
        assert a.ndim == 2 and a.shape[0] % 32 == 0, a.shape
    return pl.pallas_call(
        body, name=name, in_specs=[ANY] * na, out_specs=[ANY] * na,
        out_shape=[jax.ShapeDtypeStruct((4,) + a.shape, a.dtype) for a in arrs],
        scratch_shapes=[pltpu.SemaphoreType.DMA((6 * na,)), pltpu.SemaphoreType.DMA((6 * na,))],
    )(*arrs)


CHUNK_BYTES = 2 << 20


def _chunk_rows(rows, cols, limit):
    best = 16
    for t in range(16, rows + 1, 16):
        if rows % t == 0 and t * cols * 4 <= limit:
            best = t
    assert rows % best == 0, (rows, cols)
    return best


def pair_sum_scatter(a, place, *, name):
    _, rows, cols = a.shape
    rh = rows // 2
    tr = _chunk_rows(rh, cols, CHUNK_BYTES)
    nch = rh // tr
    steps = 4 * nch

    def body(place_ref, keep_ref, send_ref, own_ref, landed_ref, landing, out16, res, pair_send, pair_recv, credit,
             chip_send, chip_recv, local_sem):
        i, j = pl.program_id(0), pl.program_id(1)
        step = i * 4 + j
        slot = lax.rem(step, 2)
        x, y, c = _place()
        sibling = (x, y, 1 - c)
        me = 2 * x + y
        rows_i = pl.ds(pl.multiple_of(i * tr, tr), tr)

        def to_chip(p, s):
            return pltpu.make_async_remote_copy(
                src_ref=out16.at[s], dst_ref=landed_ref.at[me, rows_i, :], send_sem=chip_send.at[s], recv_sem=chip_recv.at[p - 1],
                device_id=(x ^ (p >> 1), y ^ (p & 1), c), device_id_type=MESH_ID)

        @pl.when(step >= 2)
        def _():
            pl.semaphore_wait(credit, 1)

        cp = _remote(send_ref.at[0], landing.at[slot], pair_send, pair_recv, slot, sibling)
        cp.start()
        cp.wait_recv()
        total = keep_ref[0] + landing[slot]

        for p, s, before in ((1, 0, i > 0), (2, 1, i > 0), (3, 0, None)):
            @pl.when(j == p - 1)
            def _(p=p, s=s, before=before):
                if before is None:
                    to_chip(1, s).wait_send()
                else:
                    pl.when(before)(lambda: to_chip(1, s).wait_send())
                out16[s] = total.astype(BF16)
                to_chip(p, s).start()

        @pl.when(j == 3)
        def _():
            res[...] = total
            here = pltpu.make_async_copy(res, own_ref.at[rows_i, :], local_sem)
            here.start()
            here.wait()

        cp.wait_send()

        @pl.when(step + 2 < steps)
        def _():
            pl.semaphore_signal(credit, 1, device_id=sibling, device_id_type=MESH_ID)

        @pl.when(step == steps - 1)
        def _():
            to_chip(1, 1).wait_send()
            to_chip(1, 0).wait_send()
            for p in (1, 2, 3):
                slab = landed_ref.at[me ^ p]
                pltpu.make_async_remote_copy(src_ref=slab, dst_ref=slab, send_sem=chip_send.at[0], recv_sem=chip_recv.at[p - 1],
                                             device_id=(x ^ (p >> 1), y ^ (p & 1), c), device_id_type=MESH_ID).wait_recv()

    blk = (1, tr, cols)
    slab_of = lambda j, place: place[1] ^ ((j + 1) & 3)
    grid_spec = pltpu.PrefetchScalarGridSpec(
        num_scalar_prefetch=1, grid=(nch, 4),
        in_specs=[pl.BlockSpec(blk, lambda i, j, place: (slab_of(j, place), place[0] * nch + i, 0)),
                  pl.BlockSpec(blk, lambda i, j, place: (slab_of(j, place), (1 - place[0]) * nch + i, 0))],
        out_specs=[ANY, ANY],
        scratch_shapes=[pltpu.VMEM((2, tr, cols), F32), pltpu.VMEM((2, tr, cols), BF16), pltpu.VMEM((tr, cols), F32),
                        pltpu.SemaphoreType.DMA((2,)), pltpu.SemaphoreType.DMA((2,)), pltpu.SemaphoreType.REGULAR,
                        pltpu.SemaphoreType.DMA((2,)), pltpu.SemaphoreType.DMA((3,)), pltpu.SemaphoreType.DMA])
    return pl.pallas_call(
        body, name=name, grid_spec=grid_spec,
        out_shape=[jax.ShapeDtypeStruct((rh, cols), F32), jax.ShapeDtypeStruct((4, rh, cols), BF16)],
        compiler_params=_params(("arbitrary", "arbitrary")))(place, a, a)


def chip_sum_join(own, landed, chip, *, name):
    rh, cols = own.shape
    tr = _chunk_rows(rh, cols, CHUNK_BYTES)
    nch = rh // tr

    def body(chip_ref, own_ref, l1_ref, l2_ref, l3_ref, out_ref, res, local_sem, send_sem, recv_sem):
        i = pl.program_id(0)
        x, y, c = _place()
        sibling = (x, y, 1 - c)
        res[...] = ((own_ref[...] + l1_ref[0].astype(F32)) + l2_ref[0].astype(F32)) + l3_ref[0].astype(F32)
        rows = pl.ds(pl.multiple_of(i * tr, tr), tr)
        here = pltpu.make_async_copy(res, out_ref.at[c, rows, :], local_sem)
        here.start()
        there = pltpu.make_async_remote_copy(src_ref=res, dst_ref=out_ref.at[c, rows, :], send_sem=send_sem, recv_sem=recv_sem,
                                             device_id=sibling, device_id_type=MESH_ID)
        there.start()
        here.wait()
        there.wait_send()

        @pl.when(i == nch - 1)
        def _():
            half = out_ref.at[1 - c]
            pltpu.make_async_remote_copy(src_ref=half, dst_ref=half, send_sem=send_sem, recv_sem=recv_sem,
                                         device_id=sibling, device_id_type=MESH_ID).wait_recv()

    blk = (1, tr, cols)
    slab = lambda p: pl.BlockSpec(blk, lambda i, chip: (chip[0] ^ p, i, 0))
    grid_spec = pltpu.PrefetchScalarGridSpec(
        num_scalar_prefetch=1, grid=(nch,), out_specs=ANY,
        in_specs=[pl.BlockSpec((tr, cols), lambda i, chip: (i, 0)), slab(1), slab(2), slab(3)],
        scratch_shapes=[pltpu.VMEM((tr, cols), F32), pltpu.SemaphoreType.DMA, pltpu.SemaphoreType.DMA, pltpu.SemaphoreType.DMA])
    return pl.pallas_call(
        body, name=name, grid_spec=grid_spec, out_shape=jax.ShapeDtypeStruct((2, rh, cols), F32),
        compiler_params=_params(("arbitrary",)))(chip, own, landed, landed, landed)


def all_reduce_small(a, *, name):
    def body(a_ref, o_ref, buf, send_sems, recv_sems):
        x, y, c = _place()
        me = 4 * x + 2 * y + c
        buf[me] = a_ref[...]
        sent = []
        for p in range(1, 8):
            px, py, pc = (p >> 2) & 1, (p >> 1) & 1, p & 1
            cp = _remote(a_ref, buf.at[me], send_sems, recv_sems, p - 1, (x ^ px, y ^ py, c ^ pc))
            cp.start()
            sent.append(cp)
        for p in range(1, 8):
            px, py, pc = (p >> 2) & 1, (p >> 1) & 1, p & 1
            src = 4 * (x ^ px) + 2 * (y ^ py) + (c ^ pc)
            _remote(a_ref, buf.at[src], send_sems, recv_sems, p - 1, (x ^ px, y ^ py, c ^ pc)).wait_recv()
        for cp in sent:
            cp.wait_send()
        acc = buf[0]
        for d in range(1, 8):
            acc = acc + buf[d]
        o_ref[...] = acc

    vm = pl.BlockSpec(memory_space=pltpu.VMEM)
    return pl.pallas_call(
        body, name=name, in_specs=[vm], out_specs=vm, out_shape=jax.ShapeDtypeStruct(a.shape, a.dtype),
        scratch_shapes=[pltpu.VMEM((8,) + a.shape, a.dtype), pltpu.SemaphoreType.DMA((7,)), pltpu.SemaphoreType.DMA((7,))],
    )(a)


TQ = 256


def _layer_small(sm, l):
    row = lambda v: v.reshape(1, -1)
    return dict(
        attn_norm=row(sm["attn_norm"][l]), mlp_norm=row(sm["mlp_norm"][l]),
        qgf=row(jnp.tile(sm["q_norm_fox"][l], 8)), kgf=row(jnp.tile(sm["k_norm_fox"][l], 8)),
        qgd=row(jnp.tile(sm["q_norm_dil"][l], 12)), kgd=row(jnp.tile(sm["k_norm_dil"][l], 12)),
        bfor=row(jnp.pad(sm["b_forget"][l], (0, LANES - 8))))


def _key_rows(f8, bsz, seq):
    f = f8.reshape(bsz, seq, LANES)[:, :, :8].transpose(0, 2, 1).reshape(bsz, 4, 2, seq)
    return jnp.pad(f, ((0, 0), (0, 0), (0, 6), (0, 0))).reshape(bsz * 32, seq)


def _layer_fwd(x, w, s, cos, sin, bsz, seq, l):
    nm = lambda t: f"l{l}_{t}"
    h, h_t = rmsnorm_fwd(x, s["attn_norm"], name=nm("attn_norm"))
    proj = matmul(h, w["win"], name=nm("proj"))
    qn, kn, fb, f8 = fox_prep_fwd(proj, s["qgf"], s["kgf"], s["bfor"], bsz=bsz, seq=seq, name=nm("fox_prep"))
    fk = _key_rows(f8, bsz, seq)
    oa, la = softmax_attn_fwd(qn, kn, proj, (fb, fk), qo=0, ko=0, vo=FOXV, pairs=4, bsz=bsz, seq=seq, window=seq, dilation=1,
                              tq=TQ, name=nm("fox_attn"))
    ob = sb_attn_fwd(proj, bsz=bsz, seq=seq, tq=TQ, name=nm("sb_attn"))
    qr, kr = dil_prep_fwd(proj, s["qgd"], s["kgd"], cos, sin, name=nm("dil_prep"))
    ogs, lgs = [], []
    for g, (window, dilation) in enumerate(DIL_PATTERNS):
        og, lg = softmax_attn_fwd(qr, kr, proj, None, qo=2 * g, ko=2 * g, vo=DILV + 2 * g, pairs=2, bsz=bsz, seq=seq,
                                  window=window, dilation=dilation, tq=TQ, name=nm(f"dil_attn{g}"))
        ogs.append(og)
        lgs.append(lg)
    oc = dil_combine_fwd(ogs, lgs, name=nm("dil_combine"))
    ys = [matmul(oa, w["wuf"], name=nm("up_fox")), matmul(ob, w["wus"], name=nm("up_sb")), matmul(oc, w["wud"], name=nm("up_dil"))]
    merged = merge_fwd(proj, ys, name=nm("merge"))
    x1 = matmul(merged, w["wo"], add=x, name=nm("out_proj"))
    h2, h2_t = rmsnorm_fwd(x1, s["mlp_norm"], name=nm("mlp_norm"))
    u, act = matmul(h2, w["wmi"], relu2=True, name=nm("mlp_in"))
    x2 = matmul(act, w["wmo"], add=x1, tk=2048, name=nm("mlp_out"))
    saved = dict(x=x, h_t=h_t, h2_t=h2_t, proj=proj, qn=qn, kn=kn, fb=fb, fk=fk, oa=oa, la=la, ob=ob, qr=qr, kr=kr, ogs=ogs, lgs=lgs, oc=oc,
                 ys=ys, merged=merged, x1=x1, u=u, act=act)
    return x2, saved


WIN_TILE = 256
WIN_STRIDE, WIN_TILES = 8, 9


def grad_buffers(depth, d, dff, wf, wd):
    assert dff // 4 == d
    return dict(win=lax.empty((4, depth * d, WIN_TILES * WIN_TILE), F32), ups=lax.empty((4, depth * (2 * wf + wd), d // 4), F32),
                wide=lax.empty((4, depth * (d + dff // 4 + d // 4), d), F32))


def _layer_bwd(dx2, w, s, sv, cos, sin, bsz, seq, l, depth, bufs):
    nm = lambda t: f"l{l}_{t}_bwd"
    n = bsz * seq
    proj = sv["proj"]
    d, dff = w["wmi"].shape
    wf, wd = w["wuf"].shape[0], w["wud"].shape[0]
    bufs = dict(bufs)
    rb = 512
    per_chip = dff // 4 // rb
    du = matmul(dx2, w["wmo"], tb=True, relu2_of=sv["u"], out_dtype=MM, name=nm("mlp_out_dx"))
    bufs["wide"] = matmul(sv["act"], dx2, ta=True, tm=rb, tn=d, tk=2048, name=nm("mlp_out_dw"),
                          dest=(bufs["wide"], 1, lambda j: j,
                                lambda i, j: (i // per_chip, (depth * d + l * (dff // 4)) // rb + i % per_chip, j)))
    dh2 = matmul(du, w["wmi"], tb=True, tk=2048, name=nm("mlp_in_dx"))
    bufs["wide"] = matmul(sv["h2_t"], du, tm=rb, tn=dff // 4, tk=2048, name=nm("mlp_in_dw"),
                          dest=(bufs["wide"], 4, lambda j: j, lambda i, j: (j, l * d // rb + i, 0)))
    dx1, g_mlp_norm = rmsnorm_bwd(sv["x1"], s["mlp_norm"], dh2, dx2, name=nm("mlp_norm"))

    dmerged = matmul(dx1, w["wo"], tb=True, name=nm("out_proj_dx"))
    bufs["wide"] = matmul(sv["merged"], dx1, ta=True, tm=d // 4, tn=d, tk=2048, name=nm("out_proj_dw"),
                          dest=(bufs["wide"], 1, lambda j: j, lambda i, j: (i, (depth * (d + dff // 4)) // (d // 4) + l, j)))
    dya, dyb, dyc, dga, dgb, dgc = merge_bwd(proj, sv["ys"], dmerged, name=nm("merge"))
    doa = matmul(dya, w["wuf"], tb=True, name=nm("up_fox_dx"))
    bufs["ups"] = matmul(sv["oa"], dya, ta=True, tm=wf, tn=d // 4, tk=2048, name=nm("up_fox_dw"),
                         dest=(bufs["ups"], 4, lambda j: j, lambda i, j: (j, l, 0)))
    dob = matmul(dyb, w["wus"], tb=True, name=nm("up_sb_dx"))
    bufs["ups"] = matmul(sv["ob"], dyb, ta=True, tm=wf, tn=d // 4, tk=2048, name=nm("up_sb_dw"),
                         dest=(bufs["ups"], 4, lambda j: j, lambda i, j: (j, depth + l, 0)))
    doc = matmul(dyc, w["wud"], tb=True, name=nm("up_dil_dx"))
    bufs["ups"] = matmul(sv["oc"], dyc, ta=True, tm=wd, tn=d // 4, tk=2048, name=nm("up_dil_dw"),
                         dest=(bufs["ups"], 4, lambda j: j, lambda i, j: (j, 2 * depth * wf // wd + l, 0)))

    outs = dil_combine_bwd(sv["ogs"], sv["lgs"], doc, name=nm("dil_combine"))
    dqs, dks, dvs = [], [], []
    for g, (window, dilation) in enumerate(DIL_PATTERNS):
        dq, dk, dv = softmax_attn_bwd(sv["qr"], sv["kr"], proj, sv["ogs"][g], outs[g], sv["lgs"][g], outs[3 + g], None,
                                      qo=2 * g, ko=2 * g, vo=DILV + 2 * g, pairs=2, bsz=bsz, seq=seq, window=window,
                                      dilation=dilation, tq=TQ, dq_dtype=F32, dk_dtype=F32, name=nm(f"dil_attn{g}"))
        dqs.append(dq)
        dks.append(dk)
        dvs.append(dv)
    d_dq, d_dk, g_qgd, g_kgd = dil_prep_bwd(proj, s["qgd"], s["kgd"], cos, sin, jnp.concatenate(dqs, axis=1),
                                            jnp.concatenate(dks, axis=1), name=nm("dil_prep"))

    s_dq, s_dk, s_dv = sb_attn_bwd(proj, dob, bsz=bsz, seq=seq, tq=TQ, name=nm("sb_attn"))

    dqn, dkn, f_dv, dfq, dfk = softmax_attn_bwd(sv["qn"], sv["kn"], proj, sv["oa"], doa, sv["la"], None, (sv["fb"], sv["fk"]),
                                                qo=0, ko=0, vo=FOXV, pairs=4, bsz=bsz, seq=seq, window=seq, dilation=1, tq=TQ,
                                                dq_dtype=F32, dk_dtype=F32, name=nm("fox_attn"))
    dfk8 = dfk.reshape(bsz, 4, 8, seq)[:, :, :2].reshape(bsz, 8, seq).transpose(0, 2, 1).reshape(n, 8)
    df = jnp.pad(dfq[:, ::HEAD] - dfk8, ((0, 0), (0, LANES - 8)))
    f_dq, f_dk, d_forget, g_qgf, g_kgf, g_bfor = fox_prep_bwd(proj, s["qgf"], s["kgf"], s["bfor"], dqn, dkn, df, bsz=bsz, seq=seq,
                                                              name=nm("fox_prep"))

    dproj = jnp.concatenate([f_dq, f_dk, f_dv, s_dq, s_dk, s_dv, d_dq, d_dk] + dvs + [dga, dgb, dgc, d_forget], axis=1)
    dh = matmul(dproj, w["win"], tb=True, tm=1024, tn=1024, tk=DPROJ // 4, name=nm("proj_dx"))
    bufs["win"] = matmul(sv["h_t"], dproj, tm=d, tn=WIN_TILE, tk=2048, name=nm("proj_dw"),
                         dest=(bufs["win"], 4 * WIN_TILES, lambda j: WIN_STRIDE * (j // WIN_TILES) + j % WIN_TILES,
                               lambda i, j: (j // WIN_TILES, l, j % WIN_TILES)))
    g_forget = matmul(sv["h_t"], d_forget, tk=2048, name=nm("forget_dw"))[:, :O2 - O1]
    dx, g_attn_norm = rmsnorm_bwd(sv["x"], s["attn_norm"], dh, dx1, name=nm("attn_norm"))
    gs = dict(attn_norm=g_attn_norm[0], mlp_norm=g_mlp_norm[0], b_forget=g_bfor[0, :8],
              q_norm_fox=g_qgf.reshape(8, HEAD).sum(0), k_norm_fox=g_kgf.reshape(8, HEAD).sum(0),
              q_norm_dil=g_qgd.reshape(12, HEAD).sum(0), k_norm_dil=g_kgd.reshape(12, HEAD).sum(0), w_in_forget=g_forget)
    return dx, bufs, gs


def local_step(x, positions, target, weights, small):
    bsz, seq, d = x.shape
    n = bsz * seq
    depth = len(weights)
    inv = 1.0 / (ROPE_THETA ** (jnp.arange(HEAD // 2, dtype=F32) / (HEAD // 2)))
    cos, sin = rope_table(positions.reshape(n, 1), jnp.tile(inv, 4).reshape(1, LANES), name="rope_table")
    xs = x.reshape(n, d)
    saved = []
    for l in range(depth):
        xs, sv = _layer_fwd(xs, weights[l], _layer_small(small, l), cos, sin, bsz, seq, l)
        saved.append(sv)
    dy, sq = loss_grad(xs, target.reshape(n, d), name="loss")
    loss = (0.5 / d) * jnp.sum(sq)
    w0 = weights[0]
    bufs = grad_buffers(depth, d, w0["wmi"].shape[1], w0["wuf"].shape[0], w0["wud"].shape[0])
    gss = [None] * depth
    for l in reversed(range(depth)):
        dy, bufs, gss[l] = _layer_bwd(dy, weights[l], _layer_small(small, l), saved[l], cos, sin, bsz, seq, l, depth, bufs)
    return loss, dy.reshape(bsz, seq, d), bufs, gss


SMALL = ("attn_norm", "mlp_norm", "b_forget", "q_norm_fox", "k_norm_fox", "q_norm_dil", "k_norm_dil")
SMALL_ROWS = 8


def _pack_small(vals):
    flat = jnp.concatenate([vals[k].reshape(-1) for k in SMALL])
    return jnp.pad(flat, (0, SMALL_ROWS * 1024 - flat.shape[0])).reshape(SMALL_ROWS, 1024)


def _unpack_small(packed, like):
    flat, out, at = packed.reshape(-1), {}, 0
    for k in SMALL:
        size = like[k].size
        out[k] = flat[at:at + size].reshape(like[k].shape)
        at += size
    return out


def kernel(x, positions, attn_norm, w_in, b_forget, q_norm_fox, k_norm_fox, q_norm_dil, k_norm_dil, w_up_fox, w_up_sb, w_up_dil, w_out, mlp_norm, w_mlp_in, w_mlp_out, loss_target, m_attn_norm, m_w_in, m_b_forget, m_q_norm_fox, m_k_norm_fox, m_q_norm_dil, m_k_norm_dil, m_w_up_fox, m_w_up_sb, m_w_up_dil, m_w_out, m_mlp_norm, m_w_mlp_in, m_w_mlp_out, v_attn_norm, v_w_in, v_b_forget, v_q_norm_fox, v_k_norm_fox, v_q_norm_dil, v_k_norm_dil, v_w_up_fox, v_w_up_sb, v_w_up_dil, v_w_out, v_mlp_norm, v_w_mlp_in, v_w_mlp_out):
    names = ("attn_norm", "w_in", "b_forget", "q_norm_fox", "k_norm_fox", "q_norm_dil", "k_norm_dil", "w_up_fox", "w_up_sb",
             "w_up_dil", "w_out", "mlp_norm", "w_mlp_in", "w_mlp_out")
    wv = dict(zip(names, (attn_norm, w_in, b_forget, q_norm_fox, k_norm_fox, q_norm_dil, k_norm_dil, w_up_fox, w_up_sb, w_up_dil,
                          w_out, mlp_norm, w_mlp_in, w_mlp_out)))
    mv = dict(zip(names, (m_attn_norm, m_w_in, m_b_forget, m_q_norm_fox, m_k_norm_fox, m_q_norm_dil, m_k_norm_dil, m_w_up_fox,
                          m_w_up_sb, m_w_up_dil, m_w_out, m_mlp_norm, m_w_mlp_in, m_w_mlp_out)))
    vv = dict(zip(names, (v_attn_norm, v_w_in, v_b_forget, v_q_norm_fox, v_k_norm_fox, v_q_norm_dil, v_k_norm_dil, v_w_up_fox,
                          v_w_up_sb, v_w_up_dil, v_w_out, v_mlp_norm, v_w_mlp_in, v_w_mlp_out)))
    depth = w_in.shape[0]
    flat2 = lambda a: a.reshape(-1, a.shape[-1])

    ups = ("w_up_fox", "w_up_sb", "w_up_dil")
    wide = ("w_mlp_in", "w_mlp_out", "w_out")
    send = [flat2(w_in).astype(MM), jnp.concatenate([flat2(wv[k]) for k in ups]).astype(MM),
            jnp.concatenate([flat2(wv[k]) for k in wide]).astype(MM)]
    core = lax.axis_index("c").astype(jnp.int32).reshape(1)
    chip = (2 * lax.axis_index("x") + lax.axis_index("y")).astype(jnp.int32).reshape(1)
    got_in, got_up, got_wide = [lax.dynamic_update_index_in_dim(g, s, chip[0], 0)
                                for g, s in zip(gather_chips(send, name="gather_weights"), send)]

    def layer_pieces(a, keys, l):
        out, at = {}, 0
        for k in keys:
            rows = wv[k].shape[1]
            out[k] = [a[c, at + l * rows:at + (l + 1) * rows] for c in range(4)]
            at += depth * rows
        return out

    def layer_weights(l):
        p = layer_pieces(got_in, ("w_in",), l)["w_in"]
        pad = jnp.zeros((p[0].shape[0], DPROJ - DIN), p[0].dtype)
        win = jnp.concatenate([p[0][:, :O1], p[0][:, O2:], p[1], p[2], p[3], p[0][:, O1:O2], pad], axis=1)
        up = {k: jnp.concatenate(v, axis=1) for k, v in layer_pieces(got_up, ups, l).items()}
        wd = layer_pieces(got_wide, wide, l)
        return dict(win=win, wuf=up["w_up_fox"], wus=up["w_up_sb"], wud=up["w_up_dil"], wo=jnp.concatenate(wd["w_out"], axis=0),
                    wmi=jnp.concatenate(wd["w_mlp_in"], axis=1), wmo=jnp.concatenate(wd["w_mlp_out"], axis=0))

    weights = [layer_weights(l) for l in range(depth)]
    small = {k: wv[k] for k in SMALL}

    loss, grad_x, bufs, gss = local_step(x, positions, loss_target, weights, small)
    loss = lax.psum(loss, ("x", "y", "c"))

    g_small = {k: jnp.stack([gss[l][k] for l in range(depth)]) for k in SMALL}
    g_forget = jnp.stack([gss[l]["w_in_forget"] for l in range(depth)])
    summed = all_reduce_small(jnp.concatenate([_pack_small(g_small), g_forget.reshape(-1, 1024)]), name="reduce_small")
    g_small = _unpack_small(summed[:SMALL_ROWS], small)
    g_forget = summed[SMALL_ROWS:].reshape(g_forget.shape)

    parts = [bufs["win"], bufs["ups"], bufs["wide"]]
    place = jnp.concatenate([core, chip])
    sums = [pair_sum_scatter(p, place, name=f"reduce_pair_sum{t}") for t, p in enumerate(parts)]
    joined = [chip_sum_join(own, landed, chip, name=f"reduce_chip_sum{t}").reshape(-1, parts[t].shape[-1])
              for t, (own, landed) in enumerate(sums)]

    def own_w_in_columns(window):
        cols = w_in.shape[-1]
        first = jnp.concatenate([window[..., :O1], g_forget, window[..., O1:cols - (O2 - O1)]], axis=-1)
        shift = jnp.maximum((cols - WIN_STRIDE * WIN_TILE) * chip[0] - (O2 - O1), 0)
        rest = lax.dynamic_slice_in_dim(window, shift, cols, axis=2)
        return jnp.where(chip[0] == 0, first, rest)

    g_big = {"w_in": own_w_in_columns(joined[0].reshape(depth, -1, joined[0].shape[-1]))}
    for a, keys in ((joined[1], ups), (joined[2], wide)):
        at = 0
        for k in keys:
            rows = wv[k].shape[0] * wv[k].shape[1]
            g_big[k] = a[at:at + rows].reshape(wv[k].shape)
            at += rows

    grads = {**g_small, **g_big}
    delta, new_m, new_v = {}, {}, {}
    d_s, m_s, v_s = adamw(_pack_small(small), _pack_small(g_small), _pack_small({k: mv[k] for k in SMALL}),
                          _pack_small({k: vv[k] for k in SMALL}), name="adamw_small")
    delta.update(_unpack_small(d_s, small))
    new_m.update(_unpack_small(m_s, small))
    new_v.update(_unpack_small(v_s, small))
    for k in ("w_in",) + ups + wide:
        d_k, m_k, v_k = adamw(flat2(wv[k]), flat2(g_big[k]), flat2(mv[k]), flat2(vv[k]), name=f"adamw_{k}")
        delta[k], new_m[k], new_v[k] = d_k.reshape(wv[k].shape), m_k.reshape(wv[k].shape), v_k.reshape(wv[k].shape)

    return (loss, grad_x, *[grads[k] for k in names], *[delta[k] for k in names], *[new_m[k] for k in names], *[new_v[k] for k in names])
```

```python
import functools

import jax
import jax.numpy as jnp
from jax import lax
from jax.experimental import pallas as pl
from jax.experimental.pallas import tpu as pltpu

F32 = jnp.float32
BF16 = jnp.bfloat16
MM = jnp.bfloat16

HEAD = 64
LANES = 128
EPS = 1e-6
SCALE = 0.125
ROPE_THETA = 10000.0
DIL_PATTERNS = ((128, 1), (512, 4), (2048, 16))
ADAM_LR, ADAM_B1, ADAM_B2, ADAM_EPS, ADAM_WD, ADAM_STEP = 0.001, 0.9, 0.999, 1e-08, 0.01, 10

FOXQ, FOXK, FOXV = 0, 4, 8
SBQ, SBK, SBV = 12, 16, 20
DILQ, DILK, DILV = 24, 30, 36
GATE, FORGET, NBLK = 42, 66, 68
DPROJ = NBLK * LANES
O1, O2, O3, O4, DIN = 1536, 1544, 3080, 5384, 8456

VMEM_LIMIT = 56 * 1024 * 1024
MESH_ID = pl.DeviceIdType.MESH
ANY = pl.BlockSpec(memory_space=pl.ANY)


def _params(sem=None):
    return pltpu.CompilerParams(dimension_semantics=sem, vmem_limit_bytes=VMEM_LIMIT)


def _iota(shape, dim):
    return lax.broadcasted_iota(jnp.int32, shape, dim)


def _split2(x):
    hi = x.astype(BF16)
    lo = (x - hi.astype(F32)).astype(BF16)
    return hi, lo


def _split3(x):
    hi = x.astype(BF16)
    r = x - hi.astype(F32)
    mid = r.astype(BF16)
    lo = (r - mid.astype(F32)).astype(BF16)
    return hi, mid, lo


def _dot(a, b):
    return jnp.dot(a, b, preferred_element_type=F32)


def _dot_nt(a, b):
    return lax.dot_general(a, b, (((1,), (1,)), ((), ())), preferred_element_type=F32)


def _dot_tn(a, b):
    return lax.dot_general(a, b, (((0,), (0,)), ((), ())), preferred_element_type=F32)


def _xdot2(x, m):
    hi, lo = _split2(x)
    return _dot(hi, m) + _dot(lo, m)


def _xdot3(x, m):
    hi, mid, lo = _split3(x)
    return _dot(hi, m) + _dot(mid, m) + _dot(lo, m)


def _xdot3_left(m, x):
    hi, mid, lo = _split3(x)
    return _dot(m, hi) + _dot(m, mid) + _dot(m, lo)


def _head_mat(w):
    return ((_iota((w, w), 0) >> 6) == (_iota((w, w), 1) >> 6)).astype(BF16)


def _softplus_parts(z):
    e = jnp.exp(-jnp.abs(z))
    return e, jnp.maximum(z, 0.0) + jnp.log(1.0 + e)


def _fit(dim, want):
    t = min(want, dim)
    while dim % t:
        t -= LANES
        assert t > 0, (dim, want)
    return t


def matmul(a, b, *, ta=False, tb=False, out_dtype=F32, add=None, tm=2048, tn=512, tk=1024, dest=None, relu2=False,
           relu2_of=None, name):
    K, M = a.shape if ta else a.shape[::-1]
    K2, N = b.shape[::-1] if tb else b.shape
    assert K == K2, (a.shape, b.shape, ta, tb)
    tm, tn, tk = _fit(M, tm), _fit(N, tn), _fit(K, tk)
    nk = K // tk
    dn = (((0 if ta else 1,), (1 if tb else 0,)), ((), ()))
    if dest is None:
        tiles, source = N // tn, lambda j: j
    else:
        assert add is None and not tb
        buffer, tiles, source, place = dest

    extra = add if add is not None else relu2_of
    assert add is None or relu2_of is None

    def body(*refs):
        act_ref = None
        if dest is not None:
            a_ref, b_ref, _, o_ref, acc_ref = refs
        elif relu2:
            a_ref, b_ref, o_ref, act_ref, acc_ref = refs
        elif extra is None:
            a_ref, b_ref, o_ref, acc_ref = refs
        else:
            a_ref, b_ref, add_ref, o_ref, acc_ref = refs
        k = pl.program_id(2)
        part = lax.dot_general(a_ref[...].astype(MM), b_ref[...].astype(MM), dn, preferred_element_type=F32)

        @pl.when(k == 0)
        def _():
            acc_ref[...] = part

        @pl.when(k > 0)
        def _():
            acc_ref[...] += part

        @pl.when(k == nk - 1)
        def _():
            r = acc_ref[...]
            if add is not None:
                r = r + add_ref[...]
            if relu2_of is not None:
                r = r * (2.0 * jnp.maximum(add_ref[...], 0.0))
            o_ref[...] = r.astype(o_ref.dtype).reshape(o_ref.shape)
            if act_ref is not None:
                pos = jnp.maximum(r, 0.0)
                act_ref[...] = (pos * pos).astype(act_ref.dtype)

    a_spec = pl.BlockSpec((tk, tm), lambda i, j, k: (k, i)) if ta else pl.BlockSpec((tm, tk), lambda i, j, k: (i, k))
    b_spec = pl.BlockSpec((tn, tk), lambda i, j, k: (j, k)) if tb else pl.BlockSpec((tk, tn), lambda i, j, k: (k, source(j)))
    o_spec = pl.BlockSpec((tm, tn), lambda i, j, k: (i, j))
    ins, specs, aliases = [a, b], [a_spec, b_spec], {}
    out_shape = jax.ShapeDtypeStruct((M, N), out_dtype)
    if extra is not None:
        ins.append(extra)
        specs.append(o_spec)
    if relu2:
        o_spec, out_shape = [o_spec, o_spec], [out_shape, jax.ShapeDtypeStruct((M, N), MM)]
    if dest is not None:
        ins.append(buffer)
        specs.append(ANY)
        aliases = {2: 0}
        o_spec = pl.BlockSpec((1, tm, tn), lambda i, j, k: place(i, j))
        out_shape = jax.ShapeDtypeStruct(buffer.shape, buffer.dtype)
    return pl.pallas_call(
        body, name=name, grid=(M // tm, tiles, nk), in_specs=specs, out_specs=o_spec, out_shape=out_shape,
        scratch_shapes=[pltpu.VMEM((tm, tn), F32)], input_output_aliases=aliases,
        compiler_params=_params(("parallel", "parallel", "arbitrary")),
    )(*ins)


def _rows(n, want=512):
    t = min(want, n)
    assert n % t == 0, (n, t)
    return t


def rmsnorm_fwd(x, g, *, name):
    n, d = x.shape
    tr = _rows(n)

    def body(x_ref, g_ref, o_ref, t_ref):
        xv = x_ref[...]
        r = lax.rsqrt(jnp.mean(xv * xv, axis=1, keepdims=True) + EPS)
        y = xv * r * g_ref[...]
        o_ref[...] = y.astype(o_ref.dtype)
        t_ref[...] = y.T.astype(t_ref.dtype)

    row = pl.BlockSpec((tr, d), lambda i: (i, 0))
    vec = pl.BlockSpec((1, d), lambda i: (0, 0))
    return pl.pallas_call(
        body, name=name, grid=(n // tr,), in_specs=[row, vec], out_specs=[row, pl.BlockSpec((d, tr), lambda i: (0, i))],
        out_shape=[jax.ShapeDtypeStruct((n, d), MM), jax.ShapeDtypeStruct((d, n), MM)], compiler_params=_params(("parallel",)))(x, g)


def rmsnorm_bwd(x, g, dh, dres, *, name):
    n, d = x.shape
    tr = _rows(n)

    def body(x_ref, g_ref, dh_ref, dr_ref, dx_ref, dg_ref):
        @pl.when(pl.program_id(0) == 0)
        def _():
            dg_ref[...] = jnp.zeros_like(dg_ref)

        xv = x_ref[...]
        r = lax.rsqrt(jnp.mean(xv * xv, axis=1, keepdims=True) + EPS)
        y = xv * r
        dhv = dh_ref[...]
        dy = dhv * g_ref[...]
        dx_ref[...] = dr_ref[...] + r * (dy - y * jnp.mean(dy * y, axis=1, keepdims=True))
        dg_ref[...] += jnp.sum(dhv * y, axis=0, keepdims=True)

    row = pl.BlockSpec((tr, d), lambda i: (i, 0))
    vec = pl.BlockSpec((1, d), lambda i: (0, 0))
    return pl.pallas_call(
        body, name=name, grid=(n // tr,), in_specs=[row, vec, row, row], out_specs=[row, vec],
        out_shape=[jax.ShapeDtypeStruct((n, d), F32), jax.ShapeDtypeStruct((1, d), F32)],
        compiler_params=_params(("arbitrary",)))(x, g, dh, dres)


def loss_grad(y, tgt, *, name):
    n, d = y.shape
    tr = _rows(n)

    def body(y_ref, t_ref, dy_ref, acc_ref):
        @pl.when(pl.program_id(0) == 0)
        def _():
            acc_ref[...] = jnp.zeros_like(acc_ref)

        e = y_ref[...] - t_ref[...]
        dy_ref[...] = e * (1.0 / d)
        acc_ref[...] += jnp.sum(e * e, axis=0, keepdims=True)

    row = pl.BlockSpec((tr, d), lambda i: (i, 0))
    vec = pl.BlockSpec((1, d), lambda i: (0, 0))
    return pl.pallas_call(
        body, name=name, grid=(n // tr,), in_specs=[row, row], out_specs=[row, vec],
        out_shape=[jax.ShapeDtypeStruct((n, d), F32), jax.ShapeDtypeStruct((1, d), F32)],
        compiler_params=_params(("arbitrary",)))(y, tgt)


MERGE_W = 256


def _gate_specs(tr, d):
    per = d // MERGE_W
    base = GATE * LANES // MERGE_W
    return [pl.BlockSpec((tr, MERGE_W), functools.partial(lambda i, j, b: (i, base + per * b + j), b=b)) for b in range(3)]


def merge_fwd(proj, ys, *, name):
    n, d = ys[0].shape
    tr = _rows(n)

    def body(g0, g1, g2, y0, y1, y2, o_ref):
        acc = jax.nn.sigmoid(g0[...]) * y0[...]
        acc += jax.nn.sigmoid(g1[...]) * y1[...]
        acc += jax.nn.sigmoid(g2[...]) * y2[...]
        o_ref[...] = acc.astype(o_ref.dtype)

    blk = pl.BlockSpec((tr, MERGE_W), lambda i, j: (i, j))
    return pl.pallas_call(
        body, name=name, grid=(n // tr, d // MERGE_W), in_specs=_gate_specs(tr, d) + [blk] * 3, out_specs=blk,
        out_shape=jax.ShapeDtypeStruct((n, d), MM), compiler_params=_params(("parallel", "parallel")))(proj, proj, proj, *ys)


def merge_bwd(proj, ys, dm, *, name):
    n, d = dm.shape
    tr = _rows(n)

    def body(g0, g1, g2, y0, y1, y2, dm_ref, dy0, dy1, dy2, dg0, dg1, dg2):
        dmv = dm_ref[...]
        for g, y, dy, dg in ((g0, y0, dy0, dg0), (g1, y1, dy1, dg1), (g2, y2, dy2, dg2)):
            s = jax.nn.sigmoid(g[...])
            dy[...] = (dmv * s).astype(dy.dtype)
            dg[...] = (dmv * y[...] * s * (1.0 - s)).astype(dg.dtype)

    blk = pl.BlockSpec((tr, MERGE_W), lambda i, j: (i, j))
    out = jax.ShapeDtypeStruct((n, d), MM)
    return pl.pallas_call(
        body, name=name, grid=(n // tr, d // MERGE_W), in_specs=_gate_specs(tr, d) + [blk] * 4, out_specs=[blk] * 6,
        out_shape=[out] * 6, compiler_params=_params(("parallel", "parallel")))(proj, proj, proj, *ys, dm)


def adamw(w, g, m, v, *, name):
    r, c = w.shape
    tr = r
    while tr * c * 4 > (1 << 21) and tr % 16 == 0:
        tr //= 2
    c1 = 1.0 / (1.0 - ADAM_B1 ** ADAM_STEP)
    c2 = 1.0 / (1.0 - ADAM_B2 ** ADAM_STEP)

    def body(w_ref, g_ref, m_ref, v_ref, d_ref, mo_ref, vo_ref):
        gv = g_ref[...]
        m2 = ADAM_B1 * m_ref[...] + (1.0 - ADAM_B1) * gv
        v2 = ADAM_B2 * v_ref[...] + (1.0 - ADAM_B2) * (gv * gv)
        d_ref[...] = -ADAM_LR * ((m2 * c1) / (jnp.sqrt(v2 * c2) + ADAM_EPS) + ADAM_WD * w_ref[...])
        mo_ref[...] = m2
        vo_ref[...] = v2

    blk = pl.BlockSpec((tr, c), lambda i: (i, 0))
    out = jax.ShapeDtypeStruct((r, c), F32)
    return pl.pallas_call(body, name=name, grid=(r // tr,), in_specs=[blk] * 4, out_specs=[blk] * 3, out_shape=[out] * 3,
                          compiler_params=_params(("parallel",)))(w, g, m, v)


def rope_table(pos, inv, *, name):
    n = pos.shape[0]
    tr = _rows(n)

    def body(p_ref, i_ref, c_ref, s_ref):
        ang = p_ref[...].astype(F32) * i_ref[...]
        c_ref[...] = jnp.cos(ang)
        s_ref[...] = jnp.sin(ang)

    out = jax.ShapeDtypeStruct((n, LANES), F32)
    blk = pl.BlockSpec((tr, LANES), lambda i: (i, 0))
    return pl.pallas_call(
        body, name=name, grid=(n // tr,), in_specs=[pl.BlockSpec((tr, 1), lambda i: (i, 0)), pl.BlockSpec((1, LANES), lambda i: (0, 0))],
        out_specs=[blk, blk], out_shape=[out, out], compiler_params=_params(("parallel",)))(pos, inv)


def _rot_half(x):
    first = (_iota((1, LANES), 1) & 63) < 32
    return jnp.where(first, -pltpu.roll(x, LANES - 32, axis=1), pltpu.roll(x, 32, axis=1))


def _head_norm(xv, gm):
    r = lax.rsqrt(_xdot2(xv * xv, gm) * (1.0 / HEAD) + EPS)
    return r, xv * r


def _head_norm_bwd(xh, r, dxh, gm):
    return r * (dxh - xh * (_xdot2(dxh * xh, gm) * (1.0 / HEAD)))


def fox_prep_fwd(proj, qg, kg, bf, *, bsz, seq, name):
    n = bsz * seq
    tr = min(256, seq)
    nt = seq // tr
    w = 4 * LANES

    def body(q_ref, k_ref, f_ref, qg_ref, kg_ref, b_ref, qn_ref, kn_ref, fb_ref, f8_ref, carry):
        @pl.when(pl.program_id(1) == 0)
        def _():
            carry[...] = jnp.zeros_like(carry)

        gm = _head_mat(LANES)
        for src, gain, dst in ((q_ref, qg_ref, qn_ref), (k_ref, kg_ref, kn_ref)):
            for c in range(4):
                sl = slice(c * LANES, (c + 1) * LANES)
                _, xh = _head_norm(src[:, sl], gm)
                dst[:, sl] = (xh * gain[:, sl]).astype(dst.dtype)
        logf = jax.nn.log_sigmoid(f_ref[...] + b_ref[...])
        lower = (_iota((tr, tr), 1) <= _iota((tr, tr), 0)).astype(BF16)
        fcum = _xdot3_left(lower, logf) + carry[...]
        carry[...] = fcum[tr - 1:tr, :]
        f8_ref[...] = fcum
        spread = (_iota((LANES, w), 0) == (_iota((LANES, w), 1) >> 6)).astype(BF16)
        fb_ref[...] = _xdot3(fcum, spread)

    row = lambda width, blk: pl.BlockSpec((tr, width), lambda b, t: (b * nt + t, blk))
    vec = lambda width: pl.BlockSpec((1, width), lambda b, t: (0, 0))
    return pl.pallas_call(
        body, name=name, grid=(bsz, nt),
        in_specs=[row(w, FOXQ // 4), row(w, FOXK // 4), row(LANES, FORGET), vec(w), vec(w), vec(LANES)],
        out_specs=[row(w, 0), row(w, 0), row(w, 0), row(LANES, 0)],
        out_shape=[jax.ShapeDtypeStruct((n, w), MM), jax.ShapeDtypeStruct((n, w), MM),
                   jax.ShapeDtypeStruct((n, w), F32), jax.ShapeDtypeStruct((n, LANES), F32)],
        scratch_shapes=[pltpu.VMEM((1, LANES), F32)],
        compiler_params=_params(("parallel", "arbitrary")))(proj, proj, proj, qg, kg, bf)


def fox_prep_bwd(proj, qg, kg, bf, dqn, dkn, df, *, bsz, seq, name):
    n = bsz * seq
    tr = min(256, seq)
    nt = seq // tr
    w = 4 * LANES

    def body(q_ref, k_ref, f_ref, qg_ref, kg_ref, b_ref, dqn_ref, dkn_ref, df_ref,
             dq_ref, dk_ref, dl_ref, dqg_ref, dkg_ref, db_ref, carry):
        first = (pl.program_id(0) == 0) & (pl.program_id(1) == 0)

        @pl.when(first)
        def _():
            dqg_ref[...] = jnp.zeros_like(dqg_ref)
            dkg_ref[...] = jnp.zeros_like(dkg_ref)
            db_ref[...] = jnp.zeros_like(db_ref)

        @pl.when(pl.program_id(1) == 0)
        def _():
            carry[...] = jnp.zeros_like(carry)

        gm = _head_mat(LANES)
        for src, gain, dy_ref, dx_ref, dg_ref in ((q_ref, qg_ref, dqn_ref, dq_ref, dqg_ref), (k_ref, kg_ref, dkn_ref, dk_ref, dkg_ref)):
            for c in range(4):
                sl = slice(c * LANES, (c + 1) * LANES)
                r, xh = _head_norm(src[:, sl], gm)
                dy = dy_ref[:, sl]
                dg_ref[:, sl] += jnp.sum(dy * xh, axis=0, keepdims=True)
                dx_ref[:, sl] = _head_norm_bwd(xh, r, dy * gain[:, sl], gm).astype(dx_ref.dtype)
        upper = (_iota((tr, tr), 1) >= _iota((tr, tr), 0)).astype(BF16)
        dlogf = _xdot3_left(upper, df_ref[...]) + carry[...]
        carry[...] = dlogf[0:1, :]
        dlogit = dlogf * jax.nn.sigmoid(-(f_ref[...] + b_ref[...]))
        dl_ref[:, 0:LANES] = dlogit.astype(dl_ref.dtype)
        dl_ref[:, LANES:2 * LANES] = jnp.zeros((tr, LANES), dl_ref.dtype)
        db_ref[...] += jnp.sum(dlogit, axis=0, keepdims=True)

    row = lambda width, blk: pl.BlockSpec((tr, width), lambda b, t: (b * nt + nt - 1 - t, blk))
    vec = lambda width: pl.BlockSpec((1, width), lambda b, t: (0, 0))
    return pl.pallas_call(
        body, name=name, grid=(bsz, nt),
        in_specs=[row(w, FOXQ // 4), row(w, FOXK // 4), row(LANES, FORGET), vec(w), vec(w), vec(LANES),
                  row(w, 0), row(w, 0), row(LANES, 0)],
        out_specs=[row(w, 0), row(w, 0), row(2 * LANES, 0), vec(w), vec(w), vec(LANES)],
        out_shape=[jax.ShapeDtypeStruct((n, w), MM), jax.ShapeDtypeStruct((n, w), MM), jax.ShapeDtypeStruct((n, 2 * LANES), MM),
                   jax.ShapeDtypeStruct((1, w), F32), jax.ShapeDtypeStruct((1, w), F32), jax.ShapeDtypeStruct((1, LANES), F32)],
        scratch_shapes=[pltpu.VMEM((1, LANES), F32)],
        compiler_params=_params(("arbitrary", "arbitrary")))(proj, proj, proj, qg, kg, bf, dqn, dkn, df)


DIL_W = 6 * LANES


def dil_prep_fwd(proj, qg, kg, cos, sin, *, name):
    n = proj.shape[0]
    tr = _rows(n, 256)

    def body(q_ref, k_ref, qg_ref, kg_ref, c_ref, s_ref, qo_ref, ko_ref):
        gm = _head_mat(LANES)
        cv, sv = c_ref[...], s_ref[...]
        for src, gain, dst in ((q_ref, qg_ref, qo_ref), (k_ref, kg_ref, ko_ref)):
            for c in range(6):
                sl = slice(c * LANES, (c + 1) * LANES)
                _, xh = _head_norm(src[:, sl], gm)
                xn = xh * gain[:, sl]
                dst[:, sl] = (xn * cv + _rot_half(xn) * sv).astype(dst.dtype)

    row = lambda width, blk: pl.BlockSpec((tr, width), lambda i: (i, blk))
    vec = pl.BlockSpec((1, DIL_W), lambda i: (0, 0))
    out = jax.ShapeDtypeStruct((n, DIL_W), MM)
    return pl.pallas_call(
        body, name=name, grid=(n // tr,),
        in_specs=[row(DIL_W, DILQ // 6), row(DIL_W, DILK // 6), vec, vec, row(LANES, 0), row(LANES, 0)],
        out_specs=[row(DIL_W, 0), row(DIL_W, 0)], out_shape=[out, out],
        compiler_params=_params(("parallel",)))(proj, proj, qg, kg, cos, sin)


def dil_prep_bwd(proj, qg, kg, cos, sin, dqr, dkr, *, name):
    n = proj.shape[0]
    tr = _rows(n, 256)

    def body(q_ref, k_ref, qg_ref, kg_ref, c_ref, s_ref, dqr_ref, dkr_ref, dq_ref, dk_ref, dqg_ref, dkg_ref):
        @pl.when(pl.program_id(0) == 0)
        def _():
            dqg_ref[...] = jnp.zeros_like(dqg_ref)
            dkg_ref[...] = jnp.zeros_like(dkg_ref)

        gm = _head_mat(LANES)
        cv, sv = c_ref[...], s_ref[...]
        for src, gain, dy_ref, dx_ref, dg_ref in ((q_ref, qg_ref, dqr_ref, dq_ref, dqg_ref), (k_ref, kg_ref, dkr_ref, dk_ref, dkg_ref)):
            for c in range(6):
                sl = slice(c * LANES, (c + 1) * LANES)
                r, xh = _head_norm(src[:, sl], gm)
                dy = dy_ref[:, sl]
                dxn = dy * cv - _rot_half(dy * sv)
                dg_ref[:, sl] += jnp.sum(dxn * xh, axis=0, keepdims=True)
                dx_ref[:, sl] = _head_norm_bwd(xh, r, dxn * gain[:, sl], gm).astype(dx_ref.dtype)

    row = lambda width, blk: pl.BlockSpec((tr, width), lambda i: (i, blk))
    vec = pl.BlockSpec((1, DIL_W), lambda i: (0, 0))
    out = jax.ShapeDtypeStruct((n, DIL_W), MM)
    gout = jax.ShapeDtypeStruct((1, DIL_W), F32)
    return pl.pallas_call(
        body, name=name, grid=(n // tr,),
        in_specs=[row(DIL_W, DILQ // 6), row(DIL_W, DILK // 6), vec, vec, row(LANES, 0), row(LANES, 0), row(DIL_W, 0), row(DIL_W, 0)],
        out_specs=[row(DIL_W, 0), row(DIL_W, 0), vec, vec], out_shape=[out, out, gout, gout],
        compiler_params=_params(("arbitrary",)))(proj, proj, qg, kg, cos, sin, dqr, dkr)


def dil_combine_fwd(os_, lses, *, name):
    n, w = os_[0].shape
    tr = _rows(n)

    def body(o0, o1, o2, l0, l1, l2, out_ref):
        a, b, c = l0[...], l1[...], l2[...]
        m = jnp.maximum(jnp.maximum(a, b), c)
        ea, eb, ec = jnp.exp(a - m), jnp.exp(b - m), jnp.exp(c - m)
        out_ref[...] = ((ea * o0[...] + eb * o1[...] + ec * o2[...]) / (ea + eb + ec)).astype(out_ref.dtype)

    blk = pl.BlockSpec((tr, w), lambda i: (i, 0))
    return pl.pallas_call(body, name=name, grid=(n // tr,), in_specs=[blk] * 6, out_specs=blk,
                          out_shape=jax.ShapeDtypeStruct((n, w), MM), compiler_params=_params(("parallel",)))(*os_, *lses)


def dil_combine_bwd(os_, lses, dout, *, name):
    n, w = dout.shape
    tr = _rows(n)

    def body(o0, o1, o2, l0, l1, l2, d_ref, do0, do1, do2, dl0, dl1, dl2):
        a, b, c = l0[...], l1[...], l2[...]
        m = jnp.maximum(jnp.maximum(a, b), c)
        es = [jnp.exp(a - m), jnp.exp(b - m), jnp.exp(c - m)]
        inv = 1.0 / (es[0] + es[1] + es[2])
        ws = [e * inv for e in es]
        dv = d_ref[...]
        gm = _head_mat(w)
        dws = [_xdot2(dv * o[...], gm) for o in (o0, o1, o2)]
        mean = ws[0] * dws[0] + ws[1] * dws[1] + ws[2] * dws[2]
        for wg, dw, do, dl in zip(ws, dws, (do0, do1, do2), (dl0, dl1, dl2)):
            do[...] = wg * dv
            dl[...] = wg * (dw - mean)

    blk = pl.BlockSpec((tr, w), lambda i: (i, 0))
    out = jax.ShapeDtypeStruct((n, w), F32)
    return pl.pallas_call(body, name=name, grid=(n // tr,), in_specs=[blk] * 7, out_specs=[blk] * 6, out_shape=[out] * 6,
                          compiler_params=_params(("parallel",)))(*os_, *lses, dout)


def _key_plan(qi, tq, seq, window, run):
    if window + tq >= seq:
        for bi in range(seq // tq):
            lo = bi * tq
            segs = ([(0, lo, "bulk")] if lo else []) + [(lo, tq, "diag")]
            pl.when(qi == bi)(functools.partial(run, segs))
    else:
        ext = window + tq
        run([(pl.multiple_of(jnp.maximum((qi + 1) * tq - ext, 0), LANES), ext, "band")])


def _seg_mask(seg, qi, tq, window, dilation, strict=False):
    start, width, kind = seg
    d = _iota((tq, width), 0) - _iota((tq, width), 1)
    if kind == "bulk":
        d = d + width
    elif kind == "band":
        d = d + (qi * tq - start)
    ok = None
    if kind != "bulk":
        ok = (d > 0) if strict else (d >= 0)
    if window is not None:
        ok = (d <= window) if ok is None else ok & (d <= window)
    if dilation > 1:
        on_grid = (d & (dilation - 1)) == 0
        ok = on_grid if ok is None else ok & on_grid
    return ok


def _lane_first():
    return _iota((1, LANES), 1) < HEAD


def _attn_specs(bsz, seq, tq, qo, ko, vo):
    nq = seq // tq
    qspec = lambda off: pl.BlockSpec((tq, LANES), lambda b, j, i: (b * nq + i, off + j))
    kspec = lambda off: pl.BlockSpec((seq, LANES), lambda b, j, i: (b, off + j))
    return nq, qspec, kspec


def softmax_attn_fwd(q, k, v, bias, *, qo, ko, vo, pairs, bsz, seq, window, dilation, tq, name):
    n = bsz * seq
    nq, qspec, kspec = _attn_specs(bsz, seq, tq, qo, ko, vo)

    def body(*refs):
        if bias is None:
            q_ref, k_ref, v_ref, o_ref, l_ref = refs
        else:
            q_ref, k_ref, v_ref, fq_ref, fk_ref, o_ref, l_ref = refs
        qi = pl.program_id(2)

        def run(segs):
            qv = (q_ref[...] * SCALE).astype(MM)
            first = _lane_first()
            keys = [(k_ref[pl.ds(st, w), :].astype(MM), v_ref[pl.ds(st, w), :].astype(MM),
                     _seg_mask((st, w, kind), qi, tq, None if window >= seq else window, dilation), st, w)
                    for st, w, kind in segs]
            outs, lses = [], []
            for a in range(2):
                qa = jnp.where(first if a == 0 else ~first, qv, jnp.zeros_like(qv))
                scores = []
                for kv, _, ok, st, w in keys:
                    s = _dot_nt(qa, kv)
                    if bias is not None:
                        s = s + fq_ref[:, a * HEAD:a * HEAD + 1] - fk_ref[a:a + 1, pl.ds(st, w)]
                    scores.append(s if ok is None else jnp.where(ok, s, -jnp.inf))
                m = functools.reduce(jnp.maximum, [jnp.max(s, axis=1, keepdims=True) for s in scores])
                ps = [jnp.exp(s - m) for s in scores]
                den = sum(jnp.sum(p, axis=1, keepdims=True) for p in ps)
                acc = sum(_dot(p.astype(MM), vv) for p, (_, vv, _, _, _) in zip(ps, keys))
                outs.append(acc / den)
                lses.append(m + jnp.log(den))
            o_ref[...] = jnp.where(first, outs[0], outs[1]).astype(o_ref.dtype)
            l_ref[...] = jnp.where(first, lses[0], lses[1])

        _key_plan(qi, tq, seq, window, run)

    ins, specs = [q, k, v], [qspec(qo), kspec(ko), kspec(vo)]
    if bias is not None:
        ins += list(bias)
        specs += [qspec(0), pl.BlockSpec((8, seq), lambda b, j, i: (b * pairs + j, 0))]
    out = jax.ShapeDtypeStruct((n, LANES * pairs), F32)
    return pl.pallas_call(
        body, name=name, grid=(bsz, pairs, nq), in_specs=specs, out_specs=[qspec(0), qspec(0)], out_shape=[out, out],
        compiler_params=_params(("parallel", "parallel", "arbitrary")))(*ins)


def softmax_attn_bwd(q, k, v, o, do, lse, dlse, bias, *, qo, ko, vo, pairs, bsz, seq, window, dilation, tq, dq_dtype, dk_dtype, name):
    n = bsz * seq
    nq, qspec, kspec = _attn_specs(bsz, seq, tq, qo, ko, vo)
    has_bias, has_dlse = bias is not None, dlse is not None

    def body(*refs):
        refs = list(refs)
        q_ref, k_ref, v_ref, o_ref, do_ref, l_ref = refs[:6]
        del refs[:6]
        dl_ref = refs.pop(0) if has_dlse else None
        fq_ref, fk_ref = (refs.pop(0), refs.pop(0)) if has_bias else (None, None)
        dq_ref, dk_ref, dv_ref = refs[:3]
        del refs[:3]
        dfq_ref, dfk_ref = (refs.pop(0), refs.pop(0)) if has_bias else (None, None)
        dk_acc, dv_acc = refs
        qi = pl.program_id(2)

        @pl.when(qi == 0)
        def _():
            dk_acc[...] = jnp.zeros_like(dk_acc)
            dv_acc[...] = jnp.zeros_like(dv_acc)
            if has_bias:
                dfk_ref[...] = jnp.zeros_like(dfk_ref)

        def run(segs):
            qv = (q_ref[...] * SCALE).astype(MM)
            dov = do_ref[...]
            dob = dov.astype(MM)
            prod = dov * o_ref[...]
            first = _lane_first()
            keys = [(k_ref[pl.ds(st, w), :].astype(MM), v_ref[pl.ds(st, w), :].astype(MM),
                     _seg_mask((st, w, kind), qi, tq, None if window >= seq else window, dilation), st, w)
                    for st, w, kind in segs]
            dqs, dfqs = [], []
            dks, dvs = [[] for _ in keys], [[] for _ in keys]
            for a in range(2):
                mine = first if a == 0 else ~first
                col = slice(a * HEAD, a * HEAD + 1)
                delta = jnp.sum(jnp.where(mine, prod, 0.0), axis=1, keepdims=True)
                if has_dlse:
                    delta = delta - dl_ref[:, col]
                qa = jnp.where(mine, qv, jnp.zeros_like(qv))
                doa = jnp.where(mine, dob, jnp.zeros_like(dob))
                shift = l_ref[:, col]
                if has_bias:
                    shift = shift - fq_ref[:, col]
                dq, dfq = 0.0, 0.0
                for si, (kv, vv, ok, st, w) in enumerate(keys):
                    s = _dot_nt(qa, kv)
                    if has_bias:
                        s = s - fk_ref[a:a + 1, pl.ds(st, w)]
                    p = jnp.exp(s - shift)
                    if ok is not None:
                        p = jnp.where(ok, p, 0.0)
                    ds = p * (_dot_nt(doa, vv) - delta)
                    dsb = ds.astype(MM)
                    dvs[si].append(_dot_tn(p.astype(MM), dob))
                    dks[si].append(_dot_tn(dsb, qv))
                    dq = dq + _dot(dsb, kv)
                    if has_bias:
                        dfq = dfq + jnp.sum(ds, axis=1, keepdims=True)
                        dfk_ref[a:a + 1, pl.ds(st, w)] += jnp.sum(ds, axis=0, keepdims=True)
                dqs.append(dq * SCALE)
                dfqs.append(dfq)
            dq_ref[...] = jnp.where(first, dqs[0], dqs[1]).astype(dq_ref.dtype)
            for (_, _, _, st, w), dk, dv in zip(keys, dks, dvs):
                dk_acc[pl.ds(st, w), :] += jnp.where(first, dk[0], dk[1])
                dv_acc[pl.ds(st, w), :] += jnp.where(first, dv[0], dv[1])
            if has_bias:
                dfq_ref[...] = jnp.where(first, dfqs[0], dfqs[1])

        _key_plan(qi, tq, seq, window, run)

        @pl.when(qi == nq - 1)
        def _():
            dk_ref[...] = dk_acc[...].astype(dk_ref.dtype)
            dv_ref[...] = dv_acc[...].astype(dv_ref.dtype)

    wide = LANES * pairs
    ins = [q, k, v, o, do, lse]
    specs = [qspec(qo), kspec(ko), kspec(vo), qspec(0), qspec(0), qspec(0)]
    outs = [jax.ShapeDtypeStruct((n, wide), dq_dtype), jax.ShapeDtypeStruct((n, wide), dk_dtype), jax.ShapeDtypeStruct((n, wide), MM)]
    out_specs = [qspec(0), kspec(0), kspec(0)]
    if has_dlse:
        ins.append(dlse)
        specs.append(qspec(0))
    if has_bias:
        rows = pl.BlockSpec((8, seq), lambda b, j, i: (b * pairs + j, 0))
        ins += list(bias)
        specs += [qspec(0), rows]
        outs += [jax.ShapeDtypeStruct((n, wide), F32), jax.ShapeDtypeStruct((bsz * pairs * 8, seq), F32)]
        out_specs += [qspec(0), rows]
    return pl.pallas_call(
        body, name=name, grid=(bsz, pairs, nq), in_specs=specs, out_specs=out_specs, out_shape=outs,
        scratch_shapes=[pltpu.VMEM((seq, LANES), F32), pltpu.VMEM((seq, LANES), F32)],
        compiler_params=_params(("parallel", "parallel", "arbitrary")))(*ins)


def _running_sum(vals, mat, carry, lat_ref, start, reverse):
    nb = vals.shape[1] // LANES
    for cb in (reversed(range(nb)) if reverse else range(nb)):
        blk = vals[:, cb * LANES:(cb + 1) * LANES]
        lat_ref[:, start + cb * LANES:start + (cb + 1) * LANES] = _dot(blk.astype(BF16), mat) + carry
        carry = carry + jnp.sum(blk, axis=1, keepdims=True)
    return carry


def _sb_weights(qa, keys, tq, lat_ref):
    after = (_iota((LANES, LANES), 0) > _iota((LANES, LANES), 1)).astype(BF16)
    carry = jnp.zeros((tq, 1), F32)
    logs = []
    for kv, ok, st, w in reversed(keys):
        z = _dot_nt(qa, kv)
        _, sp = _softplus_parts(z)
        visible = sp if ok is None else jnp.where(ok, sp, 0.0)
        carry = _running_sum(visible, after, carry, lat_ref, st, True)
        logs.append(z - sp)
    out = []
    for (kv, ok, st, w), log_beta in zip(keys, reversed(logs)):
        att = jnp.exp(log_beta - lat_ref[:, st:st + w])
        out.append((log_beta, att if ok is None else jnp.where(ok, att, 0.0)))
    return out


def _sb_keys(k_ref, v_ref, segs, qi, tq):
    return [(k_ref[st:st + w, :].astype(MM), v_ref[st:st + w, :].astype(MM),
             _seg_mask((st, w, kind), qi, tq, None, 1, strict=True), st, w) for st, w, kind in segs]


def sb_attn_fwd(proj, *, bsz, seq, tq, name):
    n = bsz * seq
    pairs = 4
    nq, qspec, kspec = _attn_specs(bsz, seq, tq, SBQ, SBK, SBV)

    def body(q_ref, k_ref, v_ref, o_ref, lat_ref):
        qi = pl.program_id(2)

        def run(segs):
            qv = (q_ref[...] * SCALE).astype(MM)
            keys = _sb_keys(k_ref, v_ref, segs, qi, tq)
            first = _lane_first()
            outs = []
            for a in range(2):
                qa = jnp.where(first if a == 0 else ~first, qv, jnp.zeros_like(qv))
                weights = _sb_weights(qa, [(kv, ok, st, w) for kv, _, ok, st, w in keys], tq, lat_ref)
                outs.append(sum(_dot(att.astype(MM), vv) for (_, att), (_, vv, _, _, _) in zip(weights, keys)))
            o_ref[...] = jnp.where(first, outs[0], outs[1]).astype(o_ref.dtype)

        _key_plan(qi, tq, seq, seq, run)

    return pl.pallas_call(
        body, name=name, grid=(bsz, pairs, nq), in_specs=[qspec(SBQ), kspec(SBK), kspec(SBV)], out_specs=qspec(0),
        out_shape=jax.ShapeDtypeStruct((n, LANES * pairs), MM), scratch_shapes=[pltpu.VMEM((tq, seq), F32)],
        compiler_params=_params(("parallel", "parallel", "arbitrary")))(proj, proj, proj)


def sb_attn_bwd(proj, do, *, bsz, seq, tq, name):
    n = bsz * seq
    pairs = 4
    nq, qspec, kspec = _attn_specs(bsz, seq, tq, SBQ, SBK, SBV)

    def body(q_ref, k_ref, v_ref, do_ref, dq_ref, dk_ref, dv_ref, lat_ref, dk_acc, dv_acc):
        qi = pl.program_id(2)

        @pl.when(qi == 0)
        def _():
            dk_acc[...] = jnp.zeros_like(dk_acc)
            dv_acc[...] = jnp.zeros_like(dv_acc)

        def run(segs):
            qv = (q_ref[...] * SCALE).astype(MM)
            keys = _sb_keys(k_ref, v_ref, segs, qi, tq)
            dob = do_ref[...].astype(MM)
            first = _lane_first()
            before = (_iota((LANES, LANES), 0) < _iota((LANES, LANES), 1)).astype(BF16)
            dqs = []
            dks, dvs = [[] for _ in keys], [[] for _ in keys]
            for a in range(2):
                mine = first if a == 0 else ~first
                qa = jnp.where(mine, qv, jnp.zeros_like(qv))
                doa = jnp.where(mine, dob, jnp.zeros_like(dob))
                weights = _sb_weights(qa, [(kv, ok, st, w) for kv, _, ok, st, w in keys], tq, lat_ref)
                gs = [_dot_nt(doa, vv) * att for (_, att), (_, vv, _, _, _) in zip(weights, keys)]
                carry = jnp.zeros((tq, 1), F32)
                for g, (_, _, _, st, w) in zip(gs, keys):
                    carry = _running_sum(g, before, carry, lat_ref, st, False)
                dq = 0.0
                for si, ((log_beta, att), g, (kv, _, ok, st, w)) in enumerate(zip(weights, gs, keys)):
                    dz = g - jnp.exp(log_beta) * (g + lat_ref[:, st:st + w])
                    dz = (dz if ok is None else jnp.where(ok, dz, 0.0)).astype(MM)
                    dvs[si].append(_dot_tn(att.astype(MM), dob))
                    dks[si].append(_dot_tn(dz, qv))
                    dq = dq + _dot(dz, kv)
                dqs.append(dq * SCALE)
            dq_ref[...] = jnp.where(first, dqs[0], dqs[1]).astype(dq_ref.dtype)
            for (_, _, _, st, w), dk, dv in zip(keys, dks, dvs):
                dk_acc[st:st + w, :] += jnp.where(first, dk[0], dk[1])
                dv_acc[st:st + w, :] += jnp.where(first, dv[0], dv[1])

        _key_plan(qi, tq, seq, seq, run)

        @pl.when(qi == nq - 1)
        def _():
            dk_ref[...] = dk_acc[...].astype(dk_ref.dtype)
            dv_ref[...] = dv_acc[...].astype(dv_ref.dtype)

    out = jax.ShapeDtypeStruct((n, LANES * pairs), MM)
    return pl.pallas_call(
        body, name=name, grid=(bsz, pairs, nq), in_specs=[qspec(SBQ), kspec(SBK), kspec(SBV), qspec(0)],
        out_specs=[qspec(0), kspec(0), kspec(0)], out_shape=[out, out, out],
        scratch_shapes=[pltpu.VMEM((tq, seq), F32), pltpu.VMEM((seq, LANES), F32), pltpu.VMEM((seq, LANES), F32)],
        compiler_params=_params(("parallel", "parallel", "arbitrary")))(proj, proj, proj, do)


def _place():
    return lax.axis_index("x"), lax.axis_index("y"), lax.axis_index("c")


def _other_chips(x, y):
    return [(1 - x, y), (x, 1 - y), (1 - x, 1 - y)]


def _remote(src, dst, send_sems, recv_sems, k, to):
    return pltpu.make_async_remote_copy(src_ref=src, dst_ref=dst, send_sem=send_sems.at[k], recv_sem=recv_sems.at[k],
                                        device_id=to, device_id_type=MESH_ID)


def gather_chips(arrs, *, name):
    na = len(arrs)

    def body(*refs):
        ins, outs = refs[:na], refs[na:2 * na]
        send_sems, recv_sems = refs[2 * na:]
        x, y, c = _place()
        me, sibling = 2 * x + y, (x, y, 1 - c)
        chips = _other_chips(x, y)
        sends = []
        for t in range(na):
            rh = ins[t].shape[0] // 2
            half = lambda chip, h, t=t, rh=rh: outs[t].at[chip, pl.ds(h * rh, rh), :]
            for j, (px, py) in enumerate(chips):
                cp = _remote(ins[t].at[pl.ds(c * rh, rh), :], half(me, c), send_sems, recv_sems, 6 * t + j, (px, py, c))
                cp.start()
                sends.append(cp)
        for t in range(na):
            rh = ins[t].shape[0] // 2
            half = lambda chip, h, t=t, rh=rh: outs[t].at[chip, pl.ds(h * rh, rh), :]
            for j, (px, py) in enumerate(chips):
                landed = half(2 * px + py, c)
                _remote(landed, landed, send_sems, recv_sems, 6 * t + j, (px, py, c)).wait_recv()
                fw = _remote(landed, landed, send_sems, recv_sems, 6 * t + 3 + j, sibling)
                fw.start()
                sends.append(fw)
        for t in range(na):
            rh = ins[t].shape[0] // 2
            half = lambda chip, h, t=t, rh=rh: outs[t].at[chip, pl.ds(h * rh, rh), :]
            for j, (px, py) in enumerate(chips):
                passed = half(2 * px + py, 1 - c)
                _remote(passed, passed, send_sems, recv_sems, 6 * t + 3 + j, sibling).wait_recv()
        for cp in sends:
            cp.wait_send()

    for a in arrs:
        assert a.ndim == 2 and a.shape[0] % 32 == 0, a.shape
    return pl.pallas_call(
        body, name=name, in_specs=[ANY] * na, out_specs=[ANY] * na,
        out_shape=[jax.ShapeDtypeStruct((4,) + a.shape, a.dtype) for a in arrs],
        scratch_shapes=[pltpu.SemaphoreType.DMA((6 * na,)), pltpu.SemaphoreType.DMA((6 * na,))],
    )(*arrs)


HBM = pl.BlockSpec(memory_space=pltpu.HBM)
SEMS = pl.BlockSpec(memory_space=pltpu.SEMAPHORE)
DATAFLOW = pltpu.SideEffectType.DATAFLOW_SIDE_EFFECTING


def _in_hbm(a):
    return pltpu.with_memory_space_constraint(a, pltpu.HBM)


def gather_start(arrs, *, name):
    na = len(arrs)

    def body(*refs):
        ins, lands = refs[:na], refs[na:2 * na]
        send_sems, recv_sems = refs[2 * na], refs[2 * na + 1]
        token = refs[-1]
        x, y, c = _place()
        me = 2 * x + y
        for t in range(na):
            for j, (px, py) in enumerate(_other_chips(x, y)):
                _remote(ins[t], lands[t].at[me], send_sems, recv_sems, 3 * t + j, (px, py, c)).start()
        token[...] = jnp.zeros_like(token)

    lands = [lax.empty((4,) + a.shape, a.dtype) for a in arrs]
    out = pl.pallas_call(
        body, name=name, in_specs=[HBM] * (2 * na),
        out_specs=[SEMS, SEMS] + [HBM] * (2 * na) + [pl.BlockSpec(memory_space=pltpu.VMEM)],
        out_shape=[pltpu.SemaphoreType.DMA((3 * na,)), pltpu.SemaphoreType.DMA((3 * na,))]
        + [pltpu.HBM(a.shape, a.dtype) for a in arrs] + [pltpu.HBM(a.shape, a.dtype) for a in lands]
        + [jax.ShapeDtypeStruct((8, LANES), F32)],
        input_output_aliases={i: 2 + i for i in range(2 * na)},
        compiler_params=pltpu.CompilerParams(has_side_effects=DATAFLOW),
    )(*[_in_hbm(a) for a in arrs], *[_in_hbm(a) for a in lands])
    return out[0], out[1], list(out[2:2 + na]), list(out[2 + na:2 + 2 * na]), out[-1]


def gather_wait(send_sems, recv_sems, arrs, lands, after, *, name):
    na = len(arrs)

    def body(*refs):
        ins, lands_ = refs[:na], refs[na:2 * na]
        send_sems_, recv_sems_ = refs[2 * na], refs[2 * na + 1]
        x, y, c = _place()
        me = 2 * x + y
        for t in range(na):
            for j, (px, py) in enumerate(_other_chips(x, y)):
                sent = _remote(ins[t], lands_[t].at[me], send_sems_, recv_sems_, 3 * t + j, (px, py, c))
                sent.wait_send()
                came = _remote(ins[t], lands_[t].at[2 * px + py], send_sems_, recv_sems_, 3 * t + j, (px, py, c))
                came.wait_recv()

    out = pl.pallas_call(
        body, name=name, in_specs=[HBM] * (2 * na) + [SEMS, SEMS, ANY], out_specs=[HBM] * (2 * na),
        out_shape=[pltpu.HBM(a.shape, a.dtype) for a in arrs] + [pltpu.HBM(a.shape, a.dtype) for a in lands],
        input_output_aliases={i: i for i in range(2 * na)},
        compiler_params=pltpu.CompilerParams(has_side_effects=DATAFLOW),
    )(*arrs, *lands, send_sems, recv_sems, after)
    return list(out[na:]), list(out[:na])


CHUNK_BYTES = 2 << 20


def _chunk_rows(rows, cols, limit):
    best = 16
    for t in range(16, rows + 1, 16):
        if rows % t == 0 and t * cols * 4 <= limit:
            best = t
    assert rows % best == 0, (rows, cols)
    return best


def pair_sum_scatter(a, place, *, name):
    _, rows, cols = a.shape
    rh = rows // 2
    tr = _chunk_rows(rh, cols, CHUNK_BYTES)
    nch = rh // tr
    steps = 4 * nch

    def body(place_ref, keep_ref, send_ref, own_ref, landed_ref, landing, out16, res, pair_send, pair_recv, credit,
             chip_send, chip_recv, local_sem):
        i, j = pl.program_id(0), pl.program_id(1)
        step = i * 4 + j
        slot = lax.rem(step, 2)
        x, y, c = _place()
        sibling = (x, y, 1 - c)
        me = 2 * x + y
        rows_i = pl.ds(pl.multiple_of(i * tr, tr), tr)

        def to_chip(p, s):
            return pltpu.make_async_remote_copy(
                src_ref=out16.at[s], dst_ref=landed_ref.at[me, rows_i, :], send_sem=chip_send.at[s], recv_sem=chip_recv.at[p - 1],
                device_id=(x ^ (p >> 1), y ^ (p & 1), c), device_id_type=MESH_ID)

        @pl.when(step >= 2)
        def _():
            pl.semaphore_wait(credit, 1)

        cp = _remote(send_ref.at[0], landing.at[slot], pair_send, pair_recv, slot, sibling)
        cp.start()
        cp.wait_recv()
        total = keep_ref[0] + landing[slot]

        for p, s, before in ((1, 0, i > 0), (2, 1, i > 0), (3, 0, None)):
            @pl.when(j == p - 1)
            def _(p=p, s=s, before=before):
                if before is None:
                    to_chip(1, s).wait_send()
                else:
                    pl.when(before)(lambda: to_chip(1, s).wait_send())
                out16[s] = total.astype(BF16)
                to_chip(p, s).start()

        @pl.when(j == 3)
        def _():
            res[...] = total
            here = pltpu.make_async_copy(res, own_ref.at[rows_i, :], local_sem)
            here.start()
            here.wait()

        cp.wait_send()

        @pl.when(step + 2 < steps)
        def _():
            pl.semaphore_signal(credit, 1, device_id=sibling, device_id_type=MESH_ID)

        @pl.when(step == steps - 1)
        def _():
            to_chip(1, 1).wait_send()
            to_chip(1, 0).wait_send()
            for p in (1, 2, 3):
                slab = landed_ref.at[me ^ p]
                pltpu.make_async_remote_copy(src_ref=slab, dst_ref=slab, send_sem=chip_send.at[0], recv_sem=chip_recv.at[p - 1],
                                             device_id=(x ^ (p >> 1), y ^ (p & 1), c), device_id_type=MESH_ID).wait_recv()

    blk = (1, tr, cols)
    slab_of = lambda j, place: place[1] ^ ((j + 1) & 3)
    grid_spec = pltpu.PrefetchScalarGridSpec(
        num_scalar_prefetch=1, grid=(nch, 4),
        in_specs=[pl.BlockSpec(blk, lambda i, j, place: (slab_of(j, place), place[0] * nch + i, 0)),
                  pl.BlockSpec(blk, lambda i, j, place: (slab_of(j, place), (1 - place[0]) * nch + i, 0))],
        out_specs=[ANY, ANY],
        scratch_shapes=[pltpu.VMEM((2, tr, cols), F32), pltpu.VMEM((2, tr, cols), BF16), pltpu.VMEM((tr, cols), F32),
                        pltpu.SemaphoreType.DMA((2,)), pltpu.SemaphoreType.DMA((2,)), pltpu.SemaphoreType.REGULAR,
                        pltpu.SemaphoreType.DMA((2,)), pltpu.SemaphoreType.DMA((3,)), pltpu.SemaphoreType.DMA])
    return pl.pallas_call(
        body, name=name, grid_spec=grid_spec,
        out_shape=[jax.ShapeDtypeStruct((rh, cols), F32), jax.ShapeDtypeStruct((4, rh, cols), BF16)],
        compiler_params=_params(("arbitrary", "arbitrary")))(place, a, a)


def chip_sum_join(own, landed, chip, *, name):
    rh, cols = own.shape
    tr = _chunk_rows(rh, cols, CHUNK_BYTES)
    nch = rh // tr

    def body(chip_ref, own_ref, l1_ref, l2_ref, l3_ref, out_ref, res, local_sem, send_sem, recv_sem):
        i = pl.program_id(0)
        x, y, c = _place()
        sibling = (x, y, 1 - c)
        res[...] = ((own_ref[...] + l1_ref[0].astype(F32)) + l2_ref[0].astype(F32)) + l3_ref[0].astype(F32)
        rows = pl.ds(pl.multiple_of(i * tr, tr), tr)
        here = pltpu.make_async_copy(res, out_ref.at[c, rows, :], local_sem)
        here.start()
        there = pltpu.make_async_remote_copy(src_ref=res, dst_ref=out_ref.at[c, rows, :], send_sem=send_sem, recv_sem=recv_sem,
                                             device_id=sibling, device_id_type=MESH_ID)
        there.start()
        here.wait()
        there.wait_send()

        @pl.when(i == nch - 1)
        def _():
            half = out_ref.at[1 - c]
            pltpu.make_async_remote_copy(src_ref=half, dst_ref=half, send_sem=send_sem, recv_sem=recv_sem,
                                         device_id=sibling, device_id_type=MESH_ID).wait_recv()

    blk = (1, tr, cols)
    slab = lambda p: pl.BlockSpec(blk, lambda i, chip: (chip[0] ^ p, i, 0))
    grid_spec = pltpu.PrefetchScalarGridSpec(
        num_scalar_prefetch=1, grid=(nch,), out_specs=ANY,
        in_specs=[pl.BlockSpec((tr, cols), lambda i, chip: (i, 0)), slab(1), slab(2), slab(3)],
        scratch_shapes=[pltpu.VMEM((tr, cols), F32), pltpu.SemaphoreType.DMA, pltpu.SemaphoreType.DMA, pltpu.SemaphoreType.DMA])
    return pl.pallas_call(
        body, name=name, grid_spec=grid_spec, out_shape=jax.ShapeDtypeStruct((2, rh, cols), F32),
        compiler_params=_params(("arbitrary",)))(chip, own, landed, landed, landed)


def all_reduce_small(a, *, name):
    def body(a_ref, o_ref, buf, send_sems, recv_sems):
        x, y, c = _place()
        me = 4 * x + 2 * y + c
        buf[me] = a_ref[...]
        sent = []
        for p in range(1, 8):
            px, py, pc = (p >> 2) & 1, (p >> 1) & 1, p & 1
            cp = _remote(a_ref, buf.at[me], send_sems, recv_sems, p - 1, (x ^ px, y ^ py, c ^ pc))
            cp.start()
            sent.append(cp)
        for p in range(1, 8):
            px, py, pc = (p >> 2) & 1, (p >> 1) & 1, p & 1
            src = 4 * (x ^ px) + 2 * (y ^ py) + (c ^ pc)
            _remote(a_ref, buf.at[src], send_sems, recv_sems, p - 1, (x ^ px, y ^ py, c ^ pc)).wait_recv()
        for cp in sent:
            cp.wait_send()
        acc = buf[0]
        for d in range(1, 8):
            acc = acc + buf[d]
        o_ref[...] = acc

    vm = pl.BlockSpec(memory_space=pltpu.VMEM)
    return pl.pallas_call(
        body, name=name, in_specs=[vm], out_specs=vm, out_shape=jax.ShapeDtypeStruct(a.shape, a.dtype),
        scratch_shapes=[pltpu.VMEM((8,) + a.shape, a.dtype), pltpu.SemaphoreType.DMA((7,)), pltpu.SemaphoreType.DMA((7,))],
    )(a)


TQ = 256


def _layer_small(sm, l):
    row = lambda v: v.reshape(1, -1)
    return dict(
        attn_norm=row(sm["attn_norm"][l]), mlp_norm=row(sm["mlp_norm"][l]),
        qgf=row(jnp.tile(sm["q_norm_fox"][l], 8)), kgf=row(jnp.tile(sm["k_norm_fox"][l], 8)),
        qgd=row(jnp.tile(sm["q_norm_dil"][l], 12)), kgd=row(jnp.tile(sm["k_norm_dil"][l], 12)),
        bfor=row(jnp.pad(sm["b_forget"][l], (0, LANES - 8))))


def _key_rows(f8, bsz, seq):
    f = f8.reshape(bsz, seq, LANES)[:, :, :8].transpose(0, 2, 1).reshape(bsz, 4, 2, seq)
    return jnp.pad(f, ((0, 0), (0, 0), (0, 6), (0, 0))).reshape(bsz * 32, seq)


def _layer_fwd(x, w, s, cos, sin, bsz, seq, l):
    nm = lambda t: f"l{l}_{t}"
    h, h_t = rmsnorm_fwd(x, s["attn_norm"], name=nm("attn_norm"))
    proj = matmul(h, w["win"], name=nm("proj"))
    qn, kn, fb, f8 = fox_prep_fwd(proj, s["qgf"], s["kgf"], s["bfor"], bsz=bsz, seq=seq, name=nm("fox_prep"))
    fk = _key_rows(f8, bsz, seq)
    oa, la = softmax_attn_fwd(qn, kn, proj, (fb, fk), qo=0, ko=0, vo=FOXV, pairs=4, bsz=bsz, seq=seq, window=seq, dilation=1,
                              tq=TQ, name=nm("fox_attn"))
    ob = sb_attn_fwd(proj, bsz=bsz, seq=seq, tq=TQ, name=nm("sb_attn"))
    qr, kr = dil_prep_fwd(proj, s["qgd"], s["kgd"], cos, sin, name=nm("dil_prep"))
    ogs, lgs = [], []
    for g, (window, dilation) in enumerate(DIL_PATTERNS):
        og, lg = softmax_attn_fwd(qr, kr, proj, None, qo=2 * g, ko=2 * g, vo=DILV + 2 * g, pairs=2, bsz=bsz, seq=seq,
                                  window=window, dilation=dilation, tq=TQ, name=nm(f"dil_attn{g}"))
        ogs.append(og)
        lgs.append(lg)
    oc = dil_combine_fwd(ogs, lgs, name=nm("dil_combine"))
    ys = [matmul(oa, w["wuf"], name=nm("up_fox")), matmul(ob, w["wus"], name=nm("up_sb")), matmul(oc, w["wud"], name=nm("up_dil"))]
    merged = merge_fwd(proj, ys, name=nm("merge"))
    x1 = matmul(merged, w["wo"], add=x, name=nm("out_proj"))
    h2, h2_t = rmsnorm_fwd(x1, s["mlp_norm"], name=nm("mlp_norm"))
    u, act = matmul(h2, w["wmi"], relu2=True, name=nm("mlp_in"))
    x2 = matmul(act, w["wmo"], add=x1, tk=2048, name=nm("mlp_out"))
    saved = dict(x=x, h_t=h_t, h2_t=h2_t, proj=proj, qn=qn, kn=kn, fb=fb, fk=fk, oa=oa, la=la, ob=ob, qr=qr, kr=kr, ogs=ogs, lgs=lgs, oc=oc,
                 ys=ys, merged=merged, x1=x1, u=u, act=act)
    return x2, saved


WIN_TILE = 256
WIN_STRIDE, WIN_TILES = 8, 9


def grad_buffers(depth, d, dff, wf, wd):
    assert dff // 4 == d
    return dict(win=lax.empty((4, depth * d, WIN_TILES * WIN_TILE), F32), ups=lax.empty((4, depth * (2 * wf + wd), d // 4), F32),
                wide=lax.empty((4, depth * (d + dff // 4 + d // 4), d), F32))


def _layer_bwd(dx2, w, s, sv, cos, sin, bsz, seq, l, depth, bufs):
    nm = lambda t: f"l{l}_{t}_bwd"
    n = bsz * seq
    proj = sv["proj"]
    d, dff = w["wmi"].shape
    wf, wd = w["wuf"].shape[0], w["wud"].shape[0]
    bufs = dict(bufs)
    rb = 512
    per_chip = dff // 4 // rb
    du = matmul(dx2, w["wmo"], tb=True, relu2_of=sv["u"], out_dtype=MM, name=nm("mlp_out_dx"))
    bufs["wide"] = matmul(sv["act"], dx2, ta=True, tm=rb, tn=d, tk=2048, name=nm("mlp_out_dw"),
                          dest=(bufs["wide"], 1, lambda j: j,
                                lambda i, j: (i // per_chip, (depth * d + l * (dff // 4)) // rb + i % per_chip, j)))
    dh2 = matmul(du, w["wmi"], tb=True, tk=2048, name=nm("mlp_in_dx"))
    bufs["wide"] = matmul(sv["h2_t"], du, tm=rb, tn=dff // 4, tk=2048, name=nm("mlp_in_dw"),
                          dest=(bufs["wide"], 4, lambda j: j, lambda i, j: (j, l * d // rb + i, 0)))
    dx1, g_mlp_norm = rmsnorm_bwd(sv["x1"], s["mlp_norm"], dh2, dx2, name=nm("mlp_norm"))

    dmerged = matmul(dx1, w["wo"], tb=True, name=nm("out_proj_dx"))
    bufs["wide"] = matmul(sv["merged"], dx1, ta=True, tm=d // 4, tn=d, tk=2048, name=nm("out_proj_dw"),
                          dest=(bufs["wide"], 1, lambda j: j, lambda i, j: (i, (depth * (d + dff // 4)) // (d // 4) + l, j)))
    dya, dyb, dyc, dga, dgb, dgc = merge_bwd(proj, sv["ys"], dmerged, name=nm("merge"))
    doa = matmul(dya, w["wuf"], tb=True, name=nm("up_fox_dx"))
    bufs["ups"] = matmul(sv["oa"], dya, ta=True, tm=wf, tn=d // 4, tk=2048, name=nm("up_fox_dw"),
                         dest=(bufs["ups"], 4, lambda j: j, lambda i, j: (j, l, 0)))
    dob = matmul(dyb, w["wus"], tb=True, name=nm("up_sb_dx"))
    bufs["ups"] = matmul(sv["ob"], dyb, ta=True, tm=wf, tn=d // 4, tk=2048, name=nm("up_sb_dw"),
                         dest=(bufs["ups"], 4, lambda j: j, lambda i, j: (j, depth + l, 0)))
    doc = matmul(dyc, w["wud"], tb=True, name=nm("up_dil_dx"))
    bufs["ups"] = matmul(sv["oc"], dyc, ta=True, tm=wd, tn=d // 4, tk=2048, name=nm("up_dil_dw"),
                         dest=(bufs["ups"], 4, lambda j: j, lambda i, j: (j, 2 * depth * wf // wd + l, 0)))

    outs = dil_combine_bwd(sv["ogs"], sv["lgs"], doc, name=nm("dil_combine"))
    dqs, dks, dvs = [], [], []
    for g, (window, dilation) in enumerate(DIL_PATTERNS):
        dq, dk, dv = softmax_attn_bwd(sv["qr"], sv["kr"], proj, sv["ogs"][g], outs[g], sv["lgs"][g], outs[3 + g], None,
                                      qo=2 * g, ko=2 * g, vo=DILV + 2 * g, pairs=2, bsz=bsz, seq=seq, window=window,
                                      dilation=dilation, tq=TQ, dq_dtype=F32, dk_dtype=F32, name=nm(f"dil_attn{g}"))
        dqs.append(dq)
        dks.append(dk)
        dvs.append(dv)
    d_dq, d_dk, g_qgd, g_kgd = dil_prep_bwd(proj, s["qgd"], s["kgd"], cos, sin, jnp.concatenate(dqs, axis=1),
                                            jnp.concatenate(dks, axis=1), name=nm("dil_prep"))

    s_dq, s_dk, s_dv = sb_attn_bwd(proj, dob, bsz=bsz, seq=seq, tq=TQ, name=nm("sb_attn"))

    dqn, dkn, f_dv, dfq, dfk = softmax_attn_bwd(sv["qn"], sv["kn"], proj, sv["oa"], doa, sv["la"], None, (sv["fb"], sv["fk"]),
                                                qo=0, ko=0, vo=FOXV, pairs=4, bsz=bsz, seq=seq, window=seq, dilation=1, tq=TQ,
                                                dq_dtype=F32, dk_dtype=F32, name=nm("fox_attn"))
    dfk8 = dfk.reshape(bsz, 4, 8, seq)[:, :, :2].reshape(bsz, 8, seq).transpose(0, 2, 1).reshape(n, 8)
    df = jnp.pad(dfq[:, ::HEAD] - dfk8, ((0, 0), (0, LANES - 8)))
    f_dq, f_dk, d_forget, g_qgf, g_kgf, g_bfor = fox_prep_bwd(proj, s["qgf"], s["kgf"], s["bfor"], dqn, dkn, df, bsz=bsz, seq=seq,
                                                              name=nm("fox_prep"))

    dproj = jnp.concatenate([f_dq, f_dk, f_dv, s_dq, s_dk, s_dv, d_dq, d_dk] + dvs + [dga, dgb, dgc, d_forget], axis=1)
    dh = matmul(dproj, w["win"], tb=True, tm=1024, tn=1024, tk=DPROJ // 4, name=nm("proj_dx"))
    bufs["win"] = matmul(sv["h_t"], dproj, tm=d, tn=WIN_TILE, tk=2048, name=nm("proj_dw"),
                         dest=(bufs["win"], 4 * WIN_TILES, lambda j: WIN_STRIDE * (j // WIN_TILES) + j % WIN_TILES,
                               lambda i, j: (j // WIN_TILES, l, j % WIN_TILES)))
    g_forget = matmul(sv["h_t"], d_forget, tk=2048, name=nm("forget_dw"))[:, :O2 - O1]
    dx, g_attn_norm = rmsnorm_bwd(sv["x"], s["attn_norm"], dh, dx1, name=nm("attn_norm"))
    gs = dict(attn_norm=g_attn_norm[0], mlp_norm=g_mlp_norm[0], b_forget=g_bfor[0, :8],
              q_norm_fox=g_qgf.reshape(8, HEAD).sum(0), k_norm_fox=g_kgf.reshape(8, HEAD).sum(0),
              q_norm_dil=g_qgd.reshape(12, HEAD).sum(0), k_norm_dil=g_kgd.reshape(12, HEAD).sum(0), w_in_forget=g_forget)
    return dx, bufs, gs


def local_step(x, positions, target, weights, small):
    bsz, seq, d = x.shape
    n = bsz * seq
    depth = len(weights)
    inv = 1.0 / (ROPE_THETA ** (jnp.arange(HEAD // 2, dtype=F32) / (HEAD // 2)))
    cos, sin = rope_table(positions.reshape(n, 1), jnp.tile(inv, 4).reshape(1, LANES), name="rope_table")
    xs = x.reshape(n, d)
    saved = []
    weights = list(weights)
    for l in range(depth):
        if callable(weights[l]):
            weights[l] = weights[l](xs)
        xs, sv = _layer_fwd(xs, weights[l], _layer_small(small, l), cos, sin, bsz, seq, l)
        saved.append(sv)
    dy, sq = loss_grad(xs, target.reshape(n, d), name="loss")
    loss = (0.5 / d) * jnp.sum(sq)
    w0 = weights[0]
    bufs = grad_buffers(depth, d, w0["wmi"].shape[1], w0["wuf"].shape[0], w0["wud"].shape[0])
    gss = [None] * depth
    for l in reversed(range(depth)):
        dy, bufs, gss[l] = _layer_bwd(dy, weights[l], _layer_small(small, l), saved[l], cos, sin, bsz, seq, l, depth, bufs)
    return loss, dy.reshape(bsz, seq, d), bufs, gss


SMALL = ("attn_norm", "mlp_norm", "b_forget", "q_norm_fox", "k_norm_fox", "q_norm_dil", "k_norm_dil")
SMALL_ROWS = 8


def _pack_small(vals):
    flat = jnp.concatenate([vals[k].reshape(-1) for k in SMALL])
    return jnp.pad(flat, (0, SMALL_ROWS * 1024 - flat.shape[0])).reshape(SMALL_ROWS, 1024)


def _unpack_small(packed, like):
    flat, out, at = packed.reshape(-1), {}, 0
    for k in SMALL:
        size = like[k].size
        out[k] = flat[at:at + size].reshape(like[k].shape)
        at += size
    return out


def kernel(x, positions, attn_norm, w_in, b_forget, q_norm_fox, k_norm_fox, q_norm_dil, k_norm_dil, w_up_fox, w_up_sb, w_up_dil, w_out, mlp_norm, w_mlp_in, w_mlp_out, loss_target, m_attn_norm, m_w_in, m_b_forget, m_q_norm_fox, m_k_norm_fox, m_q_norm_dil, m_k_norm_dil, m_w_up_fox, m_w_up_sb, m_w_up_dil, m_w_out, m_mlp_norm, m_w_mlp_in, m_w_mlp_out, v_attn_norm, v_w_in, v_b_forget, v_q_norm_fox, v_k_norm_fox, v_q_norm_dil, v_k_norm_dil, v_w_up_fox, v_w_up_sb, v_w_up_dil, v_w_out, v_mlp_norm, v_w_mlp_in, v_w_mlp_out):
    names = ("attn_norm", "w_in", "b_forget", "q_norm_fox", "k_norm_fox", "q_norm_dil", "k_norm_dil", "w_up_fox", "w_up_sb",
             "w_up_dil", "w_out", "mlp_norm", "w_mlp_in", "w_mlp_out")
    wv = dict(zip(names, (attn_norm, w_in, b_forget, q_norm_fox, k_norm_fox, q_norm_dil, k_norm_dil, w_up_fox, w_up_sb, w_up_dil,
                          w_out, mlp_norm, w_mlp_in, w_mlp_out)))
    mv = dict(zip(names, (m_attn_norm, m_w_in, m_b_forget, m_q_norm_fox, m_k_norm_fox, m_q_norm_dil, m_k_norm_dil, m_w_up_fox,
                          m_w_up_sb, m_w_up_dil, m_w_out, m_mlp_norm, m_w_mlp_in, m_w_mlp_out)))
    vv = dict(zip(names, (v_attn_norm, v_w_in, v_b_forget, v_q_norm_fox, v_k_norm_fox, v_q_norm_dil, v_k_norm_dil, v_w_up_fox,
                          v_w_up_sb, v_w_up_dil, v_w_out, v_mlp_norm, v_w_mlp_in, v_w_mlp_out)))
    depth = w_in.shape[0]
    flat2 = lambda a: a.reshape(-1, a.shape[-1])

    ups = ("w_up_fox", "w_up_sb", "w_up_dil")
    wide = ("w_mlp_in", "w_mlp_out", "w_out")
    core = lax.axis_index("c").astype(jnp.int32).reshape(1)
    chip = (2 * lax.axis_index("x") + lax.axis_index("y")).astype(jnp.int32).reshape(1)

    def shards(l):
        return [w_in[l].astype(MM), jnp.concatenate([wv[k][l] for k in ups]).astype(MM),
                jnp.concatenate([wv[k][l] for k in wide]).astype(MM)]

    def pieces(a, keys):
        out, at = {}, 0
        for k in keys:
            rows = wv[k].shape[1]
            out[k] = [a[c, at:at + rows] for c in range(4)]
            at += rows
        return out

    def layer_weights(gathered, own):
        got_in, got_up, got_wide = [lax.dynamic_update_index_in_dim(g, s, chip[0], 0) for g, s in zip(gathered, own)]
        p = pieces(got_in, ("w_in",))["w_in"]
        pad = jnp.zeros((p[0].shape[0], DPROJ - DIN), p[0].dtype)
        win = jnp.concatenate([p[0][:, :O1], p[0][:, O2:], p[1], p[2], p[3], p[0][:, O1:O2], pad], axis=1)
        up = {k: jnp.concatenate(v, axis=1) for k, v in pieces(got_up, ups).items()}
        wd = pieces(got_wide, wide)
        return dict(win=win, wuf=up["w_up_fox"], wus=up["w_up_sb"], wud=up["w_up_dil"], wo=jnp.concatenate(wd["w_out"], axis=0),
                    wmi=jnp.concatenate(wd["w_mlp_in"], axis=1), wmo=jnp.concatenate(wd["w_mlp_out"], axis=0))

    small = {k: wv[k] for k in SMALL}
    small_fwd = dict(small)
    first = shards(0)
    weights = [layer_weights(gather_chips(first, name="gather_weights"), first)]
    for l in range(1, depth):
        send_sems, recv_sems, sent, lands, token = gather_start(shards(l), name=f"gather_start{l}")
        if l == 1:
            small_fwd["attn_norm"] = small["attn_norm"] + token[0, 0]
        weights.append(lambda after, l=l, started=(send_sems, recv_sems, sent, lands): layer_weights(
            *gather_wait(*started, after, name=f"gather_wait{l}")))

    loss, grad_x, bufs, gss = local_step(x, positions, loss_target, weights, small_fwd)
    loss = lax.psum(loss, ("x", "y", "c"))

    g_small = {k: jnp.stack([gss[l][k] for l in range(depth)]) for k in SMALL}
    g_forget = jnp.stack([gss[l]["w_in_forget"] for l in range(depth)])
    summed = all_reduce_small(jnp.concatenate([_pack_small(g_small), g_forget.reshape(-1, 1024)]), name="reduce_small")
    g_small = _unpack_small(summed[:SMALL_ROWS], small)
    g_forget = summed[SMALL_ROWS:].reshape(g_forget.shape)

    parts = [bufs["win"], bufs["ups"], bufs["wide"]]
    place = jnp.concatenate([core, chip])
    sums = [pair_sum_scatter(p, place, name=f"reduce_pair_sum{t}") for t, p in enumerate(parts)]
    joined = [chip_sum_join(own, landed, chip, name=f"reduce_chip_sum{t}").reshape(-1, parts[t].shape[-1])
              for t, (own, landed) in enumerate(sums)]

    def own_w_in_columns(window):
        cols = w_in.shape[-1]
        first = jnp.concatenate([window[..., :O1], g_forget, window[..., O1:cols - (O2 - O1)]], axis=-1)
        shift = jnp.maximum((cols - WIN_STRIDE * WIN_TILE) * chip[0] - (O2 - O1), 0)
        rest = lax.dynamic_slice_in_dim(window, shift, cols, axis=2)
        return jnp.where(chip[0] == 0, first, rest)

    g_big = {"w_in": own_w_in_columns(joined[0].reshape(depth, -1, joined[0].shape[-1]))}
    for a, keys in ((joined[1], ups), (joined[2], wide)):
        at = 0
        for k in keys:
            rows = wv[k].shape[0] * wv[k].shape[1]
            g_big[k] = a[at:at + rows].reshape(wv[k].shape)
            at += rows

    grads = {**g_small, **g_big}
    delta, new_m, new_v = {}, {}, {}
    d_s, m_s, v_s = adamw(_pack_small(small), _pack_small(g_small), _pack_small({k: mv[k] for k in SMALL}),
                          _pack_small({k: vv[k] for k in SMALL}), name="adamw_small")
    delta.update(_unpack_small(d_s, small))
    new_m.update(_unpack_small(m_s, small))
    new_v.update(_unpack_small(v_s, small))
    for k in ("w_in",) + ups + wide:
        d_k, m_k, v_k = adamw(flat2(wv[k]), flat2(g_big[k]), flat2(mv[k]), flat2(vv[k]), name=f"adamw_{k}")
        delta[k], new_m[k], new_v[k] = d_k.reshape(wv[k].shape), m_k.reshape(wv[k].shape), v_k.reshape(wv[k].shape)

    return (loss, grad_x, *[grads[k] for k in names], *[delta[k] for k in names], *[new_m[k] for k in names], *[new_v[k] for k in names])
```

```python
import functools

import jax
import jax.numpy as jnp
from jax import lax
from jax.experimental import pallas as pl
from jax.experimental.pallas import tpu as pltpu

F32 = jnp.float32
BF16 = jnp.bfloat16
MM = jnp.bfloat16

HEAD = 64
LANES = 128
EPS = 1e-6
SCALE = 0.125
ROPE_THETA = 10000.0
DIL_PATTERNS = ((128, 1), (512, 4), (2048, 16))
ADAM_LR, ADAM_B1, ADAM_B2, ADAM_EPS, ADAM_WD, ADAM_STEP = 0.001, 0.9, 0.999, 1e-08, 0.01, 10

FOXQ, FOXK, FOXV = 0, 4, 8
SBQ, SBK, SBV = 12, 16, 20
DILQ, DILK, DILV = 24, 30, 36
GATE, FORGET, NBLK = 42, 66, 68
DPROJ = NBLK * LANES
O1, O2, O3, O4, DIN = 1536, 1544, 3080, 5384, 8456

VMEM_LIMIT = 56 * 1024 * 1024
MESH_ID = pl.DeviceIdType.MESH
ANY = pl.BlockSpec(memory_space=pl.ANY)


def _params(sem=None):
    return pltpu.CompilerParams(dimension_semantics=sem, vmem_limit_bytes=VMEM_LIMIT)


def _iota(shape, dim):
    return lax.broadcasted_iota(jnp.int32, shape, dim)


def _split2(x):
    hi = x.astype(BF16)
    lo = (x - hi.astype(F32)).astype(BF16)
    return hi, lo


def _split3(x):
    hi = x.astype(BF16)
    r = x - hi.astype(F32)
    mid = r.astype(BF16)
    lo = (r - mid.astype(F32)).astype(BF16)
    return hi, mid, lo


def _dot(a, b):
    return jnp.dot(a, b, preferred_element_type=F32)


def _dot_nt(a, b):
    return lax.dot_general(a, b, (((1,), (1,)), ((), ())), preferred_element_type=F32)


def _dot_tn(a, b):
    return lax.dot_general(a, b, (((0,), (0,)), ((), ())), preferred_element_type=F32)


def _xdot2(x, m):
    hi, lo = _split2(x)
    return _dot(hi, m) + _dot(lo, m)


def _xdot3(x, m):
    hi, mid, lo = _split3(x)
    return _dot(hi, m) + _dot(mid, m) + _dot(lo, m)


def _xdot3_left(m, x):
    hi, mid, lo = _split3(x)
    return _dot(m, hi) + _dot(m, mid) + _dot(m, lo)


def _head_mat(w):
    return ((_iota((w, w), 0) >> 6) == (_iota((w, w), 1) >> 6)).astype(BF16)


def _softplus_parts(z):
    e = jnp.exp(-jnp.abs(z))
    return e, jnp.maximum(z, 0.0) + jnp.log(1.0 + e)


def _fit(dim, want):
    t = min(want, dim)
    while dim % t:
        t -= LANES
        assert t > 0, (dim, want)
    return t


def matmul(a, b, *, ta=False, tb=False, out_dtype=F32, add=None, tm=2048, tn=512, tk=1024, dest=None, relu2=False,
           relu2_of=None, name):
    K, M = a.shape if ta else a.shape[::-1]
    K2, N = b.shape[::-1] if tb else b.shape
    assert K == K2, (a.shape, b.shape, ta, tb)
    tm, tn, tk = _fit(M, tm), _fit(N, tn), _fit(K, tk)
    nk = K // tk
    dn = (((0 if ta else 1,), (1 if tb else 0,)), ((), ()))
    if dest is None:
        tiles, source = N // tn, lambda j: j
    else:
        assert add is None and not tb
        buffer, tiles, source, place = dest

    extra = add if add is not None else relu2_of
    assert add is None or relu2_of is None

    def body(*refs):
        act_ref = None
        if dest is not None:
            a_ref, b_ref, _, o_ref, acc_ref = refs
        elif relu2:
            a_ref, b_ref, o_ref, act_ref, acc_ref = refs
        elif extra is None:
            a_ref, b_ref, o_ref, acc_ref = refs
        else:
            a_ref, b_ref, add_ref, o_ref, acc_ref = refs
        k = pl.program_id(2)
        part = lax.dot_general(a_ref[...].astype(MM), b_ref[...].astype(MM), dn, preferred_element_type=F32)

        @pl.when(k == 0)
        def _():
            acc_ref[...] = part

        @pl.when(k > 0)
        def _():
            acc_ref[...] += part

        @pl.when(k == nk - 1)
        def _():
            r = acc_ref[...]
            if add is not None:
                r = r + add_ref[...]
            if relu2_of is not None:
                r = r * (2.0 * jnp.maximum(add_ref[...], 0.0))
            o_ref[...] = r.astype(o_ref.dtype).reshape(o_ref.shape)
            if act_ref is not None:
                pos = jnp.maximum(r, 0.0)
                act_ref[...] = (pos * pos).astype(act_ref.dtype)

    a_spec = pl.BlockSpec((tk, tm), lambda i, j, k: (k, i)) if ta else pl.BlockSpec((tm, tk), lambda i, j, k: (i, k))
    b_spec = pl.BlockSpec((tn, tk), lambda i, j, k: (j, k)) if tb else pl.BlockSpec((tk, tn), lambda i, j, k: (k, source(j)))
    o_spec = pl.BlockSpec((tm, tn), lambda i, j, k: (i, j))
    ins, specs, aliases = [a, b], [a_spec, b_spec], {}
    out_shape = jax.ShapeDtypeStruct((M, N), out_dtype)
    if extra is not None:
        ins.append(extra)
        specs.append(o_spec)
    if relu2:
        o_spec, out_shape = [o_spec, o_spec], [out_shape, jax.ShapeDtypeStruct((M, N), MM)]
    if dest is not None:
        ins.append(buffer)
        specs.append(ANY)
        aliases = {2: 0}
        o_spec = pl.BlockSpec((1, tm, tn), lambda i, j, k: place(i, j))
        out_shape = jax.ShapeDtypeStruct(buffer.shape, buffer.dtype)
    return pl.pallas_call(
        body, name=name, grid=(M // tm, tiles, nk), in_specs=specs, out_specs=o_spec, out_shape=out_shape,
        scratch_shapes=[pltpu.VMEM((tm, tn), F32)], input_output_aliases=aliases,
        compiler_params=_params(("parallel", "parallel", "arbitrary")),
    )(*ins)


def _rows(n, want=512):
    t = min(want, n)
    assert n % t == 0, (n, t)
    return t


def rmsnorm_fwd(x, g, *, name):
    n, d = x.shape
    tr = _rows(n)

    def body(x_ref, g_ref, o_ref, t_ref):
        xv = x_ref[...]
        r = lax.rsqrt(jnp.mean(xv * xv, axis=1, keepdims=True) + EPS)
        y = xv * r * g_ref[...]
        o_ref[...] = y.astype(o_ref.dtype)
        t_ref[...] = y.T.astype(t_ref.dtype)

    row = pl.BlockSpec((tr, d), lambda i: (i, 0))
    vec = pl.BlockSpec((1, d), lambda i: (0, 0))
    return pl.pallas_call(
        body, name=name, grid=(n // tr,), in_specs=[row, vec], out_specs=[row, pl.BlockSpec((d, tr), lambda i: (0, i))],
        out_shape=[jax.ShapeDtypeStruct((n, d), MM), jax.ShapeDtypeStruct((d, n), MM)], compiler_params=_params(("parallel",)))(x, g)


def rmsnorm_bwd(x, g, dh, dres, *, name):
    n, d = x.shape
    tr = _rows(n)

    def body(x_ref, g_ref, dh_ref, dr_ref, dx_ref, dg_ref):
        @pl.when(pl.program_id(0) == 0)
        def _():
            dg_ref[...] = jnp.zeros_like(dg_ref)

        xv = x_ref[...]
        r = lax.rsqrt(jnp.mean(xv * xv, axis=1, keepdims=True) + EPS)
        y = xv * r
        dhv = dh_ref[...]
        dy = dhv * g_ref[...]
        dx_ref[...] = dr_ref[...] + r * (dy - y * jnp.mean(dy * y, axis=1, keepdims=True))
        dg_ref[...] += jnp.sum(dhv * y, axis=0, keepdims=True)

    row = pl.BlockSpec((tr, d), lambda i: (i, 0))
    vec = pl.BlockSpec((1, d), lambda i: (0, 0))
    return pl.pallas_call(
        body, name=name, grid=(n // tr,), in_specs=[row, vec, row, row], out_specs=[row, vec],
        out_shape=[jax.ShapeDtypeStruct((n, d), F32), jax.ShapeDtypeStruct((1, d), F32)],
        compiler_params=_params(("arbitrary",)))(x, g, dh, dres)


def loss_grad(y, tgt, *, name):
    n, d = y.shape
    tr = _rows(n)

    def body(y_ref, t_ref, dy_ref, acc_ref):
        @pl.when(pl.program_id(0) == 0)
        def _():
            acc_ref[...] = jnp.zeros_like(acc_ref)

        e = y_ref[...] - t_ref[...]
        dy_ref[...] = e * (1.0 / d)
        acc_ref[...] += jnp.sum(e * e, axis=0, keepdims=True)

    row = pl.BlockSpec((tr, d), lambda i: (i, 0))
    vec = pl.BlockSpec((1, d), lambda i: (0, 0))
    return pl.pallas_call(
        body, name=name, grid=(n // tr,), in_specs=[row, row], out_specs=[row, vec],
        out_shape=[jax.ShapeDtypeStruct((n, d), F32), jax.ShapeDtypeStruct((1, d), F32)],
        compiler_params=_params(("arbitrary",)))(y, tgt)


MERGE_W = 256


def _gate_specs(tr, d):
    per = d // MERGE_W
    base = GATE * LANES // MERGE_W
    return [pl.BlockSpec((tr, MERGE_W), functools.partial(lambda i, j, b: (i, base + per * b + j), b=b)) for b in range(3)]


def merge_fwd(proj, ys, *, name):
    n, d = ys[0].shape
    tr = _rows(n)

    def body(g0, g1, g2, y0, y1, y2, o_ref):
        acc = jax.nn.sigmoid(g0[...]) * y0[...]
        acc += jax.nn.sigmoid(g1[...]) * y1[...]
        acc += jax.nn.sigmoid(g2[...]) * y2[...]
        o_ref[...] = acc.astype(o_ref.dtype)

    blk = pl.BlockSpec((tr, MERGE_W), lambda i, j: (i, j))
    return pl.pallas_call(
        body, name=name, grid=(n // tr, d // MERGE_W), in_specs=_gate_specs(tr, d) + [blk] * 3, out_specs=blk,
        out_shape=jax.ShapeDtypeStruct((n, d), MM), compiler_params=_params(("parallel", "parallel")))(proj, proj, proj, *ys)


def merge_bwd(proj, ys, dm, *, name):
    n, d = dm.shape
    tr = _rows(n)

    def body(g0, g1, g2, y0, y1, y2, dm_ref, dy0, dy1, dy2, dg0, dg1, dg2):
        dmv = dm_ref[...]
        for g, y, dy, dg in ((g0, y0, dy0, dg0), (g1, y1, dy1, dg1), (g2, y2, dy2, dg2)):
            s = jax.nn.sigmoid(g[...])
            dy[...] = (dmv * s).astype(dy.dtype)
            dg[...] = (dmv * y[...] * s * (1.0 - s)).astype(dg.dtype)

    blk = pl.BlockSpec((tr, MERGE_W), lambda i, j: (i, j))
    out = jax.ShapeDtypeStruct((n, d), MM)
    return pl.pallas_call(
        body, name=name, grid=(n // tr, d // MERGE_W), in_specs=_gate_specs(tr, d) + [blk] * 4, out_specs=[blk] * 6,
        out_shape=[out] * 6, compiler_params=_params(("parallel", "parallel")))(proj, proj, proj, *ys, dm)


def adamw(w, g, m, v, *, name):
    r, c = w.shape
    tr = r
    while tr * c * 4 > (1 << 21) and tr % 16 == 0:
        tr //= 2
    c1 = 1.0 / (1.0 - ADAM_B1 ** ADAM_STEP)
    c2 = 1.0 / (1.0 - ADAM_B2 ** ADAM_STEP)

    def body(w_ref, g_ref, m_ref, v_ref, d_ref, mo_ref, vo_ref):
        gv = g_ref[...]
        m2 = ADAM_B1 * m_ref[...] + (1.0 - ADAM_B1) * gv
        v2 = ADAM_B2 * v_ref[...] + (1.0 - ADAM_B2) * (gv * gv)
        d_ref[...] = -ADAM_LR * ((m2 * c1) / (jnp.sqrt(v2 * c2) + ADAM_EPS) + ADAM_WD * w_ref[...])
        mo_ref[...] = m2
        vo_ref[...] = v2

    blk = pl.BlockSpec((tr, c), lambda i: (i, 0))
    out = jax.ShapeDtypeStruct((r, c), F32)
    return pl.pallas_call(body, name=name, grid=(r // tr,), in_specs=[blk] * 4, out_specs=[blk] * 3, out_shape=[out] * 3,
                          compiler_params=_params(("parallel",)))(w, g, m, v)


def rope_table(pos, inv, *, name):
    n = pos.shape[0]
    tr = _rows(n)

    def body(p_ref, i_ref, c_ref, s_ref):
        ang = p_ref[...].astype(F32) * i_ref[...]
        c_ref[...] = jnp.cos(ang)
        s_ref[...] = jnp.sin(ang)

    out = jax.ShapeDtypeStruct((n, LANES), F32)
    blk = pl.BlockSpec((tr, LANES), lambda i: (i, 0))
    return pl.pallas_call(
        body, name=name, grid=(n // tr,), in_specs=[pl.BlockSpec((tr, 1), lambda i: (i, 0)), pl.BlockSpec((1, LANES), lambda i: (0, 0))],
        out_specs=[blk, blk], out_shape=[out, out], compiler_params=_params(("parallel",)))(pos, inv)


def _rot_half(x):
    first = (_iota((1, LANES), 1) & 63) < 32
    return jnp.where(first, -pltpu.roll(x, LANES - 32, axis=1), pltpu.roll(x, 32, axis=1))


def _head_norm(xv, gm):
    r = lax.rsqrt(_xdot2(xv * xv, gm) * (1.0 / HEAD) + EPS)
    return r, xv * r


def _head_norm_bwd(xh, r, dxh, gm):
    return r * (dxh - xh * (_xdot2(dxh * xh, gm) * (1.0 / HEAD)))


def fox_prep_fwd(proj, qg, kg, bf, *, bsz, seq, name):
    n = bsz * seq
    tr = min(256, seq)
    nt = seq // tr
    w = 4 * LANES

    def body(q_ref, k_ref, f_ref, qg_ref, kg_ref, b_ref, qn_ref, kn_ref, fb_ref, f8_ref, carry):
        @pl.when(pl.program_id(1) == 0)
        def _():
            carry[...] = jnp.zeros_like(carry)

        gm = _head_mat(LANES)
        for src, gain, dst in ((q_ref, qg_ref, qn_ref), (k_ref, kg_ref, kn_ref)):
            for c in range(4):
                sl = slice(c * LANES, (c + 1) * LANES)
                _, xh = _head_norm(src[:, sl], gm)
                dst[:, sl] = (xh * gain[:, sl]).astype(dst.dtype)
        logf = jax.nn.log_sigmoid(f_ref[...] + b_ref[...])
        lower = (_iota((tr, tr), 1) <= _iota((tr, tr), 0)).astype(BF16)
        fcum = _xdot3_left(lower, logf) + carry[...]
        carry[...] = fcum[tr - 1:tr, :]
        f8_ref[...] = fcum
        spread = (_iota((LANES, w), 0) == (_iota((LANES, w), 1) >> 6)).astype(BF16)
        fb_ref[...] = _xdot3(fcum, spread)

    row = lambda width, blk: pl.BlockSpec((tr, width), lambda b, t: (b * nt + t, blk))
    vec = lambda width: pl.BlockSpec((1, width), lambda b, t: (0, 0))
    return pl.pallas_call(
        body, name=name, grid=(bsz, nt),
        in_specs=[row(w, FOXQ // 4), row(w, FOXK // 4), row(LANES, FORGET), vec(w), vec(w), vec(LANES)],
        out_specs=[row(w, 0), row(w, 0), row(w, 0), row(LANES, 0)],
        out_shape=[jax.ShapeDtypeStruct((n, w), MM), jax.ShapeDtypeStruct((n, w), MM),
                   jax.ShapeDtypeStruct((n, w), F32), jax.ShapeDtypeStruct((n, LANES), F32)],
        scratch_shapes=[pltpu.VMEM((1, LANES), F32)],
        compiler_params=_params(("parallel", "arbitrary")))(proj, proj, proj, qg, kg, bf)


def fox_prep_bwd(proj, qg, kg, bf, dqn, dkn, df, *, bsz, seq, name):
    n = bsz * seq
    tr = min(256, seq)
    nt = seq // tr
    w = 4 * LANES

    def body(q_ref, k_ref, f_ref, qg_ref, kg_ref, b_ref, dqn_ref, dkn_ref, df_ref,
             dq_ref, dk_ref, dl_ref, dqg_ref, dkg_ref, db_ref, carry):
        first = (pl.program_id(0) == 0) & (pl.program_id(1) == 0)

        @pl.when(first)
        def _():
            dqg_ref[...] = jnp.zeros_like(dqg_ref)
            dkg_ref[...] = jnp.zeros_like(dkg_ref)
            db_ref[...] = jnp.zeros_like(db_ref)

        @pl.when(pl.program_id(1) == 0)
        def _():
            carry[...] = jnp.zeros_like(carry)

        gm = _head_mat(LANES)
        for src, gain, dy_ref, dx_ref, dg_ref in ((q_ref, qg_ref, dqn_ref, dq_ref, dqg_ref), (k_ref, kg_ref, dkn_ref, dk_ref, dkg_ref)):
            for c in range(4):
                sl = slice(c * LANES, (c + 1) * LANES)
                r, xh = _head_norm(src[:, sl], gm)
                dy = dy_ref[:, sl]
                dg_ref[:, sl] += jnp.sum(dy * xh, axis=0, keepdims=True)
                dx_ref[:, sl] = _head_norm_bwd(xh, r, dy * gain[:, sl], gm).astype(dx_ref.dtype)
        upper = (_iota((tr, tr), 1) >= _iota((tr, tr), 0)).astype(BF16)
        dlogf = _xdot3_left(upper, df_ref[...]) + carry[...]
        carry[...] = dlogf[0:1, :]
        dlogit = dlogf * jax.nn.sigmoid(-(f_ref[...] + b_ref[...]))
        dl_ref[:, 0:LANES] = dlogit.astype(dl_ref.dtype)
        dl_ref[:, LANES:2 * LANES] = jnp.zeros((tr, LANES), dl_ref.dtype)
        db_ref[...] += jnp.sum(dlogit, axis=0, keepdims=True)

    row = lambda width, blk: pl.BlockSpec((tr, width), lambda b, t: (b * nt + nt - 1 - t, blk))
    vec = lambda width: pl.BlockSpec((1, width), lambda b, t: (0, 0))
    return pl.pallas_call(
        body, name=name, grid=(bsz, nt),
        in_specs=[row(w, FOXQ // 4), row(w, FOXK // 4), row(LANES, FORGET), vec(w), vec(w), vec(LANES),
                  row(w, 0), row(w, 0), row(LANES, 0)],
        out_specs=[row(w, 0), row(w, 0), row(2 * LANES, 0), vec(w), vec(w), vec(LANES)],
        out_shape=[jax.ShapeDtypeStruct((n, w), MM), jax.ShapeDtypeStruct((n, w), MM), jax.ShapeDtypeStruct((n, 2 * LANES), MM),
                   jax.ShapeDtypeStruct((1, w), F32), jax.ShapeDtypeStruct((1, w), F32), jax.ShapeDtypeStruct((1, LANES), F32)],
        scratch_shapes=[pltpu.VMEM((1, LANES), F32)],
        compiler_params=_params(("arbitrary", "arbitrary")))(proj, proj, proj, qg, kg, bf, dqn, dkn, df)


DIL_W = 6 * LANES


def dil_prep_fwd(proj, qg, kg, cos, sin, *, name):
    n = proj.shape[0]
    tr = _rows(n, 256)

    def body(q_ref, k_ref, qg_ref, kg_ref, c_ref, s_ref, qo_ref, ko_ref):
        gm = _head_mat(LANES)
        cv, sv = c_ref[...], s_ref[...]
        for src, gain, dst in ((q_ref, qg_ref, qo_ref), (k_ref, kg_ref, ko_ref)):
            for c in range(6):
                sl = slice(c * LANES, (c + 1) * LANES)
                _, xh = _head_norm(src[:, sl], gm)
                xn = xh * gain[:, sl]
                dst[:, sl] = (xn * cv + _rot_half(xn) * sv).astype(dst.dtype)

    row = lambda width, blk: pl.BlockSpec((tr, width), lambda i: (i, blk))
    vec = pl.BlockSpec((1, DIL_W), lambda i: (0, 0))
    out = jax.ShapeDtypeStruct((n, DIL_W), MM)
    return pl.pallas_call(
        body, name=name, grid=(n // tr,),
        in_specs=[row(DIL_W, DILQ // 6), row(DIL_W, DILK // 6), vec, vec, row(LANES, 0), row(LANES, 0)],
        out_specs=[row(DIL_W, 0), row(DIL_W, 0)], out_shape=[out, out],
        compiler_params=_params(("parallel",)))(proj, proj, qg, kg, cos, sin)


def dil_prep_bwd(proj, qg, kg, cos, sin, dqr, dkr, *, name):
    n = proj.shape[0]
    tr = _rows(n, 256)

    def body(q_ref, k_ref, qg_ref, kg_ref, c_ref, s_ref, dqr_ref, dkr_ref, dq_ref, dk_ref, dqg_ref, dkg_ref):
        @pl.when(pl.program_id(0) == 0)
        def _():
            dqg_ref[...] = jnp.zeros_like(dqg_ref)
            dkg_ref[...] = jnp.zeros_like(dkg_ref)

        gm = _head_mat(LANES)
        cv, sv = c_ref[...], s_ref[...]
        for src, gain, dy_ref, dx_ref, dg_ref in ((q_ref, qg_ref, dqr_ref, dq_ref, dqg_ref), (k_ref, kg_ref, dkr_ref, dk_ref, dkg_ref)):
            for c in range(6):
                sl = slice(c * LANES, (c + 1) * LANES)
                r, xh = _head_norm(src[:, sl], gm)
                dy = dy_ref[:, sl]
                dxn = dy * cv - _rot_half(dy * sv)
                dg_ref[:, sl] += jnp.sum(dxn * xh, axis=0, keepdims=True)
                dx_ref[:, sl] = _head_norm_bwd(xh, r, dxn * gain[:, sl], gm).astype(dx_ref.dtype)

    row = lambda width, blk: pl.BlockSpec((tr, width), lambda i: (i, blk))
    vec = pl.BlockSpec((1, DIL_W), lambda i: (0, 0))
    out = jax.ShapeDtypeStruct((n, DIL_W), MM)
    gout = jax.ShapeDtypeStruct((1, DIL_W), F32)
    return pl.pallas_call(
        body, name=name, grid=(n // tr,),
        in_specs=[row(DIL_W, DILQ // 6), row(DIL_W, DILK // 6), vec, vec, row(LANES, 0), row(LANES, 0), row(DIL_W, 0), row(DIL_W, 0)],
        out_specs=[row(DIL_W, 0), row(DIL_W, 0), vec, vec], out_shape=[out, out, gout, gout],
        compiler_params=_params(("arbitrary",)))(proj, proj, qg, kg, cos, sin, dqr, dkr)


def dil_combine_fwd(os_, lses, *, name):
    n, w = os_[0].shape
    tr = _rows(n)

    def body(o0, o1, o2, l0, l1, l2, out_ref):
        a, b, c = l0[...], l1[...], l2[...]
        m = jnp.maximum(jnp.maximum(a, b), c)
        ea, eb, ec = jnp.exp(a - m), jnp.exp(b - m), jnp.exp(c - m)
        out_ref[...] = ((ea * o0[...] + eb * o1[...] + ec * o2[...]) / (ea + eb + ec)).astype(out_ref.dtype)

    blk = pl.BlockSpec((tr, w), lambda i: (i, 0))
    return pl.pallas_call(body, name=name, grid=(n // tr,), in_specs=[blk] * 6, out_specs=blk,
                          out_shape=jax.ShapeDtypeStruct((n, w), MM), compiler_params=_params(("parallel",)))(*os_, *lses)


def dil_combine_bwd(os_, lses, dout, *, name):
    n, w = dout.shape
    tr = _rows(n)

    def body(o0, o1, o2, l0, l1, l2, d_ref, do0, do1, do2, dl0, dl1, dl2):
        a, b, c = l0[...], l1[...], l2[...]
        m = jnp.maximum(jnp.maximum(a, b), c)
        es = [jnp.exp(a - m), jnp.exp(b - m), jnp.exp(c - m)]
        inv = 1.0 / (es[0] + es[1] + es[2])
        ws = [e * inv for e in es]
        dv = d_ref[...]
        gm = _head_mat(w)
        dws = [_xdot2(dv * o[...], gm) for o in (o0, o1, o2)]
        mean = ws[0] * dws[0] + ws[1] * dws[1] + ws[2] * dws[2]
        for wg, dw, do, dl in zip(ws, dws, (do0, do1, do2), (dl0, dl1, dl2)):
            do[...] = wg * dv
            dl[...] = wg * (dw - mean)

    blk = pl.BlockSpec((tr, w), lambda i: (i, 0))
    out = jax.ShapeDtypeStruct((n, w), F32)
    return pl.pallas_call(body, name=name, grid=(n // tr,), in_specs=[blk] * 7, out_specs=[blk] * 6, out_shape=[out] * 6,
                          compiler_params=_params(("parallel",)))(*os_, *lses, dout)


def _key_plan(qi, tq, seq, window, run):
    if window + tq >= seq:
        for bi in range(seq // tq):
            lo = bi * tq
            segs = ([(0, lo, "bulk")] if lo else []) + [(lo, tq, "diag")]
            pl.when(qi == bi)(functools.partial(run, segs))
    else:
        ext = window + tq
        run([(pl.multiple_of(jnp.maximum((qi + 1) * tq - ext, 0), LANES), ext, "band")])


def _seg_mask(seg, qi, tq, window, dilation, strict=False):
    start, width, kind = seg
    d = _iota((tq, width), 0) - _iota((tq, width), 1)
    if kind == "bulk":
        d = d + width
    elif kind == "band":
        d = d + (qi * tq - start)
    ok = None
    if kind != "bulk":
        ok = (d > 0) if strict else (d >= 0)
    if window is not None:
        ok = (d <= window) if ok is None else ok & (d <= window)
    if dilation > 1:
        on_grid = (d & (dilation - 1)) == 0
        ok = on_grid if ok is None else ok & on_grid
    return ok


def _lane_first():
    return _iota((1, LANES), 1) < HEAD


def _attn_specs(bsz, seq, tq, qo, ko, vo):
    nq = seq // tq
    qspec = lambda off: pl.BlockSpec((tq, LANES), lambda b, j, i: (b * nq + i, off + j))
    kspec = lambda off: pl.BlockSpec((seq, LANES), lambda b, j, i: (b, off + j))
    return nq, qspec, kspec


def softmax_attn_fwd(q, k, v, bias, *, qo, ko, vo, pairs, bsz, seq, window, dilation, tq, name):
    n = bsz * seq
    nq, qspec, kspec = _attn_specs(bsz, seq, tq, qo, ko, vo)

    def body(*refs):
        if bias is None:
            q_ref, k_ref, v_ref, o_ref, l_ref = refs
        else:
            q_ref, k_ref, v_ref, fq_ref, fk_ref, o_ref, l_ref = refs
        qi = pl.program_id(2)

        def run(segs):
            qv = (q_ref[...] * SCALE).astype(MM)
            first = _lane_first()
            keys = [(k_ref[pl.ds(st, w), :].astype(MM), v_ref[pl.ds(st, w), :].astype(MM),
                     _seg_mask((st, w, kind), qi, tq, None if window >= seq else window, dilation), st, w)
                    for st, w, kind in segs]
            outs, lses = [], []
            for a in range(2):
                qa = jnp.where(first if a == 0 else ~first, qv, jnp.zeros_like(qv))
                scores = []
                for kv, _, ok, st, w in keys:
                    s = _dot_nt(qa, kv)
                    if bias is not None:
                        s = s + fq_ref[:, a * HEAD:a * HEAD + 1] - fk_ref[a:a + 1, pl.ds(st, w)]
                    scores.append(s if ok is None else jnp.where(ok, s, -jnp.inf))
                m = functools.reduce(jnp.maximum, [jnp.max(s, axis=1, keepdims=True) for s in scores])
                ps = [jnp.exp(s - m) for s in scores]
                den = sum(jnp.sum(p, axis=1, keepdims=True) for p in ps)
                acc = sum(_dot(p.astype(MM), vv) for p, (_, vv, _, _, _) in zip(ps, keys))
                outs.append(acc / den)
                lses.append(m + jnp.log(den))
            o_ref[...] = jnp.where(first, outs[0], outs[1]).astype(o_ref.dtype)
            l_ref[...] = jnp.where(first, lses[0], lses[1])

        _key_plan(qi, tq, seq, window, run)

    ins, specs = [q, k, v], [qspec(qo), kspec(ko), kspec(vo)]
    if bias is not None:
        ins += list(bias)
        specs += [qspec(0), pl.BlockSpec((8, seq), lambda b, j, i: (b * pairs + j, 0))]
    out = jax.ShapeDtypeStruct((n, LANES * pairs), F32)
    return pl.pallas_call(
        body, name=name, grid=(bsz, pairs, nq), in_specs=specs, out_specs=[qspec(0), qspec(0)], out_shape=[out, out],
        compiler_params=_params(("parallel", "parallel", "arbitrary")))(*ins)


def softmax_attn_bwd(q, k, v, o, do, lse, dlse, bias, *, qo, ko, vo, pairs, bsz, seq, window, dilation, tq, dq_dtype, dk_dtype, name):
    n = bsz * seq
    nq, qspec, kspec = _attn_specs(bsz, seq, tq, qo, ko, vo)
    has_bias, has_dlse = bias is not None, dlse is not None

    def body(*refs):
        refs = list(refs)
        q_ref, k_ref, v_ref, o_ref, do_ref, l_ref = refs[:6]
        del refs[:6]
        dl_ref = refs.pop(0) if has_dlse else None
        fq_ref, fk_ref = (refs.pop(0), refs.pop(0)) if has_bias else (None, None)
        dq_ref, dk_ref, dv_ref = refs[:3]
        del refs[:3]
        dfq_ref, dfk_ref = (refs.pop(0), refs.pop(0)) if has_bias else (None, None)
        dk_acc, dv_acc = refs
        qi = pl.program_id(2)

        @pl.when(qi == 0)
        def _():
            dk_acc[...] = jnp.zeros_like(dk_acc)
            dv_acc[...] = jnp.zeros_like(dv_acc)
            if has_bias:
                dfk_ref[...] = jnp.zeros_like(dfk_ref)

        def run(segs):
            qv = (q_ref[...] * SCALE).astype(MM)
            dov = do_ref[...]
            dob = dov.astype(MM)
            prod = dov * o_ref[...]
            first = _lane_first()
            keys = [(k_ref[pl.ds(st, w), :].astype(MM), v_ref[pl.ds(st, w), :].astype(MM),
                     _seg_mask((st, w, kind), qi, tq, None if window >= seq else window, dilation), st, w)
                    for st, w, kind in segs]
            dqs, dfqs = [], []
            dks, dvs = [[] for _ in keys], [[] for _ in keys]
            for a in range(2):
                mine = first if a == 0 else ~first
                col = slice(a * HEAD, a * HEAD + 1)
                delta = jnp.sum(jnp.where(mine, prod, 0.0), axis=1, keepdims=True)
                if has_dlse:
                    delta = delta - dl_ref[:, col]
                qa = jnp.where(mine, qv, jnp.zeros_like(qv))
                doa = jnp.where(mine, dob, jnp.zeros_like(dob))
                shift = l_ref[:, col]
                if has_bias:
                    shift = shift - fq_ref[:, col]
                dq, dfq = 0.0, 0.0
                for si, (kv, vv, ok, st, w) in enumerate(keys):
                    s = _dot_nt(qa, kv)
                    if has_bias:
                        s = s - fk_ref[a:a + 1, pl.ds(st, w)]
                    p = jnp.exp(s - shift)
                    if ok is not None:
                        p = jnp.where(ok, p, 0.0)
                    ds = p * (_dot_nt(doa, vv) - delta)
                    dsb = ds.astype(MM)
                    dvs[si].append(_dot_tn(p.astype(MM), dob))
                    dks[si].append(_dot_tn(dsb, qv))
                    dq = dq + _dot(dsb, kv)
                    if has_bias:
                        dfq = dfq + jnp.sum(ds, axis=1, keepdims=True)
                        dfk_ref[a:a + 1, pl.ds(st, w)] += jnp.sum(ds, axis=0, keepdims=True)
                dqs.append(dq * SCALE)
                dfqs.append(dfq)
            dq_ref[...] = jnp.where(first, dqs[0], dqs[1]).astype(dq_ref.dtype)
            for (_, _, _, st, w), dk, dv in zip(keys, dks, dvs):
                dk_acc[pl.ds(st, w), :] += jnp.where(first, dk[0], dk[1])
                dv_acc[pl.ds(st, w), :] += jnp.where(first, dv[0], dv[1])
            if has_bias:
                dfq_ref[...] = jnp.where(first, dfqs[0], dfqs[1])

        _key_plan(qi, tq, seq, window, run)

        @pl.when(qi == nq - 1)
        def _():
            dk_ref[...] = dk_acc[...].astype(dk_ref.dtype)
            dv_ref[...] = dv_acc[...].astype(dv_ref.dtype)

    wide = LANES * pairs
    ins = [q, k, v, o, do, lse]
    specs = [qspec(qo), kspec(ko), kspec(vo), qspec(0), qspec(0), qspec(0)]
    outs = [jax.ShapeDtypeStruct((n, wide), dq_dtype), jax.ShapeDtypeStruct((n, wide), dk_dtype), jax.ShapeDtypeStruct((n, wide), MM)]
    out_specs = [qspec(0), kspec(0), kspec(0)]
    if has_dlse:
        ins.append(dlse)
        specs.append(qspec(0))
    if has_bias:
        rows = pl.BlockSpec((8, seq), lambda b, j, i: (b * pairs + j, 0))
        ins += list(bias)
        specs += [qspec(0), rows]
        outs += [jax.ShapeDtypeStruct((n, wide), F32), jax.ShapeDtypeStruct((bsz * pairs * 8, seq), F32)]
        out_specs += [qspec(0), rows]
    return pl.pallas_call(
        body, name=name, grid=(bsz, pairs, nq), in_specs=specs, out_specs=out_specs, out_shape=outs,
        scratch_shapes=[pltpu.VMEM((seq, LANES), F32), pltpu.VMEM((seq, LANES), F32)],
        compiler_params=_params(("parallel", "parallel", "arbitrary")))(*ins)


def _running_sum(vals, mat, carry, lat_ref, start, reverse):
    nb = vals.shape[1] // LANES
    for cb in (reversed(range(nb)) if reverse else range(nb)):
        blk = vals[:, cb * LANES:(cb + 1) * LANES]
        lat_ref[:, start + cb * LANES:start + (cb + 1) * LANES] = _dot(blk.astype(BF16), mat) + carry
        carry = carry + jnp.sum(blk, axis=1, keepdims=True)
    return carry


def _sb_weights(qa, keys, tq, lat_ref):
    after = (_iota((LANES, LANES), 0) > _iota((LANES, LANES), 1)).astype(BF16)
    carry = jnp.zeros((tq, 1), F32)
    logs = []
    for kv, ok, st, w in reversed(keys):
        z = _dot_nt(qa, kv)
        _, sp = _softplus_parts(z)
        visible = sp if ok is None else jnp.where(ok, sp, 0.0)
        carry = _running_sum(visible, after, carry, lat_ref, st, True)
        logs.append(z - sp)
    out = []
    for (kv, ok, st, w), log_beta in zip(keys, reversed(logs)):
        att = jnp.exp(log_beta - lat_ref[:, st:st + w])
        out.append((log_beta, att if ok is None else jnp.where(ok, att, 0.0)))
    return out


def _sb_keys(k_ref, v_ref, segs, qi, tq):
    return [(k_ref[st:st + w, :].astype(MM), v_ref[st:st + w, :].astype(MM),
             _seg_mask((st, w, kind), qi, tq, None, 1, strict=True), st, w) for st, w, kind in segs]


def sb_attn_fwd(proj, *, bsz, seq, tq, name):
    n = bsz * seq
    pairs = 4
    nq, qspec, kspec = _attn_specs(bsz, seq, tq, SBQ, SBK, SBV)

    def body(q_ref, k_ref, v_ref, o_ref, lat_ref):
        qi = pl.program_id(2)

        def run(segs):
            qv = (q_ref[...] * SCALE).astype(MM)
            keys = _sb_keys(k_ref, v_ref, segs, qi, tq)
            first = _lane_first()
            outs = []
            for a in range(2):
                qa = jnp.where(first if a == 0 else ~first, qv, jnp.zeros_like(qv))
                weights = _sb_weights(qa, [(kv, ok, st, w) for kv, _, ok, st, w in keys], tq, lat_ref)
                outs.append(sum(_dot(att.astype(MM), vv) for (_, att), (_, vv, _, _, _) in zip(weights, keys)))
            o_ref[...] = jnp.where(first, outs[0], outs[1]).astype(o_ref.dtype)

        _key_plan(qi, tq, seq, seq, run)

    return pl.pallas_call(
        body, name=name, grid=(bsz, pairs, nq), in_specs=[qspec(SBQ), kspec(SBK), kspec(SBV)], out_specs=qspec(0),
        out_shape=jax.ShapeDtypeStruct((n, LANES * pairs), MM), scratch_shapes=[pltpu.VMEM((tq, seq), F32)],
        compiler_params=_params(("parallel", "parallel", "arbitrary")))(proj, proj, proj)


def sb_attn_bwd(proj, do, *, bsz, seq, tq, name):
    n = bsz * seq
    pairs = 4
    nq, qspec, kspec = _attn_specs(bsz, seq, tq, SBQ, SBK, SBV)

    def body(q_ref, k_ref, v_ref, do_ref, dq_ref, dk_ref, dv_ref, lat_ref, dk_acc, dv_acc):
        qi = pl.program_id(2)

        @pl.when(qi == 0)
        def _():
            dk_acc[...] = jnp.zeros_like(dk_acc)
            dv_acc[...] = jnp.zeros_like(dv_acc)

        def run(segs):
            qv = (q_ref[...] * SCALE).astype(MM)
            keys = _sb_keys(k_ref, v_ref, segs, qi, tq)
            dob = do_ref[...].astype(MM)
            first = _lane_first()
            before = (_iota((LANES, LANES), 0) < _iota((LANES, LANES), 1)).astype(BF16)
            dqs = []
            dks, dvs = [[] for _ in keys], [[] for _ in keys]
            for a in range(2):
                mine = first if a == 0 else ~first
                qa = jnp.where(mine, qv, jnp.zeros_like(qv))
                doa = jnp.where(mine, dob, jnp.zeros_like(dob))
                weights = _sb_weights(qa, [(kv, ok, st, w) for kv, _, ok, st, w in keys], tq, lat_ref)
                gs = [_dot_nt(doa, vv) * att for (_, att), (_, vv, _, _, _) in zip(weights, keys)]
                carry = jnp.zeros((tq, 1), F32)
                for g, (_, _, _, st, w) in zip(gs, keys):
                    carry = _running_sum(g, before, carry, lat_ref, st, False)
                dq = 0.0
                for si, ((log_beta, att), g, (kv, _, ok, st, w)) in enumerate(zip(weights, gs, keys)):
                    dz = g - jnp.exp(log_beta) * (g + lat_ref[:, st:st + w])
                    dz = (dz if ok is None else jnp.where(ok, dz, 0.0)).astype(MM)
                    dvs[si].append(_dot_tn(att.astype(MM), dob))
                    dks[si].append(_dot_tn(dz, qv))
                    dq = dq + _dot(dz, kv)
                dqs.append(dq * SCALE)
            dq_ref[...] = jnp.where(first, dqs[0], dqs[1]).astype(dq_ref.dtype)
            for (_, _, _, st, w), dk, dv in zip(keys, dks, dvs):
                dk_acc[st:st + w, :] += jnp.where(first, dk[0], dk[1])
                dv_acc[st:st + w, :] += jnp.where(first, dv[0], dv[1])

        _key_plan(qi, tq, seq, seq, run)

        @pl.when(qi == nq - 1)
        def _():
            dk_ref[...] = dk_acc[...].astype(dk_ref.dtype)
            dv_ref[...] = dv_acc[...].astype(dv_ref.dtype)

    out = jax.ShapeDtypeStruct((n, LANES * pairs), MM)
    return pl.pallas_call(
        body, name=name, grid=(bsz, pairs, nq), in_specs=[qspec(SBQ), kspec(SBK), kspec(SBV), qspec(0)],
        out_specs=[qspec(0), kspec(0), kspec(0)], out_shape=[out, out, out],
        scratch_shapes=[pltpu.VMEM((tq, seq), F32), pltpu.VMEM((seq, LANES), F32), pltpu.VMEM((seq, LANES), F32)],
        compiler_params=_params(("parallel", "parallel", "arbitrary")))(proj, proj, proj, do)


def _place():
    return lax.axis_index("x"), lax.axis_index("y"), lax.axis_index("c")


def _other_chips(x, y):
    return [(1 - x, y), (x, 1 - y), (1 - x, 1 - y)]


def _remote(src, dst, send_sems, recv_sems, k, to):
    return pltpu.make_async_remote_copy(src_ref=src, dst_ref=dst, send_sem=send_sems.at[k], recv_sem=recv_sems.at[k],
                                        device_id=to, device_id_type=MESH_ID)


def gather_chips(arrs, *, name):
    na = len(arrs)

    def body(*refs):
        ins, outs = refs[:na], refs[na:2 * na]
        send_sems, recv_sems = refs[2 * na:]
        x, y, c = _place()
        me, sibling = 2 * x + y, (x, y, 1 - c)
        chips = _other_chips(x, y)
        sends = []
        for t in range(na):
            rh = ins[t].shape[0] // 2
            half = lambda chip, h, t=t, rh=rh: outs[t].at[chip, pl.ds(h * rh, rh), :]
            for j, (px, py) in enumerate(chips):
                cp = _remote(ins[t].at[pl.ds(c * rh, rh), :], half(me, c), send_sems, recv_sems, 6 * t + j, (px, py, c))
                cp.start()
                sends.append(cp)
        for t in range(na):
            rh = ins[t].shape[0] // 2
            half = lambda chip, h, t=t, rh=rh: outs[t].at[chip, pl.ds(h * rh, rh), :]
            for j, (px, py) in enumerate(chips):
                landed = half(2 * px + py, c)
                _remote(landed, landed, send_sems, recv_sems, 6 * t + j, (px, py, c)).wait_recv()
                fw = _remote(landed, landed, send_sems, recv_sems, 6 * t + 3 + j, sibling)
                fw.start()
                sends.append(fw)
        for t in range(na):
            rh = ins[t].shape[0] // 2
            half = lambda chip, h, t=t, rh=rh: outs[t].at[chip, pl.ds(h * rh, rh), :]
            for j, (px, py) in enumerate(chips):
                passed = half(2 * px + py, 1 - c)
                _remote(passed, passed, send_sems, recv_sems, 6 * t + 3 + j, sibling).wait_recv()
        for cp in sends:
            cp.wait_send()

    for a in arrs:
        assert a.ndim == 2 and a.shape[0] % 32 == 0, a.shape
    return pl.pallas_call(
        body, name=name, in_specs=[ANY] * na, out_specs=[ANY] * na,
        out_shape=[jax.ShapeDtypeStruct((4,) + a.shape, a.dtype) for a in arrs],
        scratch_shapes=[pltpu.SemaphoreType.DMA((6 * na,)), pltpu.SemaphoreType.DMA((6 * na,))],
    )(*arrs)


HBM = pl.BlockSpec(memory_space=pltpu.HBM)
SEMS = pl.BlockSpec(memory_space=pltpu.SEMAPHORE)
DATAFLOW = pltpu.SideEffectType.DATAFLOW_SIDE_EFFECTING


def _in_hbm(a):
    return pltpu.with_memory_space_constraint(a, pltpu.HBM)


def gather_start(arrs, *, name):
    na = len(arrs)

    def body(*refs):
        ins, lands = refs[:na], refs[na:2 * na]
        send_sems, recv_sems = refs[2 * na], refs[2 * na + 1]
        token = refs[-1]
        x, y, c = _place()
        me = 2 * x + y
        for t in range(na):
            for j, (px, py) in enumerate(_other_chips(x, y)):
                _remote(ins[t], lands[t].at[me], send_sems, recv_sems, 3 * t + j, (px, py, c)).start()
        token[...] = jnp.zeros_like(token)

    lands = [lax.empty((4,) + a.shape, a.dtype) for a in arrs]
    out = pl.pallas_call(
        body, name=name, in_specs=[HBM] * (2 * na),
        out_specs=[SEMS, SEMS] + [HBM] * (2 * na) + [pl.BlockSpec(memory_space=pltpu.VMEM)],
        out_shape=[pltpu.SemaphoreType.DMA((3 * na,)), pltpu.SemaphoreType.DMA((3 * na,))]
        + [pltpu.HBM(a.shape, a.dtype) for a in arrs] + [pltpu.HBM(a.shape, a.dtype) for a in lands]
        + [jax.ShapeDtypeStruct((8, LANES), F32)],
        input_output_aliases={i: 2 + i for i in range(2 * na)},
        compiler_params=pltpu.CompilerParams(has_side_effects=DATAFLOW),
    )(*[_in_hbm(a) for a in arrs], *[_in_hbm(a) for a in lands])
    return out[0], out[1], list(out[2:2 + na]), list(out[2 + na:2 + 2 * na]), out[-1]


def gather_wait(send_sems, recv_sems, arrs, lands, after, *, name):
    na = len(arrs)

    def body(*refs):
        ins, lands_ = refs[:na], refs[na:2 * na]
        send_sems_, recv_sems_ = refs[2 * na], refs[2 * na + 1]
        x, y, c = _place()
        me = 2 * x + y
        for t in range(na):
            for j, (px, py) in enumerate(_other_chips(x, y)):
                sent = _remote(ins[t], lands_[t].at[me], send_sems_, recv_sems_, 3 * t + j, (px, py, c))
                sent.wait_send()
                came = _remote(ins[t], lands_[t].at[2 * px + py], send_sems_, recv_sems_, 3 * t + j, (px, py, c))
                came.wait_recv()

    out = pl.pallas_call(
        body, name=name, in_specs=[HBM] * (2 * na) + [SEMS, SEMS, ANY], out_specs=[HBM] * (2 * na),
        out_shape=[pltpu.HBM(a.shape, a.dtype) for a in arrs] + [pltpu.HBM(a.shape, a.dtype) for a in lands],
        input_output_aliases={i: i for i in range(2 * na)},
        compiler_params=pltpu.CompilerParams(has_side_effects=DATAFLOW),
    )(*arrs, *lands, send_sems, recv_sems, after)
    return list(out[na:]), list(out[:na])


CHUNK_BYTES = 2 << 20


def _chunk_rows(rows, cols, limit):
    best = 16
    for t in range(16, rows + 1, 16):
        if rows % t == 0 and t * cols * 4 <= limit:
            best = t
    assert rows % best == 0, (rows, cols)
    return best


def pair_sum_scatter(a, place, *, name):
    _, rows, cols = a.shape
    rh = rows // 2
    tr = _chunk_rows(rh, cols, CHUNK_BYTES)
    nch = rh // tr
    steps = 4 * nch

    def body(place_ref, keep_ref, send_ref, own_ref, landed_ref, landing, out16, res, pair_send, pair_recv, credit,
             chip_send, chip_recv, local_sem):
        i, j = pl.program_id(0), pl.program_id(1)
        step = i * 4 + j
        slot = lax.rem(step, 2)
        x, y, c = _place()
        sibling = (x, y, 1 - c)
        me = 2 * x + y
        rows_i = pl.ds(pl.multiple_of(i * tr, tr), tr)

        def to_chip(p, s):
            return pltpu.make_async_remote_copy(
                src_ref=out16.at[s], dst_ref=landed_ref.at[me, rows_i, :], send_sem=chip_send.at[s], recv_sem=chip_recv.at[p - 1],
                device_id=(x ^ (p >> 1), y ^ (p & 1), c), device_id_type=MESH_ID)

        @pl.when(step >= 2)
        def _():
            pl.semaphore_wait(credit, 1)

        cp = _remote(send_ref.at[0], landing.at[slot], pair_send, pair_recv, slot, sibling)
        cp.start()
        cp.wait_recv()
        total = keep_ref[0] + landing[slot]

        for p, s, before in ((1, 0, i > 0), (2, 1, i > 0), (3, 0, None)):
            @pl.when(j == p - 1)
            def _(p=p, s=s, before=before):
                if before is None:
                    to_chip(1, s).wait_send()
                else:
                    pl.when(before)(lambda: to_chip(1, s).wait_send())
                out16[s] = total.astype(BF16)
                to_chip(p, s).start()

        @pl.when(j == 3)
        def _():
            res[...] = total
            here = pltpu.make_async_copy(res, own_ref.at[rows_i, :], local_sem)
            here.start()
            here.wait()

        cp.wait_send()

        @pl.when(step + 2 < steps)
        def _():
            pl.semaphore_signal(credit, 1, device_id=sibling, device_id_type=MESH_ID)

        @pl.when(step == steps - 1)
        def _():
            to_chip(1, 1).wait_send()
            to_chip(1, 0).wait_send()
            for p in (1, 2, 3):
                slab = landed_ref.at[me ^ p]
                pltpu.make_async_remote_copy(src_ref=slab, dst_ref=slab, send_sem=chip_send.at[0], recv_sem=chip_recv.at[p - 1],
                                             device_id=(x ^ (p >> 1), y ^ (p & 1), c), device_id_type=MESH_ID).wait_recv()

    blk = (1, tr, cols)
    slab_of = lambda j, place: place[1] ^ ((j + 1) & 3)
    grid_spec = pltpu.PrefetchScalarGridSpec(
        num_scalar_prefetch=1, grid=(nch, 4),
        in_specs=[pl.BlockSpec(blk, lambda i, j, place: (slab_of(j, place), place[0] * nch + i, 0)),
                  pl.BlockSpec(blk, lambda i, j, place: (slab_of(j, place), (1 - place[0]) * nch + i, 0))],
        out_specs=[ANY, ANY],
        scratch_shapes=[pltpu.VMEM((2, tr, cols), F32), pltpu.VMEM((2, tr, cols), BF16), pltpu.VMEM((tr, cols), F32),
                        pltpu.SemaphoreType.DMA((2,)), pltpu.SemaphoreType.DMA((2,)), pltpu.SemaphoreType.REGULAR,
                        pltpu.SemaphoreType.DMA((2,)), pltpu.SemaphoreType.DMA((3,)), pltpu.SemaphoreType.DMA])
    return pl.pallas_call(
        body, name=name, grid_spec=grid_spec,
        out_shape=[jax.ShapeDtypeStruct((rh, cols), F32), jax.ShapeDtypeStruct((4, rh, cols), BF16)],
        compiler_params=_params(("arbitrary", "arbitrary")))(place, a, a)


def chip_sum_join(own, landed, chip, *, name):
    rh, cols = own.shape
    tr = _chunk_rows(rh, cols, CHUNK_BYTES)
    nch = rh // tr

    def body(chip_ref, own_ref, l1_ref, l2_ref, l3_ref, out_ref, res, local_sem, send_sem, recv_sem):
        i = pl.program_id(0)
        x, y, c = _place()
        sibling = (x, y, 1 - c)
        res[...] = ((own_ref[...] + l1_ref[0].astype(F32)) + l2_ref[0].astype(F32)) + l3_ref[0].astype(F32)
        rows = pl.ds(pl.multiple_of(i * tr, tr), tr)
        here = pltpu.make_async_copy(res, out_ref.at[c, rows, :], local_sem)
        here.start()
        there = pltpu.make_async_remote_copy(src_ref=res, dst_ref=out_ref.at[c, rows, :], send_sem=send_sem, recv_sem=recv_sem,
                                             device_id=sibling, device_id_type=MESH_ID)
        there.start()
        here.wait()
        there.wait_send()

        @pl.when(i == nch - 1)
        def _():
            half = out_ref.at[1 - c]
            pltpu.make_async_remote_copy(src_ref=half, dst_ref=half, send_sem=send_sem, recv_sem=recv_sem,
                                         device_id=sibling, device_id_type=MESH_ID).wait_recv()

    blk = (1, tr, cols)
    slab = lambda p: pl.BlockSpec(blk, lambda i, chip: (chip[0] ^ p, i, 0))
    grid_spec = pltpu.PrefetchScalarGridSpec(
        num_scalar_prefetch=1, grid=(nch,), out_specs=ANY,
        in_specs=[pl.BlockSpec((tr, cols), lambda i, chip: (i, 0)), slab(1), slab(2), slab(3)],
        scratch_shapes=[pltpu.VMEM((tr, cols), F32), pltpu.SemaphoreType.DMA, pltpu.SemaphoreType.DMA, pltpu.SemaphoreType.DMA])
    return pl.pallas_call(
        body, name=name, grid_spec=grid_spec, out_shape=jax.ShapeDtypeStruct((2, rh, cols), F32),
        compiler_params=_params(("arbitrary",)))(chip, own, landed, landed, landed)


def all_reduce_small(a, *, name):
    def body(a_ref, o_ref, buf, send_sems, recv_sems):
        x, y, c = _place()
        me = 4 * x + 2 * y + c
        buf[me] = a_ref[...]
        sent = []
        for p in range(1, 8):
            px, py, pc = (p >> 2) & 1, (p >> 1) & 1, p & 1
            cp = _remote(a_ref, buf.at[me], send_sems, recv_sems, p - 1, (x ^ px, y ^ py, c ^ pc))
            cp.start()
            sent.append(cp)
        for p in range(1, 8):
            px, py, pc = (p >> 2) & 1, (p >> 1) & 1, p & 1
            src = 4 * (x ^ px) + 2 * (y ^ py) + (c ^ pc)
            _remote(a_ref, buf.at[src], send_sems, recv_sems, p - 1, (x ^ px, y ^ py, c ^ pc)).wait_recv()
        for cp in sent:
            cp.wait_send()
        acc = buf[0]
        for d in range(1, 8):
            acc = acc + buf[d]
        o_ref[...] = acc

    vm = pl.BlockSpec(memory_space=pltpu.VMEM)
    return pl.pallas_call(
        body, name=name, in_specs=[vm], out_specs=vm, out_shape=jax.ShapeDtypeStruct(a.shape, a.dtype),
        scratch_shapes=[pltpu.VMEM((8,) + a.shape, a.dtype), pltpu.SemaphoreType.DMA((7,)), pltpu.SemaphoreType.DMA((7,))],
    )(a)


TQ = 256


def _layer_small(sm, l):
    row = lambda v: v.reshape(1, -1)
    return dict(
        attn_norm=row(sm["attn_norm"][l]), mlp_norm=row(sm["mlp_norm"][l]),
        qgf=row(jnp.tile(sm["q_norm_fox"][l], 8)), kgf=row(jnp.tile(sm["k_norm_fox"][l], 8)),
        qgd=row(jnp.tile(sm["q_norm_dil"][l], 12)), kgd=row(jnp.tile(sm["k_norm_dil"][l], 12)),
        bfor=row(jnp.pad(sm["b_forget"][l], (0, LANES - 8))))


def _key_rows(f8, bsz, seq):
    f = f8.reshape(bsz, seq, LANES)[:, :, :8].transpose(0, 2, 1).reshape(bsz, 4, 2, seq)
    return jnp.pad(f, ((0, 0), (0, 0), (0, 6), (0, 0))).reshape(bsz * 32, seq)


def _layer_fwd(x, w, s, cos, sin, bsz, seq, l):
    nm = lambda t: f"l{l}_{t}"
    h, h_t = rmsnorm_fwd(x, s["attn_norm"], name=nm("attn_norm"))
    proj = matmul(h, w["win"], name=nm("proj"))
    qn, kn, fb, f8 = fox_prep_fwd(proj, s["qgf"], s["kgf"], s["bfor"], bsz=bsz, seq=seq, name=nm("fox_prep"))
    fk = _key_rows(f8, bsz, seq)
    oa, la = softmax_attn_fwd(qn, kn, proj, (fb, fk), qo=0, ko=0, vo=FOXV, pairs=4, bsz=bsz, seq=seq, window=seq, dilation=1,
                              tq=TQ, name=nm("fox_attn"))
    ob = sb_attn_fwd(proj, bsz=bsz, seq=seq, tq=TQ, name=nm("sb_attn"))
    qr, kr = dil_prep_fwd(proj, s["qgd"], s["kgd"], cos, sin, name=nm("dil_prep"))
    ogs, lgs = [], []
    for g, (window, dilation) in enumerate(DIL_PATTERNS):
        og, lg = softmax_attn_fwd(qr, kr, proj, None, qo=2 * g, ko=2 * g, vo=DILV + 2 * g, pairs=2, bsz=bsz, seq=seq,
                                  window=window, dilation=dilation, tq=TQ, name=nm(f"dil_attn{g}"))
        ogs.append(og)
        lgs.append(lg)
    oc = dil_combine_fwd(ogs, lgs, name=nm("dil_combine"))
    ys = [matmul(oa, w["wuf"], name=nm("up_fox")), matmul(ob, w["wus"], name=nm("up_sb")), matmul(oc, w["wud"], name=nm("up_dil"))]
    merged = merge_fwd(proj, ys, name=nm("merge"))
    if "late" in w:
        w.update(w.pop("late")(merged))
    x1 = matmul(merged, w["wo"], add=x, name=nm("out_proj"))
    h2, h2_t = rmsnorm_fwd(x1, s["mlp_norm"], name=nm("mlp_norm"))
    u, act = matmul(h2, w["wmi"], relu2=True, name=nm("mlp_in"))
    x2 = matmul(act, w["wmo"], add=x1, tk=2048, name=nm("mlp_out"))
    saved = dict(x=x, h_t=h_t, h2_t=h2_t, proj=proj, qn=qn, kn=kn, fb=fb, fk=fk, oa=oa, la=la, ob=ob, qr=qr, kr=kr, ogs=ogs, lgs=lgs, oc=oc,
                 ys=ys, merged=merged, x1=x1, u=u, act=act)
    return x2, saved


WIN_TILE = 256
WIN_STRIDE, WIN_TILES = 8, 9


def grad_buffers(depth, d, dff, wf, wd):
    assert dff // 4 == d
    return dict(win=lax.empty((4, depth * d, WIN_TILES * WIN_TILE), F32), ups=lax.empty((4, depth * (2 * wf + wd), d // 4), F32),
                wide=lax.empty((4, depth * (d + dff // 4 + d // 4), d), F32))


def _layer_bwd(dx2, w, s, sv, cos, sin, bsz, seq, l, depth, bufs):
    nm = lambda t: f"l{l}_{t}_bwd"
    n = bsz * seq
    proj = sv["proj"]
    d, dff = w["wmi"].shape
    wf, wd = w["wuf"].shape[0], w["wud"].shape[0]
    bufs = dict(bufs)
    rb = 512
    per_chip = dff // 4 // rb
    du = matmul(dx2, w["wmo"], tb=True, relu2_of=sv["u"], out_dtype=MM, name=nm("mlp_out_dx"))
    bufs["wide"] = matmul(sv["act"], dx2, ta=True, tm=rb, tn=d, tk=2048, name=nm("mlp_out_dw"),
                          dest=(bufs["wide"], 1, lambda j: j,
                                lambda i, j: (i // per_chip, (depth * d + l * (dff // 4)) // rb + i % per_chip, j)))
    dh2 = matmul(du, w["wmi"], tb=True, tk=2048, name=nm("mlp_in_dx"))
    bufs["wide"] = matmul(sv["h2_t"], du, tm=rb, tn=dff // 4, tk=2048, name=nm("mlp_in_dw"),
                          dest=(bufs["wide"], 4, lambda j: j, lambda i, j: (j, l * d // rb + i, 0)))
    dx1, g_mlp_norm = rmsnorm_bwd(sv["x1"], s["mlp_norm"], dh2, dx2, name=nm("mlp_norm"))

    dmerged = matmul(dx1, w["wo"], tb=True, name=nm("out_proj_dx"))
    bufs["wide"] = matmul(sv["merged"], dx1, ta=True, tm=d // 4, tn=d, tk=2048, name=nm("out_proj_dw"),
                          dest=(bufs["wide"], 1, lambda j: j, lambda i, j: (i, (depth * (d + dff // 4)) // (d // 4) + l, j)))
    dya, dyb, dyc, dga, dgb, dgc = merge_bwd(proj, sv["ys"], dmerged, name=nm("merge"))
    doa = matmul(dya, w["wuf"], tb=True, name=nm("up_fox_dx"))
    bufs["ups"] = matmul(sv["oa"], dya, ta=True, tm=wf, tn=d // 4, tk=2048, name=nm("up_fox_dw"),
                         dest=(bufs["ups"], 4, lambda j: j, lambda i, j: (j, l, 0)))
    dob = matmul(dyb, w["wus"], tb=True, name=nm("up_sb_dx"))
    bufs["ups"] = matmul(sv["ob"], dyb, ta=True, tm=wf, tn=d // 4, tk=2048, name=nm("up_sb_dw"),
                         dest=(bufs["ups"], 4, lambda j: j, lambda i, j: (j, depth + l, 0)))
    doc = matmul(dyc, w["wud"], tb=True, name=nm("up_dil_dx"))
    bufs["ups"] = matmul(sv["oc"], dyc, ta=True, tm=wd, tn=d // 4, tk=2048, name=nm("up_dil_dw"),
                         dest=(bufs["ups"], 4, lambda j: j, lambda i, j: (j, 2 * depth * wf // wd + l, 0)))

    outs = dil_combine_bwd(sv["ogs"], sv["lgs"], doc, name=nm("dil_combine"))
    dqs, dks, dvs = [], [], []
    for g, (window, dilation) in enumerate(DIL_PATTERNS):
        dq, dk, dv = softmax_attn_bwd(sv["qr"], sv["kr"], proj, sv["ogs"][g], outs[g], sv["lgs"][g], outs[3 + g], None,
                                      qo=2 * g, ko=2 * g, vo=DILV + 2 * g, pairs=2, bsz=bsz, seq=seq, window=window,
                                      dilation=dilation, tq=TQ, dq_dtype=F32, dk_dtype=F32, name=nm(f"dil_attn{g}"))
        dqs.append(dq)
        dks.append(dk)
        dvs.append(dv)
    d_dq, d_dk, g_qgd, g_kgd = dil_prep_bwd(proj, s["qgd"], s["kgd"], cos, sin, jnp.concatenate(dqs, axis=1),
                                            jnp.concatenate(dks, axis=1), name=nm("dil_prep"))

    s_dq, s_dk, s_dv = sb_attn_bwd(proj, dob, bsz=bsz, seq=seq, tq=TQ, name=nm("sb_attn"))

    dqn, dkn, f_dv, dfq, dfk = softmax_attn_bwd(sv["qn"], sv["kn"], proj, sv["oa"], doa, sv["la"], None, (sv["fb"], sv["fk"]),
                                                qo=0, ko=0, vo=FOXV, pairs=4, bsz=bsz, seq=seq, window=seq, dilation=1, tq=TQ,
                                                dq_dtype=F32, dk_dtype=F32, name=nm("fox_attn"))
    dfk8 = dfk.reshape(bsz, 4, 8, seq)[:, :, :2].reshape(bsz, 8, seq).transpose(0, 2, 1).reshape(n, 8)
    df = jnp.pad(dfq[:, ::HEAD] - dfk8, ((0, 0), (0, LANES - 8)))
    f_dq, f_dk, d_forget, g_qgf, g_kgf, g_bfor = fox_prep_bwd(proj, s["qgf"], s["kgf"], s["bfor"], dqn, dkn, df, bsz=bsz, seq=seq,
                                                              name=nm("fox_prep"))

    dproj = jnp.concatenate([f_dq, f_dk, f_dv, s_dq, s_dk, s_dv, d_dq, d_dk] + dvs + [dga, dgb, dgc, d_forget], axis=1)
    dh = matmul(dproj, w["win"], tb=True, tm=1024, tn=1024, tk=DPROJ // 4, name=nm("proj_dx"))
    bufs["win"] = matmul(sv["h_t"], dproj, tm=d, tn=WIN_TILE, tk=2048, name=nm("proj_dw"),
                         dest=(bufs["win"], 4 * WIN_TILES, lambda j: WIN_STRIDE * (j // WIN_TILES) + j % WIN_TILES,
                               lambda i, j: (j // WIN_TILES, l, j % WIN_TILES)))
    g_forget = matmul(sv["h_t"], d_forget, tk=2048, name=nm("forget_dw"))[:, :O2 - O1]
    dx, g_attn_norm = rmsnorm_bwd(sv["x"], s["attn_norm"], dh, dx1, name=nm("attn_norm"))
    gs = dict(attn_norm=g_attn_norm[0], mlp_norm=g_mlp_norm[0], b_forget=g_bfor[0, :8],
              q_norm_fox=g_qgf.reshape(8, HEAD).sum(0), k_norm_fox=g_kgf.reshape(8, HEAD).sum(0),
              q_norm_dil=g_qgd.reshape(12, HEAD).sum(0), k_norm_dil=g_kgd.reshape(12, HEAD).sum(0), w_in_forget=g_forget)
    return dx, bufs, gs


def local_step(x, positions, target, weights, small):
    bsz, seq, d = x.shape
    n = bsz * seq
    depth = len(weights)
    inv = 1.0 / (ROPE_THETA ** (jnp.arange(HEAD // 2, dtype=F32) / (HEAD // 2)))
    cos, sin = rope_table(positions.reshape(n, 1), jnp.tile(inv, 4).reshape(1, LANES), name="rope_table")
    xs = x.reshape(n, d)
    saved = []
    weights = list(weights)
    for l in range(depth):
        if callable(weights[l]):
            weights[l] = weights[l](xs)
        xs, sv = _layer_fwd(xs, weights[l], _layer_small(small, l), cos, sin, bsz, seq, l)
        saved.append(sv)
    dy, sq = loss_grad(xs, target.reshape(n, d), name="loss")
    loss = (0.5 / d) * jnp.sum(sq)
    w0 = weights[0]
    bufs = grad_buffers(depth, d, w0["wmi"].shape[1], w0["wuf"].shape[0], w0["wud"].shape[0])
    gss = [None] * depth
    for l in reversed(range(depth)):
        dy, bufs, gss[l] = _layer_bwd(dy, weights[l], _layer_small(small, l), saved[l], cos, sin, bsz, seq, l, depth, bufs)
    return loss, dy.reshape(bsz, seq, d), bufs, gss


SMALL = ("attn_norm", "mlp_norm", "b_forget", "q_norm_fox", "k_norm_fox", "q_norm_dil", "k_norm_dil")
SMALL_ROWS = 8


def _pack_small(vals):
    flat = jnp.concatenate([vals[k].reshape(-1) for k in SMALL])
    return jnp.pad(flat, (0, SMALL_ROWS * 1024 - flat.shape[0])).reshape(SMALL_ROWS, 1024)


def _unpack_small(packed, like):
    flat, out, at = packed.reshape(-1), {}, 0
    for k in SMALL:
        size = like[k].size
        out[k] = flat[at:at + size].reshape(like[k].shape)
        at += size
    return out


def kernel(x, positions, attn_norm, w_in, b_forget, q_norm_fox, k_norm_fox, q_norm_dil, k_norm_dil, w_up_fox, w_up_sb, w_up_dil, w_out, mlp_norm, w_mlp_in, w_mlp_out, loss_target, m_attn_norm, m_w_in, m_b_forget, m_q_norm_fox, m_k_norm_fox, m_q_norm_dil, m_k_norm_dil, m_w_up_fox, m_w_up_sb, m_w_up_dil, m_w_out, m_mlp_norm, m_w_mlp_in, m_w_mlp_out, v_attn_norm, v_w_in, v_b_forget, v_q_norm_fox, v_k_norm_fox, v_q_norm_dil, v_k_norm_dil, v_w_up_fox, v_w_up_sb, v_w_up_dil, v_w_out, v_mlp_norm, v_w_mlp_in, v_w_mlp_out):
    names = ("attn_norm", "w_in", "b_forget", "q_norm_fox", "k_norm_fox", "q_norm_dil", "k_norm_dil", "w_up_fox", "w_up_sb",
             "w_up_dil", "w_out", "mlp_norm", "w_mlp_in", "w_mlp_out")
    wv = dict(zip(names, (attn_norm, w_in, b_forget, q_norm_fox, k_norm_fox, q_norm_dil, k_norm_dil, w_up_fox, w_up_sb, w_up_dil,
                          w_out, mlp_norm, w_mlp_in, w_mlp_out)))
    mv = dict(zip(names, (m_attn_norm, m_w_in, m_b_forget, m_q_norm_fox, m_k_norm_fox, m_q_norm_dil, m_k_norm_dil, m_w_up_fox,
                          m_w_up_sb, m_w_up_dil, m_w_out, m_mlp_norm, m_w_mlp_in, m_w_mlp_out)))
    vv = dict(zip(names, (v_attn_norm, v_w_in, v_b_forget, v_q_norm_fox, v_k_norm_fox, v_q_norm_dil, v_k_norm_dil, v_w_up_fox,
                          v_w_up_sb, v_w_up_dil, v_w_out, v_mlp_norm, v_w_mlp_in, v_w_mlp_out)))
    depth = w_in.shape[0]
    flat2 = lambda a: a.reshape(-1, a.shape[-1])

    ups = ("w_up_fox", "w_up_sb", "w_up_dil")
    wide = ("w_mlp_in", "w_mlp_out", "w_out")
    core = lax.axis_index("c").astype(jnp.int32).reshape(1)
    chip = (2 * lax.axis_index("x") + lax.axis_index("y")).astype(jnp.int32).reshape(1)

    def shards(l):
        return [w_in[l].astype(MM), jnp.concatenate([wv[k][l] for k in ups]).astype(MM),
                jnp.concatenate([wv[k][l] for k in wide]).astype(MM)]

    def pieces(a, keys):
        out, at = {}, 0
        for k in keys:
            rows = wv[k].shape[1]
            out[k] = [a[c, at:at + rows] for c in range(4)]
            at += rows
        return out

    def with_own(gathered, own):
        return [lax.dynamic_update_index_in_dim(g, s, chip[0], 0) for g, s in zip(gathered, own)]

    def attention_weights(gathered, own):
        got_in, got_up = with_own(gathered, own)
        p = pieces(got_in, ("w_in",))["w_in"]
        pad = jnp.zeros((p[0].shape[0], DPROJ - DIN), p[0].dtype)
        win = jnp.concatenate([p[0][:, :O1], p[0][:, O2:], p[1], p[2], p[3], p[0][:, O1:O2], pad], axis=1)
        up = {k: jnp.concatenate(v, axis=1) for k, v in pieces(got_up, ups).items()}
        return dict(win=win, wuf=up["w_up_fox"], wus=up["w_up_sb"], wud=up["w_up_dil"])

    def late_weights(gathered, own):
        wd = pieces(with_own(gathered, own)[0], wide)
        return dict(wo=jnp.concatenate(wd["w_out"], axis=0), wmi=jnp.concatenate(wd["w_mlp_in"], axis=1),
                    wmo=jnp.concatenate(wd["w_mlp_out"], axis=0))

    def layer_weights(gathered, own):
        return {**attention_weights(gathered[:2], own[:2]), **late_weights(gathered[2:], own[2:])}

    small = {k: wv[k] for k in SMALL}
    small_fwd = dict(small)
    first = shards(0)
    started0 = gather_start(first[2:], name="gather_start0")
    order = started0[-1][0, 0]
    weights = [attention_weights(gather_chips(first[:2], name="gather_weights"), first[:2])]
    weights[0]["late"] = lambda after: late_weights(*gather_wait(*started0[:-1], after, name="gather_wait0"))
    for l in range(1, depth):
        started = gather_start(shards(l), name=f"gather_start{l}")
        order = order + started[-1][0, 0]
        weights.append(lambda after, l=l, started=started: layer_weights(*gather_wait(*started[:-1], after, name=f"gather_wait{l}")))
    small_fwd["attn_norm"] = small["attn_norm"] + order

    loss, grad_x, bufs, gss = local_step(x, positions, loss_target, weights, small_fwd)
    loss = lax.psum(loss, ("x", "y", "c"))

    g_small = {k: jnp.stack([gss[l][k] for l in range(depth)]) for k in SMALL}
    g_forget = jnp.stack([gss[l]["w_in_forget"] for l in range(depth)])
    summed = all_reduce_small(jnp.concatenate([_pack_small(g_small), g_forget.reshape(-1, 1024)]), name="reduce_small")
    g_small = _unpack_small(summed[:SMALL_ROWS], small)
    g_forget = summed[SMALL_ROWS:].reshape(g_forget.shape)

    parts = [bufs["win"], bufs["ups"], bufs["wide"]]
    place = jnp.concatenate([core, chip])
    sums = [pair_sum_scatter(p, place, name=f"reduce_pair_sum{t}") for t, p in enumerate(parts)]
    joined = [chip_sum_join(own, landed, chip, name=f"reduce_chip_sum{t}").reshape(-1, parts[t].shape[-1])
              for t, (own, landed) in enumerate(sums)]

    def own_w_in_columns(window):
        cols = w_in.shape[-1]
        first = jnp.concatenate([window[..., :O1], g_forget, window[..., O1:cols - (O2 - O1)]], axis=-1)
        shift = jnp.maximum((cols - WIN_STRIDE * WIN_TILE) * chip[0] - (O2 - O1), 0)
        rest = lax.dynamic_slice_in_dim(window, shift, cols, axis=2)
        return jnp.where(chip[0] == 0, first, rest)

    g_big = {"w_in": own_w_in_columns(joined[0].reshape(depth, -1, joined[0].shape[-1]))}
    for a, keys in ((joined[1], ups), (joined[2], wide)):
        at = 0
        for k in keys:
            rows = wv[k].shape[0] * wv[k].shape[1]
            g_big[k] = a[at:at + rows].reshape(wv[k].shape)
            at += rows

    grads = {**g_small, **g_big}
    delta, new_m, new_v = {}, {}, {}
    d_s, m_s, v_s = adamw(_pack_small(small), _pack_small(g_small), _pack_small({k: mv[k] for k in SMALL}),
                          _pack_small({k: vv[k] for k in SMALL}), name="adamw_small")
    delta.update(_unpack_small(d_s, small))
    new_m.update(_unpack_small(m_s, small))
    new_v.update(_unpack_small(v_s, small))
    for k in ("w_in",) + ups + wide:
        d_k, m_k, v_k = adamw(flat2(wv[k]), flat2(g_big[k]), flat2(mv[k]), flat2(vv[k]), name=f"adamw_{k}")
        delta[k], new_m[k], new_v[k] = d_k.reshape(wv[k].shape), m_k.reshape(wv[k].shape), v_k.reshape(wv[k].shape)

    return (loss, grad_x, *[grads[k] for k in names], *[delta[k] for k in names], *[new_m[k] for k in names], *[new_v[k] for k in names])
```

```python
import functools

import jax
import jax.numpy as jnp
from jax import lax
from jax.experimental import pallas as pl
from jax.experimental.pallas import tpu as pltpu

F32 = jnp.float32
BF16 = jnp.bfloat16
MM = jnp.bfloat16

HEAD = 64
LANES = 128
EPS = 1e-6
SCALE = 0.125
ROPE_THETA = 10000.0
DIL_PATTERNS = ((128, 1), (512, 4), (2048, 16))
ADAM_LR, ADAM_B1, ADAM_B2, ADAM_EPS, ADAM_WD, ADAM_STEP = 0.001, 0.9, 0.999, 1e-08, 0.01, 10

FOXQ, FOXK, FOXV = 0, 4, 8
SBQ, SBK, SBV = 12, 16, 20
DILQ, DILK, DILV = 24, 30, 36
GATE, FORGET, NBLK = 42, 66, 68
DPROJ = NBLK * LANES
O1, O2, O3, O4, DIN = 1536, 1544, 3080, 5384, 8456

VMEM_LIMIT = 56 * 1024 * 1024
MESH_ID = pl.DeviceIdType.MESH
ANY = pl.BlockSpec(memory_space=pl.ANY)


def _params(sem=None):
    return pltpu.CompilerParams(dimension_semantics=sem, vmem_limit_bytes=VMEM_LIMIT)


def _iota(shape, dim):
    return lax.broadcasted_iota(jnp.int32, shape, dim)


def _split2(x):
    hi = x.astype(BF16)
    lo = (x - hi.astype(F32)).astype(BF16)
    return hi, lo


def _split3(x):
    hi = x.astype(BF16)
    r = x - hi.astype(F32)
    mid = r.astype(BF16)
    lo = (r - mid.astype(F32)).astype(BF16)
    return hi, mid, lo


def _dot(a, b):
    return jnp.dot(a, b, preferred_element_type=F32)


def _dot_nt(a, b):
    return lax.dot_general(a, b, (((1,), (1,)), ((), ())), preferred_element_type=F32)


def _dot_tn(a, b):
    return lax.dot_general(a, b, (((0,), (0,)), ((), ())), preferred_element_type=F32)


def _xdot2(x, m):
    hi, lo = _split2(x)
    return _dot(hi, m) + _dot(lo, m)


def _xdot3(x, m):
    hi, mid, lo = _split3(x)
    return _dot(hi, m) + _dot(mid, m) + _dot(lo, m)


def _xdot3_left(m, x):
    hi, mid, lo = _split3(x)
    return _dot(m, hi) + _dot(m, mid) + _dot(m, lo)


def _head_mat(w):
    return ((_iota((w, w), 0) >> 6) == (_iota((w, w), 1) >> 6)).astype(BF16)


def _softplus_parts(z):
    e = jnp.exp(-jnp.abs(z))
    return e, jnp.maximum(z, 0.0) + jnp.log(1.0 + e)


def _fit(dim, want):
    t = min(want, dim)
    while dim % t:
        t -= LANES
        assert t > 0, (dim, want)
    return t


def matmul(a, b, *, ta=False, tb=False, out_dtype=F32, add=None, tm=2048, tn=512, tk=1024, dest=None, relu2=False,
           relu2_of=None, name):
    K, M = a.shape if ta else a.shape[::-1]
    K2, N = b.shape[::-1] if tb else b.shape
    assert K == K2, (a.shape, b.shape, ta, tb)
    tm, tn, tk = _fit(M, tm), _fit(N, tn), _fit(K, tk)
    nk = K // tk
    dn = (((0 if ta else 1,), (1 if tb else 0,)), ((), ()))
    if dest is None:
        tiles, source = N // tn, lambda j: j
    else:
        assert add is None and not tb
        buffer, tiles, source, place = dest

    extra = add if add is not None else relu2_of
    assert add is None or relu2_of is None

    def body(*refs):
        act_ref = None
        if dest is not None:
            a_ref, b_ref, _, o_ref, acc_ref = refs
        elif relu2:
            a_ref, b_ref, o_ref, act_ref, acc_ref = refs
        elif extra is None:
            a_ref, b_ref, o_ref, acc_ref = refs
        else:
            a_ref, b_ref, add_ref, o_ref, acc_ref = refs
        k = pl.program_id(2)
        part = lax.dot_general(a_ref[...].astype(MM), b_ref[...].astype(MM), dn, preferred_element_type=F32)

        @pl.when(k == 0)
        def _():
            acc_ref[...] = part

        @pl.when(k > 0)
        def _():
            acc_ref[...] += part

        @pl.when(k == nk - 1)
        def _():
            r = acc_ref[...]
            if add is not None:
                r = r + add_ref[...]
            if relu2_of is not None:
                r = r * (2.0 * jnp.maximum(add_ref[...], 0.0))
            o_ref[...] = r.astype(o_ref.dtype).reshape(o_ref.shape)
            if act_ref is not None:
                pos = jnp.maximum(r, 0.0)
                act_ref[...] = (pos * pos).astype(act_ref.dtype)

    a_spec = pl.BlockSpec((tk, tm), lambda i, j, k: (k, i)) if ta else pl.BlockSpec((tm, tk), lambda i, j, k: (i, k))
    b_spec = pl.BlockSpec((tn, tk), lambda i, j, k: (j, k)) if tb else pl.BlockSpec((tk, tn), lambda i, j, k: (k, source(j)))
    o_spec = pl.BlockSpec((tm, tn), lambda i, j, k: (i, j))
    ins, specs, aliases = [a, b], [a_spec, b_spec], {}
    out_shape = jax.ShapeDtypeStruct((M, N), out_dtype)
    if extra is not None:
        ins.append(extra)
        specs.append(o_spec)
    if relu2:
        o_spec, out_shape = [o_spec, o_spec], [out_shape, jax.ShapeDtypeStruct((M, N), MM)]
    if dest is not None:
        ins.append(buffer)
        specs.append(ANY)
        aliases = {2: 0}
        o_spec = pl.BlockSpec((1, tm, tn), lambda i, j, k: place(i, j))
        out_shape = jax.ShapeDtypeStruct(buffer.shape, buffer.dtype)
    return pl.pallas_call(
        body, name=name, grid=(M // tm, tiles, nk), in_specs=specs, out_specs=o_spec, out_shape=out_shape,
        scratch_shapes=[pltpu.VMEM((tm, tn), F32)], input_output_aliases=aliases,
        compiler_params=_params(("parallel", "parallel", "arbitrary")),
    )(*ins)


def _rows(n, want=512):
    t = min(want, n)
    assert n % t == 0, (n, t)
    return t


def rmsnorm_fwd(x, g, *, name):
    n, d = x.shape
    tr = _rows(n)

    def body(x_ref, g_ref, o_ref, t_ref):
        xv = x_ref[...]
        r = lax.rsqrt(jnp.mean(xv * xv, axis=1, keepdims=True) + EPS)
        y = xv * r * g_ref[...]
        o_ref[...] = y.astype(o_ref.dtype)
        t_ref[...] = y.T.astype(t_ref.dtype)

    row = pl.BlockSpec((tr, d), lambda i: (i, 0))
    vec = pl.BlockSpec((1, d), lambda i: (0, 0))
    return pl.pallas_call(
        body, name=name, grid=(n // tr,), in_specs=[row, vec], out_specs=[row, pl.BlockSpec((d, tr), lambda i: (0, i))],
        out_shape=[jax.ShapeDtypeStruct((n, d), MM), jax.ShapeDtypeStruct((d, n), MM)], compiler_params=_params(("parallel",)))(x, g)


def rmsnorm_bwd(x, g, dh, dres, *, name):
    n, d = x.shape
    tr = _rows(n)

    def body(x_ref, g_ref, dh_ref, dr_ref, dx_ref, dg_ref):
        @pl.when(pl.program_id(0) == 0)
        def _():
            dg_ref[...] = jnp.zeros_like(dg_ref)

        xv = x_ref[...]
        r = lax.rsqrt(jnp.mean(xv * xv, axis=1, keepdims=True) + EPS)
        y = xv * r
        dhv = dh_ref[...]
        dy = dhv * g_ref[...]
        dx_ref[...] = dr_ref[...] + r * (dy - y * jnp.mean(dy * y, axis=1, keepdims=True))
        dg_ref[...] += jnp.sum(dhv * y, axis=0, keepdims=True)

    row = pl.BlockSpec((tr, d), lambda i: (i, 0))
    vec = pl.BlockSpec((1, d), lambda i: (0, 0))
    return pl.pallas_call(
        body, name=name, grid=(n // tr,), in_specs=[row, vec, row, row], out_specs=[row, vec],
        out_shape=[jax.ShapeDtypeStruct((n, d), F32), jax.ShapeDtypeStruct((1, d), F32)],
        compiler_params=_params(("arbitrary",)))(x, g, dh, dres)


def loss_grad(y, tgt, *, name):
    n, d = y.shape
    tr = _rows(n)

    def body(y_ref, t_ref, dy_ref, acc_ref):
        @pl.when(pl.program_id(0) == 0)
        def _():
            acc_ref[...] = jnp.zeros_like(acc_ref)

        e = y_ref[...] - t_ref[...]
        dy_ref[...] = e * (1.0 / d)
        acc_ref[...] += jnp.sum(e * e, axis=0, keepdims=True)

    row = pl.BlockSpec((tr, d), lambda i: (i, 0))
    vec = pl.BlockSpec((1, d), lambda i: (0, 0))
    return pl.pallas_call(
        body, name=name, grid=(n // tr,), in_specs=[row, row], out_specs=[row, vec],
        out_shape=[jax.ShapeDtypeStruct((n, d), F32), jax.ShapeDtypeStruct((1, d), F32)],
        compiler_params=_params(("arbitrary",)))(y, tgt)


MERGE_W = 256


def _gate_specs(tr, d):
    per = d // MERGE_W
    base = GATE * LANES // MERGE_W
    return [pl.BlockSpec((tr, MERGE_W), functools.partial(lambda i, j, b: (i, base + per * b + j), b=b)) for b in range(3)]


def merge_fwd(proj, ys, *, name):
    n, d = ys[0].shape
    tr = _rows(n)

    def body(g0, g1, g2, y0, y1, y2, o_ref):
        acc = jax.nn.sigmoid(g0[...]) * y0[...]
        acc += jax.nn.sigmoid(g1[...]) * y1[...]
        acc += jax.nn.sigmoid(g2[...]) * y2[...]
        o_ref[...] = acc.astype(o_ref.dtype)

    blk = pl.BlockSpec((tr, MERGE_W), lambda i, j: (i, j))
    return pl.pallas_call(
        body, name=name, grid=(n // tr, d // MERGE_W), in_specs=_gate_specs(tr, d) + [blk] * 3, out_specs=blk,
        out_shape=jax.ShapeDtypeStruct((n, d), MM), compiler_params=_params(("parallel", "parallel")))(proj, proj, proj, *ys)


def merge_bwd(proj, ys, dm, *, name):
    n, d = dm.shape
    tr = _rows(n)

    def body(g0, g1, g2, y0, y1, y2, dm_ref, dy0, dy1, dy2, dg0, dg1, dg2):
        dmv = dm_ref[...]
        for g, y, dy, dg in ((g0, y0, dy0, dg0), (g1, y1, dy1, dg1), (g2, y2, dy2, dg2)):
            s = jax.nn.sigmoid(g[...])
            dy[...] = (dmv * s).astype(dy.dtype)
            dg[...] = (dmv * y[...] * s * (1.0 - s)).astype(dg.dtype)

    blk = pl.BlockSpec((tr, MERGE_W), lambda i, j: (i, j))
    out = jax.ShapeDtypeStruct((n, d), MM)
    return pl.pallas_call(
        body, name=name, grid=(n // tr, d // MERGE_W), in_specs=_gate_specs(tr, d) + [blk] * 4, out_specs=[blk] * 6,
        out_shape=[out] * 6, compiler_params=_params(("parallel", "parallel")))(proj, proj, proj, *ys, dm)


def adamw(w, g, m, v, *, name):
    r, c = w.shape
    tr = r
    while tr * c * 4 > (1 << 21) and tr % 16 == 0:
        tr //= 2
    c1 = 1.0 / (1.0 - ADAM_B1 ** ADAM_STEP)
    c2 = 1.0 / (1.0 - ADAM_B2 ** ADAM_STEP)

    def body(w_ref, g_ref, m_ref, v_ref, d_ref, mo_ref, vo_ref):
        gv = g_ref[...]
        m2 = ADAM_B1 * m_ref[...] + (1.0 - ADAM_B1) * gv
        v2 = ADAM_B2 * v_ref[...] + (1.0 - ADAM_B2) * (gv * gv)
        d_ref[...] = -ADAM_LR * ((m2 * c1) / (jnp.sqrt(v2 * c2) + ADAM_EPS) + ADAM_WD * w_ref[...])
        mo_ref[...] = m2
        vo_ref[...] = v2

    blk = pl.BlockSpec((tr, c), lambda i: (i, 0))
    out = jax.ShapeDtypeStruct((r, c), F32)
    return pl.pallas_call(body, name=name, grid=(r // tr,), in_specs=[blk] * 4, out_specs=[blk] * 3, out_shape=[out] * 3,
                          compiler_params=_params(("parallel",)))(w, g, m, v)


def rope_table(pos, inv, *, name):
    n = pos.shape[0]
    tr = _rows(n)

    def body(p_ref, i_ref, c_ref, s_ref):
        ang = p_ref[...].astype(F32) * i_ref[...]
        c_ref[...] = jnp.cos(ang)
        s_ref[...] = jnp.sin(ang)

    out = jax.ShapeDtypeStruct((n, LANES), F32)
    blk = pl.BlockSpec((tr, LANES), lambda i: (i, 0))
    return pl.pallas_call(
        body, name=name, grid=(n // tr,), in_specs=[pl.BlockSpec((tr, 1), lambda i: (i, 0)), pl.BlockSpec((1, LANES), lambda i: (0, 0))],
        out_specs=[blk, blk], out_shape=[out, out], compiler_params=_params(("parallel",)))(pos, inv)


def _rot_half(x):
    first = (_iota((1, LANES), 1) & 63) < 32
    return jnp.where(first, -pltpu.roll(x, LANES - 32, axis=1), pltpu.roll(x, 32, axis=1))


def _head_norm(xv, gm):
    r = lax.rsqrt(_xdot2(xv * xv, gm) * (1.0 / HEAD) + EPS)
    return r, xv * r


def _head_norm_bwd(xh, r, dxh, gm):
    return r * (dxh - xh * (_xdot2(dxh * xh, gm) * (1.0 / HEAD)))


def fox_prep_fwd(proj, qg, kg, bf, *, bsz, seq, name):
    n = bsz * seq
    tr = min(256, seq)
    nt = seq // tr
    w = 4 * LANES

    def body(q_ref, k_ref, f_ref, qg_ref, kg_ref, b_ref, qn_ref, kn_ref, fb_ref, f8_ref, carry):
        @pl.when(pl.program_id(1) == 0)
        def _():
            carry[...] = jnp.zeros_like(carry)

        gm = _head_mat(LANES)
        for src, gain, dst in ((q_ref, qg_ref, qn_ref), (k_ref, kg_ref, kn_ref)):
            for c in range(4):
                sl = slice(c * LANES, (c + 1) * LANES)
                _, xh = _head_norm(src[:, sl], gm)
                dst[:, sl] = (xh * gain[:, sl]).astype(dst.dtype)
        logf = jax.nn.log_sigmoid(f_ref[...] + b_ref[...])
        lower = (_iota((tr, tr), 1) <= _iota((tr, tr), 0)).astype(BF16)
        fcum = _xdot3_left(lower, logf) + carry[...]
        carry[...] = fcum[tr - 1:tr, :]
        f8_ref[...] = fcum
        spread = (_iota((LANES, w), 0) == (_iota((LANES, w), 1) >> 6)).astype(BF16)
        fb_ref[...] = _xdot3(fcum, spread)

    row = lambda width, blk: pl.BlockSpec((tr, width), lambda b, t: (b * nt + t, blk))
    vec = lambda width: pl.BlockSpec((1, width), lambda b, t: (0, 0))
    return pl.pallas_call(
        body, name=name, grid=(bsz, nt),
        in_specs=[row(w, FOXQ // 4), row(w, FOXK // 4), row(LANES, FORGET), vec(w), vec(w), vec(LANES)],
        out_specs=[row(w, 0), row(w, 0), row(w, 0), row(LANES, 0)],
        out_shape=[jax.ShapeDtypeStruct((n, w), MM), jax.ShapeDtypeStruct((n, w), MM),
                   jax.ShapeDtypeStruct((n, w), F32), jax.ShapeDtypeStruct((n, LANES), F32)],
        scratch_shapes=[pltpu.VMEM((1, LANES), F32)],
        compiler_params=_params(("parallel", "arbitrary")))(proj, proj, proj, qg, kg, bf)


def fox_prep_bwd(proj, qg, kg, bf, dqn, dkn, df, *, bsz, seq, name):
    n = bsz * seq
    tr = min(256, seq)
    nt = seq // tr
    w = 4 * LANES

    def body(q_ref, k_ref, f_ref, qg_ref, kg_ref, b_ref, dqn_ref, dkn_ref, df_ref,
             dq_ref, dk_ref, dl_ref, dqg_ref, dkg_ref, db_ref, carry):
        first = (pl.program_id(0) == 0) & (pl.program_id(1) == 0)

        @pl.when(first)
        def _():
            dqg_ref[...] = jnp.zeros_like(dqg_ref)
            dkg_ref[...] = jnp.zeros_like(dkg_ref)
            db_ref[...] = jnp.zeros_like(db_ref)

        @pl.when(pl.program_id(1) == 0)
        def _():
            carry[...] = jnp.zeros_like(carry)

        gm = _head_mat(LANES)
        for src, gain, dy_ref, dx_ref, dg_ref in ((q_ref, qg_ref, dqn_ref, dq_ref, dqg_ref), (k_ref, kg_ref, dkn_ref, dk_ref, dkg_ref)):
            for c in range(4):
                sl = slice(c * LANES, (c + 1) * LANES)
                r, xh = _head_norm(src[:, sl], gm)
                dy = dy_ref[:, sl]
                dg_ref[:, sl] += jnp.sum(dy * xh, axis=0, keepdims=True)
                dx_ref[:, sl] = _head_norm_bwd(xh, r, dy * gain[:, sl], gm).astype(dx_ref.dtype)
        upper = (_iota((tr, tr), 1) >= _iota((tr, tr), 0)).astype(BF16)
        dlogf = _xdot3_left(upper, df_ref[...]) + carry[...]
        carry[...] = dlogf[0:1, :]
        dlogit = dlogf * jax.nn.sigmoid(-(f_ref[...] + b_ref[...]))
        dl_ref[:, 0:LANES] = dlogit.astype(dl_ref.dtype)
        dl_ref[:, LANES:2 * LANES] = jnp.zeros((tr, LANES), dl_ref.dtype)
        db_ref[...] += jnp.sum(dlogit, axis=0, keepdims=True)

    row = lambda width, blk: pl.BlockSpec((tr, width), lambda b, t: (b * nt + nt - 1 - t, blk))
    vec = lambda width: pl.BlockSpec((1, width), lambda b, t: (0, 0))
    return pl.pallas_call(
        body, name=name, grid=(bsz, nt),
        in_specs=[row(w, FOXQ // 4), row(w, FOXK // 4), row(LANES, FORGET), vec(w), vec(w), vec(LANES),
                  row(w, 0), row(w, 0), row(LANES, 0)],
        out_specs=[row(w, 0), row(w, 0), row(2 * LANES, 0), vec(w), vec(w), vec(LANES)],
        out_shape=[jax.ShapeDtypeStruct((n, w), MM), jax.ShapeDtypeStruct((n, w), MM), jax.ShapeDtypeStruct((n, 2 * LANES), MM),
                   jax.ShapeDtypeStruct((1, w), F32), jax.ShapeDtypeStruct((1, w), F32), jax.ShapeDtypeStruct((1, LANES), F32)],
        scratch_shapes=[pltpu.VMEM((1, LANES), F32)],
        compiler_params=_params(("arbitrary", "arbitrary")))(proj, proj, proj, qg, kg, bf, dqn, dkn, df)


DIL_W = 6 * LANES


def dil_prep_fwd(proj, qg, kg, cos, sin, *, name):
    n = proj.shape[0]
    tr = _rows(n, 256)

    def body(q_ref, k_ref, qg_ref, kg_ref, c_ref, s_ref, qo_ref, ko_ref):
        gm = _head_mat(LANES)
        cv, sv = c_ref[...], s_ref[...]
        for src, gain, dst in ((q_ref, qg_ref, qo_ref), (k_ref, kg_ref, ko_ref)):
            for c in range(6):
                sl = slice(c * LANES, (c + 1) * LANES)
                _, xh = _head_norm(src[:, sl], gm)
                xn = xh * gain[:, sl]
                dst[:, sl] = (xn * cv + _rot_half(xn) * sv).astype(dst.dtype)

    row = lambda width, blk: pl.BlockSpec((tr, width), lambda i: (i, blk))
    vec = pl.BlockSpec((1, DIL_W), lambda i: (0, 0))
    out = jax.ShapeDtypeStruct((n, DIL_W), MM)
    return pl.pallas_call(
        body, name=name, grid=(n // tr,),
        in_specs=[row(DIL_W, DILQ // 6), row(DIL_W, DILK // 6), vec, vec, row(LANES, 0), row(LANES, 0)],
        out_specs=[row(DIL_W, 0), row(DIL_W, 0)], out_shape=[out, out],
        compiler_params=_params(("parallel",)))(proj, proj, qg, kg, cos, sin)


def dil_prep_bwd(proj, qg, kg, cos, sin, dqr, dkr, *, name):
    n = proj.shape[0]
    tr = _rows(n, 256)

    def body(q_ref, k_ref, qg_ref, kg_ref, c_ref, s_ref, dqr_ref, dkr_ref, dq_ref, dk_ref, dqg_ref, dkg_ref):
        @pl.when(pl.program_id(0) == 0)
        def _():
            dqg_ref[...] = jnp.zeros_like(dqg_ref)
            dkg_ref[...] = jnp.zeros_like(dkg_ref)

        gm = _head_mat(LANES)
        cv, sv = c_ref[...], s_ref[...]
        for src, gain, dy_ref, dx_ref, dg_ref in ((q_ref, qg_ref, dqr_ref, dq_ref, dqg_ref), (k_ref, kg_ref, dkr_ref, dk_ref, dkg_ref)):
            for c in range(6):
                sl = slice(c * LANES, (c + 1) * LANES)
                r, xh = _head_norm(src[:, sl], gm)
                dy = dy_ref[:, sl]
                dxn = dy * cv - _rot_half(dy * sv)
                dg_ref[:, sl] += jnp.sum(dxn * xh, axis=0, keepdims=True)
                dx_ref[:, sl] = _head_norm_bwd(xh, r, dxn * gain[:, sl], gm).astype(dx_ref.dtype)

    row = lambda width, blk: pl.BlockSpec((tr, width), lambda i: (i, blk))
    vec = pl.BlockSpec((1, DIL_W), lambda i: (0, 0))
    out = jax.ShapeDtypeStruct((n, DIL_W), MM)
    gout = jax.ShapeDtypeStruct((1, DIL_W), F32)
    return pl.pallas_call(
        body, name=name, grid=(n // tr,),
        in_specs=[row(DIL_W, DILQ // 6), row(DIL_W, DILK // 6), vec, vec, row(LANES, 0), row(LANES, 0), row(DIL_W, 0), row(DIL_W, 0)],
        out_specs=[row(DIL_W, 0), row(DIL_W, 0), vec, vec], out_shape=[out, out, gout, gout],
        compiler_params=_params(("arbitrary",)))(proj, proj, qg, kg, cos, sin, dqr, dkr)


def dil_combine_fwd(os_, lses, *, name):
    n, w = os_[0].shape
    tr = _rows(n)

    def body(o0, o1, o2, l0, l1, l2, out_ref):
        a, b, c = l0[...], l1[...], l2[...]
        m = jnp.maximum(jnp.maximum(a, b), c)
        ea, eb, ec = jnp.exp(a - m), jnp.exp(b - m), jnp.exp(c - m)
        out_ref[...] = ((ea * o0[...] + eb * o1[...] + ec * o2[...]) / (ea + eb + ec)).astype(out_ref.dtype)

    blk = pl.BlockSpec((tr, w), lambda i: (i, 0))
    return pl.pallas_call(body, name=name, grid=(n // tr,), in_specs=[blk] * 6, out_specs=blk,
                          out_shape=jax.ShapeDtypeStruct((n, w), MM), compiler_params=_params(("parallel",)))(*os_, *lses)


def dil_combine_bwd(os_, lses, dout, *, name):
    n, w = dout.shape
    tr = _rows(n)

    def body(o0, o1, o2, l0, l1, l2, d_ref, do0, do1, do2, dl0, dl1, dl2):
        a, b, c = l0[...], l1[...], l2[...]
        m = jnp.maximum(jnp.maximum(a, b), c)
        es = [jnp.exp(a - m), jnp.exp(b - m), jnp.exp(c - m)]
        inv = 1.0 / (es[0] + es[1] + es[2])
        ws = [e * inv for e in es]
        dv = d_ref[...]
        gm = _head_mat(w)
        dws = [_xdot2(dv * o[...], gm) for o in (o0, o1, o2)]
        mean = ws[0] * dws[0] + ws[1] * dws[1] + ws[2] * dws[2]
        for wg, dw, do, dl in zip(ws, dws, (do0, do1, do2), (dl0, dl1, dl2)):
            do[...] = wg * dv
            dl[...] = wg * (dw - mean)

    blk = pl.BlockSpec((tr, w), lambda i: (i, 0))
    out = jax.ShapeDtypeStruct((n, w), F32)
    return pl.pallas_call(body, name=name, grid=(n // tr,), in_specs=[blk] * 7, out_specs=[blk] * 6, out_shape=[out] * 6,
                          compiler_params=_params(("parallel",)))(*os_, *lses, dout)


def _key_plan(qi, tq, seq, window, run):
    if window + tq >= seq:
        for bi in range(seq // tq):
            lo = bi * tq
            segs = ([(0, lo, "bulk")] if lo else []) + [(lo, tq, "diag")]
            pl.when(qi == bi)(functools.partial(run, segs))
    else:
        ext = window + tq
        run([(pl.multiple_of(jnp.maximum((qi + 1) * tq - ext, 0), LANES), ext, "band")])


def _seg_mask(seg, qi, tq, window, dilation, strict=False):
    start, width, kind = seg
    d = _iota((tq, width), 0) - _iota((tq, width), 1)
    if kind == "bulk":
        d = d + width
    elif kind == "band":
        d = d + (qi * tq - start)
    ok = None
    if kind != "bulk":
        ok = (d > 0) if strict else (d >= 0)
    if window is not None:
        ok = (d <= window) if ok is None else ok & (d <= window)
    if dilation > 1:
        on_grid = (d & (dilation - 1)) == 0
        ok = on_grid if ok is None else ok & on_grid
    return ok


def _lane_first():
    return _iota((1, LANES), 1) < HEAD


def _attn_specs(bsz, seq, tq, qo, ko, vo):
    nq = seq // tq
    qspec = lambda off: pl.BlockSpec((tq, LANES), lambda b, j, i: (b * nq + i, off + j))
    kspec = lambda off: pl.BlockSpec((seq, LANES), lambda b, j, i: (b, off + j))
    return nq, qspec, kspec


def softmax_attn_fwd(q, k, v, bias, *, qo, ko, vo, pairs, bsz, seq, window, dilation, tq, name):
    n = bsz * seq
    nq, qspec, kspec = _attn_specs(bsz, seq, tq, qo, ko, vo)

    def body(*refs):
        if bias is None:
            q_ref, k_ref, v_ref, o_ref, l_ref = refs
        else:
            q_ref, k_ref, v_ref, fq_ref, fk_ref, o_ref, l_ref = refs
        qi = pl.program_id(2)

        def run(segs):
            qv = (q_ref[...] * SCALE).astype(MM)
            first = _lane_first()
            keys = [(k_ref[pl.ds(st, w), :].astype(MM), v_ref[pl.ds(st, w), :].astype(MM),
                     _seg_mask((st, w, kind), qi, tq, None if window >= seq else window, dilation), st, w)
                    for st, w, kind in segs]
            outs, lses = [], []
            for a in range(2):
                qa = jnp.where(first if a == 0 else ~first, qv, jnp.zeros_like(qv))
                scores = []
                for kv, _, ok, st, w in keys:
                    s = _dot_nt(qa, kv)
                    if bias is not None:
                        s = s + fq_ref[:, a * HEAD:a * HEAD + 1] - fk_ref[a:a + 1, pl.ds(st, w)]
                    scores.append(s if ok is None else jnp.where(ok, s, -jnp.inf))
                m = functools.reduce(jnp.maximum, [jnp.max(s, axis=1, keepdims=True) for s in scores])
                ps = [jnp.exp(s - m) for s in scores]
                den = sum(jnp.sum(p, axis=1, keepdims=True) for p in ps)
                acc = sum(_dot(p.astype(MM), vv) for p, (_, vv, _, _, _) in zip(ps, keys))
                outs.append(acc / den)
                lses.append(m + jnp.log(den))
            o_ref[...] = jnp.where(first, outs[0], outs[1]).astype(o_ref.dtype)
            l_ref[...] = jnp.where(first, lses[0], lses[1])

        _key_plan(qi, tq, seq, window, run)

    ins, specs = [q, k, v], [qspec(qo), kspec(ko), kspec(vo)]
    if bias is not None:
        ins += list(bias)
        specs += [qspec(0), pl.BlockSpec((8, seq), lambda b, j, i: (b * pairs + j, 0))]
    out = jax.ShapeDtypeStruct((n, LANES * pairs), F32)
    return pl.pallas_call(
        body, name=name, grid=(bsz, pairs, nq), in_specs=specs, out_specs=[qspec(0), qspec(0)], out_shape=[out, out],
        compiler_params=_params(("parallel", "parallel", "arbitrary")))(*ins)


def softmax_attn_bwd(q, k, v, o, do, lse, dlse, bias, *, qo, ko, vo, pairs, bsz, seq, window, dilation, tq, dq_dtype, dk_dtype, name):
    n = bsz * seq
    nq, qspec, kspec = _attn_specs(bsz, seq, tq, qo, ko, vo)
    has_bias, has_dlse = bias is not None, dlse is not None

    def body(*refs):
        refs = list(refs)
        q_ref, k_ref, v_ref, o_ref, do_ref, l_ref = refs[:6]
        del refs[:6]
        dl_ref = refs.pop(0) if has_dlse else None
        fq_ref, fk_ref = (refs.pop(0), refs.pop(0)) if has_bias else (None, None)
        dq_ref, dk_ref, dv_ref = refs[:3]
        del refs[:3]
        dfq_ref, dfk_ref = (refs.pop(0), refs.pop(0)) if has_bias else (None, None)
        dk_acc, dv_acc = refs
        qi = pl.program_id(2)

        @pl.when(qi == 0)
        def _():
            dk_acc[...] = jnp.zeros_like(dk_acc)
            dv_acc[...] = jnp.zeros_like(dv_acc)
            if has_bias:
                dfk_ref[...] = jnp.zeros_like(dfk_ref)

        def run(segs):
            qv = (q_ref[...] * SCALE).astype(MM)
            dov = do_ref[...]
            dob = dov.astype(MM)
            prod = dov * o_ref[...]
            first = _lane_first()
            keys = [(k_ref[pl.ds(st, w), :].astype(MM), v_ref[pl.ds(st, w), :].astype(MM),
                     _seg_mask((st, w, kind), qi, tq, None if window >= seq else window, dilation), st, w)
                    for st, w, kind in segs]
            dqs, dfqs = [], []
            dks, dvs = [[] for _ in keys], [[] for _ in keys]
            for a in range(2):
                mine = first if a == 0 else ~first
                col = slice(a * HEAD, a * HEAD + 1)
                delta = jnp.sum(jnp.where(mine, prod, 0.0), axis=1, keepdims=True)
                if has_dlse:
                    delta = delta - dl_ref[:, col]
                qa = jnp.where(mine, qv, jnp.zeros_like(qv))
                doa = jnp.where(mine, dob, jnp.zeros_like(dob))
                shift = l_ref[:, col]
                if has_bias:
                    shift = shift - fq_ref[:, col]
                dq, dfq = 0.0, 0.0
                for si, (kv, vv, ok, st, w) in enumerate(keys):
                    s = _dot_nt(qa, kv)
                    if has_bias:
                        s = s - fk_ref[a:a + 1, pl.ds(st, w)]
                    p = jnp.exp(s - shift)
                    if ok is not None:
                        p = jnp.where(ok, p, 0.0)
                    ds = p * (_dot_nt(doa, vv) - delta)
                    dsb = ds.astype(MM)
                    dvs[si].append(_dot_tn(p.astype(MM), dob))
                    dks[si].append(_dot_tn(dsb, qv))
                    dq = dq + _dot(dsb, kv)
                    if has_bias:
                        dfq = dfq + jnp.sum(ds, axis=1, keepdims=True)
                        dfk_ref[a:a + 1, pl.ds(st, w)] += jnp.sum(ds, axis=0, keepdims=True)
                dqs.append(dq * SCALE)
                dfqs.append(dfq)
            dq_ref[...] = jnp.where(first, dqs[0], dqs[1]).astype(dq_ref.dtype)
            for (_, _, _, st, w), dk, dv in zip(keys, dks, dvs):
                dk_acc[pl.ds(st, w), :] += jnp.where(first, dk[0], dk[1])
                dv_acc[pl.ds(st, w), :] += jnp.where(first, dv[0], dv[1])
            if has_bias:
                dfq_ref[...] = jnp.where(first, dfqs[0], dfqs[1])

        _key_plan(qi, tq, seq, window, run)

        @pl.when(qi == nq - 1)
        def _():
            dk_ref[...] = dk_acc[...].astype(dk_ref.dtype)
            dv_ref[...] = dv_acc[...].astype(dv_ref.dtype)

    wide = LANES * pairs
    ins = [q, k, v, o, do, lse]
    specs = [qspec(qo), kspec(ko), kspec(vo), qspec(0), qspec(0), qspec(0)]
    outs = [jax.ShapeDtypeStruct((n, wide), dq_dtype), jax.ShapeDtypeStruct((n, wide), dk_dtype), jax.ShapeDtypeStruct((n, wide), MM)]
    out_specs = [qspec(0), kspec(0), kspec(0)]
    if has_dlse:
        ins.append(dlse)
        specs.append(qspec(0))
    if has_bias:
        rows = pl.BlockSpec((8, seq), lambda b, j, i: (b * pairs + j, 0))
        ins += list(bias)
        specs += [qspec(0), rows]
        outs += [jax.ShapeDtypeStruct((n, wide), F32), jax.ShapeDtypeStruct((bsz * pairs * 8, seq), F32)]
        out_specs += [qspec(0), rows]
    return pl.pallas_call(
        body, name=name, grid=(bsz, pairs, nq), in_specs=specs, out_specs=out_specs, out_shape=outs,
        scratch_shapes=[pltpu.VMEM((seq, LANES), F32), pltpu.VMEM((seq, LANES), F32)],
        compiler_params=_params(("parallel", "parallel", "arbitrary")))(*ins)


def _running_sum(vals, mat, carry, lat_ref, start, reverse):
    nb = vals.shape[1] // LANES
    for cb in (reversed(range(nb)) if reverse else range(nb)):
        blk = vals[:, cb * LANES:(cb + 1) * LANES]
        lat_ref[:, start + cb * LANES:start + (cb + 1) * LANES] = _dot(blk.astype(BF16), mat) + carry
        carry = carry + jnp.sum(blk, axis=1, keepdims=True)
    return carry


def _sb_weights(qa, keys, tq, lat_ref):
    after = (_iota((LANES, LANES), 0) > _iota((LANES, LANES), 1)).astype(BF16)
    carry = jnp.zeros((tq, 1), F32)
    logs = []
    for kv, ok, st, w in reversed(keys):
        z = _dot_nt(qa, kv)
        _, sp = _softplus_parts(z)
        visible = sp if ok is None else jnp.where(ok, sp, 0.0)
        carry = _running_sum(visible, after, carry, lat_ref, st, True)
        logs.append(z - sp)
    out = []
    for (kv, ok, st, w), log_beta in zip(keys, reversed(logs)):
        att = jnp.exp(log_beta - lat_ref[:, st:st + w])
        out.append((log_beta, att if ok is None else jnp.where(ok, att, 0.0)))
    return out


def _sb_keys(k_ref, v_ref, segs, qi, tq):
    return [(k_ref[st:st + w, :].astype(MM), v_ref[st:st + w, :].astype(MM),
             _seg_mask((st, w, kind), qi, tq, None, 1, strict=True), st, w) for st, w, kind in segs]


def sb_attn_fwd(proj, *, bsz, seq, tq, name):
    n = bsz * seq
    pairs = 4
    nq, qspec, kspec = _attn_specs(bsz, seq, tq, SBQ, SBK, SBV)

    def body(q_ref, k_ref, v_ref, o_ref, lat_ref):
        qi = pl.program_id(2)

        def run(segs):
            qv = (q_ref[...] * SCALE).astype(MM)
            keys = _sb_keys(k_ref, v_ref, segs, qi, tq)
            first = _lane_first()
            outs = []
            for a in range(2):
                qa = jnp.where(first if a == 0 else ~first, qv, jnp.zeros_like(qv))
                weights = _sb_weights(qa, [(kv, ok, st, w) for kv, _, ok, st, w in keys], tq, lat_ref)
                outs.append(sum(_dot(att.astype(MM), vv) for (_, att), (_, vv, _, _, _) in zip(weights, keys)))
            o_ref[...] = jnp.where(first, outs[0], outs[1]).astype(o_ref.dtype)

        _key_plan(qi, tq, seq, seq, run)

    return pl.pallas_call(
        body, name=name, grid=(bsz, pairs, nq), in_specs=[qspec(SBQ), kspec(SBK), kspec(SBV)], out_specs=qspec(0),
        out_shape=jax.ShapeDtypeStruct((n, LANES * pairs), MM), scratch_shapes=[pltpu.VMEM((tq, seq), F32)],
        compiler_params=_params(("parallel", "parallel", "arbitrary")))(proj, proj, proj)


def sb_attn_bwd(proj, do, *, bsz, seq, tq, name):
    n = bsz * seq
    pairs = 4
    nq, qspec, kspec = _attn_specs(bsz, seq, tq, SBQ, SBK, SBV)

    def body(q_ref, k_ref, v_ref, do_ref, dq_ref, dk_ref, dv_ref, lat_ref, dk_acc, dv_acc):
        qi = pl.program_id(2)

        @pl.when(qi == 0)
        def _():
            dk_acc[...] = jnp.zeros_like(dk_acc)
            dv_acc[...] = jnp.zeros_like(dv_acc)

        def run(segs):
            qv = (q_ref[...] * SCALE).astype(MM)
            keys = _sb_keys(k_ref, v_ref, segs, qi, tq)
            dob = do_ref[...].astype(MM)
            first = _lane_first()
            before = (_iota((LANES, LANES), 0) < _iota((LANES, LANES), 1)).astype(BF16)
            dqs = []
            dks, dvs = [[] for _ in keys], [[] for _ in keys]
            for a in range(2):
                mine = first if a == 0 else ~first
                qa = jnp.where(mine, qv, jnp.zeros_like(qv))
                doa = jnp.where(mine, dob, jnp.zeros_like(dob))
                weights = _sb_weights(qa, [(kv, ok, st, w) for kv, _, ok, st, w in keys], tq, lat_ref)
                gs = [_dot_nt(doa, vv) * att for (_, att), (_, vv, _, _, _) in zip(weights, keys)]
                carry = jnp.zeros((tq, 1), F32)
                for g, (_, _, _, st, w) in zip(gs, keys):
                    carry = _running_sum(g, before, carry, lat_ref, st, False)
                dq = 0.0
                for si, ((log_beta, att), g, (kv, _, ok, st, w)) in enumerate(zip(weights, gs, keys)):
                    dz = g - jnp.exp(log_beta) * (g + lat_ref[:, st:st + w])
                    dz = (dz if ok is None else jnp.where(ok, dz, 0.0)).astype(MM)
                    dvs[si].append(_dot_tn(att.astype(MM), dob))
                    dks[si].append(_dot_tn(dz, qv))
                    dq = dq + _dot(dz, kv)
                dqs.append(dq * SCALE)
            dq_ref[...] = jnp.where(first, dqs[0], dqs[1]).astype(dq_ref.dtype)
            for (_, _, _, st, w), dk, dv in zip(keys, dks, dvs):
                dk_acc[st:st + w, :] += jnp.where(first, dk[0], dk[1])
                dv_acc[st:st + w, :] += jnp.where(first, dv[0], dv[1])

        _key_plan(qi, tq, seq, seq, run)

        @pl.when(qi == nq - 1)
        def _():
            dk_ref[...] = dk_acc[...].astype(dk_ref.dtype)
            dv_ref[...] = dv_acc[...].astype(dv_ref.dtype)

    out = jax.ShapeDtypeStruct((n, LANES * pairs), MM)
    return pl.pallas_call(
        body, name=name, grid=(bsz, pairs, nq), in_specs=[qspec(SBQ), kspec(SBK), kspec(SBV), qspec(0)],
        out_specs=[qspec(0), kspec(0), kspec(0)], out_shape=[out, out, out],
        scratch_shapes=[pltpu.VMEM((tq, seq), F32), pltpu.VMEM((seq, LANES), F32), pltpu.VMEM((seq, LANES), F32)],
        compiler_params=_params(("parallel", "parallel", "arbitrary")))(proj, proj, proj, do)


def _place():
    return lax.axis_index("x"), lax.axis_index("y"), lax.axis_index("c")


def _other_chips(x, y):
    return [(1 - x, y), (x, 1 - y), (1 - x, 1 - y)]


def _remote(src, dst, send_sems, recv_sems, k, to):
    return pltpu.make_async_remote_copy(src_ref=src, dst_ref=dst, send_sem=send_sems.at[k], recv_sem=recv_sems.at[k],
                                        device_id=to, device_id_type=MESH_ID)


def gather_chips(arrs, *, name):
    na = len(arrs)

    def body(*refs):
        ins, outs = refs[:na], refs[na:2 * na]
        send_sems, recv_sems = refs[2 * na:]
        x, y, c = _place()
        me, sibling = 2 * x + y, (x, y, 1 - c)
        chips = _other_chips(x, y)
        sends = []
        for t in range(na):
            rh = ins[t].shape[0] // 2
            half = lambda chip, h, t=t, rh=rh: outs[t].at[chip, pl.ds(h * rh, rh), :]
            for j, (px, py) in enumerate(chips):
                cp = _remote(ins[t].at[pl.ds(c * rh, rh), :], half(me, c), send_sems, recv_sems, 6 * t + j, (px, py, c))
                cp.start()
                sends.append(cp)
        for t in range(na):
            rh = ins[t].shape[0] // 2
            half = lambda chip, h, t=t, rh=rh: outs[t].at[chip, pl.ds(h * rh, rh), :]
            for j, (px, py) in enumerate(chips):
                landed = half(2 * px + py, c)
                _remote(landed, landed, send_sems, recv_sems, 6 * t + j, (px, py, c)).wait_recv()
                fw = _remote(landed, landed, send_sems, recv_sems, 6 * t + 3 + j, sibling)
                fw.start()
                sends.append(fw)
        for t in range(na):
            rh = ins[t].shape[0] // 2
            half = lambda chip, h, t=t, rh=rh: outs[t].at[chip, pl.ds(h * rh, rh), :]
            for j, (px, py) in enumerate(chips):
                passed = half(2 * px + py, 1 - c)
                _remote(passed, passed, send_sems, recv_sems, 6 * t + 3 + j, sibling).wait_recv()
        for cp in sends:
            cp.wait_send()

    for a in arrs:
        assert a.ndim == 2 and a.shape[0] % 32 == 0, a.shape
    return pl.pallas_call(
        body, name=name, in_specs=[ANY] * na, out_specs=[ANY] * na,
        out_shape=[jax.ShapeDtypeStruct((4,) + a.shape, a.dtype) for a in arrs],
        scratch_shapes=[pltpu.SemaphoreType.DMA((6 * na,)), pltpu.SemaphoreType.DMA((6 * na,))],
    )(*arrs)


HBM = pl.BlockSpec(memory_space=pltpu.HBM)
SEMS = pl.BlockSpec(memory_space=pltpu.SEMAPHORE)
DATAFLOW = pltpu.SideEffectType.DATAFLOW_SIDE_EFFECTING


def _in_hbm(a):
    return pltpu.with_memory_space_constraint(a, pltpu.HBM)


def gather_start(arrs, *, name):
    na = len(arrs)

    def body(*refs):
        ins, lands = refs[:na], refs[na:2 * na]
        send_sems, recv_sems = refs[2 * na], refs[2 * na + 1]
        token = refs[-1]
        x, y, c = _place()
        me = 2 * x + y
        for t in range(na):
            for j, (px, py) in enumerate(_other_chips(x, y)):
                _remote(ins[t], lands[t].at[me], send_sems, recv_sems, 3 * t + j, (px, py, c)).start()
        token[...] = jnp.zeros_like(token)

    lands = [lax.empty((4,) + a.shape, a.dtype) for a in arrs]
    out = pl.pallas_call(
        body, name=name, in_specs=[HBM] * (2 * na),
        out_specs=[SEMS, SEMS] + [HBM] * (2 * na) + [pl.BlockSpec(memory_space=pltpu.VMEM)],
        out_shape=[pltpu.SemaphoreType.DMA((3 * na,)), pltpu.SemaphoreType.DMA((3 * na,))]
        + [pltpu.HBM(a.shape, a.dtype) for a in arrs] + [pltpu.HBM(a.shape, a.dtype) for a in lands]
        + [jax.ShapeDtypeStruct((8, LANES), F32)],
        input_output_aliases={i: 2 + i for i in range(2 * na)},
        compiler_params=pltpu.CompilerParams(has_side_effects=DATAFLOW),
    )(*[_in_hbm(a) for a in arrs], *[_in_hbm(a) for a in lands])
    return out[0], out[1], list(out[2:2 + na]), list(out[2 + na:2 + 2 * na]), out[-1]


def gather_wait(send_sems, recv_sems, arrs, lands, after, *, name):
    na = len(arrs)

    def body(*refs):
        ins, lands_ = refs[:na], refs[na:2 * na]
        send_sems_, recv_sems_ = refs[2 * na], refs[2 * na + 1]
        x, y, c = _place()
        me = 2 * x + y
        for t in range(na):
            for j, (px, py) in enumerate(_other_chips(x, y)):
                sent = _remote(ins[t], lands_[t].at[me], send_sems_, recv_sems_, 3 * t + j, (px, py, c))
                sent.wait_send()
                came = _remote(ins[t], lands_[t].at[2 * px + py], send_sems_, recv_sems_, 3 * t + j, (px, py, c))
                came.wait_recv()

    out = pl.pallas_call(
        body, name=name, in_specs=[HBM] * (2 * na) + [SEMS, SEMS, ANY], out_specs=[HBM] * (2 * na),
        out_shape=[pltpu.HBM(a.shape, a.dtype) for a in arrs] + [pltpu.HBM(a.shape, a.dtype) for a in lands],
        input_output_aliases={i: i for i in range(2 * na)},
        compiler_params=pltpu.CompilerParams(has_side_effects=DATAFLOW),
    )(*arrs, *lands, send_sems, recv_sems, after)
    return list(out[na:]), list(out[:na])


CHUNK_BYTES = 4 << 20


def _chunk_rows(rows, cols, limit):
    best = 16
    for t in range(16, rows + 1, 16):
        if rows % t == 0 and t * cols * 4 <= limit:
            best = t
    assert rows % best == 0, (rows, cols)
    return best


def pair_sum_scatter(a, place, *, name):
    _, rows, cols = a.shape
    rh = rows // 2
    tr = _chunk_rows(rh, cols, CHUNK_BYTES)
    nch = rh // tr
    steps = 4 * nch

    def body(place_ref, keep_ref, send_ref, own_ref, landed_ref, landing, out16, res, pair_send, pair_recv, credit,
             chip_send, chip_recv, local_sem):
        i, j = pl.program_id(0), pl.program_id(1)
        step = i * 4 + j
        slot = lax.rem(step, 2)
        x, y, c = _place()
        sibling = (x, y, 1 - c)
        me = 2 * x + y
        rows_i = pl.ds(pl.multiple_of(i * tr, tr), tr)

        def to_chip(p, s):
            return pltpu.make_async_remote_copy(
                src_ref=out16.at[s], dst_ref=landed_ref.at[me, rows_i, :], send_sem=chip_send.at[s], recv_sem=chip_recv.at[p - 1],
                device_id=(x ^ (p >> 1), y ^ (p & 1), c), device_id_type=MESH_ID)

        @pl.when(step >= 2)
        def _():
            pl.semaphore_wait(credit, 1)

        cp = _remote(send_ref.at[0], landing.at[slot], pair_send, pair_recv, slot, sibling)
        cp.start()
        cp.wait_recv()
        total = keep_ref[0] + landing[slot]

        for p, s, before in ((1, 0, i > 0), (2, 1, i > 0), (3, 0, None)):
            @pl.when(j == p - 1)
            def _(p=p, s=s, before=before):
                if before is None:
                    to_chip(1, s).wait_send()
                else:
                    pl.when(before)(lambda: to_chip(1, s).wait_send())
                out16[s] = total.astype(BF16)
                to_chip(p, s).start()

        @pl.when(j == 3)
        def _():
            res[...] = total
            here = pltpu.make_async_copy(res, own_ref.at[rows_i, :], local_sem)
            here.start()
            here.wait()

        cp.wait_send()

        @pl.when(step + 2 < steps)
        def _():
            pl.semaphore_signal(credit, 1, device_id=sibling, device_id_type=MESH_ID)

        @pl.when(step == steps - 1)
        def _():
            to_chip(1, 1).wait_send()
            to_chip(1, 0).wait_send()
            for p in (1, 2, 3):
                slab = landed_ref.at[me ^ p]
                pltpu.make_async_remote_copy(src_ref=slab, dst_ref=slab, send_sem=chip_send.at[0], recv_sem=chip_recv.at[p - 1],
                                             device_id=(x ^ (p >> 1), y ^ (p & 1), c), device_id_type=MESH_ID).wait_recv()

    blk = (1, tr, cols)
    slab_of = lambda j, place: place[1] ^ ((j + 1) & 3)
    grid_spec = pltpu.PrefetchScalarGridSpec(
        num_scalar_prefetch=1, grid=(nch, 4),
        in_specs=[pl.BlockSpec(blk, lambda i, j, place: (slab_of(j, place), place[0] * nch + i, 0)),
                  pl.BlockSpec(blk, lambda i, j, place: (slab_of(j, place), (1 - place[0]) * nch + i, 0))],
        out_specs=[ANY, ANY],
        scratch_shapes=[pltpu.VMEM((2, tr, cols), F32), pltpu.VMEM((2, tr, cols), BF16), pltpu.VMEM((tr, cols), F32),
                        pltpu.SemaphoreType.DMA((2,)), pltpu.SemaphoreType.DMA((2,)), pltpu.SemaphoreType.REGULAR,
                        pltpu.SemaphoreType.DMA((2,)), pltpu.SemaphoreType.DMA((3,)), pltpu.SemaphoreType.DMA])
    return pl.pallas_call(
        body, name=name, grid_spec=grid_spec,
        out_shape=[jax.ShapeDtypeStruct((rh, cols), F32), jax.ShapeDtypeStruct((4, rh, cols), BF16)],
        compiler_params=_params(("arbitrary", "arbitrary")))(place, a, a)


def chip_sum_join(own, landed, chip, *, name):
    rh, cols = own.shape
    tr = _chunk_rows(rh, cols, CHUNK_BYTES)
    nch = rh // tr

    def body(chip_ref, own_ref, l1_ref, l2_ref, l3_ref, out_ref, res, local_sem, send_sem, recv_sem):
        i = pl.program_id(0)
        x, y, c = _place()
        sibling = (x, y, 1 - c)
        res[...] = ((own_ref[...] + l1_ref[0].astype(F32)) + l2_ref[0].astype(F32)) + l3_ref[0].astype(F32)
        rows = pl.ds(pl.multiple_of(i * tr, tr), tr)
        here = pltpu.make_async_copy(res, out_ref.at[c, rows, :], local_sem)
        here.start()
        there = pltpu.make_async_remote_copy(src_ref=res, dst_ref=out_ref.at[c, rows, :], send_sem=send_sem, recv_sem=recv_sem,
                                             device_id=sibling, device_id_type=MESH_ID)
        there.start()
        here.wait()
        there.wait_send()

        @pl.when(i == nch - 1)
        def _():
            half = out_ref.at[1 - c]
            pltpu.make_async_remote_copy(src_ref=half, dst_ref=half, send_sem=send_sem, recv_sem=recv_sem,
                                         device_id=sibling, device_id_type=MESH_ID).wait_recv()

    blk = (1, tr, cols)
    slab = lambda p: pl.BlockSpec(blk, lambda i, chip: (chip[0] ^ p, i, 0))
    grid_spec = pltpu.PrefetchScalarGridSpec(
        num_scalar_prefetch=1, grid=(nch,), out_specs=ANY,
        in_specs=[pl.BlockSpec((tr, cols), lambda i, chip: (i, 0)), slab(1), slab(2), slab(3)],
        scratch_shapes=[pltpu.VMEM((tr, cols), F32), pltpu.SemaphoreType.DMA, pltpu.SemaphoreType.DMA, pltpu.SemaphoreType.DMA])
    return pl.pallas_call(
        body, name=name, grid_spec=grid_spec, out_shape=jax.ShapeDtypeStruct((2, rh, cols), F32),
        compiler_params=_params(("arbitrary",)))(chip, own, landed, landed, landed)


def all_reduce_small(a, *, name):
    def body(a_ref, o_ref, buf, send_sems, recv_sems):
        x, y, c = _place()
        me = 4 * x + 2 * y + c
        buf[me] = a_ref[...]
        sent = []
        for p in range(1, 8):
            px, py, pc = (p >> 2) & 1, (p >> 1) & 1, p & 1
            cp = _remote(a_ref, buf.at[me], send_sems, recv_sems, p - 1, (x ^ px, y ^ py, c ^ pc))
            cp.start()
            sent.append(cp)
        for p in range(1, 8):
            px, py, pc = (p >> 2) & 1, (p >> 1) & 1, p & 1
            src = 4 * (x ^ px) + 2 * (y ^ py) + (c ^ pc)
            _remote(a_ref, buf.at[src], send_sems, recv_sems, p - 1, (x ^ px, y ^ py, c ^ pc)).wait_recv()
        for cp in sent:
            cp.wait_send()
        acc = buf[0]
        for d in range(1, 8):
            acc = acc + buf[d]
        o_ref[...] = acc

    vm = pl.BlockSpec(memory_space=pltpu.VMEM)
    return pl.pallas_call(
        body, name=name, in_specs=[vm], out_specs=vm, out_shape=jax.ShapeDtypeStruct(a.shape, a.dtype),
        scratch_shapes=[pltpu.VMEM((8,) + a.shape, a.dtype), pltpu.SemaphoreType.DMA((7,)), pltpu.SemaphoreType.DMA((7,))],
    )(a)


TQ = 256


def _layer_small(sm, l):
    row = lambda v: v.reshape(1, -1)
    return dict(
        attn_norm=row(sm["attn_norm"][l]), mlp_norm=row(sm["mlp_norm"][l]),
        qgf=row(jnp.tile(sm["q_norm_fox"][l], 8)), kgf=row(jnp.tile(sm["k_norm_fox"][l], 8)),
        qgd=row(jnp.tile(sm["q_norm_dil"][l], 12)), kgd=row(jnp.tile(sm["k_norm_dil"][l], 12)),
        bfor=row(jnp.pad(sm["b_forget"][l], (0, LANES - 8))))


def _key_rows(f8, bsz, seq):
    f = f8.reshape(bsz, seq, LANES)[:, :, :8].transpose(0, 2, 1).reshape(bsz, 4, 2, seq)
    return jnp.pad(f, ((0, 0), (0, 0), (0, 6), (0, 0))).reshape(bsz * 32, seq)


def _layer_fwd(x, w, s, cos, sin, bsz, seq, l):
    nm = lambda t: f"l{l}_{t}"
    h, h_t = rmsnorm_fwd(x, s["attn_norm"], name=nm("attn_norm"))
    proj = matmul(h, w["win"], name=nm("proj"))
    qn, kn, fb, f8 = fox_prep_fwd(proj, s["qgf"], s["kgf"], s["bfor"], bsz=bsz, seq=seq, name=nm("fox_prep"))
    fk = _key_rows(f8, bsz, seq)
    oa, la = softmax_attn_fwd(qn, kn, proj, (fb, fk), qo=0, ko=0, vo=FOXV, pairs=4, bsz=bsz, seq=seq, window=seq, dilation=1,
                              tq=TQ, name=nm("fox_attn"))
    ob = sb_attn_fwd(proj, bsz=bsz, seq=seq, tq=TQ, name=nm("sb_attn"))
    qr, kr = dil_prep_fwd(proj, s["qgd"], s["kgd"], cos, sin, name=nm("dil_prep"))
    ogs, lgs = [], []
    for g, (window, dilation) in enumerate(DIL_PATTERNS):
        og, lg = softmax_attn_fwd(qr, kr, proj, None, qo=2 * g, ko=2 * g, vo=DILV + 2 * g, pairs=2, bsz=bsz, seq=seq,
                                  window=window, dilation=dilation, tq=TQ, name=nm(f"dil_attn{g}"))
        ogs.append(og)
        lgs.append(lg)
    oc = dil_combine_fwd(ogs, lgs, name=nm("dil_combine"))
    ys = [matmul(oa, w["wuf"], name=nm("up_fox")), matmul(ob, w["wus"], name=nm("up_sb")), matmul(oc, w["wud"], name=nm("up_dil"))]
    merged = merge_fwd(proj, ys, name=nm("merge"))
    if "late" in w:
        w.update(w.pop("late")(merged))
    x1 = matmul(merged, w["wo"], add=x, name=nm("out_proj"))
    h2, h2_t = rmsnorm_fwd(x1, s["mlp_norm"], name=nm("mlp_norm"))
    u, act = matmul(h2, w["wmi"], relu2=True, name=nm("mlp_in"))
    x2 = matmul(act, w["wmo"], add=x1, tk=2048, name=nm("mlp_out"))
    saved = dict(x=x, h_t=h_t, h2_t=h2_t, proj=proj, qn=qn, kn=kn, fb=fb, fk=fk, oa=oa, la=la, ob=ob, qr=qr, kr=kr, ogs=ogs, lgs=lgs, oc=oc,
                 ys=ys, merged=merged, x1=x1, u=u, act=act)
    return x2, saved


WIN_TILE = 256
WIN_STRIDE, WIN_TILES = 8, 9


def grad_buffers(depth, d, dff, wf, wd):
    assert dff // 4 == d
    return dict(win=lax.empty((4, depth * d, WIN_TILES * WIN_TILE), F32), ups=lax.empty((4, depth * (2 * wf + wd), d // 4), F32),
                wide=lax.empty((4, depth * (d + dff // 4 + d // 4), d), F32))


def _layer_bwd(dx2, w, s, sv, cos, sin, bsz, seq, l, depth, bufs):
    nm = lambda t: f"l{l}_{t}_bwd"
    n = bsz * seq
    proj = sv["proj"]
    d, dff = w["wmi"].shape
    wf, wd = w["wuf"].shape[0], w["wud"].shape[0]
    bufs = dict(bufs)
    rb = 512
    per_chip = dff // 4 // rb
    du = matmul(dx2, w["wmo"], tb=True, relu2_of=sv["u"], out_dtype=MM, name=nm("mlp_out_dx"))
    bufs["wide"] = matmul(sv["act"], dx2, ta=True, tm=rb, tn=d, tk=2048, name=nm("mlp_out_dw"),
                          dest=(bufs["wide"], 1, lambda j: j,
                                lambda i, j: (i // per_chip, (depth * d + l * (dff // 4)) // rb + i % per_chip, j)))
    dh2 = matmul(du, w["wmi"], tb=True, tk=2048, name=nm("mlp_in_dx"))
    bufs["wide"] = matmul(sv["h2_t"], du, tm=rb, tn=dff // 4, tk=2048, name=nm("mlp_in_dw"),
                          dest=(bufs["wide"], 4, lambda j: j, lambda i, j: (j, l * d // rb + i, 0)))
    dx1, g_mlp_norm = rmsnorm_bwd(sv["x1"], s["mlp_norm"], dh2, dx2, name=nm("mlp_norm"))

    dmerged = matmul(dx1, w["wo"], tb=True, name=nm("out_proj_dx"))
    bufs["wide"] = matmul(sv["merged"], dx1, ta=True, tm=d // 4, tn=d, tk=2048, name=nm("out_proj_dw"),
                          dest=(bufs["wide"], 1, lambda j: j, lambda i, j: (i, (depth * (d + dff // 4)) // (d // 4) + l, j)))
    dya, dyb, dyc, dga, dgb, dgc = merge_bwd(proj, sv["ys"], dmerged, name=nm("merge"))
    doa = matmul(dya, w["wuf"], tb=True, name=nm("up_fox_dx"))
    bufs["ups"] = matmul(sv["oa"], dya, ta=True, tm=wf, tn=d // 4, tk=2048, name=nm("up_fox_dw"),
                         dest=(bufs["ups"], 4, lambda j: j, lambda i, j: (j, l, 0)))
    dob = matmul(dyb, w["wus"], tb=True, name=nm("up_sb_dx"))
    bufs["ups"] = matmul(sv["ob"], dyb, ta=True, tm=wf, tn=d // 4, tk=2048, name=nm("up_sb_dw"),
                         dest=(bufs["ups"], 4, lambda j: j, lambda i, j: (j, depth + l, 0)))
    doc = matmul(dyc, w["wud"], tb=True, name=nm("up_dil_dx"))
    bufs["ups"] = matmul(sv["oc"], dyc, ta=True, tm=wd, tn=d // 4, tk=2048, name=nm("up_dil_dw"),
                         dest=(bufs["ups"], 4, lambda j: j, lambda i, j: (j, 2 * depth * wf // wd + l, 0)))

    outs = dil_combine_bwd(sv["ogs"], sv["lgs"], doc, name=nm("dil_combine"))
    dqs, dks, dvs = [], [], []
    for g, (window, dilation) in enumerate(DIL_PATTERNS):
        dq, dk, dv = softmax_attn_bwd(sv["qr"], sv["kr"], proj, sv["ogs"][g], outs[g], sv["lgs"][g], outs[3 + g], None,
                                      qo=2 * g, ko=2 * g, vo=DILV + 2 * g, pairs=2, bsz=bsz, seq=seq, window=window,
                                      dilation=dilation, tq=TQ, dq_dtype=F32, dk_dtype=F32, name=nm(f"dil_attn{g}"))
        dqs.append(dq)
        dks.append(dk)
        dvs.append(dv)
    d_dq, d_dk, g_qgd, g_kgd = dil_prep_bwd(proj, s["qgd"], s["kgd"], cos, sin, jnp.concatenate(dqs, axis=1),
                                            jnp.concatenate(dks, axis=1), name=nm("dil_prep"))

    s_dq, s_dk, s_dv = sb_attn_bwd(proj, dob, bsz=bsz, seq=seq, tq=TQ, name=nm("sb_attn"))

    dqn, dkn, f_dv, dfq, dfk = softmax_attn_bwd(sv["qn"], sv["kn"], proj, sv["oa"], doa, sv["la"], None, (sv["fb"], sv["fk"]),
                                                qo=0, ko=0, vo=FOXV, pairs=4, bsz=bsz, seq=seq, window=seq, dilation=1, tq=TQ,
                                                dq_dtype=F32, dk_dtype=F32, name=nm("fox_attn"))
    dfk8 = dfk.reshape(bsz, 4, 8, seq)[:, :, :2].reshape(bsz, 8, seq).transpose(0, 2, 1).reshape(n, 8)
    df = jnp.pad(dfq[:, ::HEAD] - dfk8, ((0, 0), (0, LANES - 8)))
    f_dq, f_dk, d_forget, g_qgf, g_kgf, g_bfor = fox_prep_bwd(proj, s["qgf"], s["kgf"], s["bfor"], dqn, dkn, df, bsz=bsz, seq=seq,
                                                              name=nm("fox_prep"))

    dproj = jnp.concatenate([f_dq, f_dk, f_dv, s_dq, s_dk, s_dv, d_dq, d_dk] + dvs + [dga, dgb, dgc, d_forget], axis=1)
    dh = matmul(dproj, w["win"], tb=True, tm=1024, tn=1024, tk=DPROJ // 4, name=nm("proj_dx"))
    bufs["win"] = matmul(sv["h_t"], dproj, tm=d, tn=WIN_TILE, tk=2048, name=nm("proj_dw"),
                         dest=(bufs["win"], 4 * WIN_TILES, lambda j: WIN_STRIDE * (j // WIN_TILES) + j % WIN_TILES,
                               lambda i, j: (j // WIN_TILES, l, j % WIN_TILES)))
    g_forget = matmul(sv["h_t"], d_forget, tk=2048, name=nm("forget_dw"))[:, :O2 - O1]
    dx, g_attn_norm = rmsnorm_bwd(sv["x"], s["attn_norm"], dh, dx1, name=nm("attn_norm"))
    gs = dict(attn_norm=g_attn_norm[0], mlp_norm=g_mlp_norm[0], b_forget=g_bfor[0, :8],
              q_norm_fox=g_qgf.reshape(8, HEAD).sum(0), k_norm_fox=g_kgf.reshape(8, HEAD).sum(0),
              q_norm_dil=g_qgd.reshape(12, HEAD).sum(0), k_norm_dil=g_kgd.reshape(12, HEAD).sum(0), w_in_forget=g_forget)
    return dx, bufs, gs


def local_step(x, positions, target, weights, small):
    bsz, seq, d = x.shape
    n = bsz * seq
    depth = len(weights)
    inv = 1.0 / (ROPE_THETA ** (jnp.arange(HEAD // 2, dtype=F32) / (HEAD // 2)))
    cos, sin = rope_table(positions.reshape(n, 1), jnp.tile(inv, 4).reshape(1, LANES), name="rope_table")
    xs = x.reshape(n, d)
    saved = []
    weights = list(weights)
    for l in range(depth):
        if callable(weights[l]):
            weights[l] = weights[l](xs)
        xs, sv = _layer_fwd(xs, weights[l], _layer_small(small, l), cos, sin, bsz, seq, l)
        saved.append(sv)
    dy, sq = loss_grad(xs, target.reshape(n, d), name="loss")
    loss = (0.5 / d) * jnp.sum(sq)
    w0 = weights[0]
    bufs = grad_buffers(depth, d, w0["wmi"].shape[1], w0["wuf"].shape[0], w0["wud"].shape[0])
    gss = [None] * depth
    for l in reversed(range(depth)):
        dy, bufs, gss[l] = _layer_bwd(dy, weights[l], _layer_small(small, l), saved[l], cos, sin, bsz, seq, l, depth, bufs)
    return loss, dy.reshape(bsz, seq, d), bufs, gss


SMALL = ("attn_norm", "mlp_norm", "b_forget", "q_norm_fox", "k_norm_fox", "q_norm_dil", "k_norm_dil")
SMALL_ROWS = 8


def _pack_small(vals):
    flat = jnp.concatenate([vals[k].reshape(-1) for k in SMALL])
    return jnp.pad(flat, (0, SMALL_ROWS * 1024 - flat.shape[0])).reshape(SMALL_ROWS, 1024)


def _unpack_small(packed, like):
    flat, out, at = packed.reshape(-1), {}, 0
    for k in SMALL:
        size = like[k].size
        out[k] = flat[at:at + size].reshape(like[k].shape)
        at += size
    return out


def kernel(x, positions, attn_norm, w_in, b_forget, q_norm_fox, k_norm_fox, q_norm_dil, k_norm_dil, w_up_fox, w_up_sb, w_up_dil, w_out, mlp_norm, w_mlp_in, w_mlp_out, loss_target, m_attn_norm, m_w_in, m_b_forget, m_q_norm_fox, m_k_norm_fox, m_q_norm_dil, m_k_norm_dil, m_w_up_fox, m_w_up_sb, m_w_up_dil, m_w_out, m_mlp_norm, m_w_mlp_in, m_w_mlp_out, v_attn_norm, v_w_in, v_b_forget, v_q_norm_fox, v_k_norm_fox, v_q_norm_dil, v_k_norm_dil, v_w_up_fox, v_w_up_sb, v_w_up_dil, v_w_out, v_mlp_norm, v_w_mlp_in, v_w_mlp_out):
    names = ("attn_norm", "w_in", "b_forget", "q_norm_fox", "k_norm_fox", "q_norm_dil", "k_norm_dil", "w_up_fox", "w_up_sb",
             "w_up_dil", "w_out", "mlp_norm", "w_mlp_in", "w_mlp_out")
    wv = dict(zip(names, (attn_norm, w_in, b_forget, q_norm_fox, k_norm_fox, q_norm_dil, k_norm_dil, w_up_fox, w_up_sb, w_up_dil,
                          w_out, mlp_norm, w_mlp_in, w_mlp_out)))
    mv = dict(zip(names, (m_attn_norm, m_w_in, m_b_forget, m_q_norm_fox, m_k_norm_fox, m_q_norm_dil, m_k_norm_dil, m_w_up_fox,
                          m_w_up_sb, m_w_up_dil, m_w_out, m_mlp_norm, m_w_mlp_in, m_w_mlp_out)))
    vv = dict(zip(names, (v_attn_norm, v_w_in, v_b_forget, v_q_norm_fox, v_k_norm_fox, v_q_norm_dil, v_k_norm_dil, v_w_up_fox,
                          v_w_up_sb, v_w_up_dil, v_w_out, v_mlp_norm, v_w_mlp_in, v_w_mlp_out)))
    depth = w_in.shape[0]
    flat2 = lambda a: a.reshape(-1, a.shape[-1])

    ups = ("w_up_fox", "w_up_sb", "w_up_dil")
    wide = ("w_mlp_in", "w_mlp_out", "w_out")
    core = lax.axis_index("c").astype(jnp.int32).reshape(1)
    chip = (2 * lax.axis_index("x") + lax.axis_index("y")).astype(jnp.int32).reshape(1)

    def shards(l):
        return [w_in[l].astype(MM), jnp.concatenate([wv[k][l] for k in ups]).astype(MM),
                jnp.concatenate([wv[k][l] for k in wide]).astype(MM)]

    def pieces(a, keys):
        out, at = {}, 0
        for k in keys:
            rows = wv[k].shape[1]
            out[k] = [a[c, at:at + rows] for c in range(4)]
            at += rows
        return out

    def with_own(gathered, own):
        return [lax.dynamic_update_index_in_dim(g, s, chip[0], 0) for g, s in zip(gathered, own)]

    def attention_weights(gathered, own):
        got_in, got_up = with_own(gathered, own)
        p = pieces(got_in, ("w_in",))["w_in"]
        pad = jnp.zeros((p[0].shape[0], DPROJ - DIN), p[0].dtype)
        win = jnp.concatenate([p[0][:, :O1], p[0][:, O2:], p[1], p[2], p[3], p[0][:, O1:O2], pad], axis=1)
        up = {k: jnp.concatenate(v, axis=1) for k, v in pieces(got_up, ups).items()}
        return dict(win=win, wuf=up["w_up_fox"], wus=up["w_up_sb"], wud=up["w_up_dil"])

    def late_weights(gathered, own):
        wd = pieces(with_own(gathered, own)[0], wide)
        return dict(wo=jnp.concatenate(wd["w_out"], axis=0), wmi=jnp.concatenate(wd["w_mlp_in"], axis=1),
                    wmo=jnp.concatenate(wd["w_mlp_out"], axis=0))

    def layer_weights(gathered, own):
        return {**attention_weights(gathered[:2], own[:2]), **late_weights(gathered[2:], own[2:])}

    small = {k: wv[k] for k in SMALL}
    small_fwd = dict(small)
    first = shards(0)
    started0 = gather_start(first[2:], name="gather_start0")
    order = started0[-1][0, 0]
    weights = [attention_weights(gather_chips(first[:2], name="gather_weights"), first[:2])]
    weights[0]["late"] = lambda after: late_weights(*gather_wait(*started0[:-1], after, name="gather_wait0"))
    for l in range(1, depth):
        started = gather_start(shards(l), name=f"gather_start{l}")
        order = order + started[-1][0, 0]
        weights.append(lambda after, l=l, started=started: layer_weights(*gather_wait(*started[:-1], after, name=f"gather_wait{l}")))
    small_fwd["attn_norm"] = small["attn_norm"] + order

    loss, grad_x, bufs, gss = local_step(x, positions, loss_target, weights, small_fwd)
    loss = lax.psum(loss, ("x", "y", "c"))

    g_small = {k: jnp.stack([gss[l][k] for l in range(depth)]) for k in SMALL}
    g_forget = jnp.stack([gss[l]["w_in_forget"] for l in range(depth)])
    summed = all_reduce_small(jnp.concatenate([_pack_small(g_small), g_forget.reshape(-1, 1024)]), name="reduce_small")
    g_small = _unpack_small(summed[:SMALL_ROWS], small)
    g_forget = summed[SMALL_ROWS:].reshape(g_forget.shape)

    parts = [bufs["win"], bufs["ups"], bufs["wide"]]
    place = jnp.concatenate([core, chip])
    sums = [pair_sum_scatter(p, place, name=f"reduce_pair_sum{t}") for t, p in enumerate(parts)]
    joined = [chip_sum_join(own, landed, chip, name=f"reduce_chip_sum{t}").reshape(-1, parts[t].shape[-1])
              for t, (own, landed) in enumerate(sums)]

    def own_w_in_columns(window):
        cols = w_in.shape[-1]
        first = jnp.concatenate([window[..., :O1], g_forget, window[..., O1:cols - (O2 - O1)]], axis=-1)
        shift = jnp.maximum((cols - WIN_STRIDE * WIN_TILE) * chip[0] - (O2 - O1), 0)
        rest = lax.dynamic_slice_in_dim(window, shift, cols, axis=2)
        return jnp.where(chip[0] == 0, first, rest)

    g_big = {"w_in": own_w_in_columns(joined[0].reshape(depth, -1, joined[0].shape[-1]))}
    for a, keys in ((joined[1], ups), (joined[2], wide)):
        at = 0
        for k in keys:
            rows = wv[k].shape[0] * wv[k].shape[1]
            g_big[k] = a[at:at + rows].reshape(wv[k].shape)
            at += rows

    grads = {**g_small, **g_big}
    delta, new_m, new_v = {}, {}, {}
    d_s, m_s, v_s = adamw(_pack_small(small), _pack_small(g_small), _pack_small({k: mv[k] for k in SMALL}),
                          _pack_small({k: vv[k] for k in SMALL}), name="adamw_small")
    delta.update(_unpack_small(d_s, small))
    new_m.update(_unpack_small(m_s, small))
    new_v.update(_unpack_small(v_s, small))
    for k in ("w_in",) + ups + wide:
        d_k, m_k, v_k = adamw(flat2(wv[k]), flat2(g_big[k]), flat2(mv[k]), flat2(vv[k]), name=f"adamw_{k}")
        delta[k], new_m[k], new_v[k] = d_k.reshape(wv[k].shape), m_k.reshape(wv[k].shape), v_k.reshape(wv[k].shape)

    return (loss, grad_x, *[grads[k] for k in names], *[delta[k] for k in names], *[new_m[k] for k in names], *[new_v[k] for k in names])
```

```python
import functools

import jax
import jax.numpy as jnp
from jax import lax
from jax.experimental import pallas as pl
from jax.experimental.pallas import tpu as pltpu

F32 = jnp.float32
BF16 = jnp.bfloat16
MM = jnp.bfloat16

HEAD = 64
LANES = 128
EPS = 1e-6
SCALE = 0.125
ROPE_THETA = 10000.0
DIL_PATTERNS = ((128, 1), (512, 4), (2048, 16))
ADAM_LR, ADAM_B1, ADAM_B2, ADAM_EPS, ADAM_WD, ADAM_STEP = 0.001, 0.9, 0.999, 1e-08, 0.01, 10

FOXQ, FOXK, FOXV = 0, 4, 8
SBQ, SBK, SBV = 12, 16, 20
DILQ, DILK, DILV = 24, 30, 36
GATE, FORGET, NBLK = 42, 66, 68
DPROJ = NBLK * LANES
O1, O2, O3, O4, DIN = 1536, 1544, 3080, 5384, 8456

VMEM_LIMIT = 56 * 1024 * 1024
MESH_ID = pl.DeviceIdType.MESH
ANY = pl.BlockSpec(memory_space=pl.ANY)


def _params(sem=None):
    return pltpu.CompilerParams(dimension_semantics=sem, vmem_limit_bytes=VMEM_LIMIT)


def _iota(shape, dim):
    return lax.broadcasted_iota(jnp.int32, shape, dim)


def _split2(x):
    hi = x.astype(BF16)
    lo = (x - hi.astype(F32)).astype(BF16)
    return hi, lo


def _split3(x):
    hi = x.astype(BF16)
    r = x - hi.astype(F32)
    mid = r.astype(BF16)
    lo = (r - mid.astype(F32)).astype(BF16)
    return hi, mid, lo


def _dot(a, b):
    return jnp.dot(a, b, preferred_element_type=F32)


def _dot_nt(a, b):
    return lax.dot_general(a, b, (((1,), (1,)), ((), ())), preferred_element_type=F32)


def _dot_tn(a, b):
    return lax.dot_general(a, b, (((0,), (0,)), ((), ())), preferred_element_type=F32)


def _xdot2(x, m):
    hi, lo = _split2(x)
    return _dot(hi, m) + _dot(lo, m)


def _xdot3(x, m):
    hi, mid, lo = _split3(x)
    return _dot(hi, m) + _dot(mid, m) + _dot(lo, m)


def _xdot3_left(m, x):
    hi, mid, lo = _split3(x)
    return _dot(m, hi) + _dot(m, mid) + _dot(m, lo)


def _head_mat(w):
    return ((_iota((w, w), 0) >> 6) == (_iota((w, w), 1) >> 6)).astype(BF16)


def _softplus_parts(z):
    e = jnp.exp(-jnp.abs(z))
    return e, jnp.maximum(z, 0.0) + jnp.log(1.0 + e)


def _fit(dim, want):
    t = min(want, dim)
    while dim % t:
        t -= LANES
        assert t > 0, (dim, want)
    return t


def matmul(a, b, *, ta=False, tb=False, out_dtype=F32, add=None, tm=2048, tn=512, tk=1024, dest=None, relu2=False,
           relu2_of=None, name):
    K, M = a.shape if ta else a.shape[::-1]
    K2, N = b.shape[::-1] if tb else b.shape
    assert K == K2, (a.shape, b.shape, ta, tb)
    tm, tn, tk = _fit(M, tm), _fit(N, tn), _fit(K, tk)
    nk = K // tk
    dn = (((0 if ta else 1,), (1 if tb else 0,)), ((), ()))
    if dest is None:
        tiles, source = N // tn, lambda j: j
    else:
        assert add is None and not tb
        buffer, tiles, source, place = dest

    extra = add if add is not None else relu2_of
    assert add is None or relu2_of is None

    def body(*refs):
        act_ref = None
        if dest is not None:
            a_ref, b_ref, _, o_ref, acc_ref = refs
        elif relu2:
            a_ref, b_ref, o_ref, act_ref, acc_ref = refs
        elif extra is None:
            a_ref, b_ref, o_ref, acc_ref = refs
        else:
            a_ref, b_ref, add_ref, o_ref, acc_ref = refs
        k = pl.program_id(2)
        part = lax.dot_general(a_ref[...].astype(MM), b_ref[...].astype(MM), dn, preferred_element_type=F32)

        @pl.when(k == 0)
        def _():
            acc_ref[...] = part

        @pl.when(k > 0)
        def _():
            acc_ref[...] += part

        @pl.when(k == nk - 1)
        def _():
            r = acc_ref[...]
            if add is not None:
                r = r + add_ref[...]
            if relu2_of is not None:
                r = r * (2.0 * jnp.maximum(add_ref[...], 0.0))
            o_ref[...] = r.astype(o_ref.dtype).reshape(o_ref.shape)
            if act_ref is not None:
                pos = jnp.maximum(r, 0.0)
                act_ref[...] = (pos * pos).astype(act_ref.dtype)

    a_spec = pl.BlockSpec((tk, tm), lambda i, j, k: (k, i)) if ta else pl.BlockSpec((tm, tk), lambda i, j, k: (i, k))
    b_spec = pl.BlockSpec((tn, tk), lambda i, j, k: (j, k)) if tb else pl.BlockSpec((tk, tn), lambda i, j, k: (k, source(j)))
    o_spec = pl.BlockSpec((tm, tn), lambda i, j, k: (i, j))
    ins, specs, aliases = [a, b], [a_spec, b_spec], {}
    out_shape = jax.ShapeDtypeStruct((M, N), out_dtype)
    if extra is not None:
        ins.append(extra)
        specs.append(o_spec)
    if relu2:
        o_spec, out_shape = [o_spec, o_spec], [out_shape, jax.ShapeDtypeStruct((M, N), MM)]
    if dest is not None:
        ins.append(buffer)
        specs.append(ANY)
        aliases = {2: 0}
        o_spec = pl.BlockSpec((1, tm, tn), lambda i, j, k: place(i, j))
        out_shape = jax.ShapeDtypeStruct(buffer.shape, buffer.dtype)
    return pl.pallas_call(
        body, name=name, grid=(M // tm, tiles, nk), in_specs=specs, out_specs=o_spec, out_shape=out_shape,
        scratch_shapes=[pltpu.VMEM((tm, tn), F32)], input_output_aliases=aliases,
        compiler_params=_params(("parallel", "parallel", "arbitrary")),
    )(*ins)


def _rows(n, want=512):
    t = min(want, n)
    assert n % t == 0, (n, t)
    return t


def rmsnorm_fwd(x, g, *, name):
    n, d = x.shape
    tr = _rows(n)

    def body(x_ref, g_ref, o_ref, t_ref):
        xv = x_ref[...]
        r = lax.rsqrt(jnp.mean(xv * xv, axis=1, keepdims=True) + EPS)
        y = xv * r * g_ref[...]
        o_ref[...] = y.astype(o_ref.dtype)
        t_ref[...] = y.T.astype(t_ref.dtype)

    row = pl.BlockSpec((tr, d), lambda i: (i, 0))
    vec = pl.BlockSpec((1, d), lambda i: (0, 0))
    return pl.pallas_call(
        body, name=name, grid=(n // tr,), in_specs=[row, vec], out_specs=[row, pl.BlockSpec((d, tr), lambda i: (0, i))],
        out_shape=[jax.ShapeDtypeStruct((n, d), MM), jax.ShapeDtypeStruct((d, n), MM)], compiler_params=_params(("parallel",)))(x, g)


def rmsnorm_bwd(x, g, dh, dres, *, name):
    n, d = x.shape
    tr = _rows(n)

    def body(x_ref, g_ref, dh_ref, dr_ref, dx_ref, dg_ref):
        @pl.when(pl.program_id(0) == 0)
        def _():
            dg_ref[...] = jnp.zeros_like(dg_ref)

        xv = x_ref[...]
        r = lax.rsqrt(jnp.mean(xv * xv, axis=1, keepdims=True) + EPS)
        y = xv * r
        dhv = dh_ref[...]
        dy = dhv * g_ref[...]
        dx_ref[...] = dr_ref[...] + r * (dy - y * jnp.mean(dy * y, axis=1, keepdims=True))
        dg_ref[...] += jnp.sum(dhv * y, axis=0, keepdims=True)

    row = pl.BlockSpec((tr, d), lambda i: (i, 0))
    vec = pl.BlockSpec((1, d), lambda i: (0, 0))
    return pl.pallas_call(
        body, name=name, grid=(n // tr,), in_specs=[row, vec, row, row], out_specs=[row, vec],
        out_shape=[jax.ShapeDtypeStruct((n, d), F32), jax.ShapeDtypeStruct((1, d), F32)],
        compiler_params=_params(("arbitrary",)))(x, g, dh, dres)


def loss_grad(y, tgt, *, name):
    n, d = y.shape
    tr = _rows(n)

    def body(y_ref, t_ref, dy_ref, acc_ref):
        @pl.when(pl.program_id(0) == 0)
        def _():
            acc_ref[...] = jnp.zeros_like(acc_ref)

        e = y_ref[...] - t_ref[...]
        dy_ref[...] = e * (1.0 / d)
        acc_ref[...] += jnp.sum(e * e, axis=0, keepdims=True)

    row = pl.BlockSpec((tr, d), lambda i: (i, 0))
    vec = pl.BlockSpec((1, d), lambda i: (0, 0))
    return pl.pallas_call(
        body, name=name, grid=(n // tr,), in_specs=[row, row], out_specs=[row, vec],
        out_shape=[jax.ShapeDtypeStruct((n, d), F32), jax.ShapeDtypeStruct((1, d), F32)],
        compiler_params=_params(("arbitrary",)))(y, tgt)


MERGE_W = 256


def _gate_specs(tr, d):
    per = d // MERGE_W
    base = GATE * LANES // MERGE_W
    return [pl.BlockSpec((tr, MERGE_W), functools.partial(lambda i, j, b: (i, base + per * b + j), b=b)) for b in range(3)]


def merge_fwd(proj, ys, *, name):
    n, d = ys[0].shape
    tr = _rows(n)

    def body(g0, g1, g2, y0, y1, y2, o_ref):
        acc = jax.nn.sigmoid(g0[...]) * y0[...]
        acc += jax.nn.sigmoid(g1[...]) * y1[...]
        acc += jax.nn.sigmoid(g2[...]) * y2[...]
        o_ref[...] = acc.astype(o_ref.dtype)

    blk = pl.BlockSpec((tr, MERGE_W), lambda i, j: (i, j))
    return pl.pallas_call(
        body, name=name, grid=(n // tr, d // MERGE_W), in_specs=_gate_specs(tr, d) + [blk] * 3, out_specs=blk,
        out_shape=jax.ShapeDtypeStruct((n, d), MM), compiler_params=_params(("parallel", "parallel")))(proj, proj, proj, *ys)


def merge_bwd(proj, ys, dm, *, name):
    n, d = dm.shape
    tr = _rows(n)

    def body(g0, g1, g2, y0, y1, y2, dm_ref, dy0, dy1, dy2, dg0, dg1, dg2):
        dmv = dm_ref[...]
        for g, y, dy, dg in ((g0, y0, dy0, dg0), (g1, y1, dy1, dg1), (g2, y2, dy2, dg2)):
            s = jax.nn.sigmoid(g[...])
            dy[...] = (dmv * s).astype(dy.dtype)
            dg[...] = (dmv * y[...] * s * (1.0 - s)).astype(dg.dtype)

    blk = pl.BlockSpec((tr, MERGE_W), lambda i, j: (i, j))
    out = jax.ShapeDtypeStruct((n, d), MM)
    return pl.pallas_call(
        body, name=name, grid=(n // tr, d // MERGE_W), in_specs=_gate_specs(tr, d) + [blk] * 4, out_specs=[blk] * 6,
        out_shape=[out] * 6, compiler_params=_params(("parallel", "parallel")))(proj, proj, proj, *ys, dm)


def adamw(w, g, m, v, *, name):
    r, c = w.shape
    tr = r
    while tr * c * 4 > (1 << 21) and tr % 16 == 0:
        tr //= 2
    c1 = 1.0 / (1.0 - ADAM_B1 ** ADAM_STEP)
    c2 = 1.0 / (1.0 - ADAM_B2 ** ADAM_STEP)

    def body(w_ref, g_ref, m_ref, v_ref, d_ref, mo_ref, vo_ref):
        gv = g_ref[...]
        m2 = ADAM_B1 * m_ref[...] + (1.0 - ADAM_B1) * gv
        v2 = ADAM_B2 * v_ref[...] + (1.0 - ADAM_B2) * (gv * gv)
        d_ref[...] = -ADAM_LR * ((m2 * c1) / (jnp.sqrt(v2 * c2) + ADAM_EPS) + ADAM_WD * w_ref[...])
        mo_ref[...] = m2
        vo_ref[...] = v2

    blk = pl.BlockSpec((tr, c), lambda i: (i, 0))
    out = jax.ShapeDtypeStruct((r, c), F32)
    return pl.pallas_call(body, name=name, grid=(r // tr,), in_specs=[blk] * 4, out_specs=[blk] * 3, out_shape=[out] * 3,
                          compiler_params=_params(("parallel",)))(w, g, m, v)


def rope_table(pos, inv, *, name):
    n = pos.shape[0]
    tr = _rows(n)

    def body(p_ref, i_ref, c_ref, s_ref):
        ang = p_ref[...].astype(F32) * i_ref[...]
        c_ref[...] = jnp.cos(ang)
        s_ref[...] = jnp.sin(ang)

    out = jax.ShapeDtypeStruct((n, LANES), F32)
    blk = pl.BlockSpec((tr, LANES), lambda i: (i, 0))
    return pl.pallas_call(
        body, name=name, grid=(n // tr,), in_specs=[pl.BlockSpec((tr, 1), lambda i: (i, 0)), pl.BlockSpec((1, LANES), lambda i: (0, 0))],
        out_specs=[blk, blk], out_shape=[out, out], compiler_params=_params(("parallel",)))(pos, inv)


def _rot_half(x):
    first = (_iota((1, LANES), 1) & 63) < 32
    return jnp.where(first, -pltpu.roll(x, LANES - 32, axis=1), pltpu.roll(x, 32, axis=1))


def _head_norm(xv, gm):
    r = lax.rsqrt(_xdot2(xv * xv, gm) * (1.0 / HEAD) + EPS)
    return r, xv * r


def _head_norm_bwd(xh, r, dxh, gm):
    return r * (dxh - xh * (_xdot2(dxh * xh, gm) * (1.0 / HEAD)))


def fox_prep_fwd(proj, qg, kg, bf, *, bsz, seq, name):
    n = bsz * seq
    tr = min(256, seq)
    nt = seq // tr
    w = 4 * LANES

    def body(q_ref, k_ref, f_ref, qg_ref, kg_ref, b_ref, qn_ref, kn_ref, fb_ref, f8_ref, carry):
        @pl.when(pl.program_id(1) == 0)
        def _():
            carry[...] = jnp.zeros_like(carry)

        gm = _head_mat(LANES)
        for src, gain, dst in ((q_ref, qg_ref, qn_ref), (k_ref, kg_ref, kn_ref)):
            for c in range(4):
                sl = slice(c * LANES, (c + 1) * LANES)
                _, xh = _head_norm(src[:, sl], gm)
                dst[:, sl] = (xh * gain[:, sl]).astype(dst.dtype)
        logf = jax.nn.log_sigmoid(f_ref[...] + b_ref[...])
        lower = (_iota((tr, tr), 1) <= _iota((tr, tr), 0)).astype(BF16)
        fcum = _xdot3_left(lower, logf) + carry[...]
        carry[...] = fcum[tr - 1:tr, :]
        f8_ref[...] = fcum
        spread = (_iota((LANES, w), 0) == (_iota((LANES, w), 1) >> 6)).astype(BF16)
        fb_ref[...] = _xdot3(fcum, spread)

    row = lambda width, blk: pl.BlockSpec((tr, width), lambda b, t: (b * nt + t, blk))
    vec = lambda width: pl.BlockSpec((1, width), lambda b, t: (0, 0))
    return pl.pallas_call(
        body, name=name, grid=(bsz, nt),
        in_specs=[row(w, FOXQ // 4), row(w, FOXK // 4), row(LANES, FORGET), vec(w), vec(w), vec(LANES)],
        out_specs=[row(w, 0), row(w, 0), row(w, 0), row(LANES, 0)],
        out_shape=[jax.ShapeDtypeStruct((n, w), MM), jax.ShapeDtypeStruct((n, w), MM),
                   jax.ShapeDtypeStruct((n, w), F32), jax.ShapeDtypeStruct((n, LANES), F32)],
        scratch_shapes=[pltpu.VMEM((1, LANES), F32)],
        compiler_params=_params(("parallel", "arbitrary")))(proj, proj, proj, qg, kg, bf)


def fox_prep_bwd(proj, qg, kg, bf, dqn, dkn, df, *, bsz, seq, name):
    n = bsz * seq
    tr = min(256, seq)
    nt = seq // tr
    w = 4 * LANES

    def body(q_ref, k_ref, f_ref, qg_ref, kg_ref, b_ref, dqn_ref, dkn_ref, df_ref,
             dq_ref, dk_ref, dl_ref, dqg_ref, dkg_ref, db_ref, carry):
        first = (pl.program_id(0) == 0) & (pl.program_id(1) == 0)

        @pl.when(first)
        def _():
            dqg_ref[...] = jnp.zeros_like(dqg_ref)
            dkg_ref[...] = jnp.zeros_like(dkg_ref)
            db_ref[...] = jnp.zeros_like(db_ref)

        @pl.when(pl.program_id(1) == 0)
        def _():
            carry[...] = jnp.zeros_like(carry)

        gm = _head_mat(LANES)
        for src, gain, dy_ref, dx_ref, dg_ref in ((q_ref, qg_ref, dqn_ref, dq_ref, dqg_ref), (k_ref, kg_ref, dkn_ref, dk_ref, dkg_ref)):
            for c in range(4):
                sl = slice(c * LANES, (c + 1) * LANES)
                r, xh = _head_norm(src[:, sl], gm)
                dy = dy_ref[:, sl]
                dg_ref[:, sl] += jnp.sum(dy * xh, axis=0, keepdims=True)
                dx_ref[:, sl] = _head_norm_bwd(xh, r, dy * gain[:, sl], gm).astype(dx_ref.dtype)
        upper = (_iota((tr, tr), 1) >= _iota((tr, tr), 0)).astype(BF16)
        dlogf = _xdot3_left(upper, df_ref[...]) + carry[...]
        carry[...] = dlogf[0:1, :]
        dlogit = dlogf * jax.nn.sigmoid(-(f_ref[...] + b_ref[...]))
        dl_ref[:, 0:LANES] = dlogit.astype(dl_ref.dtype)
        dl_ref[:, LANES:2 * LANES] = jnp.zeros((tr, LANES), dl_ref.dtype)
        db_ref[...] += jnp.sum(dlogit, axis=0, keepdims=True)

    row = lambda width, blk: pl.BlockSpec((tr, width), lambda b, t: (b * nt + nt - 1 - t, blk))
    vec = lambda width: pl.BlockSpec((1, width), lambda b, t: (0, 0))
    return pl.pallas_call(
        body, name=name, grid=(bsz, nt),
        in_specs=[row(w, FOXQ // 4), row(w, FOXK // 4), row(LANES, FORGET), vec(w), vec(w), vec(LANES),
                  row(w, 0), row(w, 0), row(LANES, 0)],
        out_specs=[row(w, 0), row(w, 0), row(2 * LANES, 0), vec(w), vec(w), vec(LANES)],
        out_shape=[jax.ShapeDtypeStruct((n, w), MM), jax.ShapeDtypeStruct((n, w), MM), jax.ShapeDtypeStruct((n, 2 * LANES), MM),
                   jax.ShapeDtypeStruct((1, w), F32), jax.ShapeDtypeStruct((1, w), F32), jax.ShapeDtypeStruct((1, LANES), F32)],
        scratch_shapes=[pltpu.VMEM((1, LANES), F32)],
        compiler_params=_params(("arbitrary", "arbitrary")))(proj, proj, proj, qg, kg, bf, dqn, dkn, df)


DIL_W = 6 * LANES


def dil_prep_fwd(proj, qg, kg, cos, sin, *, name):
    n = proj.shape[0]
    tr = _rows(n, 256)

    def body(q_ref, k_ref, qg_ref, kg_ref, c_ref, s_ref, qo_ref, ko_ref):
        gm = _head_mat(LANES)
        cv, sv = c_ref[...], s_ref[...]
        for src, gain, dst in ((q_ref, qg_ref, qo_ref), (k_ref, kg_ref, ko_ref)):
            for c in range(6):
                sl = slice(c * LANES, (c + 1) * LANES)
                _, xh = _head_norm(src[:, sl], gm)
                xn = xh * gain[:, sl]
                dst[:, sl] = (xn * cv + _rot_half(xn) * sv).astype(dst.dtype)

    row = lambda width, blk: pl.BlockSpec((tr, width), lambda i: (i, blk))
    vec = pl.BlockSpec((1, DIL_W), lambda i: (0, 0))
    out = jax.ShapeDtypeStruct((n, DIL_W), MM)
    return pl.pallas_call(
        body, name=name, grid=(n // tr,),
        in_specs=[row(DIL_W, DILQ // 6), row(DIL_W, DILK // 6), vec, vec, row(LANES, 0), row(LANES, 0)],
        out_specs=[row(DIL_W, 0), row(DIL_W, 0)], out_shape=[out, out],
        compiler_params=_params(("parallel",)))(proj, proj, qg, kg, cos, sin)


def dil_prep_bwd(proj, qg, kg, cos, sin, dqr, dkr, *, name):
    n = proj.shape[0]
    tr = _rows(n, 256)

    def body(q_ref, k_ref, qg_ref, kg_ref, c_ref, s_ref, dqr_ref, dkr_ref, dq_ref, dk_ref, dqg_ref, dkg_ref):
        @pl.when(pl.program_id(0) == 0)
        def _():
            dqg_ref[...] = jnp.zeros_like(dqg_ref)
            dkg_ref[...] = jnp.zeros_like(dkg_ref)

        gm = _head_mat(LANES)
        cv, sv = c_ref[...], s_ref[...]
        for src, gain, dy_ref, dx_ref, dg_ref in ((q_ref, qg_ref, dqr_ref, dq_ref, dqg_ref), (k_ref, kg_ref, dkr_ref, dk_ref, dkg_ref)):
            for c in range(6):
                sl = slice(c * LANES, (c + 1) * LANES)
                r, xh = _head_norm(src[:, sl], gm)
                dy = dy_ref[:, sl]
                dxn = dy * cv - _rot_half(dy * sv)
                dg_ref[:, sl] += jnp.sum(dxn * xh, axis=0, keepdims=True)
                dx_ref[:, sl] = _head_norm_bwd(xh, r, dxn * gain[:, sl], gm).astype(dx_ref.dtype)

    row = lambda width, blk: pl.BlockSpec((tr, width), lambda i: (i, blk))
    vec = pl.BlockSpec((1, DIL_W), lambda i: (0, 0))
    out = jax.ShapeDtypeStruct((n, DIL_W), MM)
    gout = jax.ShapeDtypeStruct((1, DIL_W), F32)
    return pl.pallas_call(
        body, name=name, grid=(n // tr,),
        in_specs=[row(DIL_W, DILQ // 6), row(DIL_W, DILK // 6), vec, vec, row(LANES, 0), row(LANES, 0), row(DIL_W, 0), row(DIL_W, 0)],
        out_specs=[row(DIL_W, 0), row(DIL_W, 0), vec, vec], out_shape=[out, out, gout, gout],
        compiler_params=_params(("arbitrary",)))(proj, proj, qg, kg, cos, sin, dqr, dkr)


def dil_combine_fwd(os_, lses, *, name):
    n, w = os_[0].shape
    tr = _rows(n)

    def body(o0, o1, o2, l0, l1, l2, out_ref):
        a, b, c = l0[...], l1[...], l2[...]
        m = jnp.maximum(jnp.maximum(a, b), c)
        ea, eb, ec = jnp.exp(a - m), jnp.exp(b - m), jnp.exp(c - m)
        out_ref[...] = ((ea * o0[...] + eb * o1[...] + ec * o2[...]) / (ea + eb + ec)).astype(out_ref.dtype)

    blk = pl.BlockSpec((tr, w), lambda i: (i, 0))
    return pl.pallas_call(body, name=name, grid=(n // tr,), in_specs=[blk] * 6, out_specs=blk,
                          out_shape=jax.ShapeDtypeStruct((n, w), MM), compiler_params=_params(("parallel",)))(*os_, *lses)


def dil_combine_bwd(os_, lses, dout, *, name):
    n, w = dout.shape
    tr = _rows(n)

    def body(o0, o1, o2, l0, l1, l2, d_ref, do0, do1, do2, dl0, dl1, dl2):
        a, b, c = l0[...], l1[...], l2[...]
        m = jnp.maximum(jnp.maximum(a, b), c)
        es = [jnp.exp(a - m), jnp.exp(b - m), jnp.exp(c - m)]
        inv = 1.0 / (es[0] + es[1] + es[2])
        ws = [e * inv for e in es]
        dv = d_ref[...]
        gm = _head_mat(w)
        dws = [_xdot2(dv * o[...], gm) for o in (o0, o1, o2)]
        mean = ws[0] * dws[0] + ws[1] * dws[1] + ws[2] * dws[2]
        for wg, dw, do, dl in zip(ws, dws, (do0, do1, do2), (dl0, dl1, dl2)):
            do[...] = wg * dv
            dl[...] = wg * (dw - mean)

    blk = pl.BlockSpec((tr, w), lambda i: (i, 0))
    out = jax.ShapeDtypeStruct((n, w), F32)
    return pl.pallas_call(body, name=name, grid=(n // tr,), in_specs=[blk] * 7, out_specs=[blk] * 6, out_shape=[out] * 6,
                          compiler_params=_params(("parallel",)))(*os_, *lses, dout)


def _key_plan(qi, tq, seq, window, run):
    if window + tq >= seq:
        for bi in range(seq // tq):
            lo = bi * tq
            segs = ([(0, lo, "bulk")] if lo else []) + [(lo, tq, "diag")]
            pl.when(qi == bi)(functools.partial(run, segs))
    else:
        ext = window + tq
        run([(pl.multiple_of(jnp.maximum((qi + 1) * tq - ext, 0), LANES), ext, "band")])


def _seg_mask(seg, qi, tq, window, dilation, strict=False):
    start, width, kind = seg
    d = _iota((tq, width), 0) - _iota((tq, width), 1)
    if kind == "bulk":
        d = d + width
    elif kind == "band":
        d = d + (qi * tq - start)
    ok = None
    if kind != "bulk":
        ok = (d > 0) if strict else (d >= 0)
    if window is not None:
        ok = (d <= window) if ok is None else ok & (d <= window)
    if dilation > 1:
        on_grid = (d & (dilation - 1)) == 0
        ok = on_grid if ok is None else ok & on_grid
    return ok


def _lane_first():
    return _iota((1, LANES), 1) < HEAD


def _attn_specs(bsz, seq, tq, qo, ko, vo):
    nq = seq // tq
    qspec = lambda off: pl.BlockSpec((tq, LANES), lambda b, j, i: (b * nq + i, off + j))
    kspec = lambda off: pl.BlockSpec((seq, LANES), lambda b, j, i: (b, off + j))
    return nq, qspec, kspec


def softmax_attn_fwd(q, k, v, bias, *, qo, ko, vo, pairs, bsz, seq, window, dilation, tq, name):
    n = bsz * seq
    nq, qspec, kspec = _attn_specs(bsz, seq, tq, qo, ko, vo)

    def body(*refs):
        if bias is None:
            q_ref, k_ref, v_ref, o_ref, l_ref = refs
        else:
            q_ref, k_ref, v_ref, fq_ref, fk_ref, o_ref, l_ref = refs
        qi = pl.program_id(2)

        def run(segs):
            qv = (q_ref[...] * SCALE).astype(MM)
            first = _lane_first()
            keys = [(k_ref[pl.ds(st, w), :].astype(MM), v_ref[pl.ds(st, w), :].astype(MM),
                     _seg_mask((st, w, kind), qi, tq, None if window >= seq else window, dilation), st, w)
                    for st, w, kind in segs]
            outs, lses = [], []
            for a in range(2):
                qa = jnp.where(first if a == 0 else ~first, qv, jnp.zeros_like(qv))
                scores = []
                for kv, _, ok, st, w in keys:
                    s = _dot_nt(qa, kv)
                    if bias is not None:
                        s = s + fq_ref[:, a * HEAD:a * HEAD + 1] - fk_ref[a:a + 1, pl.ds(st, w)]
                    scores.append(s if ok is None else jnp.where(ok, s, -jnp.inf))
                m = functools.reduce(jnp.maximum, [jnp.max(s, axis=1, keepdims=True) for s in scores])
                ps = [jnp.exp(s - m) for s in scores]
                den = sum(jnp.sum(p, axis=1, keepdims=True) for p in ps)
                acc = sum(_dot(p.astype(MM), vv) for p, (_, vv, _, _, _) in zip(ps, keys))
                outs.append(acc / den)
                lses.append(m + jnp.log(den))
            o_ref[...] = jnp.where(first, outs[0], outs[1]).astype(o_ref.dtype)
            l_ref[...] = jnp.where(first, lses[0], lses[1])

        _key_plan(qi, tq, seq, window, run)

    ins, specs = [q, k, v], [qspec(qo), kspec(ko), kspec(vo)]
    if bias is not None:
        ins += list(bias)
        specs += [qspec(0), pl.BlockSpec((8, seq), lambda b, j, i: (b * pairs + j, 0))]
    out = jax.ShapeDtypeStruct((n, LANES * pairs), F32)
    return pl.pallas_call(
        body, name=name, grid=(bsz, pairs, nq), in_specs=specs, out_specs=[qspec(0), qspec(0)], out_shape=[out, out],
        compiler_params=_params(("parallel", "parallel", "arbitrary")))(*ins)


def softmax_attn_bwd(q, k, v, o, do, lse, dlse, bias, *, qo, ko, vo, pairs, bsz, seq, window, dilation, tq, dq_dtype, dk_dtype, name):
    n = bsz * seq
    nq, qspec, kspec = _attn_specs(bsz, seq, tq, qo, ko, vo)
    has_bias, has_dlse = bias is not None, dlse is not None

    def body(*refs):
        refs = list(refs)
        q_ref, k_ref, v_ref, o_ref, do_ref, l_ref = refs[:6]
        del refs[:6]
        dl_ref = refs.pop(0) if has_dlse else None
        fq_ref, fk_ref = (refs.pop(0), refs.pop(0)) if has_bias else (None, None)
        dq_ref, dk_ref, dv_ref = refs[:3]
        del refs[:3]
        dfq_ref, dfk_ref = (refs.pop(0), refs.pop(0)) if has_bias else (None, None)
        dk_acc, dv_acc = refs
        qi = pl.program_id(2)

        @pl.when(qi == 0)
        def _():
            dk_acc[...] = jnp.zeros_like(dk_acc)
            dv_acc[...] = jnp.zeros_like(dv_acc)
            if has_bias:
                dfk_ref[...] = jnp.zeros_like(dfk_ref)

        def run(segs):
            qv = (q_ref[...] * SCALE).astype(MM)
            dov = do_ref[...]
            dob = dov.astype(MM)
            prod = dov * o_ref[...]
            first = _lane_first()
            keys = [(k_ref[pl.ds(st, w), :].astype(MM), v_ref[pl.ds(st, w), :].astype(MM),
                     _seg_mask((st, w, kind), qi, tq, None if window >= seq else window, dilation), st, w)
                    for st, w, kind in segs]
            dqs, dfqs = [], []
            dks, dvs = [[] for _ in keys], [[] for _ in keys]
            for a in range(2):
                mine = first if a == 0 else ~first
                col = slice(a * HEAD, a * HEAD + 1)
                delta = jnp.sum(jnp.where(mine, prod, 0.0), axis=1, keepdims=True)
                if has_dlse:
                    delta = delta - dl_ref[:, col]
                qa = jnp.where(mine, qv, jnp.zeros_like(qv))
                doa = jnp.where(mine, dob, jnp.zeros_like(dob))
                shift = l_ref[:, col]
                if has_bias:
                    shift = shift - fq_ref[:, col]
                dq, dfq = 0.0, 0.0
                for si, (kv, vv, ok, st, w) in enumerate(keys):
                    s = _dot_nt(qa, kv)
                    if has_bias:
                        s = s - fk_ref[a:a + 1, pl.ds(st, w)]
                    p = jnp.exp(s - shift)
                    if ok is not None:
                        p = jnp.where(ok, p, 0.0)
                    ds = p * (_dot_nt(doa, vv) - delta)
                    dsb = ds.astype(MM)
                    dvs[si].append(_dot_tn(p.astype(MM), dob))
                    dks[si].append(_dot_tn(dsb, qv))
                    dq = dq + _dot(dsb, kv)
                    if has_bias:
                        dfq = dfq + jnp.sum(ds, axis=1, keepdims=True)
                        dfk_ref[a:a + 1, pl.ds(st, w)] += jnp.sum(ds, axis=0, keepdims=True)
                dqs.append(dq * SCALE)
                dfqs.append(dfq)
            dq_ref[...] = jnp.where(first, dqs[0], dqs[1]).astype(dq_ref.dtype)
            for (_, _, _, st, w), dk, dv in zip(keys, dks, dvs):
                dk_acc[pl.ds(st, w), :] += jnp.where(first, dk[0], dk[1])
                dv_acc[pl.ds(st, w), :] += jnp.where(first, dv[0], dv[1])
            if has_bias:
                dfq_ref[...] = jnp.where(first, dfqs[0], dfqs[1])

        _key_plan(qi, tq, seq, window, run)

        @pl.when(qi == nq - 1)
        def _():
            dk_ref[...] = dk_acc[...].astype(dk_ref.dtype)
            dv_ref[...] = dv_acc[...].astype(dv_ref.dtype)

    wide = LANES * pairs
    ins = [q, k, v, o, do, lse]
    specs = [qspec(qo), kspec(ko), kspec(vo), qspec(0), qspec(0), qspec(0)]
    outs = [jax.ShapeDtypeStruct((n, wide), dq_dtype), jax.ShapeDtypeStruct((n, wide), dk_dtype), jax.ShapeDtypeStruct((n, wide), MM)]
    out_specs = [qspec(0), kspec(0), kspec(0)]
    if has_dlse:
        ins.append(dlse)
        specs.append(qspec(0))
    if has_bias:
        rows = pl.BlockSpec((8, seq), lambda b, j, i: (b * pairs + j, 0))
        ins += list(bias)
        specs += [qspec(0), rows]
        outs += [jax.ShapeDtypeStruct((n, wide), F32), jax.ShapeDtypeStruct((bsz * pairs * 8, seq), F32)]
        out_specs += [qspec(0), rows]
    return pl.pallas_call(
        body, name=name, grid=(bsz, pairs, nq), in_specs=specs, out_specs=out_specs, out_shape=outs,
        scratch_shapes=[pltpu.VMEM((seq, LANES), F32), pltpu.VMEM((seq, LANES), F32)],
        compiler_params=_params(("parallel", "parallel", "arbitrary")))(*ins)


def _running_sum(vals, mat, carry, lat_ref, start, reverse):
    nb = vals.shape[1] // LANES
    for cb in (reversed(range(nb)) if reverse else range(nb)):
        blk = vals[:, cb * LANES:(cb + 1) * LANES]
        lat_ref[:, start + cb * LANES:start + (cb + 1) * LANES] = _dot(blk.astype(BF16), mat) + carry
        carry = carry + jnp.sum(blk, axis=1, keepdims=True)
    return carry


def _sb_weights(qa, keys, tq, lat_ref):
    after = (_iota((LANES, LANES), 0) > _iota((LANES, LANES), 1)).astype(BF16)
    carry = jnp.zeros((tq, 1), F32)
    logs = []
    for kv, ok, st, w in reversed(keys):
        z = _dot_nt(qa, kv)
        _, sp = _softplus_parts(z)
        visible = sp if ok is None else jnp.where(ok, sp, 0.0)
        carry = _running_sum(visible, after, carry, lat_ref, st, True)
        logs.append(z - sp)
    out = []
    for (kv, ok, st, w), log_beta in zip(keys, reversed(logs)):
        att = jnp.exp(log_beta - lat_ref[:, st:st + w])
        out.append((log_beta, att if ok is None else jnp.where(ok, att, 0.0)))
    return out


def _sb_keys(k_ref, v_ref, segs, qi, tq):
    return [(k_ref[st:st + w, :].astype(MM), v_ref[st:st + w, :].astype(MM),
             _seg_mask((st, w, kind), qi, tq, None, 1, strict=True), st, w) for st, w, kind in segs]


def sb_attn_fwd(proj, *, bsz, seq, tq, name):
    n = bsz * seq
    pairs = 4
    nq, qspec, kspec = _attn_specs(bsz, seq, tq, SBQ, SBK, SBV)

    def body(q_ref, k_ref, v_ref, o_ref, lat_ref):
        qi = pl.program_id(2)

        def run(segs):
            qv = (q_ref[...] * SCALE).astype(MM)
            keys = _sb_keys(k_ref, v_ref, segs, qi, tq)
            first = _lane_first()
            outs = []
            for a in range(2):
                qa = jnp.where(first if a == 0 else ~first, qv, jnp.zeros_like(qv))
                weights = _sb_weights(qa, [(kv, ok, st, w) for kv, _, ok, st, w in keys], tq, lat_ref)
                outs.append(sum(_dot(att.astype(MM), vv) for (_, att), (_, vv, _, _, _) in zip(weights, keys)))
            o_ref[...] = jnp.where(first, outs[0], outs[1]).astype(o_ref.dtype)

        _key_plan(qi, tq, seq, seq, run)

    return pl.pallas_call(
        body, name=name, grid=(bsz, pairs, nq), in_specs=[qspec(SBQ), kspec(SBK), kspec(SBV)], out_specs=qspec(0),
        out_shape=jax.ShapeDtypeStruct((n, LANES * pairs), MM), scratch_shapes=[pltpu.VMEM((tq, seq), F32)],
        compiler_params=_params(("parallel", "parallel", "arbitrary")))(proj, proj, proj)


def sb_attn_bwd(proj, do, *, bsz, seq, tq, name):
    n = bsz * seq
    pairs = 4
    nq, qspec, kspec = _attn_specs(bsz, seq, tq, SBQ, SBK, SBV)

    def body(q_ref, k_ref, v_ref, do_ref, dq_ref, dk_ref, dv_ref, lat_ref, dk_acc, dv_acc):
        qi = pl.program_id(2)

        @pl.when(qi == 0)
        def _():
            dk_acc[...] = jnp.zeros_like(dk_acc)
            dv_acc[...] = jnp.zeros_like(dv_acc)

        def run(segs):
            qv = (q_ref[...] * SCALE).astype(MM)
            keys = _sb_keys(k_ref, v_ref, segs, qi, tq)
            dob = do_ref[...].astype(MM)
            first = _lane_first()
            before = (_iota((LANES, LANES), 0) < _iota((LANES, LANES), 1)).astype(BF16)
            dqs = []
            dks, dvs = [[] for _ in keys], [[] for _ in keys]
            for a in range(2):
                mine = first if a == 0 else ~first
                qa = jnp.where(mine, qv, jnp.zeros_like(qv))
                doa = jnp.where(mine, dob, jnp.zeros_like(dob))
                weights = _sb_weights(qa, [(kv, ok, st, w) for kv, _, ok, st, w in keys], tq, lat_ref)
                gs = [_dot_nt(doa, vv) * att for (_, att), (_, vv, _, _, _) in zip(weights, keys)]
                carry = jnp.zeros((tq, 1), F32)
                for g, (_, _, _, st, w) in zip(gs, keys):
                    carry = _running_sum(g, before, carry, lat_ref, st, False)
                dq = 0.0
                for si, ((log_beta, att), g, (kv, _, ok, st, w)) in enumerate(zip(weights, gs, keys)):
                    dz = g - jnp.exp(log_beta) * (g + lat_ref[:, st:st + w])
                    dz = (dz if ok is None else jnp.where(ok, dz, 0.0)).astype(MM)
                    dvs[si].append(_dot_tn(att.astype(MM), dob))
                    dks[si].append(_dot_tn(dz, qv))
                    dq = dq + _dot(dz, kv)
                dqs.append(dq * SCALE)
            dq_ref[...] = jnp.where(first, dqs[0], dqs[1]).astype(dq_ref.dtype)
            for (_, _, _, st, w), dk, dv in zip(keys, dks, dvs):
                dk_acc[st:st + w, :] += jnp.where(first, dk[0], dk[1])
                dv_acc[st:st + w, :] += jnp.where(first, dv[0], dv[1])

        _key_plan(qi, tq, seq, seq, run)

        @pl.when(qi == nq - 1)
        def _():
            dk_ref[...] = dk_acc[...].astype(dk_ref.dtype)
            dv_ref[...] = dv_acc[...].astype(dv_ref.dtype)

    out = jax.ShapeDtypeStruct((n, LANES * pairs), MM)
    return pl.pallas_call(
        body, name=name, grid=(bsz, pairs, nq), in_specs=[qspec(SBQ), kspec(SBK), kspec(SBV), qspec(0)],
        out_specs=[qspec(0), kspec(0), kspec(0)], out_shape=[out, out, out],
        scratch_shapes=[pltpu.VMEM((tq, seq), F32), pltpu.VMEM((seq, LANES), F32), pltpu.VMEM((seq, LANES), F32)],
        compiler_params=_params(("parallel", "parallel", "arbitrary")))(proj, proj, proj, do)


def _place():
    return lax.axis_index("x"), lax.axis_index("y"), lax.axis_index("c")


def _other_chips(x, y):
    return [(1 - x, y), (x, 1 - y), (1 - x, 1 - y)]


def _remote(src, dst, send_sems, recv_sems, k, to):
    return pltpu.make_async_remote_copy(src_ref=src, dst_ref=dst, send_sem=send_sems.at[k], recv_sem=recv_sems.at[k],
                                        device_id=to, device_id_type=MESH_ID)


def gather_chips(arrs, *, name):
    na = len(arrs)

    def body(*refs):
        ins, outs = refs[:na], refs[na:2 * na]
        send_sems, recv_sems = refs[2 * na:]
        x, y, c = _place()
        me, sibling = 2 * x + y, (x, y, 1 - c)
        chips = _other_chips(x, y)
        sends = []
        for t in range(na):
            rh = ins[t].shape[0] // 2
            half = lambda chip, h, t=t, rh=rh: outs[t].at[chip, pl.ds(h * rh, rh), :]
            for j, (px, py) in enumerate(chips):
                cp = _remote(ins[t].at[pl.ds(c * rh, rh), :], half(me, c), send_sems, recv_sems, 6 * t + j, (px, py, c))
                cp.start()
                sends.append(cp)
        for t in range(na):
            rh = ins[t].shape[0] // 2
            half = lambda chip, h, t=t, rh=rh: outs[t].at[chip, pl.ds(h * rh, rh), :]
            for j, (px, py) in enumerate(chips):
                landed = half(2 * px + py, c)
                _remote(landed, landed, send_sems, recv_sems, 6 * t + j, (px, py, c)).wait_recv()
                fw = _remote(landed, landed, send_sems, recv_sems, 6 * t + 3 + j, sibling)
                fw.start()
                sends.append(fw)
        for t in range(na):
            rh = ins[t].shape[0] // 2
            half = lambda chip, h, t=t, rh=rh: outs[t].at[chip, pl.ds(h * rh, rh), :]
            for j, (px, py) in enumerate(chips):
                passed = half(2 * px + py, 1 - c)
                _remote(passed, passed, send_sems, recv_sems, 6 * t + 3 + j, sibling).wait_recv()
        for cp in sends:
            cp.wait_send()

    for a in arrs:
        assert a.ndim == 2 and a.shape[0] % 32 == 0, a.shape
    return pl.pallas_call(
        body, name=name, in_specs=[ANY] * na, out_specs=[ANY] * na,
        out_shape=[jax.ShapeDtypeStruct((4,) + a.shape, a.dtype) for a in arrs],
        scratch_shapes=[pltpu.SemaphoreType.DMA((6 * na,)), pltpu.SemaphoreType.DMA((6 * na,))],
    )(*arrs)


HBM = pl.BlockSpec(memory_space=pltpu.HBM)
SEMS = pl.BlockSpec(memory_space=pltpu.SEMAPHORE)
DATAFLOW = pltpu.SideEffectType.DATAFLOW_SIDE_EFFECTING


def _in_hbm(a):
    return pltpu.with_memory_space_constraint(a, pltpu.HBM)


def gather_start(arrs, *, name):
    na = len(arrs)

    def body(*refs):
        ins, lands = refs[:na], refs[na:2 * na]
        send_sems, recv_sems = refs[2 * na], refs[2 * na + 1]
        token = refs[-1]
        x, y, c = _place()
        me = 2 * x + y
        for t in range(na):
            for j, (px, py) in enumerate(_other_chips(x, y)):
                _remote(ins[t], lands[t].at[me], send_sems, recv_sems, 3 * t + j, (px, py, c)).start()
        token[...] = jnp.zeros_like(token)

    lands = [lax.empty((4,) + a.shape, a.dtype) for a in arrs]
    out = pl.pallas_call(
        body, name=name, in_specs=[HBM] * (2 * na),
        out_specs=[SEMS, SEMS] + [HBM] * (2 * na) + [pl.BlockSpec(memory_space=pltpu.VMEM)],
        out_shape=[pltpu.SemaphoreType.DMA((3 * na,)), pltpu.SemaphoreType.DMA((3 * na,))]
        + [pltpu.HBM(a.shape, a.dtype) for a in arrs] + [pltpu.HBM(a.shape, a.dtype) for a in lands]
        + [jax.ShapeDtypeStruct((8, LANES), F32)],
        input_output_aliases={i: 2 + i for i in range(2 * na)},
        compiler_params=pltpu.CompilerParams(has_side_effects=DATAFLOW),
    )(*[_in_hbm(a) for a in arrs], *[_in_hbm(a) for a in lands])
    return out[0], out[1], list(out[2:2 + na]), list(out[2 + na:2 + 2 * na]), out[-1]


def gather_wait(send_sems, recv_sems, arrs, lands, after, *, name):
    na = len(arrs)

    def body(*refs):
        ins, lands_ = refs[:na], refs[na:2 * na]
        send_sems_, recv_sems_ = refs[2 * na], refs[2 * na + 1]
        x, y, c = _place()
        me = 2 * x + y
        for t in range(na):
            for j, (px, py) in enumerate(_other_chips(x, y)):
                sent = _remote(ins[t], lands_[t].at[me], send_sems_, recv_sems_, 3 * t + j, (px, py, c))
                sent.wait_send()
                came = _remote(ins[t], lands_[t].at[2 * px + py], send_sems_, recv_sems_, 3 * t + j, (px, py, c))
                came.wait_recv()

    out = pl.pallas_call(
        body, name=name, in_specs=[HBM] * (2 * na) + [SEMS, SEMS, ANY], out_specs=[HBM] * (2 * na),
        out_shape=[pltpu.HBM(a.shape, a.dtype) for a in arrs] + [pltpu.HBM(a.shape, a.dtype) for a in lands],
        input_output_aliases={i: i for i in range(2 * na)},
        compiler_params=pltpu.CompilerParams(has_side_effects=DATAFLOW),
    )(*arrs, *lands, send_sems, recv_sems, after)
    return list(out[na:]), list(out[:na])


CHUNK_BYTES = 4 << 20


def _chunk_rows(rows, cols, limit):
    best = 16
    for t in range(16, rows + 1, 16):
        if rows % t == 0 and t * cols * 4 <= limit:
            best = t
    assert rows % best == 0, (rows, cols)
    return best


def pair_sum_scatter(a, place, *, name):
    _, rows, cols = a.shape
    rh = rows // 2
    tr = _chunk_rows(rh, cols, CHUNK_BYTES)
    nch = rh // tr
    steps = 4 * nch

    def body(place_ref, keep_ref, send_ref, own_ref, landed_ref, landing, out16, res, pair_send, pair_recv, credit,
             chip_send, chip_recv, local_sem):
        i, j = pl.program_id(0), pl.program_id(1)
        step = i * 4 + j
        slot = lax.rem(step, 2)
        x, y, c = _place()
        sibling = (x, y, 1 - c)
        me = 2 * x + y
        rows_i = pl.ds(pl.multiple_of(i * tr, tr), tr)

        def to_chip(p, s):
            return pltpu.make_async_remote_copy(
                src_ref=out16.at[s], dst_ref=landed_ref.at[me, rows_i, :], send_sem=chip_send.at[s], recv_sem=chip_recv.at[p - 1],
                device_id=(x ^ (p >> 1), y ^ (p & 1), c), device_id_type=MESH_ID)

        @pl.when(step >= 2)
        def _():
            pl.semaphore_wait(credit, 1)

        cp = _remote(send_ref.at[0], landing.at[slot], pair_send, pair_recv, slot, sibling)
        cp.start()
        cp.wait_recv()
        total = keep_ref[0] + landing[slot]

        for p, s, before in ((1, 0, i > 0), (2, 1, i > 0), (3, 0, None)):
            @pl.when(j == p - 1)
            def _(p=p, s=s, before=before):
                if before is None:
                    to_chip(1, s).wait_send()
                else:
                    pl.when(before)(lambda: to_chip(1, s).wait_send())
                out16[s] = total.astype(BF16)
                to_chip(p, s).start()

        @pl.when(j == 3)
        def _():
            res[...] = total
            here = pltpu.make_async_copy(res, own_ref.at[rows_i, :], local_sem)
            here.start()
            here.wait()

        cp.wait_send()

        @pl.when(step + 2 < steps)
        def _():
            pl.semaphore_signal(credit, 1, device_id=sibling, device_id_type=MESH_ID)

        @pl.when(step == steps - 1)
        def _():
            to_chip(1, 1).wait_send()
            to_chip(1, 0).wait_send()
            for p in (1, 2, 3):
                slab = landed_ref.at[me ^ p]
                pltpu.make_async_remote_copy(src_ref=slab, dst_ref=slab, send_sem=chip_send.at[0], recv_sem=chip_recv.at[p - 1],
                                             device_id=(x ^ (p >> 1), y ^ (p & 1), c), device_id_type=MESH_ID).wait_recv()

    blk = (1, tr, cols)
    slab_of = lambda j, place: place[1] ^ ((j + 1) & 3)
    grid_spec = pltpu.PrefetchScalarGridSpec(
        num_scalar_prefetch=1, grid=(nch, 4),
        in_specs=[pl.BlockSpec(blk, lambda i, j, place: (slab_of(j, place), place[0] * nch + i, 0)),
                  pl.BlockSpec(blk, lambda i, j, place: (slab_of(j, place), (1 - place[0]) * nch + i, 0))],
        out_specs=[ANY, ANY],
        scratch_shapes=[pltpu.VMEM((2, tr, cols), F32), pltpu.VMEM((2, tr, cols), BF16), pltpu.VMEM((tr, cols), F32),
                        pltpu.SemaphoreType.DMA((2,)), pltpu.SemaphoreType.DMA((2,)), pltpu.SemaphoreType.REGULAR,
                        pltpu.SemaphoreType.DMA((2,)), pltpu.SemaphoreType.DMA((3,)), pltpu.SemaphoreType.DMA])
    return pl.pallas_call(
        body, name=name, grid_spec=grid_spec,
        out_shape=[jax.ShapeDtypeStruct((rh, cols), F32), jax.ShapeDtypeStruct((4, rh, cols), BF16)],
        compiler_params=_params(("arbitrary", "arbitrary")))(place, a, a)


def chip_sum_join(own, landed, chip, *, name):
    rh, cols = own.shape
    tr = _chunk_rows(rh, cols, CHUNK_BYTES)
    nch = rh // tr

    def body(chip_ref, own_ref, l1_ref, l2_ref, l3_ref, out_ref, res, local_sem, send_sem, recv_sem):
        i = pl.program_id(0)
        x, y, c = _place()
        sibling = (x, y, 1 - c)
        res[...] = ((own_ref[...] + l1_ref[0].astype(F32)) + l2_ref[0].astype(F32)) + l3_ref[0].astype(F32)
        rows = pl.ds(pl.multiple_of(i * tr, tr), tr)
        here = pltpu.make_async_copy(res, out_ref.at[c, rows, :], local_sem)
        here.start()
        there = pltpu.make_async_remote_copy(src_ref=res, dst_ref=out_ref.at[c, rows, :], send_sem=send_sem, recv_sem=recv_sem,
                                             device_id=sibling, device_id_type=MESH_ID)
        there.start()
        here.wait()
        there.wait_send()

        @pl.when(i == nch - 1)
        def _():
            half = out_ref.at[1 - c]
            pltpu.make_async_remote_copy(src_ref=half, dst_ref=half, send_sem=send_sem, recv_sem=recv_sem,
                                         device_id=sibling, device_id_type=MESH_ID).wait_recv()

    blk = (1, tr, cols)
    slab = lambda p: pl.BlockSpec(blk, lambda i, chip: (chip[0] ^ p, i, 0))
    grid_spec = pltpu.PrefetchScalarGridSpec(
        num_scalar_prefetch=1, grid=(nch,), out_specs=ANY,
        in_specs=[pl.BlockSpec((tr, cols), lambda i, chip: (i, 0)), slab(1), slab(2), slab(3)],
        scratch_shapes=[pltpu.VMEM((tr, cols), F32), pltpu.SemaphoreType.DMA, pltpu.SemaphoreType.DMA, pltpu.SemaphoreType.DMA])
    return pl.pallas_call(
        body, name=name, grid_spec=grid_spec, out_shape=jax.ShapeDtypeStruct((2, rh, cols), F32),
        compiler_params=_params(("arbitrary",)))(chip, own, landed, landed, landed)


def all_reduce_small(a, *, name):
    def body(a_ref, o_ref, buf, send_sems, recv_sems):
        x, y, c = _place()
        me = 4 * x + 2 * y + c
        buf[me] = a_ref[...]
        sent = []
        for p in range(1, 8):
            px, py, pc = (p >> 2) & 1, (p >> 1) & 1, p & 1
            cp = _remote(a_ref, buf.at[me], send_sems, recv_sems, p - 1, (x ^ px, y ^ py, c ^ pc))
            cp.start()
            sent.append(cp)
        for p in range(1, 8):
            px, py, pc = (p >> 2) & 1, (p >> 1) & 1, p & 1
            src = 4 * (x ^ px) + 2 * (y ^ py) + (c ^ pc)
            _remote(a_ref, buf.at[src], send_sems, recv_sems, p - 1, (x ^ px, y ^ py, c ^ pc)).wait_recv()
        for cp in sent:
            cp.wait_send()
        acc = buf[0]
        for d in range(1, 8):
            acc = acc + buf[d]
        o_ref[...] = acc

    vm = pl.BlockSpec(memory_space=pltpu.VMEM)
    return pl.pallas_call(
        body, name=name, in_specs=[vm], out_specs=vm, out_shape=jax.ShapeDtypeStruct(a.shape, a.dtype),
        scratch_shapes=[pltpu.VMEM((8,) + a.shape, a.dtype), pltpu.SemaphoreType.DMA((7,)), pltpu.SemaphoreType.DMA((7,))],
    )(a)


TQ = 256


def _band_tq(window, seq):
    return TQ // 2 if window + TQ < seq else TQ


def _layer_small(sm, l):
    row = lambda v: v.reshape(1, -1)
    return dict(
        attn_norm=row(sm["attn_norm"][l]), mlp_norm=row(sm["mlp_norm"][l]),
        qgf=row(jnp.tile(sm["q_norm_fox"][l], 8)), kgf=row(jnp.tile(sm["k_norm_fox"][l], 8)),
        qgd=row(jnp.tile(sm["q_norm_dil"][l], 12)), kgd=row(jnp.tile(sm["k_norm_dil"][l], 12)),
        bfor=row(jnp.pad(sm["b_forget"][l], (0, LANES - 8))))


def _key_rows(f8, bsz, seq):
    f = f8.reshape(bsz, seq, LANES)[:, :, :8].transpose(0, 2, 1).reshape(bsz, 4, 2, seq)
    return jnp.pad(f, ((0, 0), (0, 0), (0, 6), (0, 0))).reshape(bsz * 32, seq)


def _layer_fwd(x, w, s, cos, sin, bsz, seq, l):
    nm = lambda t: f"l{l}_{t}"
    h, h_t = rmsnorm_fwd(x, s["attn_norm"], name=nm("attn_norm"))
    proj = matmul(h, w["win"], name=nm("proj"))
    qn, kn, fb, f8 = fox_prep_fwd(proj, s["qgf"], s["kgf"], s["bfor"], bsz=bsz, seq=seq, name=nm("fox_prep"))
    fk = _key_rows(f8, bsz, seq)
    oa, la = softmax_attn_fwd(qn, kn, proj, (fb, fk), qo=0, ko=0, vo=FOXV, pairs=4, bsz=bsz, seq=seq, window=seq, dilation=1,
                              tq=TQ, name=nm("fox_attn"))
    ob = sb_attn_fwd(proj, bsz=bsz, seq=seq, tq=TQ, name=nm("sb_attn"))
    qr, kr = dil_prep_fwd(proj, s["qgd"], s["kgd"], cos, sin, name=nm("dil_prep"))
    ogs, lgs = [], []
    for g, (window, dilation) in enumerate(DIL_PATTERNS):
        og, lg = softmax_attn_fwd(qr, kr, proj, None, qo=2 * g, ko=2 * g, vo=DILV + 2 * g, pairs=2, bsz=bsz, seq=seq,
                                  window=window, dilation=dilation, tq=_band_tq(window, seq), name=nm(f"dil_attn{g}"))
        ogs.append(og)
        lgs.append(lg)
    oc = dil_combine_fwd(ogs, lgs, name=nm("dil_combine"))
    ys = [matmul(oa, w["wuf"], name=nm("up_fox")), matmul(ob, w["wus"], name=nm("up_sb")), matmul(oc, w["wud"], name=nm("up_dil"))]
    merged = merge_fwd(proj, ys, name=nm("merge"))
    if "late" in w:
        w.update(w.pop("late")(merged))
    x1 = matmul(merged, w["wo"], add=x, name=nm("out_proj"))
    h2, h2_t = rmsnorm_fwd(x1, s["mlp_norm"], name=nm("mlp_norm"))
    u, act = matmul(h2, w["wmi"], relu2=True, name=nm("mlp_in"))
    x2 = matmul(act, w["wmo"], add=x1, tk=2048, name=nm("mlp_out"))
    saved = dict(x=x, h_t=h_t, h2_t=h2_t, proj=proj, qn=qn, kn=kn, fb=fb, fk=fk, oa=oa, la=la, ob=ob, qr=qr, kr=kr, ogs=ogs, lgs=lgs, oc=oc,
                 ys=ys, merged=merged, x1=x1, u=u, act=act)
    return x2, saved


WIN_TILE = 256
WIN_STRIDE, WIN_TILES = 8, 9


def grad_buffers(depth, d, dff, wf, wd):
    assert dff // 4 == d
    return dict(win=lax.empty((4, depth * d, WIN_TILES * WIN_TILE), F32), ups=lax.empty((4, depth * (2 * wf + wd), d // 4), F32),
                wide=lax.empty((4, depth * (d + dff // 4 + d // 4), d), F32))


def _layer_bwd(dx2, w, s, sv, cos, sin, bsz, seq, l, depth, bufs):
    nm = lambda t: f"l{l}_{t}_bwd"
    n = bsz * seq
    proj = sv["proj"]
    d, dff = w["wmi"].shape
    wf, wd = w["wuf"].shape[0], w["wud"].shape[0]
    bufs = dict(bufs)
    rb = 512
    per_chip = dff // 4 // rb
    du = matmul(dx2, w["wmo"], tb=True, relu2_of=sv["u"], out_dtype=MM, name=nm("mlp_out_dx"))
    bufs["wide"] = matmul(sv["act"], dx2, ta=True, tm=rb, tn=d, tk=2048, name=nm("mlp_out_dw"),
                          dest=(bufs["wide"], 1, lambda j: j,
                                lambda i, j: (i // per_chip, (depth * d + l * (dff // 4)) // rb + i % per_chip, j)))
    dh2 = matmul(du, w["wmi"], tb=True, tk=2048, name=nm("mlp_in_dx"))
    bufs["wide"] = matmul(sv["h2_t"], du, tm=rb, tn=dff // 4, tk=2048, name=nm("mlp_in_dw"),
                          dest=(bufs["wide"], 4, lambda j: j, lambda i, j: (j, l * d // rb + i, 0)))
    dx1, g_mlp_norm = rmsnorm_bwd(sv["x1"], s["mlp_norm"], dh2, dx2, name=nm("mlp_norm"))

    dmerged = matmul(dx1, w["wo"], tb=True, name=nm("out_proj_dx"))
    bufs["wide"] = matmul(sv["merged"], dx1, ta=True, tm=d // 4, tn=d, tk=2048, name=nm("out_proj_dw"),
                          dest=(bufs["wide"], 1, lambda j: j, lambda i, j: (i, (depth * (d + dff // 4)) // (d // 4) + l, j)))
    dya, dyb, dyc, dga, dgb, dgc = merge_bwd(proj, sv["ys"], dmerged, name=nm("merge"))
    doa = matmul(dya, w["wuf"], tb=True, name=nm("up_fox_dx"))
    bufs["ups"] = matmul(sv["oa"], dya, ta=True, tm=wf, tn=d // 4, tk=2048, name=nm("up_fox_dw"),
                         dest=(bufs["ups"], 4, lambda j: j, lambda i, j: (j, l, 0)))
    dob = matmul(dyb, w["wus"], tb=True, name=nm("up_sb_dx"))
    bufs["ups"] = matmul(sv["ob"], dyb, ta=True, tm=wf, tn=d // 4, tk=2048, name=nm("up_sb_dw"),
                         dest=(bufs["ups"], 4, lambda j: j, lambda i, j: (j, depth + l, 0)))
    doc = matmul(dyc, w["wud"], tb=True, name=nm("up_dil_dx"))
    bufs["ups"] = matmul(sv["oc"], dyc, ta=True, tm=wd, tn=d // 4, tk=2048, name=nm("up_dil_dw"),
                         dest=(bufs["ups"], 4, lambda j: j, lambda i, j: (j, 2 * depth * wf // wd + l, 0)))

    outs = dil_combine_bwd(sv["ogs"], sv["lgs"], doc, name=nm("dil_combine"))
    dqs, dks, dvs = [], [], []
    for g, (window, dilation) in enumerate(DIL_PATTERNS):
        dq, dk, dv = softmax_attn_bwd(sv["qr"], sv["kr"], proj, sv["ogs"][g], outs[g], sv["lgs"][g], outs[3 + g], None,
                                      qo=2 * g, ko=2 * g, vo=DILV + 2 * g, pairs=2, bsz=bsz, seq=seq, window=window,
                                      dilation=dilation, tq=_band_tq(window, seq), dq_dtype=F32, dk_dtype=F32,
                                      name=nm(f"dil_attn{g}"))
        dqs.append(dq)
        dks.append(dk)
        dvs.append(dv)
    d_dq, d_dk, g_qgd, g_kgd = dil_prep_bwd(proj, s["qgd"], s["kgd"], cos, sin, jnp.concatenate(dqs, axis=1),
                                            jnp.concatenate(dks, axis=1), name=nm("dil_prep"))

    s_dq, s_dk, s_dv = sb_attn_bwd(proj, dob, bsz=bsz, seq=seq, tq=TQ, name=nm("sb_attn"))

    dqn, dkn, f_dv, dfq, dfk = softmax_attn_bwd(sv["qn"], sv["kn"], proj, sv["oa"], doa, sv["la"], None, (sv["fb"], sv["fk"]),
                                                qo=0, ko=0, vo=FOXV, pairs=4, bsz=bsz, seq=seq, window=seq, dilation=1, tq=TQ,
                                                dq_dtype=F32, dk_dtype=F32, name=nm("fox_attn"))
    dfk8 = dfk.reshape(bsz, 4, 8, seq)[:, :, :2].reshape(bsz, 8, seq).transpose(0, 2, 1).reshape(n, 8)
    df = jnp.pad(dfq[:, ::HEAD] - dfk8, ((0, 0), (0, LANES - 8)))
    f_dq, f_dk, d_forget, g_qgf, g_kgf, g_bfor = fox_prep_bwd(proj, s["qgf"], s["kgf"], s["bfor"], dqn, dkn, df, bsz=bsz, seq=seq,
                                                              name=nm("fox_prep"))

    dproj = jnp.concatenate([f_dq, f_dk, f_dv, s_dq, s_dk, s_dv, d_dq, d_dk] + dvs + [dga, dgb, dgc, d_forget], axis=1)
    dh = matmul(dproj, w["win"], tb=True, tm=1024, tn=1024, tk=DPROJ // 4, name=nm("proj_dx"))
    bufs["win"] = matmul(sv["h_t"], dproj, tm=d, tn=WIN_TILE, tk=2048, name=nm("proj_dw"),
                         dest=(bufs["win"], 4 * WIN_TILES, lambda j: WIN_STRIDE * (j // WIN_TILES) + j % WIN_TILES,
                               lambda i, j: (j // WIN_TILES, l, j % WIN_TILES)))
    g_forget = matmul(sv["h_t"], d_forget, tk=2048, name=nm("forget_dw"))[:, :O2 - O1]
    dx, g_attn_norm = rmsnorm_bwd(sv["x"], s["attn_norm"], dh, dx1, name=nm("attn_norm"))
    gs = dict(attn_norm=g_attn_norm[0], mlp_norm=g_mlp_norm[0], b_forget=g_bfor[0, :8],
              q_norm_fox=g_qgf.reshape(8, HEAD).sum(0), k_norm_fox=g_kgf.reshape(8, HEAD).sum(0),
              q_norm_dil=g_qgd.reshape(12, HEAD).sum(0), k_norm_dil=g_kgd.reshape(12, HEAD).sum(0), w_in_forget=g_forget)
    return dx, bufs, gs


def local_step(x, positions, target, weights, small):
    bsz, seq, d = x.shape
    n = bsz * seq
    depth = len(weights)
    inv = 1.0 / (ROPE_THETA ** (jnp.arange(HEAD // 2, dtype=F32) / (HEAD // 2)))
    cos, sin = rope_table(positions.reshape(n, 1), jnp.tile(inv, 4).reshape(1, LANES), name="rope_table")
    xs = x.reshape(n, d)
    saved = []
    weights = list(weights)
    for l in range(depth):
        if callable(weights[l]):
            weights[l] = weights[l](xs)
        xs, sv = _layer_fwd(xs, weights[l], _layer_small(small, l), cos, sin, bsz, seq, l)
        saved.append(sv)
    dy, sq = loss_grad(xs, target.reshape(n, d), name="loss")
    loss = (0.5 / d) * jnp.sum(sq)
    w0 = weights[0]
    bufs = grad_buffers(depth, d, w0["wmi"].shape[1], w0["wuf"].shape[0], w0["wud"].shape[0])
    gss = [None] * depth
    for l in reversed(range(depth)):
        dy, bufs, gss[l] = _layer_bwd(dy, weights[l], _layer_small(small, l), saved[l], cos, sin, bsz, seq, l, depth, bufs)
    return loss, dy.reshape(bsz, seq, d), bufs, gss


SMALL = ("attn_norm", "mlp_norm", "b_forget", "q_norm_fox", "k_norm_fox", "q_norm_dil", "k_norm_dil")
SMALL_ROWS = 8


def _pack_small(vals):
    flat = jnp.concatenate([vals[k].reshape(-1) for k in SMALL])
    return jnp.pad(flat, (0, SMALL_ROWS * 1024 - flat.shape[0])).reshape(SMALL_ROWS, 1024)


def _unpack_small(packed, like):
    flat, out, at = packed.reshape(-1), {}, 0
    for k in SMALL:
        size = like[k].size
        out[k] = flat[at:at + size].reshape(like[k].shape)
        at += size
    return out


def kernel(x, positions, attn_norm, w_in, b_forget, q_norm_fox, k_norm_fox, q_norm_dil, k_norm_dil, w_up_fox, w_up_sb, w_up_dil, w_out, mlp_norm, w_mlp_in, w_mlp_out, loss_target, m_attn_norm, m_w_in, m_b_forget, m_q_norm_fox, m_k_norm_fox, m_q_norm_dil, m_k_norm_dil, m_w_up_fox, m_w_up_sb, m_w_up_dil, m_w_out, m_mlp_norm, m_w_mlp_in, m_w_mlp_out, v_attn_norm, v_w_in, v_b_forget, v_q_norm_fox, v_k_norm_fox, v_q_norm_dil, v_k_norm_dil, v_w_up_fox, v_w_up_sb, v_w_up_dil, v_w_out, v_mlp_norm, v_w_mlp_in, v_w_mlp_out):
    names = ("attn_norm", "w_in", "b_forget", "q_norm_fox", "k_norm_fox", "q_norm_dil", "k_norm_dil", "w_up_fox", "w_up_sb",
             "w_up_dil", "w_out", "mlp_norm", "w_mlp_in", "w_mlp_out")
    wv = dict(zip(names, (attn_norm, w_in, b_forget, q_norm_fox, k_norm_fox, q_norm_dil, k_norm_dil, w_up_fox, w_up_sb, w_up_dil,
                          w_out, mlp_norm, w_mlp_in, w_mlp_out)))
    mv = dict(zip(names, (m_attn_norm, m_w_in, m_b_forget, m_q_norm_fox, m_k_norm_fox, m_q_norm_dil, m_k_norm_dil, m_w_up_fox,
                          m_w_up_sb, m_w_up_dil, m_w_out, m_mlp_norm, m_w_mlp_in, m_w_mlp_out)))
    vv = dict(zip(names, (v_attn_norm, v_w_in, v_b_forget, v_q_norm_fox, v_k_norm_fox, v_q_norm_dil, v_k_norm_dil, v_w_up_fox,
                          v_w_up_sb, v_w_up_dil, v_w_out, v_mlp_norm, v_w_mlp_in, v_w_mlp_out)))
    depth = w_in.shape[0]
    flat2 = lambda a: a.reshape(-1, a.shape[-1])

    ups = ("w_up_fox", "w_up_sb", "w_up_dil")
    wide = ("w_mlp_in", "w_mlp_out", "w_out")
    core = lax.axis_index("c").astype(jnp.int32).reshape(1)
    chip = (2 * lax.axis_index("x") + lax.axis_index("y")).astype(jnp.int32).reshape(1)

    def shards(l):
        return [w_in[l].astype(MM), jnp.concatenate([wv[k][l] for k in ups]).astype(MM),
                jnp.concatenate([wv[k][l] for k in wide]).astype(MM)]

    def pieces(a, keys):
        out, at = {}, 0
        for k in keys:
            rows = wv[k].shape[1]
            out[k] = [a[c, at:at + rows] for c in range(4)]
            at += rows
        return out

    def with_own(gathered, own):
        return [lax.dynamic_update_index_in_dim(g, s, chip[0], 0) for g, s in zip(gathered, own)]

    def attention_weights(gathered, own):
        got_in, got_up = with_own(gathered, own)
        p = pieces(got_in, ("w_in",))["w_in"]
        pad = jnp.zeros((p[0].shape[0], DPROJ - DIN), p[0].dtype)
        win = jnp.concatenate([p[0][:, :O1], p[0][:, O2:], p[1], p[2], p[3], p[0][:, O1:O2], pad], axis=1)
        up = {k: jnp.concatenate(v, axis=1) for k, v in pieces(got_up, ups).items()}
        return dict(win=win, wuf=up["w_up_fox"], wus=up["w_up_sb"], wud=up["w_up_dil"])

    def late_weights(gathered, own):
        wd = pieces(with_own(gathered, own)[0], wide)
        return dict(wo=jnp.concatenate(wd["w_out"], axis=0), wmi=jnp.concatenate(wd["w_mlp_in"], axis=1),
                    wmo=jnp.concatenate(wd["w_mlp_out"], axis=0))

    def layer_weights(gathered, own):
        return {**attention_weights(gathered[:2], own[:2]), **late_weights(gathered[2:], own[2:])}

    small = {k: wv[k] for k in SMALL}
    small_fwd = dict(small)
    first = shards(0)
    started0 = gather_start(first[2:], name="gather_start0")
    order = started0[-1][0, 0]
    weights = [attention_weights(gather_chips(first[:2], name="gather_weights"), first[:2])]
    weights[0]["late"] = lambda after: late_weights(*gather_wait(*started0[:-1], after, name="gather_wait0"))
    for l in range(1, depth):
        started = gather_start(shards(l), name=f"gather_start{l}")
        order = order + started[-1][0, 0]
        weights.append(lambda after, l=l, started=started: layer_weights(*gather_wait(*started[:-1], after, name=f"gather_wait{l}")))
    small_fwd["attn_norm"] = small["attn_norm"] + order

    loss, grad_x, bufs, gss = local_step(x, positions, loss_target, weights, small_fwd)
    loss = lax.psum(loss, ("x", "y", "c"))

    g_small = {k: jnp.stack([gss[l][k] for l in range(depth)]) for k in SMALL}
    g_forget = jnp.stack([gss[l]["w_in_forget"] for l in range(depth)])
    summed = all_reduce_small(jnp.concatenate([_pack_small(g_small), g_forget.reshape(-1, 1024)]), name="reduce_small")
    g_small = _unpack_small(summed[:SMALL_ROWS], small)
    g_forget = summed[SMALL_ROWS:].reshape(g_forget.shape)

    parts = [bufs["win"], bufs["ups"], bufs["wide"]]
    place = jnp.concatenate([core, chip])
    sums = [pair_sum_scatter(p, place, name=f"reduce_pair_sum{t}") for t, p in enumerate(parts)]
    joined = [chip_sum_join(own, landed, chip, name=f"reduce_chip_sum{t}").reshape(-1, parts[t].shape[-1])
              for t, (own, landed) in enumerate(sums)]

    def own_w_in_columns(window):
        cols = w_in.shape[-1]
        first = jnp.concatenate([window[..., :O1], g_forget, window[..., O1:cols - (O2 - O1)]], axis=-1)
        shift = jnp.maximum((cols - WIN_STRIDE * WIN_TILE) * chip[0] - (O2 - O1), 0)
        rest = lax.dynamic_slice_in_dim(window, shift, cols, axis=2)
        return jnp.where(chip[0] == 0, first, rest)

    g_big = {"w_in": own_w_in_columns(joined[0].reshape(depth, -1, joined[0].shape[-1]))}
    for a, keys in ((joined[1], ups), (joined[2], wide)):
        at = 0
        for k in keys:
            rows = wv[k].shape[0] * wv[k].shape[1]
            g_big[k] = a[at:at + rows].reshape(wv[k].shape)
            at += rows

    grads = {**g_small, **g_big}
    delta, new_m, new_v = {}, {}, {}
    d_s, m_s, v_s = adamw(_pack_small(small), _pack_small(g_small), _pack_small({k: mv[k] for k in SMALL}),
                          _pack_small({k: vv[k] for k in SMALL}), name="adamw_small")
    delta.update(_unpack_small(d_s, small))
    new_m.update(_unpack_small(m_s, small))
    new_v.update(_unpack_small(v_s, small))
    for k in ("w_in",) + ups + wide:
        d_k, m_k, v_k = adamw(flat2(wv[k]), flat2(g_big[k]), flat2(mv[k]), flat2(vv[k]), name=f"adamw_{k}")
        delta[k], new_m[k], new_v[k] = d_k.reshape(wv[k].shape), m_k.reshape(wv[k].shape), v_k.reshape(wv[k].shape)

    return (loss, grad_x, *[grads[k] for k in names], *[delta[k] for k in names], *[new_m[k] for k in names], *[new_v[k] for k in names])
```
